```python
import jax, jax.numpy as jnp
from jax import lax
import numpy as np

D_MODEL = 1024
BATCH = 32
SEQ = 256
DEPTH = 4
DEC_BATCH = 2
DEC_SEQ = 1024
PAST_LEN = 512

GRID_W = 64
N_AB = (DEPTH + 1) // 2
N_C = DEPTH // 2
EPS = 1e-6
ROPE_THETA = 10000.0
Q_BLOCK = 128
GLA_HEADS = 4
GLA_DK = 64
GLA_DV = 128
GLA_QK = GLA_HEADS * GLA_DK
GLA_V = GLA_HEADS * GLA_DV
GLA_RANK = 16
GLA_TAU = 16.0
GLA_CHUNK = 64
MLA_HEADS = 8
MLA_Q_RANK = 384
MLA_KV_RANK = 256
MLA_NOPE = 64
MLA_ROPE = 32
MLA_QK = MLA_NOPE + MLA_ROPE
MLA_V = 64
GQA_HEADS = 16
GQA_KV_HEADS = 4
GQA_DH = 64
C_OUT = GQA_HEADS * GQA_DH
MIX_WIDTH = GLA_V + MLA_HEADS * MLA_V
AB_WIDTHS = (GLA_QK, GLA_QK, GLA_V, GLA_V, 2 * GLA_RANK, MLA_Q_RANK, MLA_KV_RANK, MLA_ROPE)
AB_IN = GLA_QK * 2 + GLA_V * 2 + 2 * GLA_RANK + MLA_Q_RANK + MLA_KV_RANK + MLA_ROPE
C_WIDTHS = (GQA_HEADS * GQA_DH, GQA_KV_HEADS * GQA_DH, GQA_KV_HEADS * GQA_DH)
C_IN = GQA_HEADS * GQA_DH + 2 * GQA_KV_HEADS * GQA_DH
FFN_HIDDEN = -(-(8 * D_MODEL) // (3 * 256)) * 256

kernel_name = 'hybrid_diffusion_prefix_trunk_step'

F32 = jnp.float32


def split_cols(x, widths):
    offs, acc = [], 0
    for w in widths[:-1]:
        acc += w
        offs.append(acc)
    return jnp.split(x, offs, axis=-1)


def rms_norm(x, g):
    xf = x.astype(F32)
    y = xf * lax.rsqrt(jnp.mean(xf * xf, axis=-1, keepdims=True) + EPS)
    return (y * g.astype(F32)).astype(x.dtype)


def modulate(x, g, shift, scale):
    return rms_norm(x, g) * (1 + scale) + shift


def adaln(cond, w, b):
    m = jnp.einsum('...d,de->...e', jax.nn.silu(cond), w) + b
    return jnp.split(m[..., None, :], 6, axis=-1)


def axial_rope_tables(n_tok, d_rot):
    t = jnp.arange(n_tok, dtype=jnp.int32)
    pos = jnp.stack([t // GRID_W, t % GRID_W], axis=-1).astype(F32)
    quarter = d_rot // 4
    inv = jnp.power(ROPE_THETA, -jnp.arange(quarter, dtype=F32) / quarter)
    ang = pos[:, :, None] * inv
    return jnp.cos(ang), jnp.sin(ang)


def apply_axial_rope(x, cos, sin):
    B, T, H, d = x.shape
    xr = x.astype(F32).reshape(B, T, H, 2, 2, d // 4)
    x1, x2 = xr[..., 0, :], xr[..., 1, :]
    c, s = cos[None, :, None], sin[None, :, None]
    out = jnp.stack([x1 * c - x2 * s, x1 * s + x2 * c], axis=-2)
    return out.reshape(B, T, H, d).astype(x.dtype)


def rope_tail(x, cos, sin, start):
    return jnp.concatenate([x[..., :start], apply_axial_rope(x[..., start:], cos, sin)], axis=-1)


def block_attention(q, k, v):
    B, T, H, dh = q.shape
    Hkv, dv = k.shape[2], v.shape[-1]
    G = H // Hkv
    nb = T // Q_BLOCK
    scale = dh ** -0.5
    qb = q.reshape(B, nb, Q_BLOCK, Hkv, G, dh).transpose(1, 0, 2, 3, 4, 5)

    def one_block(qi):
        s = jnp.einsum('bqkgd,bskd->bkgqs', qi, k, preferred_element_type=F32) * scale
        p = jax.nn.softmax(s, axis=-1)
        return jnp.einsum('bkgqs,bskv->bqkgv', p.astype(v.dtype), v)

    o = lax.map(one_block, qb)
    return o.transpose(1, 0, 2, 3, 4, 5).reshape(B, T, H, dv)


def gla_chunk_scan(q, k, v, log_a, s0):
    B, T, H, dk = q.shape
    dv = v.shape[-1]
    C = GLA_CHUNK
    nc = T // C

    def rs(x):
        return x.astype(F32).reshape(B, nc, C, H, x.shape[-1]).transpose(1, 0, 3, 2, 4)

    qc, kc, vc, ac = rs(q), rs(k), rs(v), rs(log_a)
    b = jnp.cumsum(ac, axis=3)
    b_ref = b[:, :, :, C // 2 - 1:C // 2]
    b_last = b[:, :, :, C - 1:C]
    q_loc = qc * jnp.exp(b - b_ref)
    k_loc = kc * jnp.exp(b_ref - b)
    a_intra = jnp.einsum('nbhtd,nbhsd->nbhts', q_loc, k_loc)
    causal = jnp.tril(jnp.ones((C, C), dtype=bool))
    a_intra = jnp.where(causal, a_intra, 0.0)
    o_intra = jnp.einsum('nbhts,nbhsv->nbhtv', a_intra, vc)
    q_in = qc * jnp.exp(b)
    k_st = kc * jnp.exp(b_last - b)

    def step(S, inp):
        qi, ki, vi, bl = inp
        o = jnp.einsum('bhtd,bhdv->bhtv', qi, S)
        S = S * jnp.exp(bl)[:, :, 0, :, None] + jnp.einsum('bhsd,bhsv->bhdv', ki, vi)
        return S, o

    s_fin, o_inter = lax.scan(step, s0, (q_in, k_st, vc, b_last))
    o = (o_intra + o_inter).transpose(1, 0, 3, 2, 4).reshape(B, T, H, dv)
    return o, s_fin


def gla_prepare(q, k, v, a_lo, a_w2, a_b):
    B, T, _ = q.shape
    q = q.reshape(B, T, GLA_HEADS, GLA_DK) * (GLA_DK ** -0.5)
    k = k.reshape(B, T, GLA_HEADS, GLA_DK)
    v = v.reshape(B, T, GLA_HEADS, GLA_DV)
    logit = jnp.einsum('btzr,zre->btze', a_lo.reshape(B, T, 2, GLA_RANK), a_w2) + a_b
    log_a = (jax.nn.log_sigmoid(logit.astype(F32)) / GLA_TAU).reshape(B, T, 2, GLA_HEADS, GLA_DK)
    return q, k, v, log_a[:, :, 0], log_a[:, :, 1]


def gla_bidirectional(q, k, v, la_fwd, la_bwd, s0):
    o_f, s_f = gla_chunk_scan(q, k, v, la_fwd, s0[:, 0])
    fl = lambda a: jnp.flip(a, axis=1)
    o_b, s_b = gla_chunk_scan(fl(q), fl(k), fl(v), fl(la_bwd), s0[:, 1])
    return o_f + fl(o_b), jnp.stack([s_f, s_b], axis=1)


def gla_output(o, r, out_g):
    B, T = r.shape[:2]
    o = rms_norm(o.astype(r.dtype), out_g)
    return o.reshape(B, T, GLA_V) * jax.nn.silu(r)


def mla_queries(cq, q_norm_g, w_qb, qn_g):
    B, T, _ = cq.shape
    q = jnp.einsum('btr,re->bte', rms_norm(cq, q_norm_g), w_qb).reshape(B, T, MLA_HEADS, MLA_QK)
    return rms_norm(q, qn_g)


def mla_keys_values(ckv, kpe, w_kvb, kn_g):
    B, S, _ = ckv.shape
    kv = jnp.einsum('bsr,re->bse', ckv, w_kvb).reshape(B, S, MLA_HEADS, MLA_NOPE + MLA_V)
    k_nope, v = kv[..., :MLA_NOPE], kv[..., MLA_NOPE:]
    k_pe = jnp.broadcast_to(kpe[:, :, None, :], (B, S, MLA_HEADS, MLA_ROPE))
    k = rms_norm(jnp.concatenate([k_nope, k_pe], axis=-1), kn_g)
    return k, v


def ab_mixer_context(h, lp):
    B, T, _ = h.shape
    q, k, v, r, a_lo, cq, ckv, kpe = split_cols(jnp.einsum('btd,de->bte', h, lp['w_in']), AB_WIDTHS)
    qg, kg, vg, la_f, la_b = gla_prepare(q, k, v, a_lo, lp['a_w2'], lp['a_b'])
    s0 = jnp.zeros((B, 2, GLA_HEADS, GLA_DK, GLA_DV), F32)
    o_gla, gla_state = gla_bidirectional(qg, kg, vg, la_f, la_b, s0)
    o_gla = gla_output(o_gla, r, lp['gla_out_g'])
    ckv = rms_norm(ckv, lp['kv_norm_g'])
    qm = mla_queries(cq, lp['q_norm_g'], lp['w_qb'], lp['qn_g'])
    km, vm = mla_keys_values(ckv, kpe, lp['w_kvb'], lp['kn_g'])
    o_mla = block_attention(qm, km, vm).reshape(B, T, MLA_HEADS * MLA_V)
    out = jnp.einsum('bte,ed->btd', jnp.concatenate([o_gla, o_mla], axis=-1), lp['w_out'])
    return out, ckv, kpe, gla_state.astype(h.dtype)


def ab_mixer_latent(h, lp, ckv_ctx, kpe_ctx, gla_ctx, cos, sin):
    B, T, _ = h.shape
    q, k, v, r, a_lo, cq, ckv, kpe = split_cols(jnp.einsum('btd,de->bte', h, lp['w_in']), AB_WIDTHS)
    qg, kg, vg, la_f, la_b = gla_prepare(q, k, v, a_lo, lp['a_w2'], lp['a_b'])
    o_gla, _ = gla_bidirectional(qg, kg, vg, la_f, la_b, gla_ctx.astype(F32))
    o_gla = gla_output(o_gla, r, lp['gla_out_g'])
    ckv = rms_norm(ckv, lp['kv_norm_g'])
    qm = rope_tail(mla_queries(cq, lp['q_norm_g'], lp['w_qb'], lp['qn_g']), cos, sin, MLA_NOPE)
    k_lat, v_lat = mla_keys_values(ckv, kpe, lp['w_kvb'], lp['kn_g'])
    k_lat = rope_tail(k_lat, cos, sin, MLA_NOPE)
    k_ctx, v_ctx = mla_keys_values(ckv_ctx, kpe_ctx, lp['w_kvb'], lp['kn_g'])
    km = jnp.concatenate([k_ctx, k_lat], axis=1)
    vm = jnp.concatenate([v_ctx, v_lat], axis=1)
    o_mla = block_attention(qm, km, vm).reshape(B, T, MLA_HEADS * MLA_V)
    return jnp.einsum('bte,ed->btd', jnp.concatenate([o_gla, o_mla], axis=-1), lp['w_out'])


def gqa_project(h, lp):
    B, T, _ = h.shape
    q, k, v = split_cols(jnp.einsum('btd,de->bte', h, lp['w_in']), C_WIDTHS)
    q = rms_norm(q.reshape(B, T, GQA_HEADS, GQA_DH), lp['qn_g'])
    k = rms_norm(k.reshape(B, T, GQA_KV_HEADS, GQA_DH), lp['kn_g'])
    return q, k, v.reshape(B, T, GQA_KV_HEADS, GQA_DH)


def c_mixer_context(h, lp):
    B, T, _ = h.shape
    q, k, v = gqa_project(h, lp)
    o = block_attention(q, k, v).reshape(B, T, C_OUT)
    return jnp.einsum('bte,ed->btd', o, lp['w_out']), k, v


def c_mixer_latent(h, lp, k_ctx, v_ctx, cos, sin):
    B, T, _ = h.shape
    q, k, v = gqa_project(h, lp)
    q = apply_axial_rope(q, cos, sin)
    k = apply_axial_rope(k, cos, sin)
    o = block_attention(q, jnp.concatenate([k_ctx, k], axis=1), jnp.concatenate([v_ctx, v], axis=1))
    return jnp.einsum('bte,ed->btd', o.reshape(B, T, C_OUT), lp['w_out'])


def swiglu(h, w_in, w_out):
    g, u = jnp.split(jnp.einsum('btd,de->bte', h, w_in), 2, axis=-1)
    return jnp.einsum('btf,fd->btd', jax.nn.silu(g) * u, w_out)


def setup_inputs(seed: int = 0) -> dict:
    key = jax.random.key(seed)
    ks = jax.random.split(key, 30)
    nrm = lambda k, shape, s: jax.random.normal(k, shape, F32) * s
    gain = lambda k, shape: 1.0 + 0.05 * jax.random.normal(k, shape, F32)
    D = D_MODEL
    return {
        'x_prompt': nrm(ks[0], (BATCH, SEQ, D), 1.0),
        'x_sample': nrm(ks[1], (DEC_BATCH, DEC_SEQ, D), 1.0),
        'c': nrm(ks[2], (DEC_BATCH, D), 1.0),
        'cache_mla_ckv': nrm(ks[3], (DEC_BATCH, N_AB, PAST_LEN, MLA_KV_RANK), 1.0),
        'cache_mla_kpe': nrm(ks[4], (DEC_BATCH, N_AB, PAST_LEN, MLA_ROPE), 1.0),
        'state_gla': nrm(ks[5], (DEC_BATCH, N_AB, 2, GLA_HEADS, GLA_DK, GLA_DV), 1.0),
        'cache_gqa_k': nrm(ks[6], (DEC_BATCH, N_C, PAST_LEN, GQA_KV_HEADS, GQA_DH), 1.0),
        'cache_gqa_v': nrm(ks[7], (DEC_BATCH, N_C, PAST_LEN, GQA_KV_HEADS, GQA_DH), 1.0),
        'c_ctx': nrm(ks[8], (D,), 1.0),
        'ada_w': nrm(ks[9], (DEPTH, D, 6 * D), D ** -0.5),
        'ada_b': nrm(ks[10], (DEPTH, 6 * D), 0.01),
        'norm_mix_g': gain(ks[11], (DEPTH, D)),
        'norm_ffn_g': gain(ks[12], (DEPTH, D)),
        'ffn_w_in': nrm(ks[13], (DEPTH, D, 2 * FFN_HIDDEN), D ** -0.5),
        'ffn_w_out': nrm(ks[14], (DEPTH, FFN_HIDDEN, D), FFN_HIDDEN ** -0.5),
        'ab_w_in': nrm(ks[15], (N_AB, D, AB_IN), D ** -0.5),
        'ab_w_out': nrm(ks[16], (N_AB, MIX_WIDTH, D), MIX_WIDTH ** -0.5),
        'gla_a_w2': nrm(ks[17], (N_AB, 2, GLA_RANK, GLA_QK), GLA_RANK ** -0.5),
        'gla_a_b': nrm(ks[18], (N_AB, 2, GLA_QK), 0.1),
        'gla_out_g': gain(ks[19], (N_AB, GLA_DV)),
        'mla_q_norm_g': gain(ks[20], (N_AB, MLA_Q_RANK)),
        'mla_w_qb': nrm(ks[21], (N_AB, MLA_Q_RANK, MLA_HEADS * MLA_QK), MLA_Q_RANK ** -0.5),
        'mla_kv_norm_g': gain(ks[22], (N_AB, MLA_KV_RANK)),
        'mla_w_kvb': nrm(ks[23], (N_AB, MLA_KV_RANK, MLA_HEADS * (MLA_NOPE + MLA_V)), MLA_KV_RANK ** -0.5),
        'mla_qn_g': gain(ks[24], (N_AB, MLA_QK)),
        'mla_kn_g': gain(ks[25], (N_AB, MLA_QK)),
        'gqa_w_in': nrm(ks[26], (N_C, D, C_IN), D ** -0.5),
        'gqa_w_out': nrm(ks[27], (N_C, C_OUT, D), C_OUT ** -0.5),
        'gqa_qn_g': gain(ks[28], (N_C, GQA_DH)),
        'gqa_kn_g': gain(ks[29], (N_C, GQA_DH)),
    }


def reference(x_prompt, x_sample, c, cache_mla_ckv, cache_mla_kpe, state_gla, cache_gqa_k, cache_gqa_v,
              c_ctx, ada_w, ada_b, norm_mix_g, norm_ffn_g, ffn_w_in, ffn_w_out, ab_w_in, ab_w_out,
              gla_a_w2, gla_a_b, gla_out_g, mla_q_norm_g, mla_w_qb, mla_kv_norm_g, mla_w_kvb, mla_qn_g,
              mla_kn_g, gqa_w_in, gqa_w_out, gqa_qn_g, gqa_kn_g):
    rows = x_sample.shape[1] // GRID_W
    n_lat = rows * GRID_W
    cos_mla, sin_mla = axial_rope_tables(n_lat, MLA_ROPE)
    cos_gqa, sin_gqa = axial_rope_tables(n_lat, GQA_DH)
    xp, xs = x_prompt, x_sample
    new_ckv, new_kpe, new_gla, new_k, new_v = [], [], [], [], []
    for l in range(DEPTH):
        i = l // 2
        mp = adaln(c_ctx, ada_w[l], ada_b[l])
        ms = adaln(c, ada_w[l], ada_b[l])
        hp = modulate(xp, norm_mix_g[l], mp[0], mp[1])
        hs = modulate(xs, norm_mix_g[l], ms[0], ms[1])
        if l % 2 == 0:
            lp = {'w_in': ab_w_in[i], 'w_out': ab_w_out[i], 'a_w2': gla_a_w2[i], 'a_b': gla_a_b[i],
                  'gla_out_g': gla_out_g[i], 'q_norm_g': mla_q_norm_g[i], 'w_qb': mla_w_qb[i],
                  'kv_norm_g': mla_kv_norm_g[i], 'w_kvb': mla_w_kvb[i], 'qn_g': mla_qn_g[i], 'kn_g': mla_kn_g[i]}
            op, ckv, kpe, gst = ab_mixer_context(hp, lp)
            os_ = ab_mixer_latent(hs, lp, cache_mla_ckv[:, i], cache_mla_kpe[:, i], state_gla[:, i],
                                  cos_mla, sin_mla)
            new_ckv.append(ckv)
            new_kpe.append(kpe)
            new_gla.append(gst)
        else:
            lp = {'w_in': gqa_w_in[i], 'w_out': gqa_w_out[i], 'qn_g': gqa_qn_g[i], 'kn_g': gqa_kn_g[i]}
            op, kc, vc = c_mixer_context(hp, lp)
            os_ = c_mixer_latent(hs, lp, cache_gqa_k[:, i], cache_gqa_v[:, i], cos_gqa, sin_gqa)
            new_k.append(kc)
            new_v.append(vc)
        xp = xp + mp[2] * op
        xs = xs + ms[2] * os_
        hp = modulate(xp, norm_ffn_g[l], mp[3], mp[4])
        hs = modulate(xs, norm_ffn_g[l], ms[3], ms[4])
        xp = xp + mp[5] * swiglu(hp, ffn_w_in[l], ffn_w_out[l])
        xs = xs + ms[5] * swiglu(hs, ffn_w_in[l], ffn_w_out[l])
    return (xp, xs, jnp.stack(new_ckv, axis=1), jnp.stack(new_kpe, axis=1), jnp.stack(new_gla, axis=1),
            jnp.stack(new_k, axis=1), jnp.stack(new_v, axis=1))
```

```python
import functools

import jax
import jax.numpy as jnp
from jax import lax
from jax.experimental import pallas as pl
from jax.experimental.pallas import tpu as pltpu

F32 = jnp.float32
BF16 = jnp.bfloat16

EPS = 1e-6
ROPE_THETA = 10000.0
GRID_W = 64
LANES = 128
GLA_HEADS, GLA_DK, GLA_DV = 4, 64, 128
GLA_QK = GLA_HEADS * GLA_DK
GLA_V = GLA_HEADS * GLA_DV
GLA_RANK = 16
GLA_TAU = 16.0
GLA_CHUNK = 64
MLA_HEADS = 8
MLA_Q_RANK, MLA_KV_RANK = 384, 256
MLA_NOPE, MLA_ROPE, MLA_V = 64, 32, 64
MLA_QK = MLA_NOPE + MLA_ROPE
GQA_HEADS, GQA_KV_HEADS, GQA_DH = 16, 4, 64
VMEM_LIMIT = 56 << 20


def _cparams(sem, vmem=None):
    return pltpu.CompilerParams(dimension_semantics=sem, vmem_limit_bytes=vmem)


def _nt_dot(a, b):
    return lax.dot_general(a, b, (((1,), (1,)), ((), ())), preferred_element_type=F32)


def _tn_dot(a, b):
    return lax.dot_general(a, b, (((0,), (0,)), ((), ())), preferred_element_type=F32)


def _rms(x, g):
    ms = jnp.mean(x * x, axis=-1, keepdims=True)
    return x * lax.rsqrt(ms + EPS) * g


def _silu(x):
    return x * jax.nn.sigmoid(x)


def _rope(x, c, sp, sm, shift):
    return x * c + pltpu.roll(x, LANES - shift, 1) * sp + pltpu.roll(x, shift, 1) * sm


def _adaln_body(c_ref, w_ref, b_ref, o_ref):
    c = c_ref[...]
    s = _silu(c).astype(BF16)
    o_ref[0] = jnp.dot(s, w_ref[0].astype(BF16), preferred_element_type=F32) + b_ref[0]


def _adaln(cond, ada_w, ada_b):
    L, D, E = ada_w.shape
    tn = 1536
    return pl.pallas_call(
        _adaln_body,
        grid=(L, E // tn),
        in_specs=[pl.BlockSpec((8, D), lambda l, n: (0, 0)),
                  pl.BlockSpec((1, D, tn), lambda l, n: (l, 0, n)),
                  pl.BlockSpec((1, 1, tn), lambda l, n: (l, 0, n))],
        out_specs=pl.BlockSpec((1, 8, tn), lambda l, n: (l, 0, n)),
        out_shape=jax.ShapeDtypeStruct((L, 8, E), F32),
        compiler_params=_cparams(("arbitrary", "arbitrary")),
        name="adaln",
    )(cond, ada_w, ada_b.reshape(L, 1, E))


def _mod_index(mod_base, rows_per_cond, tm):
    if rows_per_cond is None:
        return lambda i: (mod_base, 0, 0)
    return lambda i: (mod_base + (i * tm) // rows_per_cond, 0, 0)


def _inproj_body(x_ref, g_ref, mod_ref, w_ref, *o_refs, widths):
    h = _rms(x_ref[...], g_ref[...])
    h = h * (1.0 + mod_ref[0, 1:2, :]) + mod_ref[0, 0:1, :]
    acc = jnp.dot(h.astype(BF16), w_ref[...], preferred_element_type=F32)
    off = 0
    for o_ref, w in zip(o_refs, widths):
        o_ref[...] = acc[:, off:off + w].astype(o_ref.dtype)
        off += w


def _inproj(x, g, mod, w, outs, *, mod_base, rows_per_cond, tm):
    n, d = x.shape
    widths = tuple(o[0] for o in outs)
    return pl.pallas_call(
        functools.partial(_inproj_body, widths=widths),
        grid=(n // tm,),
        in_specs=[pl.BlockSpec((tm, d), lambda i: (i, 0)),
                  pl.BlockSpec((1, d), lambda i: (0, 0)),
                  pl.BlockSpec((1, 6, d), _mod_index(mod_base, rows_per_cond, tm)),
                  pl.BlockSpec(w.shape, lambda i: (0, 0))],
        out_specs=[pl.BlockSpec((tm, wd), lambda i: (i, 0)) for wd in widths],
        out_shape=[jax.ShapeDtypeStruct((n, wd), dt) for wd, dt in outs],
        compiler_params=_cparams(("arbitrary",), VMEM_LIMIT),
        name="inproj",
    )(x, g.reshape(1, d), mod, w)


def _outproj_body(*refs, n_in):
    a_refs, w_refs = refs[:n_in], refs[n_in:2 * n_in]
    x_ref, mod_ref, o_ref = refs[2 * n_in:]
    acc = None
    for a_ref, w_ref in zip(a_refs, w_refs):
        part = jnp.dot(a_ref[...], w_ref[...], preferred_element_type=F32)
        acc = part if acc is None else acc + part
    o_ref[...] = x_ref[...] + mod_ref[0, 2:3, :] * acc


def _outproj(acts, ws, x, mod, *, mod_base, rows_per_cond, tm):
    n, d = x.shape
    n_in = len(acts)
    return pl.pallas_call(
        functools.partial(_outproj_body, n_in=n_in),
        grid=(n // tm,),
        in_specs=([pl.BlockSpec((tm, a.shape[1]), lambda i: (i, 0)) for a in acts]
                  + [pl.BlockSpec(w.shape, lambda i: (0, 0)) for w in ws]
                  + [pl.BlockSpec((tm, d), lambda i: (i, 0)),
                     pl.BlockSpec((1, 6, d), _mod_index(mod_base, rows_per_cond, tm))]),
        out_specs=pl.BlockSpec((tm, d), lambda i: (i, 0)),
        out_shape=jax.ShapeDtypeStruct((n, d), F32),
        compiler_params=_cparams(("arbitrary",), VMEM_LIMIT),
        name="outproj",
    )(*acts, *ws, x, mod)


def _ffn_body(x_ref, g_ref, mod_ref, wg_ref, wu_ref, wo_ref, o_ref, h_ref, *, nk, row_chunk):
    k = pl.program_id(1)
    tm = x_ref.shape[0]

    @pl.when(k == 0)
    def _():
        def norm_rows(c, carry):
            rows = pl.ds(pl.multiple_of(c * row_chunk, row_chunk), row_chunk)
            h = _rms(x_ref[rows, :], g_ref[...])
            h = h * (1.0 + mod_ref[0, 4:5, :]) + mod_ref[0, 3:4, :]
            h_ref[rows, :] = h.astype(BF16)
            return carry
        lax.fori_loop(0, tm // row_chunk, norm_rows, 0)

    h = h_ref[...]
    gate = jnp.dot(h, wg_ref[0].astype(BF16), preferred_element_type=F32)
    up = jnp.dot(h, wu_ref[0].astype(BF16), preferred_element_type=F32)
    a = (_silu(gate) * up).astype(BF16)
    part = jnp.dot(a, wo_ref[0].astype(BF16), preferred_element_type=F32)

    @pl.when(k == 0)
    def _():
        o_ref[...] = part

    @pl.when(k > 0)
    def _():
        o_ref[...] += part

    @pl.when(k == nk - 1)
    def _():
        o_ref[...] = x_ref[...] + mod_ref[0, 5:6, :] * o_ref[...]


def _ffn(x, g, mod, w_in, w_out, layer, *, mod_base, rows_per_cond, tm):
    n, d = x.shape
    hidden = w_out.shape[1]
    tk = 256
    nk = hidden // tk
    return pl.pallas_call(
        functools.partial(_ffn_body, nk=nk, row_chunk=256),
        grid=(n // tm, nk),
        in_specs=[pl.BlockSpec((tm, d), lambda i, k: (i, 0)),
                  pl.BlockSpec((1, d), lambda i, k: (0, 0)),
                  pl.BlockSpec((1, 6, d), (lambda f: (lambda i, k: f(i)))(_mod_index(mod_base, rows_per_cond, tm))),
                  pl.BlockSpec((1, d, tk), lambda i, k: (layer, 0, k)),
                  pl.BlockSpec((1, d, tk), lambda i, k: (layer, 0, nk + k)),
                  pl.BlockSpec((1, tk, d), lambda i, k: (layer, k, 0))],
        out_specs=pl.BlockSpec((tm, d), lambda i, k: (i, 0)),
        out_shape=jax.ShapeDtypeStruct((n, d), F32),
        scratch_shapes=[pltpu.VMEM((tm, d), BF16)],
        compiler_params=_cparams(("arbitrary", "arbitrary"), VMEM_LIMIT),
        name="ffn",
    )(x, g.reshape(1, d), mod, w_in, w_in, w_out)


def _log_sigmoid(x):
    return jnp.minimum(x, 0.0) - jnp.log1p(jnp.exp(-jnp.abs(x)))


def _gla_body(*refs, T, has_s0, want_state):
    qkvr_ref, alk_ref, w2_ref, ab_ref, og_ref = refs[:5]
    pos = 5
    s0_ref = None
    if has_s0:
        s0_ref = refs[pos]
        pos += 1
    o_ref = refs[pos]
    pos += 1
    st_out_ref = None
    if want_state:
        st_out_ref = refs[pos]
        pos += 1
    osc_ref, stf_ref, stb_ref, la_ref = refs[pos:pos + 4]

    C = GLA_CHUNK
    nc = T // C

    logit = jnp.dot(alk_ref[...].astype(BF16), w2_ref[...], preferred_element_type=F32) + ab_ref[...]
    la_ref[...] = _log_sigmoid(logit) * (1.0 / GLA_TAU)
    osc_ref[...] = jnp.zeros_like(osc_ref)

    for d, st_ref in enumerate((stf_ref, stb_ref)):
        if has_s0:
            rows = []
            for h in range(GLA_HEADS):
                blocks = [jnp.zeros((GLA_DK, GLA_DV), F32)] * GLA_HEADS
                blocks[h] = s0_ref[0, d, h]
                rows.append(jnp.concatenate(blocks, axis=1))
            st_ref[...] = jnp.concatenate(rows, axis=0).T
        else:
            st_ref[...] = jnp.zeros_like(st_ref)

    r64 = lax.broadcasted_iota(jnp.int32, (C, C), 0)
    c64 = lax.broadcasted_iota(jnp.int32, (C, C), 1)
    tri_f = (r64 >= c64).astype(BF16)
    tri_b = (c64 >= r64).astype(BF16)
    t_idx = lax.broadcasted_iota(jnp.int32, (C, GLA_QK), 0)
    s_idx = lax.broadcasted_iota(jnp.int32, (C, GLA_QK), 1) % C
    causal_f = t_idx >= s_idx
    causal_b = t_idx <= s_idx
    bm_k = (lax.broadcasted_iota(jnp.int32, (GLA_QK, GLA_QK), 0) // GLA_DK
            == lax.broadcasted_iota(jnp.int32, (GLA_QK, GLA_QK), 1) // GLA_DK)
    bm_v = (lax.broadcasted_iota(jnp.int32, (GLA_QK, GLA_V), 0) // C
            == lax.broadcasted_iota(jnp.int32, (GLA_QK, GLA_V), 1) // GLA_DV)
    bm_s = (lax.broadcasted_iota(jnp.int32, (GLA_V, GLA_QK), 0) // GLA_DV
            == lax.broadcasted_iota(jnp.int32, (GLA_V, GLA_QK), 1) // GLA_DK)

    def chunk(row0, la_off, tri, causal, ref_row, last_row, st_ref):
        rows = pl.ds(row0, C)
        qc = qkvr_ref[rows, 0:GLA_QK].astype(F32) * (GLA_DK ** -0.5)
        kc = qkvr_ref[rows, GLA_QK:2 * GLA_QK].astype(F32)
        vc = qkvr_ref[rows, 2 * GLA_QK:2 * GLA_QK + GLA_V]
        lac = la_ref[rows, la_off:la_off + GLA_QK]
        la_hi = lac.astype(BF16)
        la_lo = (lac - la_hi.astype(F32)).astype(BF16)
        b = (jnp.dot(tri, la_hi, preferred_element_type=F32)
             + jnp.dot(tri, la_lo, preferred_element_type=F32))
        b_ref = b[ref_row:ref_row + 1, :]
        b_last = b[last_row:last_row + 1, :]
        q_loc = (qc * jnp.exp(b - b_ref)).astype(BF16)
        k_loc = kc * jnp.exp(b_ref - b)
        q_in = (qc * jnp.exp(b)).astype(BF16)
        k_st = (kc * jnp.exp(b_last - b)).astype(BF16)
        k_bd = jnp.where(bm_k, jnp.concatenate([k_loc] * GLA_HEADS, axis=0), 0.0).astype(BF16)
        a = jnp.where(causal, _nt_dot(q_loc, k_bd), 0.0).astype(BF16)
        v_bd = jnp.where(bm_v, jnp.concatenate([vc] * GLA_HEADS, axis=0), jnp.zeros((), BF16))
        o_intra = jnp.dot(a, v_bd, preferred_element_type=F32)
        st = st_ref[...]
        o_inter = _nt_dot(q_in, st.astype(BF16))
        upd = _tn_dot(vc, k_st)
        st_ref[...] = st * jnp.exp(b_last) + jnp.where(bm_s, upd, 0.0)
        osc_ref[rows, :] += o_intra + o_inter

    def step(n, carry):
        chunk(pl.multiple_of(n * C, C), 0, tri_f, causal_f, C // 2 - 1, C - 1, stf_ref)
        chunk(pl.multiple_of((nc - 1 - n) * C, C), GLA_QK, tri_b, causal_b, C // 2, 0, stb_ref)
        return carry

    lax.fori_loop(0, nc, step, 0)

    rb = 256
    for i in range(T // rb):
        rows = slice(i * rb, (i + 1) * rb)
        for h in range(GLA_HEADS):
            cols = slice(h * GLA_DV, (h + 1) * GLA_DV)
            o = _rms(osc_ref[rows, cols], og_ref[:, cols])
            r = qkvr_ref[rows, 2 * GLA_QK + GLA_V + h * GLA_DV:2 * GLA_QK + GLA_V + (h + 1) * GLA_DV].astype(F32)
            o_ref[rows, cols] = (o * _silu(r)).astype(o_ref.dtype)

    if want_state:
        for d, st_ref in enumerate((stf_ref, stb_ref)):
            s_bd = st_ref[...].T
            for h in range(GLA_HEADS):
                st_out_ref[0, d, h] = s_bd[h * GLA_DK:(h + 1) * GLA_DK, h * GLA_DV:(h + 1) * GLA_DV]


def _gla(proj, alk, w2bd, a_b, out_g, s0, *, B, T, want_state):
    has_s0 = s0 is not None
    n_qkvr = 2 * GLA_QK + 2 * GLA_V
    in_specs = [pl.BlockSpec((T, n_qkvr), lambda b: (b, 0)),
                pl.BlockSpec((T, LANES), lambda b: (b, 0)),
                pl.BlockSpec(w2bd.shape, lambda b: (0, 0)),
                pl.BlockSpec((1, 2 * GLA_QK), lambda b: (0, 0)),
                pl.BlockSpec((1, GLA_V), lambda b: (0, 0))]
    args = [proj, alk, w2bd, a_b, out_g]
    st_spec = pl.BlockSpec((1, 2, GLA_HEADS, GLA_DK, GLA_DV), lambda b: (b, 0, 0, 0, 0))
    if has_s0:
        in_specs.append(st_spec)
        args.append(s0)
    out_specs = [pl.BlockSpec((T, GLA_V), lambda b: (b, 0))]
    out_shape = [jax.ShapeDtypeStruct((B * T, GLA_V), BF16)]
    if want_state:
        out_specs.append(st_spec)
        out_shape.append(jax.ShapeDtypeStruct((B, 2, GLA_HEADS, GLA_DK, GLA_DV), F32))
    res = pl.pallas_call(
        functools.partial(_gla_body, T=T, has_s0=has_s0, want_state=want_state),
        grid=(B,),
        in_specs=in_specs,
        out_specs=out_specs,
        out_shape=out_shape,
        scratch_shapes=[pltpu.VMEM((T, GLA_V), F32),
                        pltpu.VMEM((GLA_V, GLA_QK), F32),
                        pltpu.VMEM((GLA_V, GLA_QK), F32),
                        pltpu.VMEM((T, 2 * GLA_QK), F32)],
        compiler_params=_cparams(("arbitrary",), VMEM_LIMIT),
        name="gla",
    )(*args)
    return res if want_state else (res[0], None)


def _softmax_pv(s, v):
    m = jnp.max(s, axis=-1, keepdims=True)
    p = jnp.exp(s - m)
    l = jnp.sum(p, axis=-1, keepdims=True)
    return jnp.dot(p.astype(BF16), v, preferred_element_type=F32) / l


def _mla_body(*refs, T, S_ctx, rope, want_ckv):
    cq_ref, kpe_ref, ckv_ref = refs[:3]
    pos = 3
    if S_ctx:
        ckvc_ref, kpec_ref = refs[pos:pos + 2]
        pos += 2
    if rope:
        rq_refs = refs[pos:pos + 3]
        rk_refs = refs[pos + 3:pos + 6]
        pos += 6
    qng_ref, wqb_ref, qg_ref, kvg_ref, wkvb_ref, kg_ref = refs[pos:pos + 6]
    pos += 6
    o_ref = refs[pos]
    pos += 1
    if want_ckv:
        ckvn_ref = refs[pos]
        pos += 1
    k_sc, vlo_sc, vhi_sc = refs[pos:pos + 3]

    H = MLA_HEADS
    n_k = H * LANES
    lo = (lax.broadcasted_iota(jnp.int32, (1, H * MLA_V), 1) % LANES) < MLA_V

    def fill_kv(ckvn, kpe, row0, n_rows, with_rope):
        rows = slice(row0, row0 + n_rows)
        kv = jnp.dot(ckvn.astype(BF16), wkvb_ref[...], preferred_element_type=F32)
        for h in range(H):
            kh = kv[:, h * LANES:(h + 1) * LANES] + kpe
            ms = jnp.sum(kh * kh, axis=-1, keepdims=True) * (1.0 / MLA_QK)
            kh = kh * lax.rsqrt(ms + EPS) * kg_ref[...]
            if with_rope:
                kh = _rope(kh, rk_refs[0][...], rk_refs[1][...], rk_refs[2][...], MLA_ROPE // 4)
            k_sc[rows, h * LANES:(h + 1) * LANES] = kh.astype(BF16)
        v = kv[:, n_k:]
        vlo_sc[rows, :] = jnp.where(lo, v, 0.0).astype(BF16)
        vhi_sc[rows, :] = jnp.where(lo, 0.0, v).astype(BF16)

    @pl.when(pl.program_id(1) == 0)
    def _():
        if S_ctx:
            fill_kv(ckvc_ref[0], kpec_ref[0], 0, S_ctx, False)
        ckvn = _rms(ckv_ref[...], kvg_ref[...])
        if want_ckv:
            ckvn_ref[...] = ckvn
        fill_kv(ckvn, kpe_ref[...].astype(F32), S_ctx, T, rope)

    cqn = _rms(cq_ref[...].astype(F32), qng_ref[...])
    q = jnp.dot(cqn.astype(BF16), wqb_ref[...], preferred_element_type=F32)
    for p in range(H // 2):
        o_pair = None
        for h, v_sc in ((2 * p, vlo_sc), (2 * p + 1, vhi_sc)):
            qh = q[:, h * LANES:(h + 1) * LANES]
            ms = jnp.sum(qh * qh, axis=-1, keepdims=True) * (1.0 / MLA_QK)
            qh = qh * lax.rsqrt(ms + EPS) * qg_ref[...]
            if rope:
                qh = _rope(qh, rq_refs[0][...], rq_refs[1][...], rq_refs[2][...], MLA_ROPE // 4)
            s = _nt_dot(qh.astype(BF16), k_sc[:, h * LANES:(h + 1) * LANES])
            o_h = _softmax_pv(s, v_sc[:, p * LANES:(p + 1) * LANES])
            o_pair = o_h if o_pair is None else o_pair + o_h
        o_ref[:, p * LANES:(p + 1) * LANES] = o_pair.astype(o_ref.dtype)


def _mla(proj, ckv, ctx, rope_tabs, params, *, B, T, tq, want_ckv):
    nq = T // tq
    S_ctx = ctx[0].shape[1] if ctx is not None else 0
    S = S_ctx + T
    rope = rope_tabs is not None
    cq_blk = (2 * GLA_QK + 2 * GLA_V) // MLA_Q_RANK
    kpe_blk = (2 * GLA_QK + 2 * GLA_V + MLA_Q_RANK) // LANES
    in_specs = [pl.BlockSpec((tq, MLA_Q_RANK), lambda b, j: (b * nq + j, cq_blk)),
                pl.BlockSpec((T, LANES), lambda b, j: (b, kpe_blk)),
                pl.BlockSpec((T, MLA_KV_RANK), lambda b, j: (b, 0))]
    args = [proj, proj, ckv]
    if S_ctx:
        in_specs += [pl.BlockSpec((1, S_ctx, MLA_KV_RANK), lambda b, j: (b, 0, 0)),
                     pl.BlockSpec((1, S_ctx, LANES), lambda b, j: (b, 0, 0))]
        args += list(ctx)
    if rope:
        in_specs += [pl.BlockSpec((tq, LANES), lambda b, j: (j, 0))] * 3
        in_specs += [pl.BlockSpec((T, LANES), lambda b, j: (0, 0))] * 3
        args += list(rope_tabs) * 2
    in_specs += [pl.BlockSpec(p.shape, lambda b, j: (0, 0)) for p in params]
    args += list(params)
    out_specs = [pl.BlockSpec((tq, MLA_HEADS * MLA_V), lambda b, j: (b * nq + j, 0))]
    out_shape = [jax.ShapeDtypeStruct((B * T, MLA_HEADS * MLA_V), BF16)]
    if want_ckv:
        out_specs.append(pl.BlockSpec((T, MLA_KV_RANK), lambda b, j: (b, 0)))
        out_shape.append(jax.ShapeDtypeStruct((B * T, MLA_KV_RANK), F32))
    res = pl.pallas_call(
        functools.partial(_mla_body, T=T, S_ctx=S_ctx, rope=rope, want_ckv=want_ckv),
        grid=(B, nq),
        in_specs=in_specs,
        out_specs=out_specs,
        out_shape=out_shape,
        scratch_shapes=[pltpu.VMEM((S, MLA_HEADS * LANES), BF16),
                        pltpu.VMEM((S, MLA_HEADS * MLA_V), BF16),
                        pltpu.VMEM((S, MLA_HEADS * MLA_V), BF16)],
        compiler_params=_cparams(("arbitrary", "arbitrary"), VMEM_LIMIT),
        name="mla",
    )(*args)
    return res if want_ckv else (res[0], None)


def _group_ms(x, lo):
    x2 = x * x
    ms_lo = jnp.sum(jnp.where(lo, x2, 0.0), axis=-1, keepdims=True)
    ms_hi = jnp.sum(jnp.where(lo, 0.0, x2), axis=-1, keepdims=True)
    return jnp.where(lo, ms_lo, ms_hi) * (1.0 / GQA_DH)


def _gqa_body(*refs, T, S_ctx, rope, want_kv):
    q_ref, k_ref, v_ref = refs[:3]
    pos = 3
    if S_ctx:
        kc_ref, vc_ref = refs[pos:pos + 2]
        pos += 2
    if rope:
        rq_refs = refs[pos:pos + 3]
        rk_refs = refs[pos + 3:pos + 6]
        pos += 6
    qg_ref, kg_ref = refs[pos:pos + 2]
    pos += 2
    o_ref = refs[pos]
    pos += 1
    if want_kv:
        kn_ref, vo_ref = refs[pos:pos + 2]
        pos += 2
    klo_sc, khi_sc, vlo_sc, vhi_sc = refs[pos:pos + 4]

    lo = lax.broadcasted_iota(jnp.int32, (1, LANES), 1) < GQA_DH

    def scatter_halves(x, lo_sc, hi_sc, c, rows):
        rolled = pltpu.roll(x, GQA_DH, 1)
        lo_sc[2 * c, rows, :] = jnp.where(lo, x, 0.0).astype(BF16)
        hi_sc[2 * c, rows, :] = jnp.where(lo, 0.0, rolled).astype(BF16)
        lo_sc[2 * c + 1, rows, :] = jnp.where(lo, rolled, 0.0).astype(BF16)
        hi_sc[2 * c + 1, rows, :] = jnp.where(lo, 0.0, x).astype(BF16)

    @pl.when(pl.program_id(1) == 0)
    def _():
        for c in range(GQA_KV_HEADS // 2):
            cols = slice(c * LANES, (c + 1) * LANES)
            if S_ctx:
                scatter_halves(kc_ref[0, :, cols], klo_sc, khi_sc, c, slice(0, S_ctx))
                scatter_halves(vc_ref[0, :, cols], vlo_sc, vhi_sc, c, slice(0, S_ctx))
            kx = k_ref[:, cols]
            kn = kx * lax.rsqrt(_group_ms(kx, lo) + EPS) * kg_ref[...]
            vx = v_ref[:, cols]
            if want_kv:
                kn_ref[:, cols] = kn
                vo_ref[:, cols] = vx
            if rope:
                kn = _rope(kn, rk_refs[0][...], rk_refs[1][...], rk_refs[2][...], GQA_DH // 4)
            scatter_halves(kn, klo_sc, khi_sc, c, slice(S_ctx, S_ctx + T))
            scatter_halves(vx, vlo_sc, vhi_sc, c, slice(S_ctx, S_ctx + T))

    for p in range(GQA_HEADS // 2):
        cols = slice(p * LANES, (p + 1) * LANES)
        g = p // 2
        qx = q_ref[:, cols].astype(F32)
        qn = qx * lax.rsqrt(_group_ms(qx, lo) + EPS) * qg_ref[...]
        if rope:
            qn = _rope(qn, rq_refs[0][...], rq_refs[1][...], rq_refs[2][...], GQA_DH // 4)
        qb = qn.astype(BF16)
        o_a = _softmax_pv(_nt_dot(qb, klo_sc[g]), vlo_sc[g])
        o_b = _softmax_pv(_nt_dot(qb, khi_sc[g]), vhi_sc[g])
        o_ref[:, cols] = (o_a + o_b).astype(o_ref.dtype)


def _gqa(q, k, v, ctx, rope_tabs, params, *, B, T, tq, want_kv):
    nq = T // tq
    S_ctx = ctx[0].shape[1] if ctx is not None else 0
    S = S_ctx + T
    rope = rope_tabs is not None
    n_q = GQA_HEADS * GQA_DH
    n_kv = GQA_KV_HEADS * GQA_DH
    in_specs = [pl.BlockSpec((tq, n_q), lambda b, j: (b * nq + j, 0)),
                pl.BlockSpec((T, n_kv), lambda b, j: (b, 0)),
                pl.BlockSpec((T, n_kv), lambda b, j: (b, 0))]
    args = [q, k, v]
    if S_ctx:
        in_specs += [pl.BlockSpec((1, S_ctx, n_kv), lambda b, j: (b, 0, 0))] * 2
        args += list(ctx)
    if rope:
        in_specs += [pl.BlockSpec((tq, LANES), lambda b, j: (j, 0))] * 3
        in_specs += [pl.BlockSpec((T, LANES), lambda b, j: (0, 0))] * 3
        args += list(rope_tabs) * 2
    in_specs += [pl.BlockSpec(p.shape, lambda b, j: (0, 0)) for p in params]
    args += list(params)
    out_specs = [pl.BlockSpec((tq, n_q), lambda b, j: (b * nq + j, 0))]
    out_shape = [jax.ShapeDtypeStruct((B * T, n_q), BF16)]
    if want_kv:
        out_specs += [pl.BlockSpec((T, n_kv), lambda b, j: (b, 0))] * 2
        out_shape += [jax.ShapeDtypeStruct((B * T, n_kv), F32)] * 2
    res = pl.pallas_call(
        functools.partial(_gqa_body, T=T, S_ctx=S_ctx, rope=rope, want_kv=want_kv),
        grid=(B, nq),
        in_specs=in_specs,
        out_specs=out_specs,
        out_shape=out_shape,
        scratch_shapes=[pltpu.VMEM((GQA_KV_HEADS, S, LANES), BF16)] * 4,
        compiler_params=_cparams(("arbitrary", "arbitrary"), VMEM_LIMIT),
        name="gqa",
    )(*args)
    return res if want_kv else (res[0], None, None)


def _rope_tables(n_tok, d_rot, lead, reps):
    t = jnp.arange(n_tok, dtype=jnp.int32)
    posn = jnp.stack([t // GRID_W, t % GRID_W], axis=-1).astype(F32)
    quarter = d_rot // 4
    inv = jnp.power(ROPE_THETA, -jnp.arange(quarter, dtype=F32) / quarter)
    ang = posn[:, :, None] * inv
    cos, sin = jnp.cos(ang), jnp.sin(ang)
    zero = jnp.zeros_like(sin)
    c_rot = jnp.stack([cos, cos], axis=2).reshape(n_tok, d_rot)
    sp_rot = jnp.stack([-sin, zero], axis=2).reshape(n_tok, d_rot)
    sm_rot = jnp.stack([zero, sin], axis=2).reshape(n_tok, d_rot)
    tail = LANES - lead - reps * d_rot

    def embed(rot, fill):
        parts = [jnp.full((n_tok, lead), fill, F32)] + [rot] * reps + [jnp.full((n_tok, tail), fill, F32)]
        return jnp.concatenate(parts, axis=1)

    return embed(c_rot, 1.0), embed(sp_rot, 0.0), embed(sm_rot, 0.0)


def _pad_lanes(x, lead, width=LANES):
    pad = [(0, 0)] * (x.ndim - 1) + [(lead, width - lead - x.shape[-1])]
    return jnp.pad(x, pad)


def _ab_params(w_in, w_out, a_w2, a_b, out_g, q_norm_g, w_qb, kv_norm_g, w_kvb, qn_g, kn_g):
    d = w_in.shape[0]
    o_alo = 2 * GLA_QK + 2 * GLA_V
    o_cq = o_alo + 2 * GLA_RANK
    o_ckv = o_cq + MLA_Q_RANK
    o_kpe = o_ckv + MLA_KV_RANK
    w_kpe = w_in[:, o_kpe:]
    w_perm = jnp.concatenate([
        w_in[:, :o_alo], w_in[:, o_cq:o_ckv], _pad_lanes(w_kpe, MLA_NOPE),
        w_in[:, o_ckv:o_kpe],
        _pad_lanes(jnp.concatenate([w_in[:, o_alo:o_cq], w_kpe], axis=1), 0),
    ], axis=1).astype(BF16)
    w2bd = jnp.zeros((LANES, 2 * GLA_QK), F32)
    w2bd = w2bd.at[:GLA_RANK, :GLA_QK].set(a_w2[0]).at[GLA_RANK:2 * GLA_RANK, GLA_QK:].set(a_w2[1])
    w_qb_p = _pad_lanes(w_qb.reshape(MLA_Q_RANK, MLA_HEADS, MLA_QK), 0).reshape(MLA_Q_RANK, MLA_HEADS * LANES)
    kvb = w_kvb.reshape(MLA_KV_RANK, MLA_HEADS, MLA_NOPE + MLA_V)
    w_kvb_p = jnp.concatenate([
        _pad_lanes(kvb[:, :, :MLA_NOPE], 0).reshape(MLA_KV_RANK, MLA_HEADS * LANES),
        kvb[:, :, MLA_NOPE:].reshape(MLA_KV_RANK, MLA_HEADS * MLA_V)], axis=1)
    return dict(
        w_perm=w_perm,
        w2bd=w2bd.astype(BF16),
        a_b=a_b.reshape(1, 2 * GLA_QK),
        out_g=jnp.tile(out_g, GLA_HEADS).reshape(1, GLA_V),
        mla=(q_norm_g.reshape(1, -1), w_qb_p.astype(BF16),
             _pad_lanes(qn_g * (MLA_QK ** -0.5), 0).reshape(1, LANES),
             kv_norm_g.reshape(1, -1), w_kvb_p.astype(BF16),
             _pad_lanes(kn_g, 0).reshape(1, LANES)),
        w_out_gla=w_out[:GLA_V].astype(BF16),
        w_out_mla=w_out[GLA_V:].astype(BF16),
    )


AB_OUTS = ((2 * GLA_QK + 2 * GLA_V + MLA_Q_RANK + LANES, BF16), (MLA_KV_RANK, F32), (LANES, F32))
C_OUTS = ((GQA_HEADS * GQA_DH, BF16), (GQA_KV_HEADS * GQA_DH, F32), (GQA_KV_HEADS * GQA_DH, F32))


def kernel(x_prompt, x_sample, c, cache_mla_ckv, cache_mla_kpe, state_gla, cache_gqa_k, cache_gqa_v,
           c_ctx, ada_w, ada_b, norm_mix_g, norm_ffn_g, ffn_w_in, ffn_w_out, ab_w_in, ab_w_out,
           gla_a_w2, gla_a_b, gla_out_g, mla_q_norm_g, mla_w_qb, mla_kv_norm_g, mla_w_kvb, mla_qn_g,
           mla_kn_g, gqa_w_in, gqa_w_out, gqa_qn_g, gqa_kn_g):
    Bp, Tp, D = x_prompt.shape
    Bs, Ts, _ = x_sample.shape
    depth = ada_w.shape[0]
    xp = x_prompt.reshape(Bp * Tp, D)
    xs = x_sample.reshape(Bs * Ts, D)

    cond = jnp.zeros((8, D), F32).at[0].set(c_ctx).at[1:1 + Bs].set(c)
    mods = _adaln(cond, ada_w, ada_b).reshape(depth, 8, 6, D)

    rope_mla = _rope_tables(Ts, MLA_ROPE, MLA_NOPE, 1)
    rope_gqa = _rope_tables(Ts, GQA_DH, 0, 2)

    streams = (dict(mod_base=0, rows_per_cond=None, tm=512),
               dict(mod_base=1, rows_per_cond=Ts, tm=512))
    ffn_tm = (2048, 1024)

    new_ckv, new_kpe, new_gla, new_k, new_v = [], [], [], [], []
    for l in range(depth):
        i = l // 2
        mod = mods[l]
        if l % 2 == 0:
            P = _ab_params(ab_w_in[i], ab_w_out[i], gla_a_w2[i], gla_a_b[i], gla_out_g[i], mla_q_norm_g[i],
                           mla_w_qb[i], mla_kv_norm_g[i], mla_w_kvb[i], mla_qn_g[i], mla_kn_g[i])
            pj_p, ckv_p, alk_p = _inproj(xp, norm_mix_g[l], mod, P['w_perm'], AB_OUTS, **streams[0])
            pj_s, ckv_s, alk_s = _inproj(xs, norm_mix_g[l], mod, P['w_perm'], AB_OUTS, **streams[1])
            og_p, st_p = _gla(pj_p, alk_p, P['w2bd'], P['a_b'], P['out_g'], None, B=Bp, T=Tp, want_state=True)
            og_s, _ = _gla(pj_s, alk_s, P['w2bd'], P['a_b'], P['out_g'], state_gla[:, i], B=Bs, T=Ts,
                           want_state=False)
            om_p, ckvn_p = _mla(pj_p, ckv_p, None, None, P['mla'], B=Bp, T=Tp, tq=Tp, want_ckv=True)
            ctx = (cache_mla_ckv[:, i], _pad_lanes(cache_mla_kpe[:, i], MLA_NOPE))
            om_s, _ = _mla(pj_s, ckv_s, ctx, rope_mla, P['mla'], B=Bs, T=Ts, tq=256, want_ckv=False)
            ws = (P['w_out_gla'], P['w_out_mla'])
            xp = _outproj((og_p, om_p), ws, xp, mod, **streams[0])
            xs = _outproj((og_s, om_s), ws, xs, mod, **streams[1])
            new_ckv.append(ckvn_p.reshape(Bp, Tp, MLA_KV_RANK))
            new_kpe.append(alk_p[:, 2 * GLA_RANK:2 * GLA_RANK + MLA_ROPE].reshape(Bp, Tp, MLA_ROPE))
            new_gla.append(st_p)
        else:
            w_in = gqa_w_in[i].astype(BF16)
            w_out = gqa_w_out[i].astype(BF16)
            gp = (jnp.tile(gqa_qn_g[i] * (GQA_DH ** -0.5), 2).reshape(1, LANES),
                  jnp.tile(gqa_kn_g[i], 2).reshape(1, LANES))
            q_p, k_p, v_p = _inproj(xp, norm_mix_g[l], mod, w_in, C_OUTS, **streams[0])
            q_s, k_s, v_s = _inproj(xs, norm_mix_g[l], mod, w_in, C_OUTS, **streams[1])
            o_p, kn_p, vo_p = _gqa(q_p, k_p, v_p, None, None, gp, B=Bp, T=Tp, tq=Tp, want_kv=True)
            n_kv = GQA_KV_HEADS * GQA_DH
            ctx = (cache_gqa_k[:, i].reshape(Bs, -1, n_kv), cache_gqa_v[:, i].reshape(Bs, -1, n_kv))
            o_s, _, _ = _gqa(q_s, k_s, v_s, ctx, rope_gqa, gp, B=Bs, T=Ts, tq=256, want_kv=False)
            xp = _outproj((o_p,), (w_out,), xp, mod, **streams[0])
            xs = _outproj((o_s,), (w_out,), xs, mod, **streams[1])
            new_k.append(kn_p.reshape(Bp, Tp, GQA_KV_HEADS, GQA_DH))
            new_v.append(vo_p.reshape(Bp, Tp, GQA_KV_HEADS, GQA_DH))
        xp = _ffn(xp, norm_ffn_g[l], mod, ffn_w_in, ffn_w_out, l, tm=ffn_tm[0],
                  **{k: v for k, v in streams[0].items() if k != 'tm'})
        xs = _ffn(xs, norm_ffn_g[l], mod, ffn_w_in, ffn_w_out, l, tm=ffn_tm[1],
                  **{k: v for k, v in streams[1].items() if k != 'tm'})
    return (xp.reshape(Bp, Tp, D), xs.reshape(Bs, Ts, D), jnp.stack(new_ckv, axis=1),
            jnp.stack(new_kpe, axis=1), jnp.stack(new_gla, axis=1), jnp.stack(new_k, axis=1),
            jnp.stack(new_v, axis=1))
```

```python
import functools

import jax
import jax.numpy as jnp
from jax import lax
from jax.experimental import pallas as pl
from jax.experimental.pallas import tpu as pltpu

F32 = jnp.float32
BF16 = jnp.bfloat16

EPS = 1e-6
ROPE_THETA = 10000.0
GRID_W = 64
LANES = 128
GLA_HEADS, GLA_DK, GLA_DV = 4, 64, 128
GLA_QK = GLA_HEADS * GLA_DK
GLA_V = GLA_HEADS * GLA_DV
GLA_RANK = 16
GLA_TAU = 16.0
GLA_CHUNK = 64
MLA_HEADS = 8
MLA_Q_RANK, MLA_KV_RANK = 384, 256
MLA_NOPE, MLA_ROPE, MLA_V = 64, 32, 64
MLA_QK = MLA_NOPE + MLA_ROPE
GQA_HEADS, GQA_KV_HEADS, GQA_DH = 16, 4, 64
VMEM_LIMIT = 56 << 20


def _cparams(sem, vmem=None):
    return pltpu.CompilerParams(dimension_semantics=sem, vmem_limit_bytes=vmem)


def _nt_dot(a, b):
    return lax.dot_general(a, b, (((1,), (1,)), ((), ())), preferred_element_type=F32)


def _tn_dot(a, b):
    return lax.dot_general(a, b, (((0,), (0,)), ((), ())), preferred_element_type=F32)


def _ds(start, size):
    if isinstance(start, int):
        return pl.ds(start, size)
    return pl.ds(pl.multiple_of(start, size), size)


def _rms(x, g):
    ms = jnp.mean(x * x, axis=-1, keepdims=True)
    return x * lax.rsqrt(ms + EPS) * g


def _silu(x):
    return x * jax.nn.sigmoid(x)


def _rope(x, c, sp, sm, shift):
    return x * c + pltpu.roll(x, LANES - shift, 1) * sp + pltpu.roll(x, shift, 1) * sm


def _adaln_body(c_ref, w_ref, b_ref, o_ref):
    c = c_ref[...]
    s = _silu(c).astype(BF16)
    o_ref[0] = jnp.dot(s, w_ref[0].astype(BF16), preferred_element_type=F32) + b_ref[0]


def _adaln(cond, ada_w, ada_b):
    L, D, E = ada_w.shape
    tn = 1536
    return pl.pallas_call(
        _adaln_body,
        grid=(L, E // tn),
        in_specs=[pl.BlockSpec((8, D), lambda l, n: (0, 0)),
                  pl.BlockSpec((1, D, tn), lambda l, n: (l, 0, n)),
                  pl.BlockSpec((1, 1, tn), lambda l, n: (l, 0, n))],
        out_specs=pl.BlockSpec((1, 8, tn), lambda l, n: (l, 0, n)),
        out_shape=jax.ShapeDtypeStruct((L, 8, E), F32),
        compiler_params=_cparams(("arbitrary", "arbitrary")),
        name="adaln",
    )(cond, ada_w, ada_b.reshape(L, 1, E))


def _mod_index(mod_base, rows_per_cond, tm):
    if rows_per_cond is None:
        return lambda i: (mod_base, 0, 0)
    n_mod = max(tm // rows_per_cond, 1)
    return lambda i: ((mod_base + (i * tm) // rows_per_cond) // n_mod, 0, 0)


def _inproj_body(x_ref, g_ref, mod_ref, w_ref, *o_refs, widths):
    h = _rms(x_ref[...], g_ref[...])
    h = h * (1.0 + mod_ref[0, 1:2, :]) + mod_ref[0, 0:1, :]
    acc = jnp.dot(h.astype(BF16), w_ref[...], preferred_element_type=F32)
    off = 0
    for o_ref, w in zip(o_refs, widths):
        o_ref[...] = acc[:, off:off + w].astype(o_ref.dtype)
        off += w


def _inproj(x, g, mod, w, outs, *, mod_base, rows_per_cond, tm):
    n, d = x.shape
    widths = tuple(o[0] for o in outs)
    return pl.pallas_call(
        functools.partial(_inproj_body, widths=widths),
        grid=(n // tm,),
        in_specs=[pl.BlockSpec((tm, d), lambda i: (i, 0)),
                  pl.BlockSpec((1, d), lambda i: (0, 0)),
                  pl.BlockSpec((1, 6, d), _mod_index(mod_base, rows_per_cond, tm)),
                  pl.BlockSpec(w.shape, lambda i: (0, 0))],
        out_specs=[pl.BlockSpec((tm, wd), lambda i: (i, 0)) for wd in widths],
        out_shape=[jax.ShapeDtypeStruct((n, wd), dt) for wd, dt in outs],
        compiler_params=_cparams(("arbitrary",), VMEM_LIMIT),
        name="inproj",
    )(x, g.reshape(1, d), mod, w)


def _outproj_body(*refs, n_in):
    a_refs, w_refs = refs[:n_in], refs[n_in:2 * n_in]
    x_ref, mod_ref, o_ref = refs[2 * n_in:]
    acc = None
    for a_ref, w_ref in zip(a_refs, w_refs):
        part = jnp.dot(a_ref[...], w_ref[...], preferred_element_type=F32)
        acc = part if acc is None else acc + part
    o_ref[...] = x_ref[...] + mod_ref[0, 2:3, :] * acc


def _outproj(acts, ws, x, mod, *, mod_base, rows_per_cond, tm):
    n, d = x.shape
    n_in = len(acts)
    return pl.pallas_call(
        functools.partial(_outproj_body, n_in=n_in),
        grid=(n // tm,),
        in_specs=([pl.BlockSpec((tm, a.shape[1]), lambda i: (i, 0)) for a in acts]
                  + [pl.BlockSpec(w.shape, lambda i: (0, 0)) for w in ws]
                  + [pl.BlockSpec((tm, d), lambda i: (i, 0)),
                     pl.BlockSpec((1, 6, d), _mod_index(mod_base, rows_per_cond, tm))]),
        out_specs=pl.BlockSpec((tm, d), lambda i: (i, 0)),
        out_shape=jax.ShapeDtypeStruct((n, d), F32),
        compiler_params=_cparams(("arbitrary",), VMEM_LIMIT),
        name="outproj",
    )(*acts, *ws, x, mod)


def _ffn_body(x_ref, g_ref, mod_ref, wg_ref, wu_ref, wo_ref, o_ref, h_ref, wgu_ref, wob_ref, *,
              nk, row_chunk, rows_per_cond):
    k = pl.program_id(1)
    tm = x_ref.shape[0]
    tk = wg_ref.shape[2]
    n_chunks = tm // row_chunk

    def chunk_rows(c):
        return pl.ds(pl.multiple_of(c * row_chunk, row_chunk), row_chunk)

    def mod_row(c):
        return 0 if rows_per_cond is None else (c * row_chunk) // rows_per_cond

    @pl.when(k == 0)
    def _():
        def norm_rows(c, carry):
            rows, m = chunk_rows(c), mod_row(c)
            h = _rms(x_ref[rows, :], g_ref[...])
            h = h * (1.0 + mod_ref[m, 4:5, :]) + mod_ref[m, 3:4, :]
            h_ref[rows, :] = h.astype(BF16)
            o_ref[rows, :] = jnp.zeros((row_chunk, o_ref.shape[1]), F32)
            return carry
        lax.fori_loop(0, n_chunks, norm_rows, 0)

    wgu_ref[:, :tk] = wg_ref[0].astype(BF16)
    wgu_ref[:, tk:] = wu_ref[0].astype(BF16)
    wob_ref[...] = wo_ref[0].astype(BF16)

    def ffn_rows(c, carry):
        rows = chunk_rows(c)
        gu = jnp.dot(h_ref[rows, :], wgu_ref[...], preferred_element_type=F32)
        a = (_silu(gu[:, :tk]) * gu[:, tk:]).astype(BF16)
        o_ref[rows, :] += jnp.dot(a, wob_ref[...], preferred_element_type=F32)
        return carry
    lax.fori_loop(0, n_chunks, ffn_rows, 0)

    @pl.when(k == nk - 1)
    def _():
        def residual_rows(c, carry):
            rows, m = chunk_rows(c), mod_row(c)
            o_ref[rows, :] = x_ref[rows, :] + mod_ref[m, 5:6, :] * o_ref[rows, :]
            return carry
        lax.fori_loop(0, n_chunks, residual_rows, 0)


def _ffn(x, g, mod, w_in, w_out, layer, *, mod_base, rows_per_cond, tm):
    n, d = x.shape
    hidden = w_out.shape[1]
    tk = 256
    nk = hidden // tk
    n_mod = 1 if rows_per_cond is None else max(tm // rows_per_cond, 1)
    return pl.pallas_call(
        functools.partial(_ffn_body, nk=nk, row_chunk=512, rows_per_cond=rows_per_cond),
        grid=(n // tm, nk),
        in_specs=[pl.BlockSpec((tm, d), lambda i, k: (i, 0)),
                  pl.BlockSpec((1, d), lambda i, k: (0, 0)),
                  pl.BlockSpec((n_mod, 6, d),
                               (lambda f: (lambda i, k: f(i)))(_mod_index(mod_base, rows_per_cond, tm))),
                  pl.BlockSpec((1, d, tk), lambda i, k: (layer, 0, k)),
                  pl.BlockSpec((1, d, tk), lambda i, k: (layer, 0, nk + k)),
                  pl.BlockSpec((1, tk, d), lambda i, k: (layer, k, 0))],
        out_specs=pl.BlockSpec((tm, d), lambda i, k: (i, 0)),
        out_shape=jax.ShapeDtypeStruct((n, d), F32),
        scratch_shapes=[pltpu.VMEM((tm, d), BF16),
                        pltpu.VMEM((d, 2 * tk), BF16),
                        pltpu.VMEM((tk, d), BF16)],
        compiler_params=_cparams(("arbitrary", "arbitrary"), VMEM_LIMIT),
        name="ffn",
    )(x, g.reshape(1, d), mod, w_in, w_in, w_out)


def _log_sigmoid(x):
    return jnp.minimum(x, 0.0) - jnp.log1p(jnp.exp(-jnp.abs(x)))


def _gla_body(*refs, T, has_s0, want_state):
    qkvr_ref, alk_ref, w2_ref, ab_ref, og_ref = refs[:5]
    pos = 5
    s0_ref = None
    if has_s0:
        s0_ref = refs[pos]
        pos += 1
    o_ref = refs[pos]
    pos += 1
    st_out_ref = None
    if want_state:
        st_out_ref = refs[pos]
        pos += 1
    osc_ref, st_ref, la_ref, qin_ref, kst_ref, dec_ref = refs[pos:pos + 6]

    C = GLA_CHUNK
    nc = T // C

    logit = jnp.dot(alk_ref[...].astype(BF16), w2_ref[...], preferred_element_type=F32) + ab_ref[...]
    la_ref[...] = _log_sigmoid(logit) * (1.0 / GLA_TAU)

    for d in range(2):
        if has_s0:
            st_ref[d] = jnp.concatenate([s0_ref[0, d, h] for h in range(GLA_HEADS)], axis=0).T
        else:
            st_ref[d] = jnp.zeros((GLA_DV, GLA_QK), F32)

    r64 = lax.broadcasted_iota(jnp.int32, (C, C), 0)
    c64 = lax.broadcasted_iota(jnp.int32, (C, C), 1)
    tri_f = (r64 >= c64).astype(BF16)
    tri_b = (c64 >= r64).astype(BF16)
    t_idx = lax.broadcasted_iota(jnp.int32, (C, GLA_QK), 0)
    s_idx = lax.broadcasted_iota(jnp.int32, (C, GLA_QK), 1) % C
    causal_f = t_idx >= s_idx
    causal_b = t_idx <= s_idx
    bm_k = (lax.broadcasted_iota(jnp.int32, (GLA_QK, GLA_QK), 0) // GLA_DK
            == lax.broadcasted_iota(jnp.int32, (GLA_QK, GLA_QK), 1) // GLA_DK)
    bm_v = (lax.broadcasted_iota(jnp.int32, (GLA_QK, GLA_V), 0) // C
            == lax.broadcasted_iota(jnp.int32, (GLA_QK, GLA_V), 1) // GLA_DV)
    head_lanes = [lax.broadcasted_iota(jnp.int32, (1, GLA_QK), 1) // GLA_DK == h for h in range(GLA_HEADS)]
    directions = ((tri_f, causal_f, C // 2 - 1, C - 1), (tri_b, causal_b, C // 2, 0))

    def v_rows(rows):
        return qkvr_ref[rows, 2 * GLA_QK:2 * GLA_QK + GLA_V]

    def intra(n, d):
        tri, causal, ref_row, last_row = directions[d]
        rows = _ds(n * C, C)
        qc = qkvr_ref[rows, 0:GLA_QK].astype(F32) * (GLA_DK ** -0.5)
        kc = qkvr_ref[rows, GLA_QK:2 * GLA_QK].astype(F32)
        lac = la_ref[rows, d * GLA_QK:(d + 1) * GLA_QK]
        la_hi = lac.astype(BF16)
        la_lo = (lac - la_hi.astype(F32)).astype(BF16)
        b = (jnp.dot(tri, la_hi, preferred_element_type=F32)
             + jnp.dot(tri, la_lo, preferred_element_type=F32))
        b_ref = b[ref_row:ref_row + 1, :]
        b_last = b[last_row:last_row + 1, :]
        q_loc = (qc * jnp.exp(b - b_ref)).astype(BF16)
        k_loc = kc * jnp.exp(b_ref - b)
        qin_ref[d, rows, :] = (qc * jnp.exp(b)).astype(BF16)
        kst_ref[d, rows, :] = (kc * jnp.exp(b_last - b)).astype(BF16)
        dec_ref[d, _ds(n * 8, 8), :] = jnp.broadcast_to(jnp.exp(b_last), (8, GLA_QK))
        k_bd = jnp.where(bm_k, jnp.concatenate([k_loc] * GLA_HEADS, axis=0), 0.0).astype(BF16)
        a = jnp.where(causal, _nt_dot(q_loc, k_bd), 0.0).astype(BF16)
        v_bd = jnp.where(bm_v, jnp.concatenate([v_rows(rows)] * GLA_HEADS, axis=0), jnp.zeros((), BF16))
        return jnp.dot(a, v_bd, preferred_element_type=F32)

    def intra_step(n):
        osc_ref[_ds(n * C, C), :] = intra(n, 0) + intra(n, 1)

    def inter(n, d):
        rows = _ds(n * C, C)
        st = st_ref[d]
        q_in = qin_ref[d, rows, :]
        q_heads = jnp.concatenate([jnp.where(m, q_in, jnp.zeros((), BF16)) for m in head_lanes], axis=0)
        o = _nt_dot(q_heads, st.astype(BF16))
        for h in range(GLA_HEADS):
            osc_ref[rows, h * GLA_DV:(h + 1) * GLA_DV] += o[h * C:(h + 1) * C, :]
        upd = _tn_dot(v_rows(rows), kst_ref[d, rows, :])
        upd_c = None
        for h, m in enumerate(head_lanes):
            part = jnp.where(m, upd[h * GLA_DV:(h + 1) * GLA_DV, :], 0.0)
            upd_c = part if upd_c is None else upd_c + part
        st_ref[d] = st * dec_ref[d, pl.ds(n * 8, 1), :] + upd_c

    def inter_step(i):
        inter(i, 0)
        inter(nc - 1 - i, 1)

    if nc <= 4:
        for n in range(nc):
            intra_step(n)
        for i in range(nc):
            inter_step(i)
    else:
        def run(step_fn):
            def body(i, carry):
                step_fn(i)
                return carry
            lax.fori_loop(0, nc, body, 0)
        run(intra_step)
        run(inter_step)

    rb = 256
    for i in range(T // rb):
        rows = slice(i * rb, (i + 1) * rb)
        for h in range(GLA_HEADS):
            cols = slice(h * GLA_DV, (h + 1) * GLA_DV)
            o = _rms(osc_ref[rows, cols], og_ref[:, cols])
            r = qkvr_ref[rows, 2 * GLA_QK + GLA_V + h * GLA_DV:2 * GLA_QK + GLA_V + (h + 1) * GLA_DV].astype(F32)
            o_ref[rows, cols] = (o * _silu(r)).astype(o_ref.dtype)

    if want_state:
        for d in range(2):
            s_all = st_ref[d].T
            for h in range(GLA_HEADS):
                st_out_ref[0, d, h] = s_all[h * GLA_DK:(h + 1) * GLA_DK, :]


def _gla(proj, alk, w2bd, a_b, out_g, s0, *, B, T, want_state):
    has_s0 = s0 is not None
    n_qkvr = 2 * GLA_QK + 2 * GLA_V
    in_specs = [pl.BlockSpec((T, n_qkvr), lambda b: (b, 0)),
                pl.BlockSpec((T, LANES), lambda b: (b, 0)),
                pl.BlockSpec(w2bd.shape, lambda b: (0, 0)),
                pl.BlockSpec((1, 2 * GLA_QK), lambda b: (0, 0)),
                pl.BlockSpec((1, GLA_V), lambda b: (0, 0))]
    args = [proj, alk, w2bd, a_b, out_g]
    st_spec = pl.BlockSpec((1, 2, GLA_HEADS, GLA_DK, GLA_DV), lambda b: (b, 0, 0, 0, 0))
    if has_s0:
        in_specs.append(st_spec)
        args.append(s0)
    out_specs = [pl.BlockSpec((T, GLA_V), lambda b: (b, 0))]
    out_shape = [jax.ShapeDtypeStruct((B * T, GLA_V), BF16)]
    if want_state:
        out_specs.append(st_spec)
        out_shape.append(jax.ShapeDtypeStruct((B, 2, GLA_HEADS, GLA_DK, GLA_DV), F32))
    res = pl.pallas_call(
        functools.partial(_gla_body, T=T, has_s0=has_s0, want_state=want_state),
        grid=(B,),
        in_specs=in_specs,
        out_specs=out_specs,
        out_shape=out_shape,
        scratch_shapes=[pltpu.VMEM((T, GLA_V), F32),
                        pltpu.VMEM((2, GLA_DV, GLA_QK), F32),
                        pltpu.VMEM((T, 2 * GLA_QK), F32),
                        pltpu.VMEM((2, T, GLA_QK), BF16),
                        pltpu.VMEM((2, T, GLA_QK), BF16),
                        pltpu.VMEM((2, 8 * (T // GLA_CHUNK), GLA_QK), F32)],
        compiler_params=_cparams(("arbitrary",), VMEM_LIMIT),
        name="gla",
    )(*args)
    return res if want_state else (res[0], None)


def _softmax_pv(s, v):
    m = jnp.max(s, axis=-1, keepdims=True)
    p = jnp.exp(s - m)
    l = jnp.sum(p, axis=-1, keepdims=True)
    return jnp.dot(p.astype(BF16), v, preferred_element_type=F32) / l


def _mla_body(*refs, T, S_ctx, rope, want_ckv):
    cq_ref, kpe_ref, ckv_ref = refs[:3]
    pos = 3
    if S_ctx:
        ckvc_ref, kpec_ref = refs[pos:pos + 2]
        pos += 2
    if rope:
        rq_refs = refs[pos:pos + 3]
        rk_refs = refs[pos + 3:pos + 6]
        pos += 6
    qng_ref, wqb_ref, qg_ref, kvg_ref, wkvb_ref, kg_ref = refs[pos:pos + 6]
    pos += 6
    o_ref = refs[pos]
    pos += 1
    if want_ckv:
        ckvn_ref = refs[pos]
        pos += 1
    k_sc, vlo_sc, vhi_sc = refs[pos:pos + 3]

    H = MLA_HEADS
    n_k = H * LANES
    lo = (lax.broadcasted_iota(jnp.int32, (1, H * MLA_V), 1) % LANES) < MLA_V

    def fill_kv(ckvn, kpe, row0, n_rows, with_rope):
        rows = slice(row0, row0 + n_rows)
        kv = jnp.dot(ckvn.astype(BF16), wkvb_ref[...], preferred_element_type=F32)
        for h in range(H):
            kh = kv[:, h * LANES:(h + 1) * LANES] + kpe
            ms = jnp.sum(kh * kh, axis=-1, keepdims=True) * (1.0 / MLA_QK)
            kh = kh * lax.rsqrt(ms + EPS) * kg_ref[...]
            if with_rope:
                kh = _rope(kh, rk_refs[0][...], rk_refs[1][...], rk_refs[2][...], MLA_ROPE // 4)
            k_sc[rows, h * LANES:(h + 1) * LANES] = kh.astype(BF16)
        v = kv[:, n_k:]
        vlo_sc[rows, :] = jnp.where(lo, v, 0.0).astype(BF16)
        vhi_sc[rows, :] = jnp.where(lo, 0.0, v).astype(BF16)

    @pl.when(pl.program_id(1) == 0)
    def _():
        if S_ctx:
            fill_kv(ckvc_ref[0], kpec_ref[0], 0, S_ctx, False)
        ckvn = _rms(ckv_ref[...], kvg_ref[...])
        if want_ckv:
            ckvn_ref[...] = ckvn
        fill_kv(ckvn, kpe_ref[...].astype(F32), S_ctx, T, rope)

    cqn = _rms(cq_ref[...].astype(F32), qng_ref[...])
    q = jnp.dot(cqn.astype(BF16), wqb_ref[...], preferred_element_type=F32)
    for p in range(H // 2):
        o_pair = None
        for h, v_sc in ((2 * p, vlo_sc), (2 * p + 1, vhi_sc)):
            qh = q[:, h * LANES:(h + 1) * LANES]
            ms = jnp.sum(qh * qh, axis=-1, keepdims=True) * (1.0 / MLA_QK)
            qh = qh * lax.rsqrt(ms + EPS) * qg_ref[...]
            if rope:
                qh = _rope(qh, rq_refs[0][...], rq_refs[1][...], rq_refs[2][...], MLA_ROPE // 4)
            s = _nt_dot(qh.astype(BF16), k_sc[:, h * LANES:(h + 1) * LANES])
            o_h = _softmax_pv(s, v_sc[:, p * LANES:(p + 1) * LANES])
            o_pair = o_h if o_pair is None else o_pair + o_h
        o_ref[:, p * LANES:(p + 1) * LANES] = o_pair.astype(o_ref.dtype)


def _mla(proj, ckv, ctx, rope_tabs, params, *, B, T, tq, want_ckv):
    nq = T // tq
    S_ctx = ctx[0].shape[1] if ctx is not None else 0
    S = S_ctx + T
    rope = rope_tabs is not None
    cq_blk = (2 * GLA_QK + 2 * GLA_V) // MLA_Q_RANK
    kpe_blk = (2 * GLA_QK + 2 * GLA_V + MLA_Q_RANK) // LANES
    in_specs = [pl.BlockSpec((tq, MLA_Q_RANK), lambda b, j: (b * nq + j, cq_blk)),
                pl.BlockSpec((T, LANES), lambda b, j: (b, kpe_blk)),
                pl.BlockSpec((T, MLA_KV_RANK), lambda b, j: (b, 0))]
    args = [proj, proj, ckv]
    if S_ctx:
        in_specs += [pl.BlockSpec((1, S_ctx, MLA_KV_RANK), lambda b, j: (b, 0, 0)),
                     pl.BlockSpec((1, S_ctx, LANES), lambda b, j: (b, 0, 0))]
        args += list(ctx)
    if rope:
        in_specs += [pl.BlockSpec((tq, LANES), lambda b, j: (j, 0))] * 3
        in_specs += [pl.BlockSpec((T, LANES), lambda b, j: (0, 0))] * 3
        args += list(rope_tabs) * 2
    in_specs += [pl.BlockSpec(p.shape, lambda b, j: (0, 0)) for p in params]
    args += list(params)
    out_specs = [pl.BlockSpec((tq, MLA_HEADS * MLA_V), lambda b, j: (b * nq + j, 0))]
    out_shape = [jax.ShapeDtypeStruct((B * T, MLA_HEADS * MLA_V), BF16)]
    if want_ckv:
        out_specs.append(pl.BlockSpec((T, MLA_KV_RANK), lambda b, j: (b, 0)))
        out_shape.append(jax.ShapeDtypeStruct((B * T, MLA_KV_RANK), F32))
    res = pl.pallas_call(
        functools.partial(_mla_body, T=T, S_ctx=S_ctx, rope=rope, want_ckv=want_ckv),
        grid=(B, nq),
        in_specs=in_specs,
        out_specs=out_specs,
        out_shape=out_shape,
        scratch_shapes=[pltpu.VMEM((S, MLA_HEADS * LANES), BF16),
                        pltpu.VMEM((S, MLA_HEADS * MLA_V), BF16),
                        pltpu.VMEM((S, MLA_HEADS * MLA_V), BF16)],
        compiler_params=_cparams(("arbitrary", "arbitrary"), VMEM_LIMIT),
        name="mla",
    )(*args)
    return res if want_ckv else (res[0], None)


def _group_ms(x, lo):
    x2 = x * x
    ms_lo = jnp.sum(jnp.where(lo, x2, 0.0), axis=-1, keepdims=True)
    ms_hi = jnp.sum(jnp.where(lo, 0.0, x2), axis=-1, keepdims=True)
    return jnp.where(lo, ms_lo, ms_hi) * (1.0 / GQA_DH)


def _gqa_body(*refs, T, S_ctx, rope, want_kv):
    q_ref, k_ref, v_ref = refs[:3]
    pos = 3
    if S_ctx:
        kc_ref, vc_ref = refs[pos:pos + 2]
        pos += 2
    if rope:
        rq_refs = refs[pos:pos + 3]
        rk_refs = refs[pos + 3:pos + 6]
        pos += 6
    qg_ref, kg_ref = refs[pos:pos + 2]
    pos += 2
    o_ref = refs[pos]
    pos += 1
    if want_kv:
        kn_ref, vo_ref = refs[pos:pos + 2]
        pos += 2
    klo_sc, khi_sc, vlo_sc, vhi_sc = refs[pos:pos + 4]

    lo = lax.broadcasted_iota(jnp.int32, (1, LANES), 1) < GQA_DH

    def scatter_halves(x, lo_sc, hi_sc, c, rows):
        rolled = pltpu.roll(x, GQA_DH, 1)
        lo_sc[2 * c, rows, :] = jnp.where(lo, x, 0.0).astype(BF16)
        hi_sc[2 * c, rows, :] = jnp.where(lo, 0.0, rolled).astype(BF16)
        lo_sc[2 * c + 1, rows, :] = jnp.where(lo, rolled, 0.0).astype(BF16)
        hi_sc[2 * c + 1, rows, :] = jnp.where(lo, 0.0, x).astype(BF16)

    @pl.when(pl.program_id(1) == 0)
    def _():
        for c in range(GQA_KV_HEADS // 2):
            cols = slice(c * LANES, (c + 1) * LANES)
            if S_ctx:
                scatter_halves(kc_ref[0, :, cols], klo_sc, khi_sc, c, slice(0, S_ctx))
                scatter_halves(vc_ref[0, :, cols], vlo_sc, vhi_sc, c, slice(0, S_ctx))
            kx = k_ref[:, cols]
            kn = kx * lax.rsqrt(_group_ms(kx, lo) + EPS) * kg_ref[...]
            vx = v_ref[:, cols]
            if want_kv:
                kn_ref[:, cols] = kn
                vo_ref[:, cols] = vx
            if rope:
                kn = _rope(kn, rk_refs[0][...], rk_refs[1][...], rk_refs[2][...], GQA_DH // 4)
            scatter_halves(kn, klo_sc, khi_sc, c, slice(S_ctx, S_ctx + T))
            scatter_halves(vx, vlo_sc, vhi_sc, c, slice(S_ctx, S_ctx + T))

    for p in range(GQA_HEADS // 2):
        cols = slice(p * LANES, (p + 1) * LANES)
        g = p // 2
        qx = q_ref[:, cols].astype(F32)
        qn = qx * lax.rsqrt(_group_ms(qx, lo) + EPS) * qg_ref[...]
        if rope:
            qn = _rope(qn, rq_refs[0][...], rq_refs[1][...], rq_refs[2][...], GQA_DH // 4)
        qb = qn.astype(BF16)
        o_a = _softmax_pv(_nt_dot(qb, klo_sc[g]), vlo_sc[g])
        o_b = _softmax_pv(_nt_dot(qb, khi_sc[g]), vhi_sc[g])
        o_ref[:, cols] = (o_a + o_b).astype(o_ref.dtype)


def _gqa(q, k, v, ctx, rope_tabs, params, *, B, T, tq, want_kv):
    nq = T // tq
    S_ctx = ctx[0].shape[1] if ctx is not None else 0
    S = S_ctx + T
    rope = rope_tabs is not None
    n_q = GQA_HEADS * GQA_DH
    n_kv = GQA_KV_HEADS * GQA_DH
    in_specs = [pl.BlockSpec((tq, n_q), lambda b, j: (b * nq + j, 0)),
                pl.BlockSpec((T, n_kv), lambda b, j: (b, 0)),
                pl.BlockSpec((T, n_kv), lambda b, j: (b, 0))]
    args = [q, k, v]
    if S_ctx:
        in_specs += [pl.BlockSpec((1, S_ctx, n_kv), lambda b, j: (b, 0, 0))] * 2
        args += list(ctx)
    if rope:
        in_specs += [pl.BlockSpec((tq, LANES), lambda b, j: (j, 0))] * 3
        in_specs += [pl.BlockSpec((T, LANES), lambda b, j: (0, 0))] * 3
        args += list(rope_tabs) * 2
    in_specs += [pl.BlockSpec(p.shape, lambda b, j: (0, 0)) for p in params]
    args += list(params)
    out_specs = [pl.BlockSpec((tq, n_q), lambda b, j: (b * nq + j, 0))]
    out_shape = [jax.ShapeDtypeStruct((B * T, n_q), BF16)]
    if want_kv:
        out_specs += [pl.BlockSpec((T, n_kv), lambda b, j: (b, 0))] * 2
        out_shape += [jax.ShapeDtypeStruct((B * T, n_kv), F32)] * 2
    res = pl.pallas_call(
        functools.partial(_gqa_body, T=T, S_ctx=S_ctx, rope=rope, want_kv=want_kv),
        grid=(B, nq),
        in_specs=in_specs,
        out_specs=out_specs,
        out_shape=out_shape,
        scratch_shapes=[pltpu.VMEM((GQA_KV_HEADS, S, LANES), BF16)] * 4,
        compiler_params=_cparams(("arbitrary", "arbitrary"), VMEM_LIMIT),
        name="gqa",
    )(*args)
    return res if want_kv else (res[0], None, None)


def _rope_tables(n_tok, d_rot, lead, reps):
    t = jnp.arange(n_tok, dtype=jnp.int32)
    posn = jnp.stack([t // GRID_W, t % GRID_W], axis=-1).astype(F32)
    quarter = d_rot // 4
    inv = jnp.power(ROPE_THETA, -jnp.arange(quarter, dtype=F32) / quarter)
    ang = posn[:, :, None] * inv
    cos, sin = jnp.cos(ang), jnp.sin(ang)
    zero = jnp.zeros_like(sin)
    c_rot = jnp.stack([cos, cos], axis=2).reshape(n_tok, d_rot)
    sp_rot = jnp.stack([-sin, zero], axis=2).reshape(n_tok, d_rot)
    sm_rot = jnp.stack([zero, sin], axis=2).reshape(n_tok, d_rot)
    tail = LANES - lead - reps * d_rot

    def embed(rot, fill):
        parts = [jnp.full((n_tok, lead), fill, F32)] + [rot] * reps + [jnp.full((n_tok, tail), fill, F32)]
        return jnp.concatenate(parts, axis=1)

    return embed(c_rot, 1.0), embed(sp_rot, 0.0), embed(sm_rot, 0.0)


def _pad_lanes(x, lead, width=LANES):
    pad = [(0, 0)] * (x.ndim - 1) + [(lead, width - lead - x.shape[-1])]
    return jnp.pad(x, pad)


def _ab_params(w_in, w_out, a_w2, a_b, out_g, q_norm_g, w_qb, kv_norm_g, w_kvb, qn_g, kn_g):
    d = w_in.shape[0]
    o_alo = 2 * GLA_QK + 2 * GLA_V
    o_cq = o_alo + 2 * GLA_RANK
    o_ckv = o_cq + MLA_Q_RANK
    o_kpe = o_ckv + MLA_KV_RANK
    w_kpe = w_in[:, o_kpe:]
    w_perm = jnp.concatenate([
        w_in[:, :o_alo], w_in[:, o_cq:o_ckv], _pad_lanes(w_kpe, MLA_NOPE),
        w_in[:, o_ckv:o_kpe],
        _pad_lanes(jnp.concatenate([w_in[:, o_alo:o_cq], w_kpe], axis=1), 0),
    ], axis=1).astype(BF16)
    w2bd = jnp.zeros((LANES, 2 * GLA_QK), F32)
    w2bd = w2bd.at[:GLA_RANK, :GLA_QK].set(a_w2[0]).at[GLA_RANK:2 * GLA_RANK, GLA_QK:].set(a_w2[1])
    w_qb_p = _pad_lanes(w_qb.reshape(MLA_Q_RANK, MLA_HEADS, MLA_QK), 0).reshape(MLA_Q_RANK, MLA_HEADS * LANES)
    kvb = w_kvb.reshape(MLA_KV_RANK, MLA_HEADS, MLA_NOPE + MLA_V)
    w_kvb_p = jnp.concatenate([
        _pad_lanes(kvb[:, :, :MLA_NOPE], 0).reshape(MLA_KV_RANK, MLA_HEADS * LANES),
        kvb[:, :, MLA_NOPE:].reshape(MLA_KV_RANK, MLA_HEADS * MLA_V)], axis=1)
    return dict(
        w_perm=w_perm,
        w2bd=w2bd.astype(BF16),
        a_b=a_b.reshape(1, 2 * GLA_QK),
        out_g=jnp.tile(out_g, GLA_HEADS).reshape(1, GLA_V),
        mla=(q_norm_g.reshape(1, -1), w_qb_p.astype(BF16),
             _pad_lanes(qn_g * (MLA_QK ** -0.5), 0).reshape(1, LANES),
             kv_norm_g.reshape(1, -1), w_kvb_p.astype(BF16),
             _pad_lanes(kn_g, 0).reshape(1, LANES)),
        w_out_gla=w_out[:GLA_V].astype(BF16),
        w_out_mla=w_out[GLA_V:].astype(BF16),
    )


AB_OUTS = ((2 * GLA_QK + 2 * GLA_V + MLA_Q_RANK + LANES, BF16), (MLA_KV_RANK, F32), (LANES, F32))
C_OUTS = ((GQA_HEADS * GQA_DH, BF16), (GQA_KV_HEADS * GQA_DH, F32), (GQA_KV_HEADS * GQA_DH, F32))


def kernel(x_prompt, x_sample, c, cache_mla_ckv, cache_mla_kpe, state_gla, cache_gqa_k, cache_gqa_v,
           c_ctx, ada_w, ada_b, norm_mix_g, norm_ffn_g, ffn_w_in, ffn_w_out, ab_w_in, ab_w_out,
           gla_a_w2, gla_a_b, gla_out_g, mla_q_norm_g, mla_w_qb, mla_kv_norm_g, mla_w_kvb, mla_qn_g,
           mla_kn_g, gqa_w_in, gqa_w_out, gqa_qn_g, gqa_kn_g):
    Bp, Tp, D = x_prompt.shape
    Bs, Ts, _ = x_sample.shape
    depth = ada_w.shape[0]
    xp = x_prompt.reshape(Bp * Tp, D)
    xs = x_sample.reshape(Bs * Ts, D)

    cond = jnp.zeros((8, D), F32).at[0].set(c_ctx).at[2:2 + Bs].set(c)
    mods = _adaln(cond, ada_w, ada_b).reshape(depth, 8, 6, D)

    rope_mla = _rope_tables(Ts, MLA_ROPE, MLA_NOPE, 1)
    rope_gqa = _rope_tables(Ts, GQA_DH, 0, 2)

    streams = (dict(mod_base=0, rows_per_cond=None, tm=512),
               dict(mod_base=2, rows_per_cond=Ts, tm=512))
    ffn_tm = (2048, 2048)

    new_ckv, new_kpe, new_gla, new_k, new_v = [], [], [], [], []
    for l in range(depth):
        i = l // 2
        mod = mods[l]
        if l % 2 == 0:
            P = _ab_params(ab_w_in[i], ab_w_out[i], gla_a_w2[i], gla_a_b[i], gla_out_g[i], mla_q_norm_g[i],
                           mla_w_qb[i], mla_kv_norm_g[i], mla_w_kvb[i], mla_qn_g[i], mla_kn_g[i])
            pj_p, ckv_p, alk_p = _inproj(xp, norm_mix_g[l], mod, P['w_perm'], AB_OUTS, **streams[0])
            pj_s, ckv_s, alk_s = _inproj(xs, norm_mix_g[l], mod, P['w_perm'], AB_OUTS, **streams[1])
            og_p, st_p = _gla(pj_p, alk_p, P['w2bd'], P['a_b'], P['out_g'], None, B=Bp, T=Tp, want_state=True)
            og_s, _ = _gla(pj_s, alk_s, P['w2bd'], P['a_b'], P['out_g'], state_gla[:, i], B=Bs, T=Ts,
                           want_state=False)
            om_p, ckvn_p = _mla(pj_p, ckv_p, None, None, P['mla'], B=Bp, T=Tp, tq=Tp, want_ckv=True)
            ctx = (cache_mla_ckv[:, i], _pad_lanes(cache_mla_kpe[:, i], MLA_NOPE))
            om_s, _ = _mla(pj_s, ckv_s, ctx, rope_mla, P['mla'], B=Bs, T=Ts, tq=256, want_ckv=False)
            ws = (P['w_out_gla'], P['w_out_mla'])
            xp = _outproj((og_p, om_p), ws, xp, mod, **streams[0])
            xs = _outproj((og_s, om_s), ws, xs, mod, **streams[1])
            new_ckv.append(ckvn_p.reshape(Bp, Tp, MLA_KV_RANK))
            new_kpe.append(alk_p[:, 2 * GLA_RANK:2 * GLA_RANK + MLA_ROPE].reshape(Bp, Tp, MLA_ROPE))
            new_gla.append(st_p)
        else:
            w_in = gqa_w_in[i].astype(BF16)
            w_out = gqa_w_out[i].astype(BF16)
            gp = (jnp.tile(gqa_qn_g[i] * (GQA_DH ** -0.5), 2).reshape(1, LANES),
                  jnp.tile(gqa_kn_g[i], 2).reshape(1, LANES))
            q_p, k_p, v_p = _inproj(xp, norm_mix_g[l], mod, w_in, C_OUTS, **streams[0])
            q_s, k_s, v_s = _inproj(xs, norm_mix_g[l], mod, w_in, C_OUTS, **streams[1])
            o_p, kn_p, vo_p = _gqa(q_p, k_p, v_p, None, None, gp, B=Bp, T=Tp, tq=Tp, want_kv=True)
            n_kv = GQA_KV_HEADS * GQA_DH
            ctx = (cache_gqa_k[:, i].reshape(Bs, -1, n_kv), cache_gqa_v[:, i].reshape(Bs, -1, n_kv))
            o_s, _, _ = _gqa(q_s, k_s, v_s, ctx, rope_gqa, gp, B=Bs, T=Ts, tq=256, want_kv=False)
            xp = _outproj((o_p,), (w_out,), xp, mod, **streams[0])
            xs = _outproj((o_s,), (w_out,), xs, mod, **streams[1])
            new_k.append(kn_p.reshape(Bp, Tp, GQA_KV_HEADS, GQA_DH))
            new_v.append(vo_p.reshape(Bp, Tp, GQA_KV_HEADS, GQA_DH))
        xp = _ffn(xp, norm_ffn_g[l], mod, ffn_w_in, ffn_w_out, l, tm=ffn_tm[0],
                  **{k: v for k, v in streams[0].items() if k != 'tm'})
        xs = _ffn(xs, norm_ffn_g[l], mod, ffn_w_in, ffn_w_out, l, tm=ffn_tm[1],
                  **{k: v for k, v in streams[1].items() if k != 'tm'})
    return (xp.reshape(Bp, Tp, D), xs.reshape(Bs, Ts, D), jnp.stack(new_ckv, axis=1),
            jnp.stack(new_kpe, axis=1), jnp.stack(new_gla, axis=1), jnp.stack(new_k, axis=1),
            jnp.stack(new_v, axis=1))
```

```python
import functools

import jax
import jax.numpy as jnp
from jax import lax
from jax.experimental import pallas as pl
from jax.experimental.pallas import tpu as pltpu

F32 = jnp.float32
BF16 = jnp.bfloat16

EPS = 1e-6
ROPE_THETA = 10000.0
GRID_W = 64
LANES = 128
GLA_HEADS, GLA_DK, GLA_DV = 4, 64, 128
GLA_QK = GLA_HEADS * GLA_DK
GLA_V = GLA_HEADS * GLA_DV
GLA_RANK = 16
GLA_TAU = 16.0
GLA_CHUNK = 64
MLA_HEADS = 8
MLA_Q_RANK, MLA_KV_RANK = 384, 256
MLA_NOPE, MLA_ROPE, MLA_V = 64, 32, 64
MLA_QK = MLA_NOPE + MLA_ROPE
GQA_HEADS, GQA_KV_HEADS, GQA_DH = 16, 4, 64
VMEM_LIMIT = 56 << 20


def _cparams(sem, vmem=None):
    return pltpu.CompilerParams(dimension_semantics=sem, vmem_limit_bytes=vmem)


def _nt_dot(a, b):
    return lax.dot_general(a, b, (((1,), (1,)), ((), ())), preferred_element_type=F32)


def _tn_dot(a, b):
    return lax.dot_general(a, b, (((0,), (0,)), ((), ())), preferred_element_type=F32)


def _ds(start, size):
    if isinstance(start, int):
        return pl.ds(start, size)
    return pl.ds(pl.multiple_of(start, size), size)


def _rms(x, g):
    ms = jnp.mean(x * x, axis=-1, keepdims=True)
    return x * lax.rsqrt(ms + EPS) * g


def _silu(x):
    return x * jax.nn.sigmoid(x)


def _rope(x, c, sp, sm, shift):
    return x * c + pltpu.roll(x, LANES - shift, 1) * sp + pltpu.roll(x, shift, 1) * sm


def _adaln_body(c_ref, w_ref, b_ref, o_ref):
    c = c_ref[...]
    s = _silu(c).astype(BF16)
    o_ref[0] = jnp.dot(s, w_ref[0].astype(BF16), preferred_element_type=F32) + b_ref[0]


def _adaln(cond, ada_w, ada_b):
    L, D, E = ada_w.shape
    tn = 1536
    return pl.pallas_call(
        _adaln_body,
        grid=(L, E // tn),
        in_specs=[pl.BlockSpec((8, D), lambda l, n: (0, 0)),
                  pl.BlockSpec((1, D, tn), lambda l, n: (l, 0, n)),
                  pl.BlockSpec((1, 1, tn), lambda l, n: (l, 0, n))],
        out_specs=pl.BlockSpec((1, 8, tn), lambda l, n: (l, 0, n)),
        out_shape=jax.ShapeDtypeStruct((L, 8, E), F32),
        compiler_params=_cparams(("arbitrary", "arbitrary")),
        name="adaln",
    )(cond, ada_w, ada_b.reshape(L, 1, E))


def _mod_index(mod_base, rows_per_cond, tm):
    if rows_per_cond is None:
        return lambda i: (mod_base, 0, 0)
    n_mod = max(tm // rows_per_cond, 1)
    return lambda i: ((mod_base + (i * tm) // rows_per_cond) // n_mod, 0, 0)


def _inproj_body(x_ref, g_ref, mod_ref, w_ref, *o_refs, widths):
    h = _rms(x_ref[...], g_ref[...])
    h = h * (1.0 + mod_ref[0, 1:2, :]) + mod_ref[0, 0:1, :]
    acc = jnp.dot(h.astype(BF16), w_ref[...], preferred_element_type=F32)
    off = 0
    for o_ref, w in zip(o_refs, widths):
        o_ref[...] = acc[:, off:off + w].astype(o_ref.dtype)
        off += w


def _inproj(x, g, mod, w, outs, *, mod_base, rows_per_cond, tm):
    n, d = x.shape
    widths = tuple(o[0] for o in outs)
    return pl.pallas_call(
        functools.partial(_inproj_body, widths=widths),
        grid=(n // tm,),
        in_specs=[pl.BlockSpec((tm, d), lambda i: (i, 0)),
                  pl.BlockSpec((1, d), lambda i: (0, 0)),
                  pl.BlockSpec((1, 6, d), _mod_index(mod_base, rows_per_cond, tm)),
                  pl.BlockSpec(w.shape, lambda i: (0, 0))],
        out_specs=[pl.BlockSpec((tm, wd), lambda i: (i, 0)) for wd in widths],
        out_shape=[jax.ShapeDtypeStruct((n, wd), dt) for wd, dt in outs],
        compiler_params=_cparams(("arbitrary",), VMEM_LIMIT),
        name="inproj",
    )(x, g.reshape(1, d), mod, w)


def _outproj_body(*refs, n_in):
    a_refs, w_refs = refs[:n_in], refs[n_in:2 * n_in]
    x_ref, mod_ref, o_ref = refs[2 * n_in:]
    acc = None
    for a_ref, w_ref in zip(a_refs, w_refs):
        part = jnp.dot(a_ref[...], w_ref[...], preferred_element_type=F32)
        acc = part if acc is None else acc + part
    o_ref[...] = x_ref[...] + mod_ref[0, 2:3, :] * acc


def _outproj(acts, ws, x, mod, *, mod_base, rows_per_cond, tm):
    n, d = x.shape
    n_in = len(acts)
    return pl.pallas_call(
        functools.partial(_outproj_body, n_in=n_in),
        grid=(n // tm,),
        in_specs=([pl.BlockSpec((tm, a.shape[1]), lambda i: (i, 0)) for a in acts]
                  + [pl.BlockSpec(w.shape, lambda i: (0, 0)) for w in ws]
                  + [pl.BlockSpec((tm, d), lambda i: (i, 0)),
                     pl.BlockSpec((1, 6, d), _mod_index(mod_base, rows_per_cond, tm))]),
        out_specs=pl.BlockSpec((tm, d), lambda i: (i, 0)),
        out_shape=jax.ShapeDtypeStruct((n, d), F32),
        compiler_params=_cparams(("arbitrary",), VMEM_LIMIT),
        name="outproj",
    )(*acts, *ws, x, mod)


def _ffn_body(x_ref, g_ref, mod_ref, wg_ref, wu_ref, wo_ref, o_ref, h_ref, wgu_ref, wob_ref, *,
              nk, row_chunk, rows_per_cond):
    k = pl.program_id(1)
    tm = x_ref.shape[0]
    tk = wg_ref.shape[2]
    n_chunks = tm // row_chunk

    def chunk_rows(c):
        return pl.ds(pl.multiple_of(c * row_chunk, row_chunk), row_chunk)

    def mod_row(c):
        return 0 if rows_per_cond is None else (c * row_chunk) // rows_per_cond

    @pl.when(k == 0)
    def _():
        def norm_rows(c, carry):
            rows, m = chunk_rows(c), mod_row(c)
            h = _rms(x_ref[rows, :], g_ref[...])
            h = h * (1.0 + mod_ref[m, 4:5, :]) + mod_ref[m, 3:4, :]
            h_ref[rows, :] = h.astype(BF16)
            o_ref[rows, :] = jnp.zeros((row_chunk, o_ref.shape[1]), F32)
            return carry
        lax.fori_loop(0, n_chunks, norm_rows, 0)

    wgu_ref[:, :tk] = wg_ref[0].astype(BF16)
    wgu_ref[:, tk:] = wu_ref[0].astype(BF16)
    wob_ref[...] = wo_ref[0].astype(BF16)

    def ffn_rows(c, carry):
        rows = chunk_rows(c)
        gu = jnp.dot(h_ref[rows, :], wgu_ref[...], preferred_element_type=F32)
        a = (_silu(gu[:, :tk]) * gu[:, tk:]).astype(BF16)
        o_ref[rows, :] += jnp.dot(a, wob_ref[...], preferred_element_type=F32)
        return carry
    lax.fori_loop(0, n_chunks, ffn_rows, 0, unroll=True)

    @pl.when(k == nk - 1)
    def _():
        def residual_rows(c, carry):
            rows, m = chunk_rows(c), mod_row(c)
            o_ref[rows, :] = x_ref[rows, :] + mod_ref[m, 5:6, :] * o_ref[rows, :]
            return carry
        lax.fori_loop(0, n_chunks, residual_rows, 0)


def _ffn(x, g, mod, w_in, w_out, layer, *, mod_base, rows_per_cond, tm):
    n, d = x.shape
    hidden = w_out.shape[1]
    tk = 256
    nk = hidden // tk
    n_mod = 1 if rows_per_cond is None else max(tm // rows_per_cond, 1)
    return pl.pallas_call(
        functools.partial(_ffn_body, nk=nk, row_chunk=512, rows_per_cond=rows_per_cond),
        grid=(n // tm, nk),
        in_specs=[pl.BlockSpec((tm, d), lambda i, k: (i, 0)),
                  pl.BlockSpec((1, d), lambda i, k: (0, 0)),
                  pl.BlockSpec((n_mod, 6, d),
                               (lambda f: (lambda i, k: f(i)))(_mod_index(mod_base, rows_per_cond, tm))),
                  pl.BlockSpec((1, d, tk), lambda i, k: (layer, 0, k)),
                  pl.BlockSpec((1, d, tk), lambda i, k: (layer, 0, nk + k)),
                  pl.BlockSpec((1, tk, d), lambda i, k: (layer, k, 0))],
        out_specs=pl.BlockSpec((tm, d), lambda i, k: (i, 0)),
        out_shape=jax.ShapeDtypeStruct((n, d), F32),
        scratch_shapes=[pltpu.VMEM((tm, d), BF16),
                        pltpu.VMEM((d, 2 * tk), BF16),
                        pltpu.VMEM((tk, d), BF16)],
        compiler_params=_cparams(("arbitrary", "arbitrary"), VMEM_LIMIT),
        name="ffn",
    )(x, g.reshape(1, d), mod, w_in, w_in, w_out)


def _log_sigmoid(x):
    return jnp.minimum(x, 0.0) - jnp.log1p(jnp.exp(-jnp.abs(x)))


def _gla_body(*refs, T, has_s0, want_state):
    qkvr_ref, alk_ref, w2_ref, ab_ref, og_ref = refs[:5]
    pos = 5
    s0_ref = None
    if has_s0:
        s0_ref = refs[pos]
        pos += 1
    o_ref = refs[pos]
    pos += 1
    st_out_ref = None
    if want_state:
        st_out_ref = refs[pos]
        pos += 1
    osc_ref, st_ref, la_ref, qin_ref, kst_ref, dec_ref = refs[pos:pos + 6]

    C = GLA_CHUNK
    nc = T // C

    logit = jnp.dot(alk_ref[...].astype(BF16), w2_ref[...], preferred_element_type=F32) + ab_ref[...]
    la_ref[...] = _log_sigmoid(logit) * (1.0 / GLA_TAU)

    for d in range(2):
        if has_s0:
            st_ref[d] = jnp.concatenate([s0_ref[0, d, h] for h in range(GLA_HEADS)], axis=0).T
        else:
            st_ref[d] = jnp.zeros((GLA_DV, GLA_QK), F32)

    r64 = lax.broadcasted_iota(jnp.int32, (C, C), 0)
    c64 = lax.broadcasted_iota(jnp.int32, (C, C), 1)
    tri_f = (r64 >= c64).astype(BF16)
    tri_b = (c64 >= r64).astype(BF16)
    t_idx = lax.broadcasted_iota(jnp.int32, (C, GLA_QK), 0)
    s_idx = lax.broadcasted_iota(jnp.int32, (C, GLA_QK), 1) % C
    causal_f = t_idx >= s_idx
    causal_b = t_idx <= s_idx
    bm_k = (lax.broadcasted_iota(jnp.int32, (GLA_QK, GLA_QK), 0) // GLA_DK
            == lax.broadcasted_iota(jnp.int32, (GLA_QK, GLA_QK), 1) // GLA_DK)
    bm_v = (lax.broadcasted_iota(jnp.int32, (GLA_QK, GLA_V), 0) // C
            == lax.broadcasted_iota(jnp.int32, (GLA_QK, GLA_V), 1) // GLA_DV)
    head_lanes = [lax.broadcasted_iota(jnp.int32, (1, GLA_QK), 1) // GLA_DK == h for h in range(GLA_HEADS)]
    directions = ((tri_f, causal_f, C // 2 - 1, C - 1), (tri_b, causal_b, C // 2, 0))

    def v_rows(rows):
        return qkvr_ref[rows, 2 * GLA_QK:2 * GLA_QK + GLA_V]

    def intra(n, d):
        tri, causal, ref_row, last_row = directions[d]
        rows = _ds(n * C, C)
        qc = qkvr_ref[rows, 0:GLA_QK].astype(F32) * (GLA_DK ** -0.5)
        kc = qkvr_ref[rows, GLA_QK:2 * GLA_QK].astype(F32)
        lac = la_ref[rows, d * GLA_QK:(d + 1) * GLA_QK]
        la_hi = lac.astype(BF16)
        la_lo = (lac - la_hi.astype(F32)).astype(BF16)
        b = (jnp.dot(tri, la_hi, preferred_element_type=F32)
             + jnp.dot(tri, la_lo, preferred_element_type=F32))
        b_ref = b[ref_row:ref_row + 1, :]
        b_last = b[last_row:last_row + 1, :]
        q_loc = (qc * jnp.exp(b - b_ref)).astype(BF16)
        k_loc = kc * jnp.exp(b_ref - b)
        qin_ref[d, rows, :] = (qc * jnp.exp(b)).astype(BF16)
        kst_ref[d, rows, :] = (kc * jnp.exp(b_last - b)).astype(BF16)
        dec_ref[d, _ds(n * 8, 8), :] = jnp.broadcast_to(jnp.exp(b_last), (8, GLA_QK))
        k_bd = jnp.where(bm_k, jnp.concatenate([k_loc] * GLA_HEADS, axis=0), 0.0).astype(BF16)
        a = jnp.where(causal, _nt_dot(q_loc, k_bd), 0.0).astype(BF16)
        v_bd = jnp.where(bm_v, jnp.concatenate([v_rows(rows)] * GLA_HEADS, axis=0), jnp.zeros((), BF16))
        return jnp.dot(a, v_bd, preferred_element_type=F32)

    def intra_step(n):
        osc_ref[_ds(n * C, C), :] = intra(n, 0) + intra(n, 1)

    def inter(n, d):
        rows = _ds(n * C, C)
        st = st_ref[d]
        q_in = qin_ref[d, rows, :]
        q_heads = jnp.concatenate([jnp.where(m, q_in, jnp.zeros((), BF16)) for m in head_lanes], axis=0)
        o = _nt_dot(q_heads, st.astype(BF16))
        for h in range(GLA_HEADS):
            osc_ref[rows, h * GLA_DV:(h + 1) * GLA_DV] += o[h * C:(h + 1) * C, :]
        upd = _tn_dot(v_rows(rows), kst_ref[d, rows, :])
        upd_c = None
        for h, m in enumerate(head_lanes):
            part = jnp.where(m, upd[h * GLA_DV:(h + 1) * GLA_DV, :], 0.0)
            upd_c = part if upd_c is None else upd_c + part
        st_ref[d] = st * dec_ref[d, pl.ds(n * 8, 1), :] + upd_c

    def inter_step(i):
        inter(i, 0)
        inter(nc - 1 - i, 1)

    if nc <= 4:
        for n in range(nc):
            intra_step(n)
        for i in range(nc):
            inter_step(i)
    else:
        def run(step_fn):
            def body(i, carry):
                step_fn(i)
                return carry
            lax.fori_loop(0, nc, body, 0)
        run(intra_step)
        run(inter_step)

    rb = 256
    for i in range(T // rb):
        rows = slice(i * rb, (i + 1) * rb)
        for h in range(GLA_HEADS):
            cols = slice(h * GLA_DV, (h + 1) * GLA_DV)
            o = _rms(osc_ref[rows, cols], og_ref[:, cols])
            r = qkvr_ref[rows, 2 * GLA_QK + GLA_V + h * GLA_DV:2 * GLA_QK + GLA_V + (h + 1) * GLA_DV].astype(F32)
            o_ref[rows, cols] = (o * _silu(r)).astype(o_ref.dtype)

    if want_state:
        for d in range(2):
            s_all = st_ref[d].T
            for h in range(GLA_HEADS):
                st_out_ref[0, d, h] = s_all[h * GLA_DK:(h + 1) * GLA_DK, :]


def _gla(proj, alk, w2bd, a_b, out_g, s0, *, B, T, want_state):
    has_s0 = s0 is not None
    n_qkvr = 2 * GLA_QK + 2 * GLA_V
    in_specs = [pl.BlockSpec((T, n_qkvr), lambda b: (b, 0)),
                pl.BlockSpec((T, LANES), lambda b: (b, 0)),
                pl.BlockSpec(w2bd.shape, lambda b: (0, 0)),
                pl.BlockSpec((1, 2 * GLA_QK), lambda b: (0, 0)),
                pl.BlockSpec((1, GLA_V), lambda b: (0, 0))]
    args = [proj, alk, w2bd, a_b, out_g]
    st_spec = pl.BlockSpec((1, 2, GLA_HEADS, GLA_DK, GLA_DV), lambda b: (b, 0, 0, 0, 0))
    if has_s0:
        in_specs.append(st_spec)
        args.append(s0)
    out_specs = [pl.BlockSpec((T, GLA_V), lambda b: (b, 0))]
    out_shape = [jax.ShapeDtypeStruct((B * T, GLA_V), BF16)]
    if want_state:
        out_specs.append(st_spec)
        out_shape.append(jax.ShapeDtypeStruct((B, 2, GLA_HEADS, GLA_DK, GLA_DV), F32))
    res = pl.pallas_call(
        functools.partial(_gla_body, T=T, has_s0=has_s0, want_state=want_state),
        grid=(B,),
        in_specs=in_specs,
        out_specs=out_specs,
        out_shape=out_shape,
        scratch_shapes=[pltpu.VMEM((T, GLA_V), F32),
                        pltpu.VMEM((2, GLA_DV, GLA_QK), F32),
                        pltpu.VMEM((T, 2 * GLA_QK), F32),
                        pltpu.VMEM((2, T, GLA_QK), BF16),
                        pltpu.VMEM((2, T, GLA_QK), BF16),
                        pltpu.VMEM((2, 8 * (T // GLA_CHUNK), GLA_QK), F32)],
        compiler_params=_cparams(("arbitrary",), VMEM_LIMIT),
        name="gla",
    )(*args)
    return res if want_state else (res[0], None)


LOG2E = 1.4426950408889634


def _attend(qs, ks, vs, group):
    outs = []
    for g0 in range(0, len(qs), group):
        sl = slice(g0, g0 + group)
        scores = [_nt_dot(q, k()) for q, k in zip(qs[sl], ks[sl])]
        probs = [jnp.exp2(s - jnp.max(s, axis=-1, keepdims=True)).astype(BF16) for s in scores]
        res = [jnp.dot(p, v(), preferred_element_type=F32) for p, v in zip(probs, vs[sl])]
        outs += [r[:, :LANES] / r[:, LANES:] for r in res]
    return outs


def _head_group(n_keys, n_heads):
    return n_heads if n_keys <= 512 else 2


def _stacked_ms(xs, ones_mat, inv_n):
    n = xs[0].shape[0]
    ms = _split_dot(jnp.concatenate([x * x for x in xs], axis=0), ones_mat) * inv_n
    return [ms[i * n:(i + 1) * n] for i in range(len(xs))]


def _split_dot(x, w):
    hi = x.astype(BF16)
    lo = (x - hi.astype(F32)).astype(BF16)
    return (jnp.dot(hi, w, preferred_element_type=F32) + jnp.dot(lo, w, preferred_element_type=F32))


def _group_ones(group):
    r = lax.broadcasted_iota(jnp.int32, (LANES, LANES), 0) // group
    c = lax.broadcasted_iota(jnp.int32, (LANES, LANES), 1) // group
    return (r == c).astype(BF16)


def _mla_body(*refs, T, S_ctx, rope, want_ckv, group):
    cq_ref, kpe_ref, ckv_ref = refs[:3]
    pos = 3
    if S_ctx:
        ckvc_ref, kpec_ref = refs[pos:pos + 2]
        pos += 2
    if rope:
        rq_refs = refs[pos:pos + 3]
        rk_refs = refs[pos + 3:pos + 6]
        pos += 6
    qng_ref, wqb_ref, qg_ref, kvg_ref, wkvb_ref, kg_ref = refs[pos:pos + 6]
    pos += 6
    o_ref = refs[pos]
    pos += 1
    if want_ckv:
        ckvn_ref = refs[pos]
        pos += 1
    k_sc, vlo_sc, vhi_sc = refs[pos:pos + 3]

    H = MLA_HEADS
    n_k = H * LANES
    lo = lax.broadcasted_iota(jnp.int32, (1, LANES), 1) < MLA_V

    def head_ms(xs):
        return [jnp.sum(x * x, axis=-1, keepdims=True) * (1.0 / MLA_QK) for x in xs]

    def fill_kv(ckvn, kpe, row0, n_rows, with_rope):
        rows = slice(row0, row0 + n_rows)
        kv = jnp.dot(ckvn.astype(BF16), wkvb_ref[...], preferred_element_type=F32)
        khs = [kv[:, h * LANES:(h + 1) * LANES] + kpe for h in range(H)]
        for h, (kh, ms) in enumerate(zip(khs, head_ms(khs))):
            kh = kh * lax.rsqrt(ms + EPS) * kg_ref[...]
            if with_rope:
                kh = _rope(kh, rk_refs[0][...], rk_refs[1][...], rk_refs[2][...], MLA_ROPE // 4)
            k_sc[rows, h * LANES:(h + 1) * LANES] = kh.astype(BF16)
        ones = jnp.ones((n_rows, LANES), BF16)
        for p in range(H // 2):
            v = kv[:, n_k + p * LANES:n_k + (p + 1) * LANES]
            vlo_sc[rows, 2 * p * LANES:(2 * p + 1) * LANES] = jnp.where(lo, v, 0.0).astype(BF16)
            vhi_sc[rows, 2 * p * LANES:(2 * p + 1) * LANES] = jnp.where(lo, 0.0, v).astype(BF16)
            vlo_sc[rows, (2 * p + 1) * LANES:(2 * p + 2) * LANES] = ones
            vhi_sc[rows, (2 * p + 1) * LANES:(2 * p + 2) * LANES] = ones

    @pl.when(pl.program_id(1) == 0)
    def _():
        if S_ctx:
            fill_kv(ckvc_ref[0], kpec_ref[0], 0, S_ctx, False)
        ckvn = _rms(ckv_ref[...], kvg_ref[...])
        if want_ckv:
            ckvn_ref[...] = ckvn
        fill_kv(ckvn, kpe_ref[...].astype(F32), S_ctx, T, rope)

    cqn = _rms(cq_ref[...].astype(F32), qng_ref[...])
    q = jnp.dot(cqn.astype(BF16), wqb_ref[...], preferred_element_type=F32)
    qhs = [q[:, h * LANES:(h + 1) * LANES] for h in range(H)]
    qs = []
    for qh, ms in zip(qhs, head_ms(qhs)):
        qh = qh * lax.rsqrt(ms + EPS) * qg_ref[...]
        if rope:
            qh = _rope(qh, rq_refs[0][...], rq_refs[1][...], rq_refs[2][...], MLA_ROPE // 4)
        qs.append(qh.astype(BF16))
    ks = [functools.partial(lambda h: k_sc[:, h * LANES:(h + 1) * LANES], h) for h in range(H)]
    vs = [functools.partial(lambda h: (vlo_sc, vhi_sc)[h % 2][:, (h // 2) * 2 * LANES:(h // 2 + 1) * 2 * LANES], h)
          for h in range(H)]
    outs = _attend(qs, ks, vs, group)
    for p in range(H // 2):
        o_ref[:, p * LANES:(p + 1) * LANES] = (outs[2 * p] + outs[2 * p + 1]).astype(o_ref.dtype)


def _mla(proj, ckv, ctx, rope_tabs, params, *, B, T, tq, want_ckv):
    nq = T // tq
    S_ctx = ctx[0].shape[1] if ctx is not None else 0
    S = S_ctx + T
    rope = rope_tabs is not None
    cq_blk = (2 * GLA_QK + 2 * GLA_V) // MLA_Q_RANK
    kpe_blk = (2 * GLA_QK + 2 * GLA_V + MLA_Q_RANK) // LANES
    in_specs = [pl.BlockSpec((tq, MLA_Q_RANK), lambda b, j: (b * nq + j, cq_blk)),
                pl.BlockSpec((T, LANES), lambda b, j: (b, kpe_blk)),
                pl.BlockSpec((T, MLA_KV_RANK), lambda b, j: (b, 0))]
    args = [proj, proj, ckv]
    if S_ctx:
        in_specs += [pl.BlockSpec((1, S_ctx, MLA_KV_RANK), lambda b, j: (b, 0, 0)),
                     pl.BlockSpec((1, S_ctx, LANES), lambda b, j: (b, 0, 0))]
        args += list(ctx)
    if rope:
        in_specs += [pl.BlockSpec((tq, LANES), lambda b, j: (j, 0))] * 3
        in_specs += [pl.BlockSpec((T, LANES), lambda b, j: (0, 0))] * 3
        args += list(rope_tabs) * 2
    in_specs += [pl.BlockSpec(p.shape, lambda b, j: (0, 0)) for p in params]
    args += list(params)
    out_specs = [pl.BlockSpec((tq, MLA_HEADS * MLA_V), lambda b, j: (b * nq + j, 0))]
    out_shape = [jax.ShapeDtypeStruct((B * T, MLA_HEADS * MLA_V), BF16)]
    if want_ckv:
        out_specs.append(pl.BlockSpec((T, MLA_KV_RANK), lambda b, j: (b, 0)))
        out_shape.append(jax.ShapeDtypeStruct((B * T, MLA_KV_RANK), F32))
    res = pl.pallas_call(
        functools.partial(_mla_body, T=T, S_ctx=S_ctx, rope=rope, want_ckv=want_ckv,
                          group=_head_group(S, MLA_HEADS)),
        grid=(B, nq),
        in_specs=in_specs,
        out_specs=out_specs,
        out_shape=out_shape,
        scratch_shapes=[pltpu.VMEM((S, MLA_HEADS * LANES), BF16),
                        pltpu.VMEM((S, MLA_HEADS * LANES), BF16),
                        pltpu.VMEM((S, MLA_HEADS * LANES), BF16)],
        compiler_params=_cparams(("arbitrary", "arbitrary"), VMEM_LIMIT),
        name="mla",
    )(*args)
    return res if want_ckv else (res[0], None)


def _gqa_body(*refs, T, S_ctx, rope, want_kv, group):
    q_ref, k_ref, v_ref = refs[:3]
    pos = 3
    if S_ctx:
        kc_ref, vc_ref = refs[pos:pos + 2]
        pos += 2
    if rope:
        rq_refs = refs[pos:pos + 3]
        rk_refs = refs[pos + 3:pos + 6]
        pos += 6
    qg_ref, kg_ref = refs[pos:pos + 2]
    pos += 2
    o_ref = refs[pos]
    pos += 1
    if want_kv:
        kn_ref, vo_ref = refs[pos:pos + 2]
        pos += 2
    klo_sc, khi_sc, vlo_sc, vhi_sc = refs[pos:pos + 4]

    lo = lax.broadcasted_iota(jnp.int32, (1, LANES), 1) < GQA_DH
    half_sum = _group_ones(GQA_DH)

    def head_ms(x):
        return _split_dot(x * x, half_sum) * (1.0 / GQA_DH)

    def scatter_halves(x, lo_sc, hi_sc, c, rows):
        rolled = pltpu.roll(x, GQA_DH, 1)
        lo_sc[2 * c, rows, :LANES] = jnp.where(lo, x, 0.0).astype(BF16)
        hi_sc[2 * c, rows, :LANES] = jnp.where(lo, 0.0, rolled).astype(BF16)
        lo_sc[2 * c + 1, rows, :LANES] = jnp.where(lo, rolled, 0.0).astype(BF16)
        hi_sc[2 * c + 1, rows, :LANES] = jnp.where(lo, 0.0, x).astype(BF16)

    def fill_ones(rows):
        ones = jnp.ones((rows.stop - rows.start, LANES), BF16)
        for g in range(GQA_KV_HEADS):
            vlo_sc[g, rows, LANES:] = ones
            vhi_sc[g, rows, LANES:] = ones

    @pl.when(pl.program_id(1) == 0)
    def _():
        fill_ones(slice(0, S_ctx + T))
        for c in range(GQA_KV_HEADS // 2):
            cols = slice(c * LANES, (c + 1) * LANES)
            if S_ctx:
                scatter_halves(kc_ref[0, :, cols], klo_sc, khi_sc, c, slice(0, S_ctx))
                scatter_halves(vc_ref[0, :, cols], vlo_sc, vhi_sc, c, slice(0, S_ctx))
            kx = k_ref[:, cols]
            kn = kx * lax.rsqrt(head_ms(kx) + EPS) * kg_ref[...]
            vx = v_ref[:, cols]
            if want_kv:
                kn_ref[:, cols] = kn
                vo_ref[:, cols] = vx
            if rope:
                kn = _rope(kn, rk_refs[0][...], rk_refs[1][...], rk_refs[2][...], GQA_DH // 4)
            scatter_halves(kn, klo_sc, khi_sc, c, slice(S_ctx, S_ctx + T))
            scatter_halves(vx, vlo_sc, vhi_sc, c, slice(S_ctx, S_ctx + T))

    n_pairs = GQA_HEADS // 2
    qxs = [q_ref[:, p * LANES:(p + 1) * LANES].astype(F32) for p in range(n_pairs)]
    qs, ks, vs = [], [], []
    for p, (qx, ms) in enumerate(zip(qxs, _stacked_ms(qxs, half_sum, 1.0 / GQA_DH))):
        qn = qx * lax.rsqrt(ms + EPS) * qg_ref[...]
        if rope:
            qn = _rope(qn, rq_refs[0][...], rq_refs[1][...], rq_refs[2][...], GQA_DH // 4)
        g = p // 2
        qs += [qn.astype(BF16)] * 2
        ks += [functools.partial(lambda r, g: r[g], r, g) for r in (klo_sc, khi_sc)]
        vs += [functools.partial(lambda r, g: r[g], r, g) for r in (vlo_sc, vhi_sc)]
    outs = _attend(qs, ks, vs, group)
    for p in range(n_pairs):
        o_ref[:, p * LANES:(p + 1) * LANES] = (outs[2 * p] + outs[2 * p + 1]).astype(o_ref.dtype)


def _gqa(q, k, v, ctx, rope_tabs, params, *, B, T, tq, want_kv):
    nq = T // tq
    S_ctx = ctx[0].shape[1] if ctx is not None else 0
    S = S_ctx + T
    rope = rope_tabs is not None
    n_q = GQA_HEADS * GQA_DH
    n_kv = GQA_KV_HEADS * GQA_DH
    in_specs = [pl.BlockSpec((tq, n_q), lambda b, j: (b * nq + j, 0)),
                pl.BlockSpec((T, n_kv), lambda b, j: (b, 0)),
                pl.BlockSpec((T, n_kv), lambda b, j: (b, 0))]
    args = [q, k, v]
    if S_ctx:
        in_specs += [pl.BlockSpec((1, S_ctx, n_kv), lambda b, j: (b, 0, 0))] * 2
        args += list(ctx)
    if rope:
        in_specs += [pl.BlockSpec((tq, LANES), lambda b, j: (j, 0))] * 3
        in_specs += [pl.BlockSpec((T, LANES), lambda b, j: (0, 0))] * 3
        args += list(rope_tabs) * 2
    in_specs += [pl.BlockSpec(p.shape, lambda b, j: (0, 0)) for p in params]
    args += list(params)
    out_specs = [pl.BlockSpec((tq, n_q), lambda b, j: (b * nq + j, 0))]
    out_shape = [jax.ShapeDtypeStruct((B * T, n_q), BF16)]
    if want_kv:
        out_specs += [pl.BlockSpec((T, n_kv), lambda b, j: (b, 0))] * 2
        out_shape += [jax.ShapeDtypeStruct((B * T, n_kv), F32)] * 2
    res = pl.pallas_call(
        functools.partial(_gqa_body, T=T, S_ctx=S_ctx, rope=rope, want_kv=want_kv,
                          group=_head_group(S, GQA_HEADS)),
        grid=(B, nq),
        in_specs=in_specs,
        out_specs=out_specs,
        out_shape=out_shape,
        scratch_shapes=([pltpu.VMEM((GQA_KV_HEADS, S, LANES), BF16)] * 2
                        + [pltpu.VMEM((GQA_KV_HEADS, S, 2 * LANES), BF16)] * 2),
        compiler_params=_cparams(("arbitrary", "arbitrary"), VMEM_LIMIT),
        name="gqa",
    )(*args)
    return res if want_kv else (res[0], None, None)


def _rope_tables(n_tok, d_rot, lead, reps):
    t = jnp.arange(n_tok, dtype=jnp.int32)
    posn = jnp.stack([t // GRID_W, t % GRID_W], axis=-1).astype(F32)
    quarter = d_rot // 4
    inv = jnp.power(ROPE_THETA, -jnp.arange(quarter, dtype=F32) / quarter)
    ang = posn[:, :, None] * inv
    cos, sin = jnp.cos(ang), jnp.sin(ang)
    zero = jnp.zeros_like(sin)
    c_rot = jnp.stack([cos, cos], axis=2).reshape(n_tok, d_rot)
    sp_rot = jnp.stack([-sin, zero], axis=2).reshape(n_tok, d_rot)
    sm_rot = jnp.stack([zero, sin], axis=2).reshape(n_tok, d_rot)
    tail = LANES - lead - reps * d_rot

    def embed(rot, fill):
        parts = [jnp.full((n_tok, lead), fill, F32)] + [rot] * reps + [jnp.full((n_tok, tail), fill, F32)]
        return jnp.concatenate(parts, axis=1)

    return embed(c_rot, 1.0), embed(sp_rot, 0.0), embed(sm_rot, 0.0)


def _pad_lanes(x, lead, width=LANES):
    pad = [(0, 0)] * (x.ndim - 1) + [(lead, width - lead - x.shape[-1])]
    return jnp.pad(x, pad)


def _ab_params(w_in, w_out, a_w2, a_b, out_g, q_norm_g, w_qb, kv_norm_g, w_kvb, qn_g, kn_g):
    d = w_in.shape[0]
    o_alo = 2 * GLA_QK + 2 * GLA_V
    o_cq = o_alo + 2 * GLA_RANK
    o_ckv = o_cq + MLA_Q_RANK
    o_kpe = o_ckv + MLA_KV_RANK
    w_kpe = w_in[:, o_kpe:]
    w_perm = jnp.concatenate([
        w_in[:, :o_alo], w_in[:, o_cq:o_ckv], _pad_lanes(w_kpe, MLA_NOPE),
        w_in[:, o_ckv:o_kpe],
        _pad_lanes(jnp.concatenate([w_in[:, o_alo:o_cq], w_kpe], axis=1), 0),
    ], axis=1).astype(BF16)
    w2bd = jnp.zeros((LANES, 2 * GLA_QK), F32)
    w2bd = w2bd.at[:GLA_RANK, :GLA_QK].set(a_w2[0]).at[GLA_RANK:2 * GLA_RANK, GLA_QK:].set(a_w2[1])
    w_qb_p = _pad_lanes(w_qb.reshape(MLA_Q_RANK, MLA_HEADS, MLA_QK), 0).reshape(MLA_Q_RANK, MLA_HEADS * LANES)
    kvb = w_kvb.reshape(MLA_KV_RANK, MLA_HEADS, MLA_NOPE + MLA_V)
    w_kvb_p = jnp.concatenate([
        _pad_lanes(kvb[:, :, :MLA_NOPE], 0).reshape(MLA_KV_RANK, MLA_HEADS * LANES),
        kvb[:, :, MLA_NOPE:].reshape(MLA_KV_RANK, MLA_HEADS * MLA_V)], axis=1)
    return dict(
        w_perm=w_perm,
        w2bd=w2bd.astype(BF16),
        a_b=a_b.reshape(1, 2 * GLA_QK),
        out_g=jnp.tile(out_g, GLA_HEADS).reshape(1, GLA_V),
        mla=(q_norm_g.reshape(1, -1), w_qb_p.astype(BF16),
             _pad_lanes(qn_g * (MLA_QK ** -0.5 * LOG2E), 0).reshape(1, LANES),
             kv_norm_g.reshape(1, -1), w_kvb_p.astype(BF16),
             _pad_lanes(kn_g, 0).reshape(1, LANES)),
        w_out_gla=w_out[:GLA_V].astype(BF16),
        w_out_mla=w_out[GLA_V:].astype(BF16),
    )


AB_OUTS = ((2 * GLA_QK + 2 * GLA_V + MLA_Q_RANK + LANES, BF16), (MLA_KV_RANK, F32), (LANES, F32))
C_OUTS = ((GQA_HEADS * GQA_DH, BF16), (GQA_KV_HEADS * GQA_DH, F32), (GQA_KV_HEADS * GQA_DH, F32))


def kernel(x_prompt, x_sample, c, cache_mla_ckv, cache_mla_kpe, state_gla, cache_gqa_k, cache_gqa_v,
           c_ctx, ada_w, ada_b, norm_mix_g, norm_ffn_g, ffn_w_in, ffn_w_out, ab_w_in, ab_w_out,
           gla_a_w2, gla_a_b, gla_out_g, mla_q_norm_g, mla_w_qb, mla_kv_norm_g, mla_w_kvb, mla_qn_g,
           mla_kn_g, gqa_w_in, gqa_w_out, gqa_qn_g, gqa_kn_g):
    Bp, Tp, D = x_prompt.shape
    Bs, Ts, _ = x_sample.shape
    depth = ada_w.shape[0]
    xp = x_prompt.reshape(Bp * Tp, D)
    xs = x_sample.reshape(Bs * Ts, D)

    cond = jnp.zeros((8, D), F32).at[0].set(c_ctx).at[2:2 + Bs].set(c)
    mods = _adaln(cond, ada_w, ada_b).reshape(depth, 8, 6, D)

    rope_mla = _rope_tables(Ts, MLA_ROPE, MLA_NOPE, 1)
    rope_gqa = _rope_tables(Ts, GQA_DH, 0, 2)

    streams = (dict(mod_base=0, rows_per_cond=None, tm=512),
               dict(mod_base=2, rows_per_cond=Ts, tm=512))
    ffn_tm = (2048, 2048)

    new_ckv, new_kpe, new_gla, new_k, new_v = [], [], [], [], []
    for l in range(depth):
        i = l // 2
        mod = mods[l]
        if l % 2 == 0:
            P = _ab_params(ab_w_in[i], ab_w_out[i], gla_a_w2[i], gla_a_b[i], gla_out_g[i], mla_q_norm_g[i],
                           mla_w_qb[i], mla_kv_norm_g[i], mla_w_kvb[i], mla_qn_g[i], mla_kn_g[i])
            pj_p, ckv_p, alk_p = _inproj(xp, norm_mix_g[l], mod, P['w_perm'], AB_OUTS, **streams[0])
            pj_s, ckv_s, alk_s = _inproj(xs, norm_mix_g[l], mod, P['w_perm'], AB_OUTS, **streams[1])
            og_p, st_p = _gla(pj_p, alk_p, P['w2bd'], P['a_b'], P['out_g'], None, B=Bp, T=Tp, want_state=True)
            og_s, _ = _gla(pj_s, alk_s, P['w2bd'], P['a_b'], P['out_g'], state_gla[:, i], B=Bs, T=Ts,
                           want_state=False)
            om_p, ckvn_p = _mla(pj_p, ckv_p, None, None, P['mla'], B=Bp, T=Tp, tq=Tp, want_ckv=True)
            ctx = (cache_mla_ckv[:, i], _pad_lanes(cache_mla_kpe[:, i], MLA_NOPE))
            om_s, _ = _mla(pj_s, ckv_s, ctx, rope_mla, P['mla'], B=Bs, T=Ts, tq=256, want_ckv=False)
            ws = (P['w_out_gla'], P['w_out_mla'])
            xp = _outproj((og_p, om_p), ws, xp, mod, **streams[0])
            xs = _outproj((og_s, om_s), ws, xs, mod, **streams[1])
            new_ckv.append(ckvn_p.reshape(Bp, Tp, MLA_KV_RANK))
            new_kpe.append(alk_p[:, 2 * GLA_RANK:2 * GLA_RANK + MLA_ROPE].reshape(Bp, Tp, MLA_ROPE))
            new_gla.append(st_p)
        else:
            w_in = gqa_w_in[i].astype(BF16)
            w_out = gqa_w_out[i].astype(BF16)
            gp = (jnp.tile(gqa_qn_g[i] * (GQA_DH ** -0.5 * LOG2E), 2).reshape(1, LANES),
                  jnp.tile(gqa_kn_g[i], 2).reshape(1, LANES))
            q_p, k_p, v_p = _inproj(xp, norm_mix_g[l], mod, w_in, C_OUTS, **streams[0])
            q_s, k_s, v_s = _inproj(xs, norm_mix_g[l], mod, w_in, C_OUTS, **streams[1])
            o_p, kn_p, vo_p = _gqa(q_p, k_p, v_p, None, None, gp, B=Bp, T=Tp, tq=Tp, want_kv=True)
            n_kv = GQA_KV_HEADS * GQA_DH
            ctx = (cache_gqa_k[:, i].reshape(Bs, -1, n_kv), cache_gqa_v[:, i].reshape(Bs, -1, n_kv))
            o_s, _, _ = _gqa(q_s, k_s, v_s, ctx, rope_gqa, gp, B=Bs, T=Ts, tq=256, want_kv=False)
            xp = _outproj((o_p,), (w_out,), xp, mod, **streams[0])
            xs = _outproj((o_s,), (w_out,), xs, mod, **streams[1])
            new_k.append(kn_p.reshape(Bp, Tp, GQA_KV_HEADS, GQA_DH))
            new_v.append(vo_p.reshape(Bp, Tp, GQA_KV_HEADS, GQA_DH))
        xp = _ffn(xp, norm_ffn_g[l], mod, ffn_w_in, ffn_w_out, l, tm=ffn_tm[0],
                  **{k: v for k, v in streams[0].items() if k != 'tm'})
        xs = _ffn(xs, norm_ffn_g[l], mod, ffn_w_in, ffn_w_out, l, tm=ffn_tm[1],
                  **{k: v for k, v in streams[1].items() if k != 'tm'})
    return (xp.reshape(Bp, Tp, D), xs.reshape(Bs, Ts, D), jnp.stack(new_ckv, axis=1),
            jnp.stack(new_kpe, axis=1), jnp.stack(new_gla, axis=1), jnp.stack(new_k, axis=1),
            jnp.stack(new_v, axis=1))
```

```python
import functools

import jax
import jax.numpy as jnp
from jax import lax
from jax.experimental import pallas as pl
from jax.experimental.pallas import tpu as pltpu

F32 = jnp.float32
BF16 = jnp.bfloat16

EPS = 1e-6
ROPE_THETA = 10000.0
GRID_W = 64
LANES = 128
GLA_HEADS, GLA_DK, GLA_DV = 4, 64, 128
GLA_QK = GLA_HEADS * GLA_DK
GLA_V = GLA_HEADS * GLA_DV
GLA_RANK = 16
GLA_TAU = 16.0
GLA_CHUNK = 64
MLA_HEADS = 8
MLA_Q_RANK, MLA_KV_RANK = 384, 256
MLA_NOPE, MLA_ROPE, MLA_V = 64, 32, 64
MLA_QK = MLA_NOPE + MLA_ROPE
GQA_HEADS, GQA_KV_HEADS, GQA_DH = 16, 4, 64
VMEM_LIMIT = 56 << 20


def _cparams(sem, vmem=None):
    return pltpu.CompilerParams(dimension_semantics=sem, vmem_limit_bytes=vmem)


def _nt_dot(a, b):
    return lax.dot_general(a, b, (((1,), (1,)), ((), ())), preferred_element_type=F32)


def _tn_dot(a, b):
    return lax.dot_general(a, b, (((0,), (0,)), ((), ())), preferred_element_type=F32)


def _ds(start, size):
    if isinstance(start, int):
        return pl.ds(start, size)
    return pl.ds(pl.multiple_of(start, size), size)


def _rms(x, g):
    ms = jnp.mean(x * x, axis=-1, keepdims=True)
    return x * lax.rsqrt(ms + EPS) * g


def _silu(x):
    return x * jax.nn.sigmoid(x)


def _rope(x, c, sp, sm, shift):
    return x * c + pltpu.roll(x, LANES - shift, 1) * sp + pltpu.roll(x, shift, 1) * sm


def _adaln_body(c_ref, w_ref, b_ref, o_ref):
    c = c_ref[...]
    s = _silu(c).astype(BF16)
    o_ref[0] = jnp.dot(s, w_ref[0].astype(BF16), preferred_element_type=F32) + b_ref[0]


def _adaln(cond, ada_w, ada_b):
    L, D, E = ada_w.shape
    tn = 1536
    return pl.pallas_call(
        _adaln_body,
        grid=(L, E // tn),
        in_specs=[pl.BlockSpec((8, D), lambda l, n: (0, 0)),
                  pl.BlockSpec((1, D, tn), lambda l, n: (l, 0, n)),
                  pl.BlockSpec((1, 1, tn), lambda l, n: (l, 0, n))],
        out_specs=pl.BlockSpec((1, 8, tn), lambda l, n: (l, 0, n)),
        out_shape=jax.ShapeDtypeStruct((L, 8, E), F32),
        compiler_params=_cparams(("arbitrary", "arbitrary")),
        name="adaln",
    )(cond, ada_w, ada_b.reshape(L, 1, E))


def _mod_index(mod_base, rows_per_cond, tm):
    if rows_per_cond is None:
        return lambda i: (mod_base, 0, 0)
    n_mod = max(tm // rows_per_cond, 1)
    return lambda i: ((mod_base + (i * tm) // rows_per_cond) // n_mod, 0, 0)


def _inproj_body(x_ref, g_ref, mod_ref, w_ref, *o_refs, widths):
    h = _rms(x_ref[...], g_ref[...])
    h = h * (1.0 + mod_ref[0, 1:2, :]) + mod_ref[0, 0:1, :]
    acc = jnp.dot(h.astype(BF16), w_ref[...], preferred_element_type=F32)
    off = 0
    for o_ref, w in zip(o_refs, widths):
        o_ref[...] = acc[:, off:off + w].astype(o_ref.dtype)
        off += w


def _inproj(x, g, mod, w, outs, *, mod_base, rows_per_cond, tm):
    n, d = x.shape
    widths = tuple(o[0] for o in outs)
    return pl.pallas_call(
        functools.partial(_inproj_body, widths=widths),
        grid=(n // tm,),
        in_specs=[pl.BlockSpec((tm, d), lambda i: (i, 0)),
                  pl.BlockSpec((1, d), lambda i: (0, 0)),
                  pl.BlockSpec((1, 6, d), _mod_index(mod_base, rows_per_cond, tm)),
                  pl.BlockSpec(w.shape, lambda i: (0, 0))],
        out_specs=[pl.BlockSpec((tm, wd), lambda i: (i, 0)) for wd in widths],
        out_shape=[jax.ShapeDtypeStruct((n, wd), dt) for wd, dt in outs],
        compiler_params=_cparams(("arbitrary",), VMEM_LIMIT),
        name="inproj",
    )(x, g.reshape(1, d), mod, w)


def _ffn_body(*refs, n_in, nk, row_chunk, rows_per_cond):
    a_refs, wp_refs = refs[:n_in], refs[n_in:2 * n_in]
    x_ref, g_ref, mod_ref, wg_ref, wu_ref, wo_ref, o_ref, h_ref, wgu_ref, wob_ref = refs[2 * n_in:]
    k = pl.program_id(1)
    tm = x_ref.shape[0]
    tk = wg_ref.shape[2]
    n_chunks = tm // row_chunk

    def mod_row(c):
        return 0 if rows_per_cond is None else (c * row_chunk) // rows_per_cond

    @pl.when(k == 0)
    def _():
        def mix_rows(c, carry):
            rows, m = _ds(c * row_chunk, row_chunk), mod_row(c)
            acc = None
            for a_ref, wp_ref in zip(a_refs, wp_refs):
                part = jnp.dot(a_ref[rows, :], wp_ref[...], preferred_element_type=F32)
                acc = part if acc is None else acc + part
            x_new = x_ref[rows, :] + mod_ref[m, 2:3, :] * acc
            o_ref[rows, :] = x_new
            h = _rms(x_new, g_ref[...])
            h = h * (1.0 + mod_ref[m, 4:5, :]) + mod_ref[m, 3:4, :]
            h_ref[rows, :] = h.astype(BF16)
            return carry
        lax.fori_loop(0, n_chunks, mix_rows, 0)

    wgu_ref[:, :tk] = wg_ref[0].astype(BF16)
    wgu_ref[:, tk:] = wu_ref[0].astype(BF16)
    wob_ref[...] = wo_ref[0].astype(BF16)

    for c in range(n_chunks):
        rows, m = _ds(c * row_chunk, row_chunk), mod_row(c)
        gu = jnp.dot(h_ref[rows, :], wgu_ref[...], preferred_element_type=F32)
        a = (_silu(gu[:, :tk]) * gu[:, tk:]).astype(BF16)
        o_ref[rows, :] += mod_ref[m, 5:6, :] * jnp.dot(a, wob_ref[...], preferred_element_type=F32)


def _mix_ffn(acts, wps, x, g, mod, w_in, w_out, layer, *, mod_base, rows_per_cond, tm):
    n, d = x.shape
    hidden = w_out.shape[1]
    tk = 256
    nk = hidden // tk
    n_in = len(acts)
    n_mod = 1 if rows_per_cond is None else max(tm // rows_per_cond, 1)
    once = dict(pipeline_mode=pl.Buffered(1))
    return pl.pallas_call(
        functools.partial(_ffn_body, n_in=n_in, nk=nk, row_chunk=512, rows_per_cond=rows_per_cond),
        grid=(n // tm, nk),
        in_specs=([pl.BlockSpec((tm, a.shape[1]), lambda i, k: (i, 0), **once) for a in acts]
                  + [pl.BlockSpec(w.shape, lambda i, k: (0, 0), **once) for w in wps]
                  + [pl.BlockSpec((tm, d), lambda i, k: (i, 0), **once),
                     pl.BlockSpec((1, d), lambda i, k: (0, 0)),
                     pl.BlockSpec((n_mod, 6, d),
                                  (lambda f: (lambda i, k: f(i)))(_mod_index(mod_base, rows_per_cond, tm))),
                     pl.BlockSpec((1, d, tk), lambda i, k: (layer, 0, k)),
                     pl.BlockSpec((1, d, tk), lambda i, k: (layer, 0, nk + k)),
                     pl.BlockSpec((1, tk, d), lambda i, k: (layer, k, 0))]),
        out_specs=pl.BlockSpec((tm, d), lambda i, k: (i, 0)),
        out_shape=jax.ShapeDtypeStruct((n, d), F32),
        scratch_shapes=[pltpu.VMEM((tm, d), BF16),
                        pltpu.VMEM((d, 2 * tk), BF16),
                        pltpu.VMEM((tk, d), BF16)],
        compiler_params=_cparams(("arbitrary", "arbitrary"), VMEM_LIMIT),
        name="mix_ffn",
    )(*acts, *wps, x, g.reshape(1, d), mod, w_in, w_in, w_out)


def _log_sigmoid(x):
    return jnp.minimum(x, 0.0) - jnp.log1p(jnp.exp(-jnp.abs(x)))


def _gla_body(*refs, T, has_s0, want_state):
    qkvr_ref, alk_ref, w2_ref, ab_ref, og_ref = refs[:5]
    pos = 5
    s0_ref = None
    if has_s0:
        s0_ref = refs[pos]
        pos += 1
    o_ref = refs[pos]
    pos += 1
    st_out_ref = None
    if want_state:
        st_out_ref = refs[pos]
        pos += 1
    osc_ref, st_ref, la_ref, qin_ref, kst_ref, dec_ref, upd_ref, stq_ref = refs[pos:pos + 8]

    C = GLA_CHUNK
    nc = T // C

    logit = jnp.dot(alk_ref[...].astype(BF16), w2_ref[...], preferred_element_type=F32) + ab_ref[...]
    la_ref[...] = _log_sigmoid(logit) * (1.0 / GLA_TAU)

    for d in range(2):
        if has_s0:
            st_ref[d] = jnp.concatenate([s0_ref[0, d, h] for h in range(GLA_HEADS)], axis=0).T
        else:
            st_ref[d] = jnp.zeros((GLA_DV, GLA_QK), F32)

    r64 = lax.broadcasted_iota(jnp.int32, (C, C), 0)
    c64 = lax.broadcasted_iota(jnp.int32, (C, C), 1)
    tri_f = (r64 >= c64).astype(BF16)
    tri_b = (c64 >= r64).astype(BF16)
    t_idx = lax.broadcasted_iota(jnp.int32, (C, GLA_QK), 0)
    s_idx = lax.broadcasted_iota(jnp.int32, (C, GLA_QK), 1) % C
    causal_f = t_idx >= s_idx
    causal_b = t_idx <= s_idx
    bm_k = (lax.broadcasted_iota(jnp.int32, (GLA_QK, GLA_QK), 0) // GLA_DK
            == lax.broadcasted_iota(jnp.int32, (GLA_QK, GLA_QK), 1) // GLA_DK)
    bm_v = (lax.broadcasted_iota(jnp.int32, (GLA_QK, GLA_V), 0) // C
            == lax.broadcasted_iota(jnp.int32, (GLA_QK, GLA_V), 1) // GLA_DV)
    head_lanes = [lax.broadcasted_iota(jnp.int32, (1, GLA_QK), 1) // GLA_DK == h for h in range(GLA_HEADS)]
    directions = ((tri_f, causal_f, C // 2 - 1, C - 1), (tri_b, causal_b, C // 2, 0))

    def v_rows(rows):
        return qkvr_ref[rows, 2 * GLA_QK:2 * GLA_QK + GLA_V]

    G = 4
    units = [(i, d) for i in range(G) for d in range(2)]

    def for_groups(fn):
        if nc == G:
            fn(0)
        else:
            def body(g, carry):
                fn(g * G)
                return carry
            lax.fori_loop(0, nc // G, body, 0)

    def intra_group(n0):
        rows = [_ds((n0 + i) * C, C) for i in range(G)]
        b = {}
        for i, d in units:
            b[i, d] = _split_dot(la_ref[rows[i], d * GLA_QK:(d + 1) * GLA_QK], directions[d][0], w_left=True)
        q_loc, k_bd = {}, {}
        for i in range(G):
            qc = qkvr_ref[rows[i], 0:GLA_QK].astype(F32) * (GLA_DK ** -0.5)
            kc = qkvr_ref[rows[i], GLA_QK:2 * GLA_QK].astype(F32)
            for d in range(2):
                _, _, ref_row, last_row = directions[d]
                bb = b[i, d]
                b_ref = bb[ref_row:ref_row + 1, :]
                b_last = bb[last_row:last_row + 1, :]
                q_loc[i, d] = (qc * jnp.exp(bb - b_ref)).astype(BF16)
                k_loc = kc * jnp.exp(b_ref - bb)
                k_bd[i, d] = jnp.where(bm_k, jnp.concatenate([k_loc] * GLA_HEADS, axis=0), 0.0).astype(BF16)
                qin_ref[d, rows[i], :] = (qc * jnp.exp(bb)).astype(BF16)
                kst_ref[d, rows[i], :] = (kc * jnp.exp(b_last - bb)).astype(BF16)
                dec_ref[d, _ds((n0 + i) * 8, 8), :] = jnp.broadcast_to(jnp.exp(b_last), (8, GLA_QK))
        a = {u: jnp.where(directions[u[1]][1], _nt_dot(q_loc[u], k_bd[u]), 0.0).astype(BF16) for u in units}
        v_bd = [jnp.where(bm_v, jnp.concatenate([v_rows(rows[i])] * GLA_HEADS, axis=0), jnp.zeros((), BF16))
                for i in range(G)]
        o = {u: jnp.dot(a[u], v_bd[u[0]], preferred_element_type=F32) for u in units}
        for i in range(G):
            osc_ref[rows[i], :] = o[i, 0] + o[i, 1]

    def update_group(n0):
        rows = [_ds((n0 + i) * C, C) for i in range(G)]
        upd = {u: _tn_dot(v_rows(rows[u[0]]), kst_ref[u[1], rows[u[0]], :]) for u in units}
        for i, d in units:
            acc = None
            for h, m in enumerate(head_lanes):
                part = jnp.where(m, upd[i, d][h * GLA_DV:(h + 1) * GLA_DV, :], 0.0)
                acc = part if acc is None else acc + part
            upd_ref[d, n0 + i] = acc

    def scan_step(i, carry):
        for d, n in ((0, i), (1, nc - 1 - i)):
            st = st_ref[d]
            stq_ref[d, n] = st.astype(BF16)
            st_ref[d] = st * dec_ref[d, pl.ds(n * 8, 1), :] + upd_ref[d, n]
        return carry

    def readout_group(n0):
        rows = [_ds((n0 + i) * C, C) for i in range(G)]
        o = {}
        for i, d in units:
            q_in = qin_ref[d, rows[i], :]
            q_heads = jnp.concatenate([jnp.where(m, q_in, jnp.zeros((), BF16)) for m in head_lanes], axis=0)
            o[i, d] = _nt_dot(q_heads, stq_ref[d, n0 + i])
        for i in range(G):
            both = o[i, 0] + o[i, 1]
            for h in range(GLA_HEADS):
                osc_ref[rows[i], h * GLA_DV:(h + 1) * GLA_DV] += both[h * C:(h + 1) * C, :]

    for_groups(intra_group)
    for_groups(update_group)
    if nc == G:
        for i in range(nc):
            scan_step(i, 0)
    else:
        lax.fori_loop(0, nc, scan_step, 0)
    for_groups(readout_group)

    rb = 256
    for i in range(T // rb):
        rows = slice(i * rb, (i + 1) * rb)
        for h in range(GLA_HEADS):
            cols = slice(h * GLA_DV, (h + 1) * GLA_DV)
            o = _rms(osc_ref[rows, cols], og_ref[:, cols])
            r = qkvr_ref[rows, 2 * GLA_QK + GLA_V + h * GLA_DV:2 * GLA_QK + GLA_V + (h + 1) * GLA_DV].astype(F32)
            o_ref[rows, cols] = (o * _silu(r)).astype(o_ref.dtype)

    if want_state:
        for d in range(2):
            s_all = st_ref[d].T
            for h in range(GLA_HEADS):
                st_out_ref[0, d, h] = s_all[h * GLA_DK:(h + 1) * GLA_DK, :]


def _gla(proj, alk, w2bd, a_b, out_g, s0, *, B, T, want_state):
    has_s0 = s0 is not None
    n_qkvr = 2 * GLA_QK + 2 * GLA_V
    in_specs = [pl.BlockSpec((T, n_qkvr), lambda b: (b, 0)),
                pl.BlockSpec((T, LANES), lambda b: (b, 0)),
                pl.BlockSpec(w2bd.shape, lambda b: (0, 0)),
                pl.BlockSpec((1, 2 * GLA_QK), lambda b: (0, 0)),
                pl.BlockSpec((1, GLA_V), lambda b: (0, 0))]
    args = [proj, alk, w2bd, a_b, out_g]
    st_spec = pl.BlockSpec((1, 2, GLA_HEADS, GLA_DK, GLA_DV), lambda b: (b, 0, 0, 0, 0))
    if has_s0:
        in_specs.append(st_spec)
        args.append(s0)
    out_specs = [pl.BlockSpec((T, GLA_V), lambda b: (b, 0))]
    out_shape = [jax.ShapeDtypeStruct((B * T, GLA_V), BF16)]
    if want_state:
        out_specs.append(st_spec)
        out_shape.append(jax.ShapeDtypeStruct((B, 2, GLA_HEADS, GLA_DK, GLA_DV), F32))
    res = pl.pallas_call(
        functools.partial(_gla_body, T=T, has_s0=has_s0, want_state=want_state),
        grid=(B,),
        in_specs=in_specs,
        out_specs=out_specs,
        out_shape=out_shape,
        scratch_shapes=[pltpu.VMEM((T, GLA_V), F32),
                        pltpu.VMEM((2, GLA_DV, GLA_QK), F32),
                        pltpu.VMEM((T, 2 * GLA_QK), F32),
                        pltpu.VMEM((2, T, GLA_QK), BF16),
                        pltpu.VMEM((2, T, GLA_QK), BF16),
                        pltpu.VMEM((2, 8 * (T // GLA_CHUNK), GLA_QK), F32),
                        pltpu.VMEM((2, T // GLA_CHUNK, GLA_DV, GLA_QK), F32),
                        pltpu.VMEM((2, T // GLA_CHUNK, GLA_DV, GLA_QK), BF16)],
        compiler_params=_cparams(("arbitrary",), VMEM_LIMIT),
        name="gla",
    )(*args)
    return res if want_state else (res[0], None)


LOG2E = 1.4426950408889634


def _attend(qs, ks, vs, group):
    outs = []
    for g0 in range(0, len(qs), group):
        sl = slice(g0, g0 + group)
        scores = [_nt_dot(q, k()) for q, k in zip(qs[sl], ks[sl])]
        probs = [jnp.exp2(s - jnp.max(s, axis=-1, keepdims=True)).astype(BF16) for s in scores]
        res = [jnp.dot(p, v(), preferred_element_type=F32) for p, v in zip(probs, vs[sl])]
        outs += [r[:, :LANES] / r[:, LANES:] for r in res]
    return outs


def _head_group(n_keys, n_heads):
    return n_heads if n_keys <= 512 else 2


def _stacked_ms(xs, ones_mat, inv_n):
    n = xs[0].shape[0]
    ms = _split_dot(jnp.concatenate([x * x for x in xs], axis=0), ones_mat) * inv_n
    return [ms[i * n:(i + 1) * n] for i in range(len(xs))]


def _split_dot(x, w, w_left=False):
    hi = x.astype(BF16)
    lo = (x - hi.astype(F32)).astype(BF16)
    if w_left:
        return jnp.dot(w, hi, preferred_element_type=F32) + jnp.dot(w, lo, preferred_element_type=F32)
    return jnp.dot(hi, w, preferred_element_type=F32) + jnp.dot(lo, w, preferred_element_type=F32)


def _group_ones(group):
    r = lax.broadcasted_iota(jnp.int32, (LANES, LANES), 0) // group
    c = lax.broadcasted_iota(jnp.int32, (LANES, LANES), 1) // group
    return (r == c).astype(BF16)


def _mla_body(*refs, T, S_ctx, rope, want_ckv, group):
    cq_ref, kpe_ref, ckv_ref = refs[:3]
    pos = 3
    if S_ctx:
        ckvc_ref, kpec_ref = refs[pos:pos + 2]
        pos += 2
    if rope:
        rq_refs = refs[pos:pos + 3]
        rk_refs = refs[pos + 3:pos + 6]
        pos += 6
    qng_ref, wqb_ref, qg_ref, kvg_ref, wkvb_ref, kg_ref = refs[pos:pos + 6]
    pos += 6
    o_ref = refs[pos]
    pos += 1
    if want_ckv:
        ckvn_ref = refs[pos]
        pos += 1
    k_sc, vlo_sc, vhi_sc = refs[pos:pos + 3]

    H = MLA_HEADS
    n_k = H * LANES
    lo = lax.broadcasted_iota(jnp.int32, (1, LANES), 1) < MLA_V

    def head_ms(xs):
        return [jnp.sum(x * x, axis=-1, keepdims=True) * (1.0 / MLA_QK) for x in xs]

    def fill_kv(ckvn, kpe, row0, n_rows, with_rope):
        rows = slice(row0, row0 + n_rows)
        kv = jnp.dot(ckvn.astype(BF16), wkvb_ref[...], preferred_element_type=F32)
        khs = [kv[:, h * LANES:(h + 1) * LANES] + kpe for h in range(H)]
        for h, (kh, ms) in enumerate(zip(khs, head_ms(khs))):
            kh = kh * lax.rsqrt(ms + EPS) * kg_ref[...]
            if with_rope:
                kh = _rope(kh, rk_refs[0][...], rk_refs[1][...], rk_refs[2][...], MLA_ROPE // 4)
            k_sc[rows, h * LANES:(h + 1) * LANES] = kh.astype(BF16)
        ones = jnp.ones((n_rows, LANES), BF16)
        for p in range(H // 2):
            v = kv[:, n_k + p * LANES:n_k + (p + 1) * LANES]
            vlo_sc[rows, 2 * p * LANES:(2 * p + 1) * LANES] = jnp.where(lo, v, 0.0).astype(BF16)
            vhi_sc[rows, 2 * p * LANES:(2 * p + 1) * LANES] = jnp.where(lo, 0.0, v).astype(BF16)
            vlo_sc[rows, (2 * p + 1) * LANES:(2 * p + 2) * LANES] = ones
            vhi_sc[rows, (2 * p + 1) * LANES:(2 * p + 2) * LANES] = ones

    @pl.when(pl.program_id(1) == 0)
    def _():
        if S_ctx:
            fill_kv(ckvc_ref[0], kpec_ref[0], 0, S_ctx, False)
        ckvn = _rms(ckv_ref[...], kvg_ref[...])
        if want_ckv:
            ckvn_ref[...] = ckvn
        fill_kv(ckvn, kpe_ref[...].astype(F32), S_ctx, T, rope)

    cqn = _rms(cq_ref[...].astype(F32), qng_ref[...])
    q = jnp.dot(cqn.astype(BF16), wqb_ref[...], preferred_element_type=F32)
    qhs = [q[:, h * LANES:(h + 1) * LANES] for h in range(H)]
    qs = []
    for qh, ms in zip(qhs, head_ms(qhs)):
        qh = qh * lax.rsqrt(ms + EPS) * qg_ref[...]
        if rope:
            qh = _rope(qh, rq_refs[0][...], rq_refs[1][...], rq_refs[2][...], MLA_ROPE // 4)
        qs.append(qh.astype(BF16))
    ks = [functools.partial(lambda h: k_sc[:, h * LANES:(h + 1) * LANES], h) for h in range(H)]
    vs = [functools.partial(lambda h: (vlo_sc, vhi_sc)[h % 2][:, (h // 2) * 2 * LANES:(h // 2 + 1) * 2 * LANES], h)
          for h in range(H)]
    outs = _attend(qs, ks, vs, group)
    for p in range(H // 2):
        o_ref[:, p * LANES:(p + 1) * LANES] = (outs[2 * p] + outs[2 * p + 1]).astype(o_ref.dtype)


def _mla(proj, ckv, ctx, rope_tabs, params, *, B, T, tq, want_ckv):
    nq = T // tq
    S_ctx = ctx[0].shape[1] if ctx is not None else 0
    S = S_ctx + T
    rope = rope_tabs is not None
    cq_blk = (2 * GLA_QK + 2 * GLA_V) // MLA_Q_RANK
    kpe_blk = (2 * GLA_QK + 2 * GLA_V + MLA_Q_RANK) // LANES
    in_specs = [pl.BlockSpec((tq, MLA_Q_RANK), lambda b, j: (b * nq + j, cq_blk)),
                pl.BlockSpec((T, LANES), lambda b, j: (b, kpe_blk)),
                pl.BlockSpec((T, MLA_KV_RANK), lambda b, j: (b, 0))]
    args = [proj, proj, ckv]
    if S_ctx:
        in_specs += [pl.BlockSpec((1, S_ctx, MLA_KV_RANK), lambda b, j: (b, 0, 0)),
                     pl.BlockSpec((1, S_ctx, LANES), lambda b, j: (b, 0, 0))]
        args += list(ctx)
    if rope:
        in_specs += [pl.BlockSpec((tq, LANES), lambda b, j: (j, 0))] * 3
        in_specs += [pl.BlockSpec((T, LANES), lambda b, j: (0, 0))] * 3
        args += list(rope_tabs) * 2
    in_specs += [pl.BlockSpec(p.shape, lambda b, j: (0, 0)) for p in params]
    args += list(params)
    out_specs = [pl.BlockSpec((tq, MLA_HEADS * MLA_V), lambda b, j: (b * nq + j, 0))]
    out_shape = [jax.ShapeDtypeStruct((B * T, MLA_HEADS * MLA_V), BF16)]
    if want_ckv:
        out_specs.append(pl.BlockSpec((T, MLA_KV_RANK), lambda b, j: (b, 0)))
        out_shape.append(jax.ShapeDtypeStruct((B * T, MLA_KV_RANK), F32))
    res = pl.pallas_call(
        functools.partial(_mla_body, T=T, S_ctx=S_ctx, rope=rope, want_ckv=want_ckv,
                          group=_head_group(S, MLA_HEADS)),
        grid=(B, nq),
        in_specs=in_specs,
        out_specs=out_specs,
        out_shape=out_shape,
        scratch_shapes=[pltpu.VMEM((S, MLA_HEADS * LANES), BF16),
                        pltpu.VMEM((S, MLA_HEADS * LANES), BF16),
                        pltpu.VMEM((S, MLA_HEADS * LANES), BF16)],
        compiler_params=_cparams(("arbitrary", "arbitrary"), VMEM_LIMIT),
        name="mla",
    )(*args)
    return res if want_ckv else (res[0], None)


def _gqa_body(*refs, T, S_ctx, rope, want_kv, group):
    q_ref, k_ref, v_ref = refs[:3]
    pos = 3
    if S_ctx:
        kc_ref, vc_ref = refs[pos:pos + 2]
        pos += 2
    if rope:
        rq_refs = refs[pos:pos + 3]
        rk_refs = refs[pos + 3:pos + 6]
        pos += 6
    qg_ref, kg_ref = refs[pos:pos + 2]
    pos += 2
    o_ref = refs[pos]
    pos += 1
    if want_kv:
        kn_ref, vo_ref = refs[pos:pos + 2]
        pos += 2
    klo_sc, khi_sc, vlo_sc, vhi_sc = refs[pos:pos + 4]

    lo = lax.broadcasted_iota(jnp.int32, (1, LANES), 1) < GQA_DH
    half_sum = _group_ones(GQA_DH)

    def head_ms(x):
        return _split_dot(x * x, half_sum) * (1.0 / GQA_DH)

    def scatter_halves(x, lo_sc, hi_sc, c, rows):
        rolled = pltpu.roll(x, GQA_DH, 1)
        lo_sc[2 * c, rows, :LANES] = jnp.where(lo, x, 0.0).astype(BF16)
        hi_sc[2 * c, rows, :LANES] = jnp.where(lo, 0.0, rolled).astype(BF16)
        lo_sc[2 * c + 1, rows, :LANES] = jnp.where(lo, rolled, 0.0).astype(BF16)
        hi_sc[2 * c + 1, rows, :LANES] = jnp.where(lo, 0.0, x).astype(BF16)

    def fill_ones(rows):
        ones = jnp.ones((rows.stop - rows.start, LANES), BF16)
        for g in range(GQA_KV_HEADS):
            vlo_sc[g, rows, LANES:] = ones
            vhi_sc[g, rows, LANES:] = ones

    @pl.when(pl.program_id(1) == 0)
    def _():
        fill_ones(slice(0, S_ctx + T))
        for c in range(GQA_KV_HEADS // 2):
            cols = slice(c * LANES, (c + 1) * LANES)
            if S_ctx:
                scatter_halves(kc_ref[0, :, cols], klo_sc, khi_sc, c, slice(0, S_ctx))
                scatter_halves(vc_ref[0, :, cols], vlo_sc, vhi_sc, c, slice(0, S_ctx))
            kx = k_ref[:, cols]
            kn = kx * lax.rsqrt(head_ms(kx) + EPS) * kg_ref[...]
            vx = v_ref[:, cols]
            if want_kv:
                kn_ref[:, cols] = kn
                vo_ref[:, cols] = vx
            if rope:
                kn = _rope(kn, rk_refs[0][...], rk_refs[1][...], rk_refs[2][...], GQA_DH // 4)
            scatter_halves(kn, klo_sc, khi_sc, c, slice(S_ctx, S_ctx + T))
            scatter_halves(vx, vlo_sc, vhi_sc, c, slice(S_ctx, S_ctx + T))

    n_pairs = GQA_HEADS // 2
    qxs = [q_ref[:, p * LANES:(p + 1) * LANES].astype(F32) for p in range(n_pairs)]
    qs, ks, vs = [], [], []
    for p, (qx, ms) in enumerate(zip(qxs, _stacked_ms(qxs, half_sum, 1.0 / GQA_DH))):
        qn = qx * lax.rsqrt(ms + EPS) * qg_ref[...]
        if rope:
            qn = _rope(qn, rq_refs[0][...], rq_refs[1][...], rq_refs[2][...], GQA_DH // 4)
        g = p // 2
        qs += [qn.astype(BF16)] * 2
        ks += [functools.partial(lambda r, g: r[g], r, g) for r in (klo_sc, khi_sc)]
        vs += [functools.partial(lambda r, g: r[g], r, g) for r in (vlo_sc, vhi_sc)]
    outs = _attend(qs, ks, vs, group)
    for p in range(n_pairs):
        o_ref[:, p * LANES:(p + 1) * LANES] = (outs[2 * p] + outs[2 * p + 1]).astype(o_ref.dtype)


def _gqa(q, k, v, ctx, rope_tabs, params, *, B, T, tq, want_kv):
    nq = T // tq
    S_ctx = ctx[0].shape[1] if ctx is not None else 0
    S = S_ctx + T
    rope = rope_tabs is not None
    n_q = GQA_HEADS * GQA_DH
    n_kv = GQA_KV_HEADS * GQA_DH
    in_specs = [pl.BlockSpec((tq, n_q), lambda b, j: (b * nq + j, 0)),
                pl.BlockSpec((T, n_kv), lambda b, j: (b, 0)),
                pl.BlockSpec((T, n_kv), lambda b, j: (b, 0))]
    args = [q, k, v]
    if S_ctx:
        in_specs += [pl.BlockSpec((1, S_ctx, n_kv), lambda b, j: (b, 0, 0))] * 2
        args += list(ctx)
    if rope:
        in_specs += [pl.BlockSpec((tq, LANES), lambda b, j: (j, 0))] * 3
        in_specs += [pl.BlockSpec((T, LANES), lambda b, j: (0, 0))] * 3
        args += list(rope_tabs) * 2
    in_specs += [pl.BlockSpec(p.shape, lambda b, j: (0, 0)) for p in params]
    args += list(params)
    out_specs = [pl.BlockSpec((tq, n_q), lambda b, j: (b * nq + j, 0))]
    out_shape = [jax.ShapeDtypeStruct((B * T, n_q), BF16)]
    if want_kv:
        out_specs += [pl.BlockSpec((T, n_kv), lambda b, j: (b, 0))] * 2
        out_shape += [jax.ShapeDtypeStruct((B * T, n_kv), F32)] * 2
    res = pl.pallas_call(
        functools.partial(_gqa_body, T=T, S_ctx=S_ctx, rope=rope, want_kv=want_kv,
                          group=_head_group(S, GQA_HEADS)),
        grid=(B, nq),
        in_specs=in_specs,
        out_specs=out_specs,
        out_shape=out_shape,
        scratch_shapes=([pltpu.VMEM((GQA_KV_HEADS, S, LANES), BF16)] * 2
                        + [pltpu.VMEM((GQA_KV_HEADS, S, 2 * LANES), BF16)] * 2),
        compiler_params=_cparams(("arbitrary", "arbitrary"), VMEM_LIMIT),
        name="gqa",
    )(*args)
    return res if want_kv else (res[0], None, None)


def _rope_tables(n_tok, d_rot, lead, reps):
    t = jnp.arange(n_tok, dtype=jnp.int32)
    posn = jnp.stack([t // GRID_W, t % GRID_W], axis=-1).astype(F32)
    quarter = d_rot // 4
    inv = jnp.power(ROPE_THETA, -jnp.arange(quarter, dtype=F32) / quarter)
    ang = posn[:, :, None] * inv
    cos, sin = jnp.cos(ang), jnp.sin(ang)
    zero = jnp.zeros_like(sin)
    c_rot = jnp.stack([cos, cos], axis=2).reshape(n_tok, d_rot)
    sp_rot = jnp.stack([-sin, zero], axis=2).reshape(n_tok, d_rot)
    sm_rot = jnp.stack([zero, sin], axis=2).reshape(n_tok, d_rot)
    tail = LANES - lead - reps * d_rot

    def embed(rot, fill):
        parts = [jnp.full((n_tok, lead), fill, F32)] + [rot] * reps + [jnp.full((n_tok, tail), fill, F32)]
        return jnp.concatenate(parts, axis=1)

    return embed(c_rot, 1.0), embed(sp_rot, 0.0), embed(sm_rot, 0.0)


def _pad_lanes(x, lead, width=LANES):
    pad = [(0, 0)] * (x.ndim - 1) + [(lead, width - lead - x.shape[-1])]
    return jnp.pad(x, pad)


def _ab_params(w_in, w_out, a_w2, a_b, out_g, q_norm_g, w_qb, kv_norm_g, w_kvb, qn_g, kn_g):
    d = w_in.shape[0]
    o_alo = 2 * GLA_QK + 2 * GLA_V
    o_cq = o_alo + 2 * GLA_RANK
    o_ckv = o_cq + MLA_Q_RANK
    o_kpe = o_ckv + MLA_KV_RANK
    w_kpe = w_in[:, o_kpe:]
    w_perm = jnp.concatenate([
        w_in[:, :o_alo], w_in[:, o_cq:o_ckv], _pad_lanes(w_kpe, MLA_NOPE),
        w_in[:, o_ckv:o_kpe],
        _pad_lanes(jnp.concatenate([w_in[:, o_alo:o_cq], w_kpe], axis=1), 0),
    ], axis=1).astype(BF16)
    w2bd = jnp.zeros((LANES, 2 * GLA_QK), F32)
    w2bd = w2bd.at[:GLA_RANK, :GLA_QK].set(a_w2[0]).at[GLA_RANK:2 * GLA_RANK, GLA_QK:].set(a_w2[1])
    w_qb_p = _pad_lanes(w_qb.reshape(MLA_Q_RANK, MLA_HEADS, MLA_QK), 0).reshape(MLA_Q_RANK, MLA_HEADS * LANES)
    kvb = w_kvb.reshape(MLA_KV_RANK, MLA_HEADS, MLA_NOPE + MLA_V)
    w_kvb_p = jnp.concatenate([
        _pad_lanes(kvb[:, :, :MLA_NOPE], 0).reshape(MLA_KV_RANK, MLA_HEADS * LANES),
        kvb[:, :, MLA_NOPE:].reshape(MLA_KV_RANK, MLA_HEADS * MLA_V)], axis=1)
    return dict(
        w_perm=w_perm,
        w2bd=w2bd.astype(BF16),
        a_b=a_b.reshape(1, 2 * GLA_QK),
        out_g=jnp.tile(out_g, GLA_HEADS).reshape(1, GLA_V),
        mla=(q_norm_g.reshape(1, -1), w_qb_p.astype(BF16),
             _pad_lanes(qn_g * (MLA_QK ** -0.5 * LOG2E), 0).reshape(1, LANES),
             kv_norm_g.reshape(1, -1), w_kvb_p.astype(BF16),
             _pad_lanes(kn_g, 0).reshape(1, LANES)),
        w_out_gla=w_out[:GLA_V].astype(BF16),
        w_out_mla=w_out[GLA_V:].astype(BF16),
    )


AB_OUTS = ((2 * GLA_QK + 2 * GLA_V + MLA_Q_RANK + LANES, BF16), (MLA_KV_RANK, F32), (LANES, F32))
C_OUTS = ((GQA_HEADS * GQA_DH, BF16), (GQA_KV_HEADS * GQA_DH, F32), (GQA_KV_HEADS * GQA_DH, F32))


def kernel(x_prompt, x_sample, c, cache_mla_ckv, cache_mla_kpe, state_gla, cache_gqa_k, cache_gqa_v,
           c_ctx, ada_w, ada_b, norm_mix_g, norm_ffn_g, ffn_w_in, ffn_w_out, ab_w_in, ab_w_out,
           gla_a_w2, gla_a_b, gla_out_g, mla_q_norm_g, mla_w_qb, mla_kv_norm_g, mla_w_kvb, mla_qn_g,
           mla_kn_g, gqa_w_in, gqa_w_out, gqa_qn_g, gqa_kn_g):
    Bp, Tp, D = x_prompt.shape
    Bs, Ts, _ = x_sample.shape
    depth = ada_w.shape[0]
    xp = x_prompt.reshape(Bp * Tp, D)
    xs = x_sample.reshape(Bs * Ts, D)

    cond = jnp.zeros((8, D), F32).at[0].set(c_ctx).at[2:2 + Bs].set(c)
    mods = _adaln(cond, ada_w, ada_b).reshape(depth, 8, 6, D)

    rope_mla = _rope_tables(Ts, MLA_ROPE, MLA_NOPE, 1)
    rope_gqa = _rope_tables(Ts, GQA_DH, 0, 2)

    streams = (dict(mod_base=0, rows_per_cond=None, tm=512),
               dict(mod_base=2, rows_per_cond=Ts, tm=512))
    ffn_tm = 2048

    new_ckv, new_kpe, new_gla, new_k, new_v = [], [], [], [], []
    for l in range(depth):
        i = l // 2
        mod = mods[l]
        if l % 2 == 0:
            P = _ab_params(ab_w_in[i], ab_w_out[i], gla_a_w2[i], gla_a_b[i], gla_out_g[i], mla_q_norm_g[i],
                           mla_w_qb[i], mla_kv_norm_g[i], mla_w_kvb[i], mla_qn_g[i], mla_kn_g[i])
            pj_p, ckv_p, alk_p = _inproj(xp, norm_mix_g[l], mod, P['w_perm'], AB_OUTS, **streams[0])
            pj_s, ckv_s, alk_s = _inproj(xs, norm_mix_g[l], mod, P['w_perm'], AB_OUTS, **streams[1])
            og_p, st_p = _gla(pj_p, alk_p, P['w2bd'], P['a_b'], P['out_g'], None, B=Bp, T=Tp, want_state=True)
            og_s, _ = _gla(pj_s, alk_s, P['w2bd'], P['a_b'], P['out_g'], state_gla[:, i], B=Bs, T=Ts,
                           want_state=False)
            om_p, ckvn_p = _mla(pj_p, ckv_p, None, None, P['mla'], B=Bp, T=Tp, tq=Tp, want_ckv=True)
            ctx = (cache_mla_ckv[:, i], _pad_lanes(cache_mla_kpe[:, i], MLA_NOPE))
            om_s, _ = _mla(pj_s, ckv_s, ctx, rope_mla, P['mla'], B=Bs, T=Ts, tq=256, want_ckv=False)
            wps = (P['w_out_gla'], P['w_out_mla'])
            acts_p, acts_s = (og_p, om_p), (og_s, om_s)
            new_ckv.append(ckvn_p.reshape(Bp, Tp, MLA_KV_RANK))
            new_kpe.append(alk_p[:, 2 * GLA_RANK:2 * GLA_RANK + MLA_ROPE].reshape(Bp, Tp, MLA_ROPE))
            new_gla.append(st_p)
        else:
            w_in = gqa_w_in[i].astype(BF16)
            w_out = gqa_w_out[i].astype(BF16)
            gp = (jnp.tile(gqa_qn_g[i] * (GQA_DH ** -0.5 * LOG2E), 2).reshape(1, LANES),
                  jnp.tile(gqa_kn_g[i], 2).reshape(1, LANES))
            q_p, k_p, v_p = _inproj(xp, norm_mix_g[l], mod, w_in, C_OUTS, **streams[0])
            q_s, k_s, v_s = _inproj(xs, norm_mix_g[l], mod, w_in, C_OUTS, **streams[1])
            o_p, kn_p, vo_p = _gqa(q_p, k_p, v_p, None, None, gp, B=Bp, T=Tp, tq=Tp, want_kv=True)
            n_kv = GQA_KV_HEADS * GQA_DH
            ctx = (cache_gqa_k[:, i].reshape(Bs, -1, n_kv), cache_gqa_v[:, i].reshape(Bs, -1, n_kv))
            o_s, _, _ = _gqa(q_s, k_s, v_s, ctx, rope_gqa, gp, B=Bs, T=Ts, tq=256, want_kv=False)
            wps = (w_out,)
            acts_p, acts_s = (o_p,), (o_s,)
            new_k.append(kn_p.reshape(Bp, Tp, GQA_KV_HEADS, GQA_DH))
            new_v.append(vo_p.reshape(Bp, Tp, GQA_KV_HEADS, GQA_DH))
        xp = _mix_ffn(acts_p, wps, xp, norm_ffn_g[l], mod, ffn_w_in, ffn_w_out, l, **dict(streams[0], tm=ffn_tm))
        xs = _mix_ffn(acts_s, wps, xs, norm_ffn_g[l], mod, ffn_w_in, ffn_w_out, l, **dict(streams[1], tm=ffn_tm))
    return (xp.reshape(Bp, Tp, D), xs.reshape(Bs, Ts, D), jnp.stack(new_ckv, axis=1),
            jnp.stack(new_kpe, axis=1), jnp.stack(new_gla, axis=1), jnp.stack(new_k, axis=1),
            jnp.stack(new_v, axis=1))
```

```python
import functools

import jax
import jax.numpy as jnp
from jax import lax
from jax.experimental import pallas as pl
from jax.experimental.pallas import tpu as pltpu

F32 = jnp.float32
BF16 = jnp.bfloat16

EPS = 1e-6
ROPE_THETA = 10000.0
GRID_W = 64
LANES = 128
GLA_HEADS, GLA_DK, GLA_DV = 4, 64, 128
GLA_QK = GLA_HEADS * GLA_DK
GLA_V = GLA_HEADS * GLA_DV
GLA_RANK = 16
GLA_TAU = 16.0
GLA_CHUNK = 64
MLA_HEADS = 8
MLA_Q_RANK, MLA_KV_RANK = 384, 256
MLA_NOPE, MLA_ROPE, MLA_V = 64, 32, 64
MLA_QK = MLA_NOPE + MLA_ROPE
GQA_HEADS, GQA_KV_HEADS, GQA_DH = 16, 4, 64
VMEM_LIMIT = 56 << 20


def _cparams(sem, vmem=None):
    return pltpu.CompilerParams(dimension_semantics=sem, vmem_limit_bytes=vmem)


def _nt_dot(a, b):
    return lax.dot_general(a, b, (((1,), (1,)), ((), ())), preferred_element_type=F32)


def _tn_dot(a, b):
    return lax.dot_general(a, b, (((0,), (0,)), ((), ())), preferred_element_type=F32)


def _ds(start, size):
    if isinstance(start, int):
        return pl.ds(start, size)
    return pl.ds(pl.multiple_of(start, size), size)


def _rms(x, g):
    ms = jnp.mean(x * x, axis=-1, keepdims=True)
    return x * lax.rsqrt(ms + EPS) * g


def _silu(x):
    return x * jax.nn.sigmoid(x)


def _rope(x, c, sp, sm, shift):
    return x * c + pltpu.roll(x, LANES - shift, 1) * sp + pltpu.roll(x, shift, 1) * sm


def _adaln_body(c_ref, w_ref, b_ref, o_ref):
    c = c_ref[...]
    s = _silu(c).astype(BF16)
    o_ref[0] = jnp.dot(s, w_ref[0].astype(BF16), preferred_element_type=F32) + b_ref[0]


def _adaln(cond, ada_w, ada_b):
    L, D, E = ada_w.shape
    tn = 1536
    return pl.pallas_call(
        _adaln_body,
        grid=(L, E // tn),
        in_specs=[pl.BlockSpec((8, D), lambda l, n: (0, 0)),
                  pl.BlockSpec((1, D, tn), lambda l, n: (l, 0, n)),
                  pl.BlockSpec((1, 1, tn), lambda l, n: (l, 0, n))],
        out_specs=pl.BlockSpec((1, 8, tn), lambda l, n: (l, 0, n)),
        out_shape=jax.ShapeDtypeStruct((L, 8, E), F32),
        compiler_params=_cparams(("arbitrary", "arbitrary")),
        name="adaln",
    )(cond, ada_w, ada_b.reshape(L, 1, E))


def _slot_output(slot_out, batch, tail, dtype):
    slot, n_slots, prev = slot_out
    zeros = (0,) * len(tail)
    shape = jax.ShapeDtypeStruct((batch, n_slots) + tail, dtype)
    if slot == 0:
        return pl.BlockSpec((1, n_slots) + tail, lambda b, *_: (b, 0) + zeros), shape, [], n_slots - 1
    return pl.BlockSpec((1, 1) + tail, lambda b, *_: (b, slot) + zeros), shape, [prev], 0


def _mod_index(mod_base, rows_per_cond, tm):
    if rows_per_cond is None:
        return lambda i: (mod_base, 0, 0)
    n_mod = max(tm // rows_per_cond, 1)
    return lambda i: ((mod_base + (i * tm) // rows_per_cond) // n_mod, 0, 0)


def _inproj_body(x_ref, g_ref, mod_ref, w_ref, *o_refs, widths):
    h = _rms(x_ref[...], g_ref[...])
    h = h * (1.0 + mod_ref[0, 1:2, :]) + mod_ref[0, 0:1, :]
    acc = jnp.dot(h.astype(BF16), w_ref[...], preferred_element_type=F32)
    off = 0
    for o_ref, w in zip(o_refs, widths):
        o_ref[...] = acc[:, off:off + w].astype(o_ref.dtype)
        off += w


def _inproj(x, g, mod, w, outs, *, mod_base, rows_per_cond, tm):
    n, d = x.shape
    widths = tuple(o[0] for o in outs)
    return pl.pallas_call(
        functools.partial(_inproj_body, widths=widths),
        grid=(n // tm,),
        in_specs=[pl.BlockSpec((tm, d), lambda i: (i, 0)),
                  pl.BlockSpec((1, d), lambda i: (0, 0)),
                  pl.BlockSpec((1, 6, d), _mod_index(mod_base, rows_per_cond, tm)),
                  pl.BlockSpec(w.shape, lambda i: (0, 0))],
        out_specs=[pl.BlockSpec((tm, wd), lambda i: (i, 0)) for wd in widths],
        out_shape=[jax.ShapeDtypeStruct((n, wd), dt) for wd, dt in outs],
        compiler_params=_cparams(("arbitrary",), VMEM_LIMIT),
        name="inproj",
    )(x, g.reshape(1, d), mod, w)


def _ffn_body(*refs, n_in, nk, row_chunk, rows_per_cond):
    a_refs, wp_refs = refs[:n_in], refs[n_in:2 * n_in]
    x_ref, g_ref, mod_ref, wg_ref, wu_ref, wo_ref, o_ref, h_ref, wgu_ref, wob_ref = refs[2 * n_in:]
    k = pl.program_id(1)
    tm = x_ref.shape[0]
    tk = wg_ref.shape[2]
    n_chunks = tm // row_chunk

    def mod_row(c):
        return 0 if rows_per_cond is None else (c * row_chunk) // rows_per_cond

    @pl.when(k == 0)
    def _():
        def mix_rows(c, carry):
            rows, m = _ds(c * row_chunk, row_chunk), mod_row(c)
            acc = None
            for a_ref, wp_ref in zip(a_refs, wp_refs):
                part = jnp.dot(a_ref[rows, :], wp_ref[...], preferred_element_type=F32)
                acc = part if acc is None else acc + part
            x_new = x_ref[rows, :] + mod_ref[m, 2:3, :] * acc
            o_ref[rows, :] = x_new
            h = _rms(x_new, g_ref[...])
            h = h * (1.0 + mod_ref[m, 4:5, :]) + mod_ref[m, 3:4, :]
            h_ref[rows, :] = h.astype(BF16)
            return carry
        lax.fori_loop(0, n_chunks, mix_rows, 0)

    wgu_ref[:, :tk] = wg_ref[0].astype(BF16)
    wgu_ref[:, tk:] = wu_ref[0].astype(BF16)
    wob_ref[...] = wo_ref[0].astype(BF16)

    for c in range(n_chunks):
        rows, m = _ds(c * row_chunk, row_chunk), mod_row(c)
        gu = jnp.dot(h_ref[rows, :], wgu_ref[...], preferred_element_type=F32)
        a = (_silu(gu[:, :tk]) * gu[:, tk:]).astype(BF16)
        o_ref[rows, :] += mod_ref[m, 5:6, :] * jnp.dot(a, wob_ref[...], preferred_element_type=F32)


def _mix_ffn(acts, wps, x, g, mod, w_in, w_out, layer, *, mod_base, rows_per_cond, tm):
    n, d = x.shape
    hidden = w_out.shape[1]
    tk = 256
    nk = hidden // tk
    n_in = len(acts)
    n_mod = 1 if rows_per_cond is None else max(tm // rows_per_cond, 1)
    once = dict(pipeline_mode=pl.Buffered(1))
    return pl.pallas_call(
        functools.partial(_ffn_body, n_in=n_in, nk=nk, row_chunk=512, rows_per_cond=rows_per_cond),
        grid=(n // tm, nk),
        in_specs=([pl.BlockSpec((tm, a.shape[1]), lambda i, k: (i, 0), **once) for a in acts]
                  + [pl.BlockSpec(w.shape, lambda i, k: (0, 0), **once) for w in wps]
                  + [pl.BlockSpec((tm, d), lambda i, k: (i, 0)),
                     pl.BlockSpec((1, d), lambda i, k: (0, 0)),
                     pl.BlockSpec((n_mod, 6, d),
                                  (lambda f: (lambda i, k: f(i)))(_mod_index(mod_base, rows_per_cond, tm))),
                     pl.BlockSpec((1, d, tk), lambda i, k: (layer, 0, k)),
                     pl.BlockSpec((1, d, tk), lambda i, k: (layer, 0, nk + k)),
                     pl.BlockSpec((1, tk, d), lambda i, k: (layer, k, 0))]),
        out_specs=pl.BlockSpec((tm, d), lambda i, k: (i, 0)),
        out_shape=jax.ShapeDtypeStruct((n, d), F32),
        scratch_shapes=[pltpu.VMEM((tm, d), BF16),
                        pltpu.VMEM((d, 2 * tk), BF16),
                        pltpu.VMEM((tk, d), BF16)],
        compiler_params=_cparams(("arbitrary", "arbitrary"), VMEM_LIMIT),
        name="mix_ffn",
    )(*acts, *wps, x, g.reshape(1, d), mod, w_in, w_in, w_out)


def _log_sigmoid(x):
    return jnp.minimum(x, 0.0) - jnp.log1p(jnp.exp(-jnp.abs(x)))


def _gla_body(*refs, T, has_s0, want_state, n_alias, n_zero_slots):
    refs = refs[n_alias:]
    qkvr_ref, alk_ref, w2_ref, ab_ref, og_ref = refs[:5]
    pos = 5
    s0_ref = None
    if has_s0:
        s0_ref = refs[pos]
        pos += 1
    o_ref = refs[pos]
    pos += 1
    st_out_ref = None
    if want_state:
        st_out_ref = refs[pos]
        pos += 1
    osc_ref, st_ref, la_ref, qin_ref, kst_ref, dec_ref, upd_ref, stq_ref = refs[pos:pos + 8]

    C = GLA_CHUNK
    nc = T // C

    logit = jnp.dot(alk_ref[...].astype(BF16), w2_ref[...], preferred_element_type=F32) + ab_ref[...]
    la_ref[...] = _log_sigmoid(logit) * (1.0 / GLA_TAU)

    for d in range(2):
        if has_s0:
            st_ref[d] = jnp.concatenate([s0_ref[0, d, h] for h in range(GLA_HEADS)], axis=0).T
        else:
            st_ref[d] = jnp.zeros((GLA_DV, GLA_QK), F32)

    r64 = lax.broadcasted_iota(jnp.int32, (C, C), 0)
    c64 = lax.broadcasted_iota(jnp.int32, (C, C), 1)
    tri_f = (r64 >= c64).astype(BF16)
    tri_b = (c64 >= r64).astype(BF16)
    t_idx = lax.broadcasted_iota(jnp.int32, (C, GLA_QK), 0)
    s_idx = lax.broadcasted_iota(jnp.int32, (C, GLA_QK), 1) % C
    causal_f = t_idx >= s_idx
    causal_b = t_idx <= s_idx
    bm_k = (lax.broadcasted_iota(jnp.int32, (GLA_QK, GLA_QK), 0) // GLA_DK
            == lax.broadcasted_iota(jnp.int32, (GLA_QK, GLA_QK), 1) // GLA_DK)
    bm_v = (lax.broadcasted_iota(jnp.int32, (GLA_QK, GLA_V), 0) // C
            == lax.broadcasted_iota(jnp.int32, (GLA_QK, GLA_V), 1) // GLA_DV)
    head_lanes = [lax.broadcasted_iota(jnp.int32, (1, GLA_QK), 1) // GLA_DK == h for h in range(GLA_HEADS)]
    directions = ((tri_f, causal_f, C // 2 - 1, C - 1), (tri_b, causal_b, C // 2, 0))

    def v_rows(rows):
        return qkvr_ref[rows, 2 * GLA_QK:2 * GLA_QK + GLA_V]

    G = 4
    units = [(i, d) for i in range(G) for d in range(2)]

    def for_groups(fn):
        if nc == G:
            fn(0)
        else:
            def body(g, carry):
                fn(g * G)
                return carry
            lax.fori_loop(0, nc // G, body, 0)

    def intra_group(n0):
        rows = [_ds((n0 + i) * C, C) for i in range(G)]
        b = {}
        for i, d in units:
            b[i, d] = _split_dot(la_ref[rows[i], d * GLA_QK:(d + 1) * GLA_QK], directions[d][0], w_left=True)
        q_loc, k_bd = {}, {}
        for i in range(G):
            qc = qkvr_ref[rows[i], 0:GLA_QK].astype(F32) * (GLA_DK ** -0.5)
            kc = qkvr_ref[rows[i], GLA_QK:2 * GLA_QK].astype(F32)
            for d in range(2):
                _, _, ref_row, last_row = directions[d]
                bb = b[i, d]
                b_ref = bb[ref_row:ref_row + 1, :]
                b_last = bb[last_row:last_row + 1, :]
                q_loc[i, d] = (qc * jnp.exp(bb - b_ref)).astype(BF16)
                k_loc = kc * jnp.exp(b_ref - bb)
                k_bd[i, d] = jnp.where(bm_k, jnp.concatenate([k_loc] * GLA_HEADS, axis=0), 0.0).astype(BF16)
                qin_ref[d, rows[i], :] = (qc * jnp.exp(bb)).astype(BF16)
                kst_ref[d, rows[i], :] = (kc * jnp.exp(b_last - bb)).astype(BF16)
                dec_ref[d, _ds((n0 + i) * 8, 8), :] = jnp.broadcast_to(jnp.exp(b_last), (8, GLA_QK))
        a = {u: jnp.where(directions[u[1]][1], _nt_dot(q_loc[u], k_bd[u]), 0.0).astype(BF16) for u in units}
        v_bd = [jnp.where(bm_v, jnp.concatenate([v_rows(rows[i])] * GLA_HEADS, axis=0), jnp.zeros((), BF16))
                for i in range(G)]
        o = {u: jnp.dot(a[u], v_bd[u[0]], preferred_element_type=F32) for u in units}
        for i in range(G):
            osc_ref[rows[i], :] = o[i, 0] + o[i, 1]

    def update_group(n0):
        rows = [_ds((n0 + i) * C, C) for i in range(G)]
        upd = {u: _tn_dot(v_rows(rows[u[0]]), kst_ref[u[1], rows[u[0]], :]) for u in units}
        for i, d in units:
            acc = None
            for h, m in enumerate(head_lanes):
                part = jnp.where(m, upd[i, d][h * GLA_DV:(h + 1) * GLA_DV, :], 0.0)
                acc = part if acc is None else acc + part
            upd_ref[d, n0 + i] = acc

    def scan_step(i, carry):
        for d, n in ((0, i), (1, nc - 1 - i)):
            st = st_ref[d]
            stq_ref[d, n] = st.astype(BF16)
            st_ref[d] = st * dec_ref[d, pl.ds(n * 8, 1), :] + upd_ref[d, n]
        return carry

    def readout_group(n0):
        rows = [_ds((n0 + i) * C, C) for i in range(G)]
        o = {}
        for i, d in units:
            q_in = qin_ref[d, rows[i], :]
            q_heads = jnp.concatenate([jnp.where(m, q_in, jnp.zeros((), BF16)) for m in head_lanes], axis=0)
            o[i, d] = _nt_dot(q_heads, stq_ref[d, n0 + i])
        for i in range(G):
            both = o[i, 0] + o[i, 1]
            for h in range(GLA_HEADS):
                osc_ref[rows[i], h * GLA_DV:(h + 1) * GLA_DV] += both[h * C:(h + 1) * C, :]

    for_groups(intra_group)
    for_groups(update_group)
    if nc == G:
        for i in range(nc):
            scan_step(i, 0)
    else:
        lax.fori_loop(0, nc, scan_step, 0)
    for_groups(readout_group)

    rb = 256
    for i in range(T // rb):
        rows = slice(i * rb, (i + 1) * rb)
        for h in range(GLA_HEADS):
            cols = slice(h * GLA_DV, (h + 1) * GLA_DV)
            o = _rms(osc_ref[rows, cols], og_ref[:, cols])
            r = qkvr_ref[rows, 2 * GLA_QK + GLA_V + h * GLA_DV:2 * GLA_QK + GLA_V + (h + 1) * GLA_DV].astype(F32)
            o_ref[rows, cols] = (o * _silu(r)).astype(o_ref.dtype)

    if want_state:
        for d in range(2):
            s_all = st_ref[d].T
            for h in range(GLA_HEADS):
                st_out_ref[0, 0, d, h] = s_all[h * GLA_DK:(h + 1) * GLA_DK, :]
        if n_zero_slots:
            st_out_ref[0, 1:] = jnp.zeros((n_zero_slots, 2, GLA_HEADS, GLA_DK, GLA_DV), F32)


def _gla(proj, alk, w2bd, a_b, out_g, s0, *, B, T, state_out=None):
    has_s0 = s0 is not None
    want_state = state_out is not None
    n_qkvr = 2 * GLA_QK + 2 * GLA_V
    in_specs = [pl.BlockSpec((T, n_qkvr), lambda b: (b, 0)),
                pl.BlockSpec((T, LANES), lambda b: (b, 0)),
                pl.BlockSpec(w2bd.shape, lambda b: (0, 0)),
                pl.BlockSpec((1, 2 * GLA_QK), lambda b: (0, 0)),
                pl.BlockSpec((1, GLA_V), lambda b: (0, 0))]
    args = [proj, alk, w2bd, a_b, out_g]
    st_spec = pl.BlockSpec((1, 2, GLA_HEADS, GLA_DK, GLA_DV), lambda b: (b, 0, 0, 0, 0))
    if has_s0:
        in_specs.append(st_spec)
        args.append(s0)
    out_specs = [pl.BlockSpec((T, GLA_V), lambda b: (b, 0))]
    out_shape = [jax.ShapeDtypeStruct((B * T, GLA_V), BF16)]
    aliased, n_zero = [], 0
    if want_state:
        spec, shape, aliased, n_zero = _slot_output(state_out, B, (2, GLA_HEADS, GLA_DK, GLA_DV), F32)
        out_specs.append(spec)
        out_shape.append(shape)
    res = pl.pallas_call(
        functools.partial(_gla_body, T=T, has_s0=has_s0, want_state=want_state, n_alias=len(aliased),
                          n_zero_slots=n_zero),
        grid=(B,),
        in_specs=[pl.BlockSpec(memory_space=pl.ANY)] * len(aliased) + in_specs,
        out_specs=out_specs,
        out_shape=out_shape,
        input_output_aliases={i: 1 + i for i in range(len(aliased))},
        scratch_shapes=[pltpu.VMEM((T, GLA_V), F32),
                        pltpu.VMEM((2, GLA_DV, GLA_QK), F32),
                        pltpu.VMEM((T, 2 * GLA_QK), F32),
                        pltpu.VMEM((2, T, GLA_QK), BF16),
                        pltpu.VMEM((2, T, GLA_QK), BF16),
                        pltpu.VMEM((2, 8 * (T // GLA_CHUNK), GLA_QK), F32),
                        pltpu.VMEM((2, T // GLA_CHUNK, GLA_DV, GLA_QK), F32),
                        pltpu.VMEM((2, T // GLA_CHUNK, GLA_DV, GLA_QK), BF16)],
        compiler_params=_cparams(("arbitrary",), VMEM_LIMIT),
        name="gla",
    )(*aliased, *args)
    return res if want_state else (res[0], None)


LOG2E = 1.4426950408889634


def _attend(qs, ks, vs, group):
    outs = []
    for g0 in range(0, len(qs), group):
        sl = slice(g0, g0 + group)
        scores = [_nt_dot(q, k()) for q, k in zip(qs[sl], ks[sl])]
        probs = [jnp.exp2(s - jnp.max(s, axis=-1, keepdims=True)).astype(BF16) for s in scores]
        res = [jnp.dot(p, v(), preferred_element_type=F32) for p, v in zip(probs, vs[sl])]
        outs += [r[:, :LANES] / r[:, LANES:] for r in res]
    return outs


def _head_group(n_keys, n_heads):
    return n_heads if n_keys <= 512 else 2


def _stacked_ms(xs, ones_mat, inv_n):
    n = xs[0].shape[0]
    ms = _split_dot(jnp.concatenate([x * x for x in xs], axis=0), ones_mat) * inv_n
    return [ms[i * n:(i + 1) * n] for i in range(len(xs))]


def _split_dot(x, w, w_left=False):
    hi = x.astype(BF16)
    lo = (x - hi.astype(F32)).astype(BF16)
    if w_left:
        return jnp.dot(w, hi, preferred_element_type=F32) + jnp.dot(w, lo, preferred_element_type=F32)
    return jnp.dot(hi, w, preferred_element_type=F32) + jnp.dot(lo, w, preferred_element_type=F32)


def _group_ones(group):
    r = lax.broadcasted_iota(jnp.int32, (LANES, LANES), 0) // group
    c = lax.broadcasted_iota(jnp.int32, (LANES, LANES), 1) // group
    return (r == c).astype(BF16)


def _mla_body(*refs, T, S_ctx, rope, want_ckv, group, n_alias, n_zero_slots):
    refs = refs[n_alias:]
    cq_ref, kpe_ref, ckv_ref = refs[:3]
    pos = 3
    if S_ctx:
        ckvc_ref, kpec_ref = refs[pos:pos + 2]
        pos += 2
    if rope:
        rq_refs = refs[pos:pos + 3]
        rk_refs = refs[pos + 3:pos + 6]
        pos += 6
    qng_ref, wqb_ref, qg_ref, kvg_ref, wkvb_ref, kg_ref = refs[pos:pos + 6]
    pos += 6
    o_ref = refs[pos]
    pos += 1
    if want_ckv:
        ckvn_ref = refs[pos]
        pos += 1
    k_sc, vlo_sc, vhi_sc = refs[pos:pos + 3]

    H = MLA_HEADS
    n_k = H * LANES
    lo = lax.broadcasted_iota(jnp.int32, (1, LANES), 1) < MLA_V

    def head_ms(xs):
        return [jnp.sum(x * x, axis=-1, keepdims=True) * (1.0 / MLA_QK) for x in xs]

    def fill_kv(ckvn, kpe, row0, n_rows, with_rope):
        rows = slice(row0, row0 + n_rows)
        kv = jnp.dot(ckvn.astype(BF16), wkvb_ref[...], preferred_element_type=F32)
        khs = [kv[:, h * LANES:(h + 1) * LANES] + kpe for h in range(H)]
        for h, (kh, ms) in enumerate(zip(khs, head_ms(khs))):
            kh = kh * lax.rsqrt(ms + EPS) * kg_ref[...]
            if with_rope:
                kh = _rope(kh, rk_refs[0][...], rk_refs[1][...], rk_refs[2][...], MLA_ROPE // 4)
            k_sc[rows, h * LANES:(h + 1) * LANES] = kh.astype(BF16)
        ones = jnp.ones((n_rows, LANES), BF16)
        for p in range(H // 2):
            v = kv[:, n_k + p * LANES:n_k + (p + 1) * LANES]
            vlo_sc[rows, 2 * p * LANES:(2 * p + 1) * LANES] = jnp.where(lo, v, 0.0).astype(BF16)
            vhi_sc[rows, 2 * p * LANES:(2 * p + 1) * LANES] = jnp.where(lo, 0.0, v).astype(BF16)
            vlo_sc[rows, (2 * p + 1) * LANES:(2 * p + 2) * LANES] = ones
            vhi_sc[rows, (2 * p + 1) * LANES:(2 * p + 2) * LANES] = ones

    @pl.when(pl.program_id(1) == 0)
    def _():
        if S_ctx:
            fill_kv(ckvc_ref[0], kpec_ref[0], 0, S_ctx, False)
        ckvn = _rms(ckv_ref[...], kvg_ref[...])
        if want_ckv:
            ckvn_ref[0, 0] = ckvn
            if n_zero_slots:
                ckvn_ref[0, 1:] = jnp.zeros((n_zero_slots, T, MLA_KV_RANK), F32)
        fill_kv(ckvn, kpe_ref[...].astype(F32), S_ctx, T, rope)

    cqn = _rms(cq_ref[...].astype(F32), qng_ref[...])
    q = jnp.dot(cqn.astype(BF16), wqb_ref[...], preferred_element_type=F32)
    qhs = [q[:, h * LANES:(h + 1) * LANES] for h in range(H)]
    qs = []
    for qh, ms in zip(qhs, head_ms(qhs)):
        qh = qh * lax.rsqrt(ms + EPS) * qg_ref[...]
        if rope:
            qh = _rope(qh, rq_refs[0][...], rq_refs[1][...], rq_refs[2][...], MLA_ROPE // 4)
        qs.append(qh.astype(BF16))
    ks = [functools.partial(lambda h: k_sc[:, h * LANES:(h + 1) * LANES], h) for h in range(H)]
    vs = [functools.partial(lambda h: (vlo_sc, vhi_sc)[h % 2][:, (h // 2) * 2 * LANES:(h // 2 + 1) * 2 * LANES], h)
          for h in range(H)]
    outs = _attend(qs, ks, vs, group)
    for p in range(H // 2):
        o_ref[:, p * LANES:(p + 1) * LANES] = (outs[2 * p] + outs[2 * p + 1]).astype(o_ref.dtype)


def _mla(proj, ckv, ctx, rope_tabs, params, *, B, T, tq, ckv_out=None):
    want_ckv = ckv_out is not None
    nq = T // tq
    S_ctx = ctx[0].shape[1] if ctx is not None else 0
    S = S_ctx + T
    rope = rope_tabs is not None
    cq_blk = (2 * GLA_QK + 2 * GLA_V) // MLA_Q_RANK
    kpe_blk = (2 * GLA_QK + 2 * GLA_V + MLA_Q_RANK) // LANES
    in_specs = [pl.BlockSpec((tq, MLA_Q_RANK), lambda b, j: (b * nq + j, cq_blk)),
                pl.BlockSpec((T, LANES), lambda b, j: (b, kpe_blk)),
                pl.BlockSpec((T, MLA_KV_RANK), lambda b, j: (b, 0))]
    args = [proj, proj, ckv]
    if S_ctx:
        in_specs += [pl.BlockSpec((1, S_ctx, MLA_KV_RANK), lambda b, j: (b, 0, 0)),
                     pl.BlockSpec((1, S_ctx, LANES), lambda b, j: (b, 0, 0))]
        args += list(ctx)
    if rope:
        in_specs += [pl.BlockSpec((tq, LANES), lambda b, j: (j, 0))] * 3
        in_specs += [pl.BlockSpec((T, LANES), lambda b, j: (0, 0))] * 3
        args += list(rope_tabs) * 2
    in_specs += [pl.BlockSpec(p.shape, lambda b, j: (0, 0)) for p in params]
    args += list(params)
    out_specs = [pl.BlockSpec((tq, MLA_HEADS * MLA_V), lambda b, j: (b * nq + j, 0))]
    out_shape = [jax.ShapeDtypeStruct((B * T, MLA_HEADS * MLA_V), BF16)]
    aliased, n_zero = [], 0
    if want_ckv:
        spec, shape, aliased, n_zero = _slot_output(ckv_out, B, (T, MLA_KV_RANK), F32)
        out_specs.append(spec)
        out_shape.append(shape)
    res = pl.pallas_call(
        functools.partial(_mla_body, T=T, S_ctx=S_ctx, rope=rope, want_ckv=want_ckv,
                          group=_head_group(S, MLA_HEADS), n_alias=len(aliased), n_zero_slots=n_zero),
        grid=(B, nq),
        in_specs=[pl.BlockSpec(memory_space=pl.ANY)] * len(aliased) + in_specs,
        out_specs=out_specs,
        out_shape=out_shape,
        input_output_aliases={i: 1 + i for i in range(len(aliased))},
        scratch_shapes=[pltpu.VMEM((S, MLA_HEADS * LANES), BF16),
                        pltpu.VMEM((S, MLA_HEADS * LANES), BF16),
                        pltpu.VMEM((S, MLA_HEADS * LANES), BF16)],
        compiler_params=_cparams(("arbitrary", "arbitrary"), VMEM_LIMIT),
        name="mla",
    )(*aliased, *args)
    return res if want_ckv else (res[0], None)


def _gqa_body(*refs, T, S_ctx, rope, want_kv, group, n_alias, n_zero_slots):
    refs = refs[n_alias:]
    q_ref, k_ref, v_ref = refs[:3]
    pos = 3
    if S_ctx:
        kc_ref, vc_ref = refs[pos:pos + 2]
        pos += 2
    if rope:
        rq_refs = refs[pos:pos + 3]
        rk_refs = refs[pos + 3:pos + 6]
        pos += 6
    qg_ref, kg_ref = refs[pos:pos + 2]
    pos += 2
    o_ref = refs[pos]
    pos += 1
    if want_kv:
        kn_ref, vo_ref = refs[pos:pos + 2]
        pos += 2
    klo_sc, khi_sc, vlo_sc, vhi_sc = refs[pos:pos + 4]

    lo = lax.broadcasted_iota(jnp.int32, (1, LANES), 1) < GQA_DH
    half_sum = _group_ones(GQA_DH)

    def head_ms(x):
        return _split_dot(x * x, half_sum) * (1.0 / GQA_DH)

    def scatter_halves(x, lo_sc, hi_sc, c, rows):
        rolled = pltpu.roll(x, GQA_DH, 1)
        lo_sc[2 * c, rows, :LANES] = jnp.where(lo, x, 0.0).astype(BF16)
        hi_sc[2 * c, rows, :LANES] = jnp.where(lo, 0.0, rolled).astype(BF16)
        lo_sc[2 * c + 1, rows, :LANES] = jnp.where(lo, rolled, 0.0).astype(BF16)
        hi_sc[2 * c + 1, rows, :LANES] = jnp.where(lo, 0.0, x).astype(BF16)

    def fill_ones(rows):
        ones = jnp.ones((rows.stop - rows.start, LANES), BF16)
        for g in range(GQA_KV_HEADS):
            vlo_sc[g, rows, LANES:] = ones
            vhi_sc[g, rows, LANES:] = ones

    @pl.when(pl.program_id(1) == 0)
    def _():
        fill_ones(slice(0, S_ctx + T))
        if want_kv and n_zero_slots:
            zeros = jnp.zeros((n_zero_slots, T, GQA_KV_HEADS * GQA_DH), F32)
            kn_ref[0, 1:] = zeros
            vo_ref[0, 1:] = zeros
        for c in range(GQA_KV_HEADS // 2):
            cols = slice(c * LANES, (c + 1) * LANES)
            if S_ctx:
                scatter_halves(kc_ref[0, :, cols], klo_sc, khi_sc, c, slice(0, S_ctx))
                scatter_halves(vc_ref[0, :, cols], vlo_sc, vhi_sc, c, slice(0, S_ctx))
            kx = k_ref[:, cols]
            kn = kx * lax.rsqrt(head_ms(kx) + EPS) * kg_ref[...]
            vx = v_ref[:, cols]
            if want_kv:
                kn_ref[0, 0, :, cols] = kn
                vo_ref[0, 0, :, cols] = vx
            if rope:
                kn = _rope(kn, rk_refs[0][...], rk_refs[1][...], rk_refs[2][...], GQA_DH // 4)
            scatter_halves(kn, klo_sc, khi_sc, c, slice(S_ctx, S_ctx + T))
            scatter_halves(vx, vlo_sc, vhi_sc, c, slice(S_ctx, S_ctx + T))

    n_pairs = GQA_HEADS // 2
    qxs = [q_ref[:, p * LANES:(p + 1) * LANES].astype(F32) for p in range(n_pairs)]
    qs, ks, vs = [], [], []
    for p, (qx, ms) in enumerate(zip(qxs, _stacked_ms(qxs, half_sum, 1.0 / GQA_DH))):
        qn = qx * lax.rsqrt(ms + EPS) * qg_ref[...]
        if rope:
            qn = _rope(qn, rq_refs[0][...], rq_refs[1][...], rq_refs[2][...], GQA_DH // 4)
        g = p // 2
        qs += [qn.astype(BF16)] * 2
        ks += [functools.partial(lambda r, g: r[g], r, g) for r in (klo_sc, khi_sc)]
        vs += [functools.partial(lambda r, g: r[g], r, g) for r in (vlo_sc, vhi_sc)]
    outs = _attend(qs, ks, vs, group)
    for p in range(n_pairs):
        o_ref[:, p * LANES:(p + 1) * LANES] = (outs[2 * p] + outs[2 * p + 1]).astype(o_ref.dtype)


def _gqa(q, k, v, ctx, rope_tabs, params, *, B, T, tq, kv_out=None):
    want_kv = kv_out is not None
    nq = T // tq
    S_ctx = ctx[0].shape[1] if ctx is not None else 0
    S = S_ctx + T
    rope = rope_tabs is not None
    n_q = GQA_HEADS * GQA_DH
    n_kv = GQA_KV_HEADS * GQA_DH
    in_specs = [pl.BlockSpec((tq, n_q), lambda b, j: (b * nq + j, 0)),
                pl.BlockSpec((T, n_kv), lambda b, j: (b, 0)),
                pl.BlockSpec((T, n_kv), lambda b, j: (b, 0))]
    args = [q, k, v]
    if S_ctx:
        in_specs += [pl.BlockSpec((1, S_ctx, n_kv), lambda b, j: (b, 0, 0))] * 2
        args += list(ctx)
    if rope:
        in_specs += [pl.BlockSpec((tq, LANES), lambda b, j: (j, 0))] * 3
        in_specs += [pl.BlockSpec((T, LANES), lambda b, j: (0, 0))] * 3
        args += list(rope_tabs) * 2
    in_specs += [pl.BlockSpec(p.shape, lambda b, j: (0, 0)) for p in params]
    args += list(params)
    out_specs = [pl.BlockSpec((tq, n_q), lambda b, j: (b * nq + j, 0))]
    out_shape = [jax.ShapeDtypeStruct((B * T, n_q), BF16)]
    aliased, n_zero = [], 0
    if want_kv:
        slot, n_slots, prev_k, prev_v = kv_out
        for prev in (prev_k, prev_v):
            spec, shape, al, n_zero = _slot_output((slot, n_slots, prev), B, (T, n_kv), F32)
            out_specs.append(spec)
            out_shape.append(shape)
            aliased += al
    res = pl.pallas_call(
        functools.partial(_gqa_body, T=T, S_ctx=S_ctx, rope=rope, want_kv=want_kv,
                          group=_head_group(S, GQA_HEADS), n_alias=len(aliased), n_zero_slots=n_zero),
        grid=(B, nq),
        in_specs=[pl.BlockSpec(memory_space=pl.ANY)] * len(aliased) + in_specs,
        out_specs=out_specs,
        out_shape=out_shape,
        input_output_aliases={i: 1 + i for i in range(len(aliased))},
        scratch_shapes=([pltpu.VMEM((GQA_KV_HEADS, S, LANES), BF16)] * 2
                        + [pltpu.VMEM((GQA_KV_HEADS, S, 2 * LANES), BF16)] * 2),
        compiler_params=_cparams(("arbitrary", "arbitrary"), VMEM_LIMIT),
        name="gqa",
    )(*aliased, *args)
    return res if want_kv else (res[0], None, None)


def _rope_tables(n_tok, d_rot, lead, reps):
    t = jnp.arange(n_tok, dtype=jnp.int32)
    posn = jnp.stack([t // GRID_W, t % GRID_W], axis=-1).astype(F32)
    quarter = d_rot // 4
    inv = jnp.power(ROPE_THETA, -jnp.arange(quarter, dtype=F32) / quarter)
    ang = posn[:, :, None] * inv
    cos, sin = jnp.cos(ang), jnp.sin(ang)
    zero = jnp.zeros_like(sin)
    c_rot = jnp.stack([cos, cos], axis=2).reshape(n_tok, d_rot)
    sp_rot = jnp.stack([-sin, zero], axis=2).reshape(n_tok, d_rot)
    sm_rot = jnp.stack([zero, sin], axis=2).reshape(n_tok, d_rot)
    tail = LANES - lead - reps * d_rot

    def embed(rot, fill):
        parts = [jnp.full((n_tok, lead), fill, F32)] + [rot] * reps + [jnp.full((n_tok, tail), fill, F32)]
        return jnp.concatenate(parts, axis=1)

    return embed(c_rot, 1.0), embed(sp_rot, 0.0), embed(sm_rot, 0.0)


def _pad_lanes(x, lead, width=LANES):
    pad = [(0, 0)] * (x.ndim - 1) + [(lead, width - lead - x.shape[-1])]
    return jnp.pad(x, pad)


def _ab_params(w_in, w_out, a_w2, a_b, out_g, q_norm_g, w_qb, kv_norm_g, w_kvb, qn_g, kn_g):
    d = w_in.shape[0]
    o_alo = 2 * GLA_QK + 2 * GLA_V
    o_cq = o_alo + 2 * GLA_RANK
    o_ckv = o_cq + MLA_Q_RANK
    o_kpe = o_ckv + MLA_KV_RANK
    w_kpe = w_in[:, o_kpe:]
    w_perm = jnp.concatenate([
        w_in[:, :o_alo], w_in[:, o_cq:o_ckv], _pad_lanes(w_kpe, MLA_NOPE),
        w_in[:, o_ckv:o_kpe],
        _pad_lanes(jnp.concatenate([w_in[:, o_alo:o_cq], w_kpe], axis=1), 0),
    ], axis=1).astype(BF16)
    w2bd = jnp.zeros((LANES, 2 * GLA_QK), F32)
    w2bd = w2bd.at[:GLA_RANK, :GLA_QK].set(a_w2[0]).at[GLA_RANK:2 * GLA_RANK, GLA_QK:].set(a_w2[1])
    w_qb_p = _pad_lanes(w_qb.reshape(MLA_Q_RANK, MLA_HEADS, MLA_QK), 0).reshape(MLA_Q_RANK, MLA_HEADS * LANES)
    kvb = w_kvb.reshape(MLA_KV_RANK, MLA_HEADS, MLA_NOPE + MLA_V)
    w_kvb_p = jnp.concatenate([
        _pad_lanes(kvb[:, :, :MLA_NOPE], 0).reshape(MLA_KV_RANK, MLA_HEADS * LANES),
        kvb[:, :, MLA_NOPE:].reshape(MLA_KV_RANK, MLA_HEADS * MLA_V)], axis=1)
    return dict(
        w_perm=w_perm,
        w2bd=w2bd.astype(BF16),
        a_b=a_b.reshape(1, 2 * GLA_QK),
        out_g=jnp.tile(out_g, GLA_HEADS).reshape(1, GLA_V),
        mla=(q_norm_g.reshape(1, -1), w_qb_p.astype(BF16),
             _pad_lanes(qn_g * (MLA_QK ** -0.5 * LOG2E), 0).reshape(1, LANES),
             kv_norm_g.reshape(1, -1), w_kvb_p.astype(BF16),
             _pad_lanes(kn_g, 0).reshape(1, LANES)),
        w_out_gla=w_out[:GLA_V].astype(BF16),
        w_out_mla=w_out[GLA_V:].astype(BF16),
    )


AB_OUTS = ((2 * GLA_QK + 2 * GLA_V + MLA_Q_RANK + LANES, BF16), (MLA_KV_RANK, F32), (LANES, F32))
C_OUTS = ((GQA_HEADS * GQA_DH, BF16), (GQA_KV_HEADS * GQA_DH, F32), (GQA_KV_HEADS * GQA_DH, F32))


def kernel(x_prompt, x_sample, c, cache_mla_ckv, cache_mla_kpe, state_gla, cache_gqa_k, cache_gqa_v,
           c_ctx, ada_w, ada_b, norm_mix_g, norm_ffn_g, ffn_w_in, ffn_w_out, ab_w_in, ab_w_out,
           gla_a_w2, gla_a_b, gla_out_g, mla_q_norm_g, mla_w_qb, mla_kv_norm_g, mla_w_kvb, mla_qn_g,
           mla_kn_g, gqa_w_in, gqa_w_out, gqa_qn_g, gqa_kn_g):
    Bp, Tp, D = x_prompt.shape
    Bs, Ts, _ = x_sample.shape
    depth = ada_w.shape[0]
    xp = x_prompt.reshape(Bp * Tp, D)
    xs = x_sample.reshape(Bs * Ts, D)

    cond = jnp.zeros((8, D), F32).at[0].set(c_ctx).at[2:2 + Bs].set(c)
    mods = _adaln(cond, ada_w, ada_b).reshape(depth, 8, 6, D)

    rope_mla = _rope_tables(Ts, MLA_ROPE, MLA_NOPE, 1)
    rope_gqa = _rope_tables(Ts, GQA_DH, 0, 2)

    streams = (dict(mod_base=0, rows_per_cond=None, tm=512),
               dict(mod_base=2, rows_per_cond=Ts, tm=512))
    ffn_tm = 2048

    n_ab, n_c = (depth + 1) // 2, depth // 2
    new_ckv = new_gla = new_k = new_v = None
    new_kpe = []
    for l in range(depth):
        i = l // 2
        mod = mods[l]
        if l % 2 == 0:
            P = _ab_params(ab_w_in[i], ab_w_out[i], gla_a_w2[i], gla_a_b[i], gla_out_g[i], mla_q_norm_g[i],
                           mla_w_qb[i], mla_kv_norm_g[i], mla_w_kvb[i], mla_qn_g[i], mla_kn_g[i])
            pj_p, ckv_p, alk_p = _inproj(xp, norm_mix_g[l], mod, P['w_perm'], AB_OUTS, **streams[0])
            pj_s, ckv_s, alk_s = _inproj(xs, norm_mix_g[l], mod, P['w_perm'], AB_OUTS, **streams[1])
            og_p, new_gla = _gla(pj_p, alk_p, P['w2bd'], P['a_b'], P['out_g'], None, B=Bp, T=Tp,
                                 state_out=(i, n_ab, new_gla))
            og_s, _ = _gla(pj_s, alk_s, P['w2bd'], P['a_b'], P['out_g'], state_gla[:, i], B=Bs, T=Ts)
            om_p, new_ckv = _mla(pj_p, ckv_p, None, None, P['mla'], B=Bp, T=Tp, tq=Tp,
                                 ckv_out=(i, n_ab, new_ckv))
            ctx = (cache_mla_ckv[:, i], _pad_lanes(cache_mla_kpe[:, i], MLA_NOPE))
            om_s, _ = _mla(pj_s, ckv_s, ctx, rope_mla, P['mla'], B=Bs, T=Ts, tq=256)
            wps = (P['w_out_gla'], P['w_out_mla'])
            acts_p, acts_s = (og_p, om_p), (og_s, om_s)
            new_kpe.append(alk_p[:, 2 * GLA_RANK:2 * GLA_RANK + MLA_ROPE].reshape(Bp, Tp, MLA_ROPE))
        else:
            w_in = gqa_w_in[i].astype(BF16)
            w_out = gqa_w_out[i].astype(BF16)
            gp = (jnp.tile(gqa_qn_g[i] * (GQA_DH ** -0.5 * LOG2E), 2).reshape(1, LANES),
                  jnp.tile(gqa_kn_g[i], 2).reshape(1, LANES))
            q_p, k_p, v_p = _inproj(xp, norm_mix_g[l], mod, w_in, C_OUTS, **streams[0])
            q_s, k_s, v_s = _inproj(xs, norm_mix_g[l], mod, w_in, C_OUTS, **streams[1])
            o_p, new_k, new_v = _gqa(q_p, k_p, v_p, None, None, gp, B=Bp, T=Tp, tq=Tp,
                                     kv_out=(i, n_c, new_k, new_v))
            n_kv = GQA_KV_HEADS * GQA_DH
            ctx = (cache_gqa_k[:, i].reshape(Bs, -1, n_kv), cache_gqa_v[:, i].reshape(Bs, -1, n_kv))
            o_s, _, _ = _gqa(q_s, k_s, v_s, ctx, rope_gqa, gp, B=Bs, T=Ts, tq=256)
            wps = (w_out,)
            acts_p, acts_s = (o_p,), (o_s,)
        xp = _mix_ffn(acts_p, wps, xp, norm_ffn_g[l], mod, ffn_w_in, ffn_w_out, l, **dict(streams[0], tm=ffn_tm))
        xs = _mix_ffn(acts_s, wps, xs, norm_ffn_g[l], mod, ffn_w_in, ffn_w_out, l, **dict(streams[1], tm=ffn_tm))
    kv_shape = (Bp, n_c, Tp, GQA_KV_HEADS, GQA_DH)
    return (xp.reshape(Bp, Tp, D), xs.reshape(Bs, Ts, D), new_ckv, jnp.stack(new_kpe, axis=1), new_gla,
            new_k.reshape(kv_shape), new_v.reshape(kv_shape))
```

```python
import functools

import jax
import jax.numpy as jnp
from jax import lax
from jax.experimental import pallas as pl
from jax.experimental.pallas import tpu as pltpu

F32 = jnp.float32
BF16 = jnp.bfloat16

EPS = 1e-6
ROPE_THETA = 10000.0
GRID_W = 64
LANES = 128
GLA_HEADS, GLA_DK, GLA_DV = 4, 64, 128
GLA_QK = GLA_HEADS * GLA_DK
GLA_V = GLA_HEADS * GLA_DV
GLA_RANK = 16
GLA_TAU = 16.0
GLA_CHUNK = 64
MLA_HEADS = 8
MLA_Q_RANK, MLA_KV_RANK = 384, 256
MLA_NOPE, MLA_ROPE, MLA_V = 64, 32, 64
MLA_QK = MLA_NOPE + MLA_ROPE
GQA_HEADS, GQA_KV_HEADS, GQA_DH = 16, 4, 64
VMEM_LIMIT = 56 << 20


def _cparams(sem, vmem=None):
    return pltpu.CompilerParams(dimension_semantics=sem, vmem_limit_bytes=vmem)


def _nt_dot(a, b):
    return lax.dot_general(a, b, (((1,), (1,)), ((), ())), preferred_element_type=F32)


def _tn_dot(a, b):
    return lax.dot_general(a, b, (((0,), (0,)), ((), ())), preferred_element_type=F32)


def _ds(start, size):
    if isinstance(start, int):
        return pl.ds(start, size)
    return pl.ds(pl.multiple_of(start, size), size)


def _rms(x, g):
    ms = jnp.mean(x * x, axis=-1, keepdims=True)
    return x * lax.rsqrt(ms + EPS) * g


def _silu(x):
    return x * jax.nn.sigmoid(x)


def _rope(x, c, sp, sm, shift):
    return x * c + pltpu.roll(x, LANES - shift, 1) * sp + pltpu.roll(x, shift, 1) * sm


def _adaln_body(c_ref, w_ref, b_ref, o_ref):
    c = c_ref[...]
    s = _silu(c).astype(BF16)
    o_ref[0] = jnp.dot(s, w_ref[0].astype(BF16), preferred_element_type=F32) + b_ref[0]


def _adaln(cond, ada_w, ada_b):
    L, D, E = ada_w.shape
    tn = 1536
    return pl.pallas_call(
        _adaln_body,
        grid=(L, E // tn),
        in_specs=[pl.BlockSpec((8, D), lambda l, n: (0, 0)),
                  pl.BlockSpec((1, D, tn), lambda l, n: (l, 0, n)),
                  pl.BlockSpec((1, 1, tn), lambda l, n: (l, 0, n))],
        out_specs=pl.BlockSpec((1, 8, tn), lambda l, n: (l, 0, n)),
        out_shape=jax.ShapeDtypeStruct((L, 8, E), F32),
        compiler_params=_cparams(("arbitrary", "arbitrary")),
        name="adaln",
    )(cond, ada_w, ada_b.reshape(L, 1, E))


def _slot_output(slot_out, batch, tail, dtype, nb=1):
    slot, n_slots, prev = slot_out
    zeros = (0,) * len(tail)
    shape = jax.ShapeDtypeStruct((batch, n_slots) + tail, dtype)
    if slot == 0:
        return pl.BlockSpec((nb, n_slots) + tail, lambda b, *_: (b, 0) + zeros), shape, [], n_slots - 1
    return pl.BlockSpec((nb, 1) + tail, lambda b, *_: (b, slot) + zeros), shape, [prev], 0


def _mod_index(mod_base, rows_per_cond, tm):
    if rows_per_cond is None:
        return lambda i: (mod_base, 0, 0)
    n_mod = max(tm // rows_per_cond, 1)
    return lambda i: ((mod_base + (i * tm) // rows_per_cond) // n_mod, 0, 0)


def _inproj_body(x_ref, g_ref, mod_ref, w_ref, *o_refs, widths):
    h = _rms(x_ref[...], g_ref[...])
    h = h * (1.0 + mod_ref[0, 1:2, :]) + mod_ref[0, 0:1, :]
    acc = jnp.dot(h.astype(BF16), w_ref[...], preferred_element_type=F32)
    off = 0
    for o_ref, w in zip(o_refs, widths):
        o_ref[...] = acc[:, off:off + w].astype(o_ref.dtype)
        off += w


def _inproj(x, g, mod, w, outs, *, mod_base, rows_per_cond, tm):
    n, d = x.shape
    widths = tuple(o[0] for o in outs)
    return pl.pallas_call(
        functools.partial(_inproj_body, widths=widths),
        grid=(n // tm,),
        in_specs=[pl.BlockSpec((tm, d), lambda i: (i, 0)),
                  pl.BlockSpec((1, d), lambda i: (0, 0)),
                  pl.BlockSpec((1, 6, d), _mod_index(mod_base, rows_per_cond, tm)),
                  pl.BlockSpec(w.shape, lambda i: (0, 0))],
        out_specs=[pl.BlockSpec((tm, wd), lambda i: (i, 0)) for wd in widths],
        out_shape=[jax.ShapeDtypeStruct((n, wd), dt) for wd, dt in outs],
        compiler_params=_cparams(("arbitrary",), VMEM_LIMIT),
        name="inproj",
    )(x, g.reshape(1, d), mod, w)


def _ffn_body(*refs, n_in, nk, row_chunk, rows_per_cond):
    a_refs, wp_refs = refs[:n_in], refs[n_in:2 * n_in]
    x_ref, g_ref, mod_ref, wg_ref, wu_ref, wo_ref, o_ref, h_ref, wgu_ref, wob_ref = refs[2 * n_in:]
    k = pl.program_id(1)
    tm = x_ref.shape[0]
    tk = wg_ref.shape[2]
    n_chunks = tm // row_chunk

    def mod_row(c):
        return 0 if rows_per_cond is None else (c * row_chunk) // rows_per_cond

    @pl.when(k == 0)
    def _():
        def mix_rows(c, carry):
            rows, m = _ds(c * row_chunk, row_chunk), mod_row(c)
            acc = None
            for a_ref, wp_ref in zip(a_refs, wp_refs):
                part = jnp.dot(a_ref[rows, :], wp_ref[...], preferred_element_type=F32)
                acc = part if acc is None else acc + part
            x_new = x_ref[rows, :] + mod_ref[m, 2:3, :] * acc
            o_ref[rows, :] = x_new
            h = _rms(x_new, g_ref[...])
            h = h * (1.0 + mod_ref[m, 4:5, :]) + mod_ref[m, 3:4, :]
            h_ref[rows, :] = h.astype(BF16)
            return carry
        lax.fori_loop(0, n_chunks, mix_rows, 0)

    wgu_ref[:, :tk] = wg_ref[0].astype(BF16)
    wgu_ref[:, tk:] = wu_ref[0].astype(BF16)
    wob_ref[...] = wo_ref[0].astype(BF16)

    for c in range(n_chunks):
        rows, m = _ds(c * row_chunk, row_chunk), mod_row(c)
        gu = jnp.dot(h_ref[rows, :], wgu_ref[...], preferred_element_type=F32)
        a = (_silu(gu[:, :tk]) * gu[:, tk:]).astype(BF16)
        o_ref[rows, :] += mod_ref[m, 5:6, :] * jnp.dot(a, wob_ref[...], preferred_element_type=F32)


def _mix_ffn(acts, wps, x, g, mod, w_in, w_out, layer, *, mod_base, rows_per_cond, tm):
    n, d = x.shape
    hidden = w_out.shape[1]
    tk = 256
    nk = hidden // tk
    n_in = len(acts)
    n_mod = 1 if rows_per_cond is None else max(tm // rows_per_cond, 1)
    once = dict(pipeline_mode=pl.Buffered(1))
    return pl.pallas_call(
        functools.partial(_ffn_body, n_in=n_in, nk=nk, row_chunk=512, rows_per_cond=rows_per_cond),
        grid=(n // tm, nk),
        in_specs=([pl.BlockSpec((tm, a.shape[1]), lambda i, k: (i, 0), **once) for a in acts]
                  + [pl.BlockSpec(w.shape, lambda i, k: (0, 0), **once) for w in wps]
                  + [pl.BlockSpec((tm, d), lambda i, k: (i, 0)),
                     pl.BlockSpec((1, d), lambda i, k: (0, 0)),
                     pl.BlockSpec((n_mod, 6, d),
                                  (lambda f: (lambda i, k: f(i)))(_mod_index(mod_base, rows_per_cond, tm))),
                     pl.BlockSpec((1, d, tk), lambda i, k: (layer, 0, k)),
                     pl.BlockSpec((1, d, tk), lambda i, k: (layer, 0, nk + k)),
                     pl.BlockSpec((1, tk, d), lambda i, k: (layer, k, 0))]),
        out_specs=pl.BlockSpec((tm, d), lambda i, k: (i, 0)),
        out_shape=jax.ShapeDtypeStruct((n, d), F32),
        scratch_shapes=[pltpu.VMEM((tm, d), BF16),
                        pltpu.VMEM((d, 2 * tk), BF16),
                        pltpu.VMEM((tk, d), BF16)],
        compiler_params=_cparams(("arbitrary", "arbitrary"), VMEM_LIMIT),
        name="mix_ffn",
    )(*acts, *wps, x, g.reshape(1, d), mod, w_in, w_in, w_out)


def _log_sigmoid(x):
    return jnp.minimum(x, 0.0) - jnp.log1p(jnp.exp(-jnp.abs(x)))


def _gla_body(*refs, T, has_s0, want_state, n_alias, n_zero_slots, nb):
    refs = refs[n_alias:]
    qkvr_ref, alk_ref, w2_ref, ab_ref, og_ref = refs[:5]
    pos = 5
    s0_ref = None
    if has_s0:
        s0_ref = refs[pos]
        pos += 1
    o_ref = refs[pos]
    pos += 1
    st_out_ref = None
    if want_state:
        st_out_ref = refs[pos]
        pos += 1
    osc_ref, st_ref, la_ref, qin_ref, kst_ref, dec_ref, upd_ref, stq_ref = refs[pos:pos + 8]

    C = GLA_CHUNK
    nc = T // C

    logit = jnp.dot(alk_ref[...].astype(BF16), w2_ref[...], preferred_element_type=F32) + ab_ref[...]
    la_ref[...] = _log_sigmoid(logit) * (1.0 / GLA_TAU)

    for s in range(nb):
        for d in range(2):
            if has_s0:
                st_ref[2 * s + d] = jnp.concatenate([s0_ref[s, d, h] for h in range(GLA_HEADS)], axis=0).T
            else:
                st_ref[2 * s + d] = jnp.zeros((GLA_DV, GLA_QK), F32)

    r64 = lax.broadcasted_iota(jnp.int32, (C, C), 0)
    c64 = lax.broadcasted_iota(jnp.int32, (C, C), 1)
    tri_f = (r64 >= c64).astype(BF16)
    tri_b = (c64 >= r64).astype(BF16)
    t_idx = lax.broadcasted_iota(jnp.int32, (C, GLA_QK), 0)
    s_idx = lax.broadcasted_iota(jnp.int32, (C, GLA_QK), 1) % C
    causal_f = t_idx >= s_idx
    causal_b = t_idx <= s_idx
    bm_k = (lax.broadcasted_iota(jnp.int32, (GLA_QK, GLA_QK), 0) // GLA_DK
            == lax.broadcasted_iota(jnp.int32, (GLA_QK, GLA_QK), 1) // GLA_DK)
    bm_v = (lax.broadcasted_iota(jnp.int32, (GLA_QK, GLA_V), 0) // C
            == lax.broadcasted_iota(jnp.int32, (GLA_QK, GLA_V), 1) // GLA_DV)
    head_lanes = [lax.broadcasted_iota(jnp.int32, (1, GLA_QK), 1) // GLA_DK == h for h in range(GLA_HEADS)]
    directions = ((tri_f, causal_f, C // 2 - 1, C - 1), (tri_b, causal_b, C // 2, 0))

    def v_rows(rows):
        return qkvr_ref[rows, 2 * GLA_QK:2 * GLA_QK + GLA_V]

    G = 4
    units = [(i, d) for i in range(G) for d in range(2)]

    n_groups = nb * nc // G

    def for_groups(fn):
        if n_groups <= 2:
            for g in range(n_groups):
                fn(g * G)
        else:
            def body(g, carry):
                fn(g * G)
                return carry
            lax.fori_loop(0, n_groups, body, 0)

    def intra_group(n0):
        rows = [_ds((n0 + i) * C, C) for i in range(G)]
        b = {}
        for i, d in units:
            b[i, d] = _split_dot(la_ref[rows[i], d * GLA_QK:(d + 1) * GLA_QK], directions[d][0], w_left=True)
        q_loc, k_bd = {}, {}
        for i in range(G):
            qc = qkvr_ref[rows[i], 0:GLA_QK].astype(F32) * (GLA_DK ** -0.5)
            kc = qkvr_ref[rows[i], GLA_QK:2 * GLA_QK].astype(F32)
            for d in range(2):
                _, _, ref_row, last_row = directions[d]
                bb = b[i, d]
                b_ref = bb[ref_row:ref_row + 1, :]
                b_last = bb[last_row:last_row + 1, :]
                q_loc[i, d] = (qc * jnp.exp(bb - b_ref)).astype(BF16)
                k_loc = kc * jnp.exp(b_ref - bb)
                k_bd[i, d] = jnp.where(bm_k, jnp.concatenate([k_loc] * GLA_HEADS, axis=0), 0.0).astype(BF16)
                qin_ref[d, rows[i], :] = (qc * jnp.exp(bb)).astype(BF16)
                kst_ref[d, rows[i], :] = (kc * jnp.exp(b_last - bb)).astype(BF16)
                dec_ref[d, _ds((n0 + i) * 8, 8), :] = jnp.broadcast_to(jnp.exp(b_last), (8, GLA_QK))
        a = {u: jnp.where(directions[u[1]][1], _nt_dot(q_loc[u], k_bd[u]), 0.0).astype(BF16) for u in units}
        v_bd = [jnp.where(bm_v, jnp.concatenate([v_rows(rows[i])] * GLA_HEADS, axis=0), jnp.zeros((), BF16))
                for i in range(G)]
        o = {u: jnp.dot(a[u], v_bd[u[0]], preferred_element_type=F32) for u in units}
        for i in range(G):
            osc_ref[rows[i], :] = o[i, 0] + o[i, 1]

    def update_group(n0):
        rows = [_ds((n0 + i) * C, C) for i in range(G)]
        upd = {u: _tn_dot(v_rows(rows[u[0]]), kst_ref[u[1], rows[u[0]], :]) for u in units}
        for i, d in units:
            acc = None
            for h, m in enumerate(head_lanes):
                part = jnp.where(m, upd[i, d][h * GLA_DV:(h + 1) * GLA_DV, :], 0.0)
                acc = part if acc is None else acc + part
            upd_ref[d, n0 + i] = acc

    def scan_step(i, carry):
        for s in range(nb):
            for d, n in ((0, s * nc + i), (1, s * nc + nc - 1 - i)):
                st = st_ref[2 * s + d]
                stq_ref[d, n] = st.astype(BF16)
                st_ref[2 * s + d] = st * dec_ref[d, pl.ds(n * 8, 1), :] + upd_ref[d, n]
        return carry

    def readout_group(n0):
        rows = [_ds((n0 + i) * C, C) for i in range(G)]
        o = {}
        for i, d in units:
            q_in = qin_ref[d, rows[i], :]
            q_heads = jnp.concatenate([jnp.where(m, q_in, jnp.zeros((), BF16)) for m in head_lanes], axis=0)
            o[i, d] = _nt_dot(q_heads, stq_ref[d, n0 + i])
        for i in range(G):
            both = o[i, 0] + o[i, 1]
            for h in range(GLA_HEADS):
                osc_ref[rows[i], h * GLA_DV:(h + 1) * GLA_DV] += both[h * C:(h + 1) * C, :]

    for_groups(intra_group)
    for_groups(update_group)
    if nc == G:
        for i in range(nc):
            scan_step(i, 0)
    else:
        lax.fori_loop(0, nc, scan_step, 0)
    for_groups(readout_group)

    rb = 256
    for i in range(nb * T // rb):
        rows = slice(i * rb, (i + 1) * rb)
        for h in range(GLA_HEADS):
            cols = slice(h * GLA_DV, (h + 1) * GLA_DV)
            o = _rms(osc_ref[rows, cols], og_ref[:, cols])
            r = qkvr_ref[rows, 2 * GLA_QK + GLA_V + h * GLA_DV:2 * GLA_QK + GLA_V + (h + 1) * GLA_DV].astype(F32)
            o_ref[rows, cols] = (o * _silu(r)).astype(o_ref.dtype)

    if want_state:
        for s in range(nb):
            for d in range(2):
                s_all = st_ref[2 * s + d].T
                for h in range(GLA_HEADS):
                    st_out_ref[s, 0, d, h] = s_all[h * GLA_DK:(h + 1) * GLA_DK, :]
            if n_zero_slots:
                st_out_ref[s, 1:] = jnp.zeros((n_zero_slots, 2, GLA_HEADS, GLA_DK, GLA_DV), F32)


def _gla(proj, alk, w2bd, a_b, out_g, s0, *, B, T, state_out=None, nb=1):
    has_s0 = s0 is not None
    want_state = state_out is not None
    n_qkvr = 2 * GLA_QK + 2 * GLA_V
    R = nb * T
    n_chunks = R // GLA_CHUNK
    in_specs = [pl.BlockSpec((R, n_qkvr), lambda b: (b, 0)),
                pl.BlockSpec((R, LANES), lambda b: (b, 0)),
                pl.BlockSpec(w2bd.shape, lambda b: (0, 0)),
                pl.BlockSpec((1, 2 * GLA_QK), lambda b: (0, 0)),
                pl.BlockSpec((1, GLA_V), lambda b: (0, 0))]
    args = [proj, alk, w2bd, a_b, out_g]
    if has_s0:
        in_specs.append(pl.BlockSpec((nb, 2, GLA_HEADS, GLA_DK, GLA_DV), lambda b: (b, 0, 0, 0, 0)))
        args.append(s0)
    out_specs = [pl.BlockSpec((R, GLA_V), lambda b: (b, 0))]
    out_shape = [jax.ShapeDtypeStruct((B * T, GLA_V), BF16)]
    aliased, n_zero = [], 0
    if want_state:
        spec, shape, aliased, n_zero = _slot_output(state_out, B, (2, GLA_HEADS, GLA_DK, GLA_DV), F32, nb)
        out_specs.append(spec)
        out_shape.append(shape)
    res = pl.pallas_call(
        functools.partial(_gla_body, T=T, has_s0=has_s0, want_state=want_state, n_alias=len(aliased),
                          n_zero_slots=n_zero, nb=nb),
        grid=(B // nb,),
        in_specs=[pl.BlockSpec(memory_space=pl.ANY)] * len(aliased) + in_specs,
        out_specs=out_specs,
        out_shape=out_shape,
        input_output_aliases={i: 1 + i for i in range(len(aliased))},
        scratch_shapes=[pltpu.VMEM((R, GLA_V), F32),
                        pltpu.VMEM((2 * nb, GLA_DV, GLA_QK), F32),
                        pltpu.VMEM((R, 2 * GLA_QK), F32),
                        pltpu.VMEM((2, R, GLA_QK), BF16),
                        pltpu.VMEM((2, R, GLA_QK), BF16),
                        pltpu.VMEM((2, 8 * n_chunks, GLA_QK), F32),
                        pltpu.VMEM((2, n_chunks, GLA_DV, GLA_QK), F32),
                        pltpu.VMEM((2, n_chunks, GLA_DV, GLA_QK), BF16)],
        compiler_params=_cparams(("arbitrary",), VMEM_LIMIT),
        name="gla",
    )(*aliased, *args)
    return res if want_state else (res[0], None)


LOG2E = 1.4426950408889634


def _attend(qs, ks, vs, group):
    def qk(g0):
        return [_nt_dot(q, k()) for q, k in zip(qs[g0:g0 + group], ks[g0:g0 + group])]

    outs = []
    scores = qk(0)
    for g0 in range(0, len(qs), group):
        probs = [jnp.exp2(s - jnp.max(s, axis=-1, keepdims=True)).astype(BF16) for s in scores]
        if g0 + group < len(qs):
            scores = qk(g0 + group)
        res = [jnp.dot(p, v(), preferred_element_type=F32) for p, v in zip(probs, vs[g0:g0 + group])]
        outs += [r[:, :LANES] / r[:, LANES:] for r in res]
    return outs


def _head_group(n_keys, n_heads):
    return 4 if n_keys <= 512 else 2


def _stacked_ms(xs, ones_mat, inv_n):
    n = xs[0].shape[0]
    ms = _split_dot(jnp.concatenate([x * x for x in xs], axis=0), ones_mat) * inv_n
    return [ms[i * n:(i + 1) * n] for i in range(len(xs))]


def _split_dot(x, w, w_left=False):
    hi = x.astype(BF16)
    lo = (x - hi.astype(F32)).astype(BF16)
    if w_left:
        return jnp.dot(w, hi, preferred_element_type=F32) + jnp.dot(w, lo, preferred_element_type=F32)
    return jnp.dot(hi, w, preferred_element_type=F32) + jnp.dot(lo, w, preferred_element_type=F32)


def _group_ones(group):
    r = lax.broadcasted_iota(jnp.int32, (LANES, LANES), 0) // group
    c = lax.broadcasted_iota(jnp.int32, (LANES, LANES), 1) // group
    return (r == c).astype(BF16)


def _mla_body(*refs, T, S_ctx, rope, want_ckv, group, n_alias, n_zero_slots, nb):
    refs = refs[n_alias:]
    cq_ref, kpe_ref, ckv_ref = refs[:3]
    pos = 3
    if S_ctx:
        ckvc_ref, kpec_ref = refs[pos:pos + 2]
        pos += 2
    if rope:
        rq_refs = refs[pos:pos + 3]
        rk_refs = refs[pos + 3:pos + 6]
        pos += 6
    qng_ref, wqb_ref, qg_ref, kvg_ref, wkvb_ref, kg_ref = refs[pos:pos + 6]
    pos += 6
    o_ref = refs[pos]
    pos += 1
    if want_ckv:
        ckvn_ref = refs[pos]
        pos += 1
    k_sc, vlo_sc, vhi_sc = refs[pos:pos + 3]

    H = MLA_HEADS
    n_k = H * LANES
    lo = lax.broadcasted_iota(jnp.int32, (1, LANES), 1) < MLA_V

    def head_ms(xs):
        return [jnp.sum(x * x, axis=-1, keepdims=True) * (1.0 / MLA_QK) for x in xs]

    def fill_kv(ckvn, kpe, row0, n_rows, with_rope):
        rows = slice(row0, row0 + n_rows)
        kv = jnp.dot(ckvn.astype(BF16), wkvb_ref[...], preferred_element_type=F32)
        khs = [kv[:, h * LANES:(h + 1) * LANES] + kpe for h in range(H)]
        for h, (kh, ms) in enumerate(zip(khs, head_ms(khs))):
            kh = kh * lax.rsqrt(ms + EPS) * kg_ref[...]
            if with_rope:
                kh = _rope(kh, rk_refs[0][...], rk_refs[1][...], rk_refs[2][...], MLA_ROPE // 4)
            k_sc[rows, h * LANES:(h + 1) * LANES] = kh.astype(BF16)
        ones = jnp.ones((n_rows, LANES), BF16)
        for p in range(H // 2):
            v = kv[:, n_k + p * LANES:n_k + (p + 1) * LANES]
            vlo_sc[rows, 2 * p * LANES:(2 * p + 1) * LANES] = jnp.where(lo, v, 0.0).astype(BF16)
            vhi_sc[rows, 2 * p * LANES:(2 * p + 1) * LANES] = jnp.where(lo, 0.0, v).astype(BF16)
            vlo_sc[rows, (2 * p + 1) * LANES:(2 * p + 2) * LANES] = ones
            vhi_sc[rows, (2 * p + 1) * LANES:(2 * p + 2) * LANES] = ones

    S = S_ctx + T
    tq = cq_ref.shape[0] // nb

    @pl.when(pl.program_id(1) == 0)
    def _():
        for s in range(nb):
            if S_ctx:
                fill_kv(ckvc_ref[s], kpec_ref[s], s * S, S_ctx, False)
            ckvn = _rms(ckv_ref[s * T:(s + 1) * T, :], kvg_ref[...])
            if want_ckv:
                ckvn_ref[s, 0] = ckvn
                if n_zero_slots:
                    ckvn_ref[s, 1:] = jnp.zeros((n_zero_slots, T, MLA_KV_RANK), F32)
            fill_kv(ckvn, kpe_ref[s * T:(s + 1) * T, :].astype(F32), s * S + S_ctx, T, rope)

    cqn = _rms(cq_ref[...].astype(F32), qng_ref[...])
    q = jnp.dot(cqn.astype(BF16), wqb_ref[...], preferred_element_type=F32)
    units = [(s, h) for s in range(nb) for h in range(H)]
    qhs = [q[s * tq:(s + 1) * tq, h * LANES:(h + 1) * LANES] for s, h in units]
    qs = []
    for qh, ms in zip(qhs, head_ms(qhs)):
        qh = qh * lax.rsqrt(ms + EPS) * qg_ref[...]
        if rope:
            qh = _rope(qh, rq_refs[0][...], rq_refs[1][...], rq_refs[2][...], MLA_ROPE // 4)
        qs.append(qh.astype(BF16))
    ks = [functools.partial(lambda s, h: k_sc[s * S:(s + 1) * S, h * LANES:(h + 1) * LANES], s, h)
          for s, h in units]
    vs = [functools.partial(lambda s, h: (vlo_sc, vhi_sc)[h % 2][s * S:(s + 1) * S,
                                                                (h // 2) * 2 * LANES:(h // 2 + 1) * 2 * LANES], s, h)
          for s, h in units]
    outs = _attend(qs, ks, vs, group)
    for s in range(nb):
        for p in range(H // 2):
            o_pair = outs[s * H + 2 * p] + outs[s * H + 2 * p + 1]
            o_ref[s * tq:(s + 1) * tq, p * LANES:(p + 1) * LANES] = o_pair.astype(o_ref.dtype)


def _mla(proj, ckv, ctx, rope_tabs, params, *, B, T, tq, ckv_out=None, nb=1):
    want_ckv = ckv_out is not None
    nq = T // tq
    assert nb == 1 or nq == 1
    S_ctx = ctx[0].shape[1] if ctx is not None else 0
    S = S_ctx + T
    rope = rope_tabs is not None
    cq_blk = (2 * GLA_QK + 2 * GLA_V) // MLA_Q_RANK
    kpe_blk = (2 * GLA_QK + 2 * GLA_V + MLA_Q_RANK) // LANES
    in_specs = [pl.BlockSpec((nb * tq, MLA_Q_RANK), lambda b, j: (b * nq + j, cq_blk)),
                pl.BlockSpec((nb * T, LANES), lambda b, j: (b, kpe_blk)),
                pl.BlockSpec((nb * T, MLA_KV_RANK), lambda b, j: (b, 0))]
    args = [proj, proj, ckv]
    if S_ctx:
        in_specs += [pl.BlockSpec((nb, S_ctx, MLA_KV_RANK), lambda b, j: (b, 0, 0)),
                     pl.BlockSpec((nb, S_ctx, LANES), lambda b, j: (b, 0, 0))]
        args += list(ctx)
    if rope:
        in_specs += [pl.BlockSpec((tq, LANES), lambda b, j: (j, 0))] * 3
        in_specs += [pl.BlockSpec((T, LANES), lambda b, j: (0, 0))] * 3
        args += list(rope_tabs) * 2
    in_specs += [pl.BlockSpec(p.shape, lambda b, j: (0, 0)) for p in params]
    args += list(params)
    out_specs = [pl.BlockSpec((nb * tq, MLA_HEADS * MLA_V), lambda b, j: (b * nq + j, 0))]
    out_shape = [jax.ShapeDtypeStruct((B * T, MLA_HEADS * MLA_V), BF16)]
    aliased, n_zero = [], 0
    if want_ckv:
        spec, shape, aliased, n_zero = _slot_output(ckv_out, B, (T, MLA_KV_RANK), F32, nb)
        out_specs.append(spec)
        out_shape.append(shape)
    res = pl.pallas_call(
        functools.partial(_mla_body, T=T, S_ctx=S_ctx, rope=rope, want_ckv=want_ckv,
                          group=_head_group(S, MLA_HEADS), n_alias=len(aliased), n_zero_slots=n_zero, nb=nb),
        grid=(B // nb, nq),
        in_specs=[pl.BlockSpec(memory_space=pl.ANY)] * len(aliased) + in_specs,
        out_specs=out_specs,
        out_shape=out_shape,
        input_output_aliases={i: 1 + i for i in range(len(aliased))},
        scratch_shapes=[pltpu.VMEM((nb * S, MLA_HEADS * LANES), BF16)] * 3,
        compiler_params=_cparams(("arbitrary", "arbitrary"), VMEM_LIMIT),
        name="mla",
    )(*aliased, *args)
    return res if want_ckv else (res[0], None)


def _gqa_body(*refs, T, S_ctx, rope, want_kv, group, n_alias, n_zero_slots, nb):
    refs = refs[n_alias:]
    q_ref, k_ref, v_ref = refs[:3]
    pos = 3
    if S_ctx:
        kc_ref, vc_ref = refs[pos:pos + 2]
        pos += 2
    if rope:
        rq_refs = refs[pos:pos + 3]
        rk_refs = refs[pos + 3:pos + 6]
        pos += 6
    qg_ref, kg_ref = refs[pos:pos + 2]
    pos += 2
    o_ref = refs[pos]
    pos += 1
    if want_kv:
        kn_ref, vo_ref = refs[pos:pos + 2]
        pos += 2
    klo_sc, khi_sc, vlo_sc, vhi_sc = refs[pos:pos + 4]

    lo = lax.broadcasted_iota(jnp.int32, (1, LANES), 1) < GQA_DH
    half_sum = _group_ones(GQA_DH)

    def head_ms(x):
        return _split_dot(x * x, half_sum) * (1.0 / GQA_DH)

    def scatter_halves(x, lo_sc, hi_sc, c, rows):
        rolled = pltpu.roll(x, GQA_DH, 1)
        lo_sc[2 * c, rows, :LANES] = jnp.where(lo, x, 0.0).astype(BF16)
        hi_sc[2 * c, rows, :LANES] = jnp.where(lo, 0.0, rolled).astype(BF16)
        lo_sc[2 * c + 1, rows, :LANES] = jnp.where(lo, rolled, 0.0).astype(BF16)
        hi_sc[2 * c + 1, rows, :LANES] = jnp.where(lo, 0.0, x).astype(BF16)

    def fill_ones(rows):
        ones = jnp.ones((rows.stop - rows.start, LANES), BF16)
        for g in range(GQA_KV_HEADS):
            vlo_sc[g, rows, LANES:] = ones
            vhi_sc[g, rows, LANES:] = ones

    S = S_ctx + T
    tq = q_ref.shape[0] // nb

    @pl.when(pl.program_id(1) == 0)
    def _():
        fill_ones(slice(0, nb * S))
        for s in range(nb):
            if want_kv and n_zero_slots:
                zeros = jnp.zeros((n_zero_slots, T, GQA_KV_HEADS * GQA_DH), F32)
                kn_ref[s, 1:] = zeros
                vo_ref[s, 1:] = zeros
            for c in range(GQA_KV_HEADS // 2):
                cols = slice(c * LANES, (c + 1) * LANES)
                if S_ctx:
                    scatter_halves(kc_ref[s, :, cols], klo_sc, khi_sc, c, slice(s * S, s * S + S_ctx))
                    scatter_halves(vc_ref[s, :, cols], vlo_sc, vhi_sc, c, slice(s * S, s * S + S_ctx))
                kx = k_ref[s * T:(s + 1) * T, cols]
                kn = kx * lax.rsqrt(head_ms(kx) + EPS) * kg_ref[...]
                vx = v_ref[s * T:(s + 1) * T, cols]
                if want_kv:
                    kn_ref[s, 0, :, cols] = kn
                    vo_ref[s, 0, :, cols] = vx
                if rope:
                    kn = _rope(kn, rk_refs[0][...], rk_refs[1][...], rk_refs[2][...], GQA_DH // 4)
                scatter_halves(kn, klo_sc, khi_sc, c, slice(s * S + S_ctx, (s + 1) * S))
                scatter_halves(vx, vlo_sc, vhi_sc, c, slice(s * S + S_ctx, (s + 1) * S))

    n_pairs = GQA_HEADS // 2
    units = [(s, p) for s in range(nb) for p in range(n_pairs)]
    qxs = [q_ref[s * tq:(s + 1) * tq, p * LANES:(p + 1) * LANES].astype(F32) for s, p in units]
    qs, ks, vs = [], [], []
    for (s, p), qx, ms in zip(units, qxs, _stacked_ms(qxs, half_sum, 1.0 / GQA_DH)):
        qn = qx * lax.rsqrt(ms + EPS) * qg_ref[...]
        if rope:
            qn = _rope(qn, rq_refs[0][...], rq_refs[1][...], rq_refs[2][...], GQA_DH // 4)
        g = p // 2
        qs += [qn.astype(BF16)] * 2
        ks += [functools.partial(lambda r, g, s: r[g, s * S:(s + 1) * S, :], r, g, s) for r in (klo_sc, khi_sc)]
        vs += [functools.partial(lambda r, g, s: r[g, s * S:(s + 1) * S, :], r, g, s) for r in (vlo_sc, vhi_sc)]
    outs = _attend(qs, ks, vs, group)
    for i, (s, p) in enumerate(units):
        o_pair = outs[2 * i] + outs[2 * i + 1]
        o_ref[s * tq:(s + 1) * tq, p * LANES:(p + 1) * LANES] = o_pair.astype(o_ref.dtype)


def _gqa(q, k, v, ctx, rope_tabs, params, *, B, T, tq, kv_out=None, nb=1):
    want_kv = kv_out is not None
    nq = T // tq
    assert nb == 1 or nq == 1
    S_ctx = ctx[0].shape[1] if ctx is not None else 0
    S = S_ctx + T
    rope = rope_tabs is not None
    n_q = GQA_HEADS * GQA_DH
    n_kv = GQA_KV_HEADS * GQA_DH
    in_specs = [pl.BlockSpec((nb * tq, n_q), lambda b, j: (b * nq + j, 0)),
                pl.BlockSpec((nb * T, n_kv), lambda b, j: (b, 0)),
                pl.BlockSpec((nb * T, n_kv), lambda b, j: (b, 0))]
    args = [q, k, v]
    if S_ctx:
        in_specs += [pl.BlockSpec((nb, S_ctx, n_kv), lambda b, j: (b, 0, 0))] * 2
        args += list(ctx)
    if rope:
        in_specs += [pl.BlockSpec((tq, LANES), lambda b, j: (j, 0))] * 3
        in_specs += [pl.BlockSpec((T, LANES), lambda b, j: (0, 0))] * 3
        args += list(rope_tabs) * 2
    in_specs += [pl.BlockSpec(p.shape, lambda b, j: (0, 0)) for p in params]
    args += list(params)
    out_specs = [pl.BlockSpec((nb * tq, n_q), lambda b, j: (b * nq + j, 0))]
    out_shape = [jax.ShapeDtypeStruct((B * T, n_q), BF16)]
    aliased, n_zero = [], 0
    if want_kv:
        slot, n_slots, prev_k, prev_v = kv_out
        for prev in (prev_k, prev_v):
            spec, shape, al, n_zero = _slot_output((slot, n_slots, prev), B, (T, n_kv), F32, nb)
            out_specs.append(spec)
            out_shape.append(shape)
            aliased += al
    res = pl.pallas_call(
        functools.partial(_gqa_body, T=T, S_ctx=S_ctx, rope=rope, want_kv=want_kv,
                          group=_head_group(S, GQA_HEADS), n_alias=len(aliased), n_zero_slots=n_zero, nb=nb),
        grid=(B // nb, nq),
        in_specs=[pl.BlockSpec(memory_space=pl.ANY)] * len(aliased) + in_specs,
        out_specs=out_specs,
        out_shape=out_shape,
        input_output_aliases={i: 1 + i for i in range(len(aliased))},
        scratch_shapes=([pltpu.VMEM((GQA_KV_HEADS, nb * S, LANES), BF16)] * 2
                        + [pltpu.VMEM((GQA_KV_HEADS, nb * S, 2 * LANES), BF16)] * 2),
        compiler_params=_cparams(("arbitrary", "arbitrary"), VMEM_LIMIT),
        name="gqa",
    )(*aliased, *args)
    return res if want_kv else (res[0], None, None)


def _rope_tables(n_tok, d_rot, lead, reps):
    t = jnp.arange(n_tok, dtype=jnp.int32)
    posn = jnp.stack([t // GRID_W, t % GRID_W], axis=-1).astype(F32)
    quarter = d_rot // 4
    inv = jnp.power(ROPE_THETA, -jnp.arange(quarter, dtype=F32) / quarter)
    ang = posn[:, :, None] * inv
    cos, sin = jnp.cos(ang), jnp.sin(ang)
    zero = jnp.zeros_like(sin)
    c_rot = jnp.stack([cos, cos], axis=2).reshape(n_tok, d_rot)
    sp_rot = jnp.stack([-sin, zero], axis=2).reshape(n_tok, d_rot)
    sm_rot = jnp.stack([zero, sin], axis=2).reshape(n_tok, d_rot)
    tail = LANES - lead - reps * d_rot

    def embed(rot, fill):
        parts = [jnp.full((n_tok, lead), fill, F32)] + [rot] * reps + [jnp.full((n_tok, tail), fill, F32)]
        return jnp.concatenate(parts, axis=1)

    return embed(c_rot, 1.0), embed(sp_rot, 0.0), embed(sm_rot, 0.0)


def _pad_lanes(x, lead, width=LANES):
    pad = [(0, 0)] * (x.ndim - 1) + [(lead, width - lead - x.shape[-1])]
    return jnp.pad(x, pad)


def _ab_params(w_in, w_out, a_w2, a_b, out_g, q_norm_g, w_qb, kv_norm_g, w_kvb, qn_g, kn_g):
    d = w_in.shape[0]
    o_alo = 2 * GLA_QK + 2 * GLA_V
    o_cq = o_alo + 2 * GLA_RANK
    o_ckv = o_cq + MLA_Q_RANK
    o_kpe = o_ckv + MLA_KV_RANK
    w_kpe = w_in[:, o_kpe:]
    w_perm = jnp.concatenate([
        w_in[:, :o_alo], w_in[:, o_cq:o_ckv], _pad_lanes(w_kpe, MLA_NOPE),
        w_in[:, o_ckv:o_kpe],
        _pad_lanes(jnp.concatenate([w_in[:, o_alo:o_cq], w_kpe], axis=1), 0),
    ], axis=1).astype(BF16)
    w2bd = jnp.zeros((LANES, 2 * GLA_QK), F32)
    w2bd = w2bd.at[:GLA_RANK, :GLA_QK].set(a_w2[0]).at[GLA_RANK:2 * GLA_RANK, GLA_QK:].set(a_w2[1])
    w_qb_p = _pad_lanes(w_qb.reshape(MLA_Q_RANK, MLA_HEADS, MLA_QK), 0).reshape(MLA_Q_RANK, MLA_HEADS * LANES)
    kvb = w_kvb.reshape(MLA_KV_RANK, MLA_HEADS, MLA_NOPE + MLA_V)
    w_kvb_p = jnp.concatenate([
        _pad_lanes(kvb[:, :, :MLA_NOPE], 0).reshape(MLA_KV_RANK, MLA_HEADS * LANES),
        kvb[:, :, MLA_NOPE:].reshape(MLA_KV_RANK, MLA_HEADS * MLA_V)], axis=1)
    return dict(
        w_perm=w_perm,
        w2bd=w2bd.astype(BF16),
        a_b=a_b.reshape(1, 2 * GLA_QK),
        out_g=jnp.tile(out_g, GLA_HEADS).reshape(1, GLA_V),
        mla=(q_norm_g.reshape(1, -1), w_qb_p.astype(BF16),
             _pad_lanes(qn_g * (MLA_QK ** -0.5 * LOG2E), 0).reshape(1, LANES),
             kv_norm_g.reshape(1, -1), w_kvb_p.astype(BF16),
             _pad_lanes(kn_g, 0).reshape(1, LANES)),
        w_out_gla=w_out[:GLA_V].astype(BF16),
        w_out_mla=w_out[GLA_V:].astype(BF16),
    )


AB_OUTS = ((2 * GLA_QK + 2 * GLA_V + MLA_Q_RANK + LANES, BF16), (MLA_KV_RANK, F32), (LANES, F32))
C_OUTS = ((GQA_HEADS * GQA_DH, BF16), (GQA_KV_HEADS * GQA_DH, F32), (GQA_KV_HEADS * GQA_DH, F32))


def kernel(x_prompt, x_sample, c, cache_mla_ckv, cache_mla_kpe, state_gla, cache_gqa_k, cache_gqa_v,
           c_ctx, ada_w, ada_b, norm_mix_g, norm_ffn_g, ffn_w_in, ffn_w_out, ab_w_in, ab_w_out,
           gla_a_w2, gla_a_b, gla_out_g, mla_q_norm_g, mla_w_qb, mla_kv_norm_g, mla_w_kvb, mla_qn_g,
           mla_kn_g, gqa_w_in, gqa_w_out, gqa_qn_g, gqa_kn_g):
    Bp, Tp, D = x_prompt.shape
    Bs, Ts, _ = x_sample.shape
    depth = ada_w.shape[0]
    xp = x_prompt.reshape(Bp * Tp, D)
    xs = x_sample.reshape(Bs * Ts, D)

    cond = jnp.zeros((8, D), F32).at[0].set(c_ctx).at[2:2 + Bs].set(c)
    mods = _adaln(cond, ada_w, ada_b).reshape(depth, 8, 6, D)

    rope_mla = _rope_tables(Ts, MLA_ROPE, MLA_NOPE, 1)
    rope_gqa = _rope_tables(Ts, GQA_DH, 0, 2)

    streams = (dict(mod_base=0, rows_per_cond=None, tm=512),
               dict(mod_base=2, rows_per_cond=Ts, tm=512))
    ffn_tm = 2048
    prompt_nb = 2

    n_ab, n_c = (depth + 1) // 2, depth // 2
    new_ckv = new_gla = new_k = new_v = None
    new_kpe = []
    for l in range(depth):
        i = l // 2
        mod = mods[l]
        if l % 2 == 0:
            P = _ab_params(ab_w_in[i], ab_w_out[i], gla_a_w2[i], gla_a_b[i], gla_out_g[i], mla_q_norm_g[i],
                           mla_w_qb[i], mla_kv_norm_g[i], mla_w_kvb[i], mla_qn_g[i], mla_kn_g[i])
            pj_p, ckv_p, alk_p = _inproj(xp, norm_mix_g[l], mod, P['w_perm'], AB_OUTS, **streams[0])
            pj_s, ckv_s, alk_s = _inproj(xs, norm_mix_g[l], mod, P['w_perm'], AB_OUTS, **streams[1])
            og_p, new_gla = _gla(pj_p, alk_p, P['w2bd'], P['a_b'], P['out_g'], None, B=Bp, T=Tp,
                                 state_out=(i, n_ab, new_gla), nb=prompt_nb)
            og_s, _ = _gla(pj_s, alk_s, P['w2bd'], P['a_b'], P['out_g'], state_gla[:, i], B=Bs, T=Ts)
            om_p, new_ckv = _mla(pj_p, ckv_p, None, None, P['mla'], B=Bp, T=Tp, tq=Tp,
                                 ckv_out=(i, n_ab, new_ckv), nb=prompt_nb)
            ctx = (cache_mla_ckv[:, i], _pad_lanes(cache_mla_kpe[:, i], MLA_NOPE))
            om_s, _ = _mla(pj_s, ckv_s, ctx, rope_mla, P['mla'], B=Bs, T=Ts, tq=256)
            wps = (P['w_out_gla'], P['w_out_mla'])
            acts_p, acts_s = (og_p, om_p), (og_s, om_s)
            new_kpe.append(alk_p[:, 2 * GLA_RANK:2 * GLA_RANK + MLA_ROPE].reshape(Bp, Tp, MLA_ROPE))
        else:
            w_in = gqa_w_in[i].astype(BF16)
            w_out = gqa_w_out[i].astype(BF16)
            gp = (jnp.tile(gqa_qn_g[i] * (GQA_DH ** -0.5 * LOG2E), 2).reshape(1, LANES),
                  jnp.tile(gqa_kn_g[i], 2).reshape(1, LANES))
            q_p, k_p, v_p = _inproj(xp, norm_mix_g[l], mod, w_in, C_OUTS, **streams[0])
            q_s, k_s, v_s = _inproj(xs, norm_mix_g[l], mod, w_in, C_OUTS, **streams[1])
            o_p, new_k, new_v = _gqa(q_p, k_p, v_p, None, None, gp, B=Bp, T=Tp, tq=Tp,
                                     kv_out=(i, n_c, new_k, new_v), nb=prompt_nb)
            n_kv = GQA_KV_HEADS * GQA_DH
            ctx = (cache_gqa_k[:, i].reshape(Bs, -1, n_kv), cache_gqa_v[:, i].reshape(Bs, -1, n_kv))
            o_s, _, _ = _gqa(q_s, k_s, v_s, ctx, rope_gqa, gp, B=Bs, T=Ts, tq=256)
            wps = (w_out,)
            acts_p, acts_s = (o_p,), (o_s,)
        xp = _mix_ffn(acts_p, wps, xp, norm_ffn_g[l], mod, ffn_w_in, ffn_w_out, l, **dict(streams[0], tm=ffn_tm))
        xs = _mix_ffn(acts_s, wps, xs, norm_ffn_g[l], mod, ffn_w_in, ffn_w_out, l, **dict(streams[1], tm=ffn_tm))
    kv_shape = (Bp, n_c, Tp, GQA_KV_HEADS, GQA_DH)
    return (xp.reshape(Bp, Tp, D), xs.reshape(Bs, Ts, D), new_ckv, jnp.stack(new_kpe, axis=1), new_gla,
            new_k.reshape(kv_shape), new_v.reshape(kv_shape))
```

```python
import functools

import jax
import jax.numpy as jnp
from jax import lax
from jax.experimental import pallas as pl
from jax.experimental.pallas import tpu as pltpu

F32 = jnp.float32
BF16 = jnp.bfloat16

EPS = 1e-6
ROPE_THETA = 10000.0
GRID_W = 64
LANES = 128
GLA_HEADS, GLA_DK, GLA_DV = 4, 64, 128
GLA_QK = GLA_HEADS * GLA_DK
GLA_V = GLA_HEADS * GLA_DV
GLA_RANK = 16
GLA_TAU = 16.0
GLA_CHUNK = 64
MLA_HEADS = 8
MLA_Q_RANK, MLA_KV_RANK = 384, 256
MLA_NOPE, MLA_ROPE, MLA_V = 64, 32, 64
MLA_QK = MLA_NOPE + MLA_ROPE
GQA_HEADS, GQA_KV_HEADS, GQA_DH = 16, 4, 64
VMEM_LIMIT = 56 << 20


def _cparams(sem, vmem=None):
    return pltpu.CompilerParams(dimension_semantics=sem, vmem_limit_bytes=vmem)


def _nt_dot(a, b):
    return lax.dot_general(a, b, (((1,), (1,)), ((), ())), preferred_element_type=F32)


def _tn_dot(a, b):
    return lax.dot_general(a, b, (((0,), (0,)), ((), ())), preferred_element_type=F32)


def _ds(start, size):
    if isinstance(start, int):
        return pl.ds(start, size)
    return pl.ds(pl.multiple_of(start, size), size)


def _rms(x, g):
    ms = jnp.mean(x * x, axis=-1, keepdims=True)
    return x * lax.rsqrt(ms + EPS) * g


def _silu(x):
    return x * jax.nn.sigmoid(x)


def _rope(x, c, sp, sm, shift):
    return x * c + pltpu.roll(x, LANES - shift, 1) * sp + pltpu.roll(x, shift, 1) * sm


def _adaln_body(c_ref, w_ref, b_ref, o_ref):
    c = c_ref[...]
    s = _silu(c).astype(BF16)
    o_ref[0] = jnp.dot(s, w_ref[0].astype(BF16), preferred_element_type=F32) + b_ref[0]


def _adaln(cond, ada_w, ada_b):
    L, D, E = ada_w.shape
    tn = 1536
    return pl.pallas_call(
        _adaln_body,
        grid=(L, E // tn),
        in_specs=[pl.BlockSpec((8, D), lambda l, n: (0, 0)),
                  pl.BlockSpec((1, D, tn), lambda l, n: (l, 0, n)),
                  pl.BlockSpec((1, 1, tn), lambda l, n: (l, 0, n))],
        out_specs=pl.BlockSpec((1, 8, tn), lambda l, n: (l, 0, n)),
        out_shape=jax.ShapeDtypeStruct((L, 8, E), F32),
        compiler_params=_cparams(("arbitrary", "arbitrary")),
        name="adaln",
    )(cond, ada_w, ada_b.reshape(L, 1, E))


def _slot_output(slot_out, batch, tail, dtype, nb=1):
    slot, n_slots, prev = slot_out
    zeros = (0,) * len(tail)
    shape = jax.ShapeDtypeStruct((batch, n_slots) + tail, dtype)
    if slot == 0:
        return pl.BlockSpec((nb, n_slots) + tail, lambda b, *_: (b, 0) + zeros), shape, [], n_slots - 1
    return pl.BlockSpec((nb, 1) + tail, lambda b, *_: (b, slot) + zeros), shape, [prev], 0


def _mod_index(mod_base, rows_per_cond, tm):
    if rows_per_cond is None:
        return lambda i: (mod_base, 0, 0)
    n_mod = max(tm // rows_per_cond, 1)
    return lambda i: ((mod_base + (i * tm) // rows_per_cond) // n_mod, 0, 0)


def _inproj_body(xp_ref, xs_ref, g_ref, mod_ref, w_ref, *o_refs, widths, n_prompt_tiles):
    def project(x_ref):
        h = _rms(x_ref[...], g_ref[...])
        h = h * (1.0 + mod_ref[0, 1:2, :]) + mod_ref[0, 0:1, :]
        acc = jnp.dot(h.astype(BF16), w_ref[...], preferred_element_type=F32)
        off = 0
        for o_ref, w in zip(o_refs, widths):
            o_ref[...] = acc[:, off:off + w].astype(o_ref.dtype)
            off += w

    is_prompt = pl.program_id(0) < n_prompt_tiles
    pl.when(is_prompt)(lambda: project(xp_ref))
    pl.when(jnp.logical_not(is_prompt))(lambda: project(xs_ref))


def _inproj(xp, xs, g, mod, w, outs, *, sample_mod_base, rows_per_cond, tm):
    n_p, d = xp.shape
    n = n_p + xs.shape[0]
    np_t = n_p // tm
    widths = tuple(o[0] for o in outs)

    def mod_idx(i):
        return (jnp.where(i < np_t, 0, sample_mod_base + ((i - np_t) * tm) // rows_per_cond), 0, 0)

    return pl.pallas_call(
        functools.partial(_inproj_body, widths=widths, n_prompt_tiles=np_t),
        grid=(n // tm,),
        in_specs=[pl.BlockSpec((tm, d), lambda i: (jnp.minimum(i, np_t - 1), 0)),
                  pl.BlockSpec((tm, d), lambda i: (jnp.maximum(i - np_t, 0), 0)),
                  pl.BlockSpec((1, d), lambda i: (0, 0)),
                  pl.BlockSpec((1, 6, d), mod_idx),
                  pl.BlockSpec(w.shape, lambda i: (0, 0))],
        out_specs=[pl.BlockSpec((tm, wd), lambda i: (i, 0)) for wd in widths],
        out_shape=[jax.ShapeDtypeStruct((n, wd), dt) for wd, dt in outs],
        compiler_params=_cparams(("arbitrary",), VMEM_LIMIT),
        name="inproj",
    )(xp, xs, g.reshape(1, d), mod, w)


def _ffn_body(*refs, n_in, nk, row_chunk, rows_per_cond):
    a_refs, wp_refs = refs[:n_in], refs[n_in:2 * n_in]
    x_ref, g_ref, mod_ref, wg_ref, wu_ref, wo_ref, o_ref, h_ref, wgu_ref, wob_ref = refs[2 * n_in:]
    k = pl.program_id(1)
    tm = x_ref.shape[0]
    tk = wg_ref.shape[2]
    n_chunks = tm // row_chunk

    def mod_row(c):
        return 0 if rows_per_cond is None else (c * row_chunk) // rows_per_cond

    @pl.when(k == 0)
    def _():
        def mix_rows(c, carry):
            rows, m = _ds(c * row_chunk, row_chunk), mod_row(c)
            acc = None
            for a_ref, wp_ref in zip(a_refs, wp_refs):
                part = jnp.dot(a_ref[rows, :], wp_ref[...], preferred_element_type=F32)
                acc = part if acc is None else acc + part
            x_new = x_ref[rows, :] + mod_ref[m, 2:3, :] * acc
            o_ref[rows, :] = x_new
            h = _rms(x_new, g_ref[...])
            h = h * (1.0 + mod_ref[m, 4:5, :]) + mod_ref[m, 3:4, :]
            h_ref[rows, :] = h.astype(BF16)
            return carry
        lax.fori_loop(0, n_chunks, mix_rows, 0)

    wgu_ref[:, :tk] = wg_ref[0].astype(BF16)
    wgu_ref[:, tk:] = wu_ref[0].astype(BF16)
    wob_ref[...] = wo_ref[0].astype(BF16)

    for c in range(n_chunks):
        rows, m = _ds(c * row_chunk, row_chunk), mod_row(c)
        gu = jnp.dot(h_ref[rows, :], wgu_ref[...], preferred_element_type=F32)
        a = (_silu(gu[:, :tk]) * gu[:, tk:]).astype(BF16)
        o_ref[rows, :] += mod_ref[m, 5:6, :] * jnp.dot(a, wob_ref[...], preferred_element_type=F32)


def _mix_ffn(acts, wps, x, g, mod, w_in, w_out, layer, *, mod_base, rows_per_cond, tm):
    n, d = x.shape
    hidden = w_out.shape[1]
    tk = 256
    nk = hidden // tk
    n_in = len(acts)
    n_mod = 1 if rows_per_cond is None else max(tm // rows_per_cond, 1)
    once = dict(pipeline_mode=pl.Buffered(1))
    return pl.pallas_call(
        functools.partial(_ffn_body, n_in=n_in, nk=nk, row_chunk=512, rows_per_cond=rows_per_cond),
        grid=(n // tm, nk),
        in_specs=([pl.BlockSpec((tm, a.shape[1]), lambda i, k: (i, 0), **once) for a in acts]
                  + [pl.BlockSpec(w.shape, lambda i, k: (0, 0), **once) for w in wps]
                  + [pl.BlockSpec((tm, d), lambda i, k: (i, 0)),
                     pl.BlockSpec((1, d), lambda i, k: (0, 0)),
                     pl.BlockSpec((n_mod, 6, d),
                                  (lambda f: (lambda i, k: f(i)))(_mod_index(mod_base, rows_per_cond, tm))),
                     pl.BlockSpec((1, d, tk), lambda i, k: (layer, 0, k)),
                     pl.BlockSpec((1, d, tk), lambda i, k: (layer, 0, nk + k)),
                     pl.BlockSpec((1, tk, d), lambda i, k: (layer, k, 0))]),
        out_specs=pl.BlockSpec((tm, d), lambda i, k: (i, 0)),
        out_shape=jax.ShapeDtypeStruct((n, d), F32),
        scratch_shapes=[pltpu.VMEM((tm, d), BF16),
                        pltpu.VMEM((d, 2 * tk), BF16),
                        pltpu.VMEM((tk, d), BF16)],
        compiler_params=_cparams(("arbitrary", "arbitrary"), VMEM_LIMIT),
        name="mix_ffn",
    )(*acts, *wps, x, g.reshape(1, d), mod, w_in, w_in, w_out)


def _log_sigmoid(x):
    return jnp.minimum(x, 0.0) - jnp.log1p(jnp.exp(-jnp.abs(x)))


def _gla_body(*refs, T, has_s0, want_state, n_alias, n_zero_slots, nb):
    refs = refs[n_alias:]
    qkvr_ref, alk_ref, w2_ref, ab_ref, og_ref = refs[:5]
    pos = 5
    s0_ref = None
    if has_s0:
        s0_ref = refs[pos]
        pos += 1
    o_ref = refs[pos]
    pos += 1
    st_out_ref = kpe_out_ref = None
    if want_state:
        st_out_ref, kpe_out_ref = refs[pos:pos + 2]
        pos += 2
    osc_ref, st_ref, la_ref, qin_ref, kst_ref, dec_ref, upd_ref, stq_ref = refs[pos:pos + 8]

    C = GLA_CHUNK
    nc = T // C

    logit = jnp.dot(alk_ref[...].astype(BF16), w2_ref[...], preferred_element_type=F32) + ab_ref[...]
    la_ref[...] = _log_sigmoid(logit) * (1.0 / GLA_TAU)

    for s in range(nb):
        for d in range(2):
            if has_s0:
                st_ref[2 * s + d] = jnp.concatenate([s0_ref[s, d, h] for h in range(GLA_HEADS)], axis=0).T
            else:
                st_ref[2 * s + d] = jnp.zeros((GLA_DV, GLA_QK), F32)

    r64 = lax.broadcasted_iota(jnp.int32, (C, C), 0)
    c64 = lax.broadcasted_iota(jnp.int32, (C, C), 1)
    tri_f = (r64 >= c64).astype(BF16)
    tri_b = (c64 >= r64).astype(BF16)
    t_idx = lax.broadcasted_iota(jnp.int32, (C, GLA_QK), 0)
    s_idx = lax.broadcasted_iota(jnp.int32, (C, GLA_QK), 1) % C
    causal_f = t_idx >= s_idx
    causal_b = t_idx <= s_idx
    bm_k = (lax.broadcasted_iota(jnp.int32, (GLA_QK, GLA_QK), 0) // GLA_DK
            == lax.broadcasted_iota(jnp.int32, (GLA_QK, GLA_QK), 1) // GLA_DK)
    bm_v = (lax.broadcasted_iota(jnp.int32, (GLA_QK, GLA_V), 0) // C
            == lax.broadcasted_iota(jnp.int32, (GLA_QK, GLA_V), 1) // GLA_DV)
    head_lanes = [lax.broadcasted_iota(jnp.int32, (1, GLA_QK), 1) // GLA_DK == h for h in range(GLA_HEADS)]
    directions = ((tri_f, causal_f, C // 2 - 1, C - 1), (tri_b, causal_b, C // 2, 0))

    def v_rows(rows):
        return qkvr_ref[rows, 2 * GLA_QK:2 * GLA_QK + GLA_V]

    G = 4
    units = [(i, d) for i in range(G) for d in range(2)]

    n_groups = nb * nc // G

    def for_groups(fn):
        if n_groups <= 4:
            for g in range(n_groups):
                fn(g * G)
        else:
            def body(g, carry):
                fn(g * G)
                return carry
            lax.fori_loop(0, n_groups, body, 0)

    def intra_group(n0):
        rows = [_ds((n0 + i) * C, C) for i in range(G)]
        b = {}
        for i, d in units:
            b[i, d] = _split_dot(la_ref[rows[i], d * GLA_QK:(d + 1) * GLA_QK], directions[d][0], w_left=True)
        q_loc, k_bd = {}, {}
        for i in range(G):
            qc = qkvr_ref[rows[i], 0:GLA_QK].astype(F32) * (GLA_DK ** -0.5)
            kc = qkvr_ref[rows[i], GLA_QK:2 * GLA_QK].astype(F32)
            for d in range(2):
                _, _, ref_row, last_row = directions[d]
                bb = b[i, d]
                b_ref = bb[ref_row:ref_row + 1, :]
                b_last = bb[last_row:last_row + 1, :]
                q_loc[i, d] = (qc * jnp.exp(bb - b_ref)).astype(BF16)
                k_loc = kc * jnp.exp(b_ref - bb)
                k_bd[i, d] = jnp.where(bm_k, jnp.concatenate([k_loc] * GLA_HEADS, axis=0), 0.0).astype(BF16)
                qin_ref[d, rows[i], :] = (qc * jnp.exp(bb)).astype(BF16)
                kst_ref[d, rows[i], :] = (kc * jnp.exp(b_last - bb)).astype(BF16)
                dec_ref[d, _ds((n0 + i) * 8, 8), :] = jnp.broadcast_to(jnp.exp(b_last), (8, GLA_QK))
        a = {u: jnp.where(directions[u[1]][1], _nt_dot(q_loc[u], k_bd[u]), 0.0).astype(BF16) for u in units}
        v_bd = [jnp.where(bm_v, jnp.concatenate([v_rows(rows[i])] * GLA_HEADS, axis=0), jnp.zeros((), BF16))
                for i in range(G)]
        o = {u: jnp.dot(a[u], v_bd[u[0]], preferred_element_type=F32) for u in units}
        for i in range(G):
            osc_ref[rows[i], :] = o[i, 0] + o[i, 1]

    def update_group(n0):
        rows = [_ds((n0 + i) * C, C) for i in range(G)]
        upd = {u: _tn_dot(v_rows(rows[u[0]]), kst_ref[u[1], rows[u[0]], :]) for u in units}
        for i, d in units:
            acc = None
            for h, m in enumerate(head_lanes):
                part = jnp.where(m, upd[i, d][h * GLA_DV:(h + 1) * GLA_DV, :], 0.0)
                acc = part if acc is None else acc + part
            upd_ref[d, n0 + i] = acc

    def scan_step(i, carry):
        for s in range(nb):
            for d, n in ((0, s * nc + i), (1, s * nc + nc - 1 - i)):
                st = st_ref[2 * s + d]
                stq_ref[d, n] = st.astype(BF16)
                st_ref[2 * s + d] = st * dec_ref[d, pl.ds(n * 8, 1), :] + upd_ref[d, n]
        return carry

    def readout_group(n0):
        rows = [_ds((n0 + i) * C, C) for i in range(G)]
        o = {}
        for i, d in units:
            q_in = qin_ref[d, rows[i], :]
            q_heads = jnp.concatenate([jnp.where(m, q_in, jnp.zeros((), BF16)) for m in head_lanes], axis=0)
            o[i, d] = _nt_dot(q_heads, stq_ref[d, n0 + i])
        for i in range(G):
            both = o[i, 0] + o[i, 1]
            for h in range(GLA_HEADS):
                osc_ref[rows[i], h * GLA_DV:(h + 1) * GLA_DV] += both[h * C:(h + 1) * C, :]

    for_groups(intra_group)
    for_groups(update_group)
    if nc == G:
        for i in range(nc):
            scan_step(i, 0)
    else:
        lax.fori_loop(0, nc, scan_step, 0)
    for_groups(readout_group)

    rb = 256
    for i in range(nb * T // rb):
        rows = slice(i * rb, (i + 1) * rb)
        for h in range(GLA_HEADS):
            cols = slice(h * GLA_DV, (h + 1) * GLA_DV)
            o = _rms(osc_ref[rows, cols], og_ref[:, cols])
            r = qkvr_ref[rows, 2 * GLA_QK + GLA_V + h * GLA_DV:2 * GLA_QK + GLA_V + (h + 1) * GLA_DV].astype(F32)
            o_ref[rows, cols] = (o * _silu(r)).astype(o_ref.dtype)

    if want_state:
        for s in range(nb):
            for d in range(2):
                s_all = st_ref[2 * s + d].T
                for h in range(GLA_HEADS):
                    st_out_ref[s, 0, d, h] = s_all[h * GLA_DK:(h + 1) * GLA_DK, :]
            kpe_out_ref[s, 0] = alk_ref[s * T:(s + 1) * T, 0:MLA_ROPE]
            if n_zero_slots:
                st_out_ref[s, 1:] = jnp.zeros((n_zero_slots, 2, GLA_HEADS, GLA_DK, GLA_DV), F32)
                kpe_out_ref[s, 1:] = jnp.zeros((n_zero_slots, T, MLA_ROPE), F32)


def _gla(proj, alk, w2bd, a_b, out_g, s0, *, B, T, row0=0, state_out=None, nb=1):
    has_s0 = s0 is not None
    want_state = state_out is not None
    n_qkvr = 2 * GLA_QK + 2 * GLA_V
    R = nb * T
    n_chunks = R // GLA_CHUNK
    off = row0 // R
    assert row0 % R == 0
    in_specs = [pl.BlockSpec((R, n_qkvr), lambda b: (off + b, 0)),
                pl.BlockSpec((R, LANES), lambda b: (off + b, 0)),
                pl.BlockSpec(w2bd.shape, lambda b: (0, 0)),
                pl.BlockSpec((1, 2 * GLA_QK), lambda b: (0, 0)),
                pl.BlockSpec((1, GLA_V), lambda b: (0, 0))]
    args = [proj, alk, w2bd, a_b, out_g]
    if has_s0:
        states, layer = s0
        in_specs.append(pl.BlockSpec((nb, None, 2, GLA_HEADS, GLA_DK, GLA_DV),
                                     lambda b: (b, layer, 0, 0, 0, 0)))
        args.append(states)
    out_specs = [pl.BlockSpec((R, GLA_V), lambda b: (b, 0))]
    out_shape = [jax.ShapeDtypeStruct((B * T, GLA_V), BF16)]
    aliased, n_zero = [], 0
    if want_state:
        slot, n_slots, prev_state, prev_kpe = state_out
        for prev, tail in ((prev_state, (2, GLA_HEADS, GLA_DK, GLA_DV)), (prev_kpe, (T, MLA_ROPE))):
            spec, shape, al, n_zero = _slot_output((slot, n_slots, prev), B, tail, F32, nb)
            out_specs.append(spec)
            out_shape.append(shape)
            aliased += al
    res = pl.pallas_call(
        functools.partial(_gla_body, T=T, has_s0=has_s0, want_state=want_state, n_alias=len(aliased),
                          n_zero_slots=n_zero, nb=nb),
        grid=(B // nb,),
        in_specs=[pl.BlockSpec(memory_space=pl.ANY)] * len(aliased) + in_specs,
        out_specs=out_specs,
        out_shape=out_shape,
        input_output_aliases={i: 1 + i for i in range(len(aliased))},
        scratch_shapes=[pltpu.VMEM((R, GLA_V), F32),
                        pltpu.VMEM((2 * nb, GLA_DV, GLA_QK), F32),
                        pltpu.VMEM((R, 2 * GLA_QK), F32),
                        pltpu.VMEM((2, R, GLA_QK), BF16),
                        pltpu.VMEM((2, R, GLA_QK), BF16),
                        pltpu.VMEM((2, 8 * n_chunks, GLA_QK), F32),
                        pltpu.VMEM((2, n_chunks, GLA_DV, GLA_QK), F32),
                        pltpu.VMEM((2, n_chunks, GLA_DV, GLA_QK), BF16)],
        compiler_params=_cparams(("arbitrary",), VMEM_LIMIT),
        name="gla",
    )(*aliased, *args)
    return res if want_state else (res[0], None)


LOG2E = 1.4426950408889634


def _attend(qs, ks, vs, group):
    def qk(g0):
        return [_nt_dot(q, k()) for q, k in zip(qs[g0:g0 + group], ks[g0:g0 + group])]

    outs = []
    scores = qk(0)
    for g0 in range(0, len(qs), group):
        probs = [jnp.exp2(s - jnp.max(s, axis=-1, keepdims=True)).astype(BF16) for s in scores]
        if g0 + group < len(qs):
            scores = qk(g0 + group)
        res = [jnp.dot(p, v(), preferred_element_type=F32) for p, v in zip(probs, vs[g0:g0 + group])]
        outs += [r[:, :LANES] / r[:, LANES:] for r in res]
    return outs


def _head_group(n_keys, n_heads):
    return 4 if n_keys <= 512 else 2


def _stacked_ms(xs, ones_mat, inv_n):
    n = xs[0].shape[0]
    ms = _split_dot(jnp.concatenate([x * x for x in xs], axis=0), ones_mat) * inv_n
    return [ms[i * n:(i + 1) * n] for i in range(len(xs))]


def _split_dot(x, w, w_left=False):
    hi = x.astype(BF16)
    lo = (x - hi.astype(F32)).astype(BF16)
    if w_left:
        return jnp.dot(w, hi, preferred_element_type=F32) + jnp.dot(w, lo, preferred_element_type=F32)
    return jnp.dot(hi, w, preferred_element_type=F32) + jnp.dot(lo, w, preferred_element_type=F32)


def _group_ones(group):
    r = lax.broadcasted_iota(jnp.int32, (LANES, LANES), 0) // group
    c = lax.broadcasted_iota(jnp.int32, (LANES, LANES), 1) // group
    return (r == c).astype(BF16)


def _mla_body(*refs, T, S_ctx, rope, want_ckv, group, n_alias, n_zero_slots, nb):
    refs = refs[n_alias:]
    cq_ref, kpe_ref, ckv_ref = refs[:3]
    pos = 3
    if S_ctx:
        ckvc_ref, kpec_ref = refs[pos:pos + 2]
        pos += 2
    if rope:
        rq_refs = refs[pos:pos + 3]
        rk_refs = refs[pos + 3:pos + 6]
        pos += 6
    qng_ref, wqb_ref, qg_ref, kvg_ref, wkvb_ref, kg_ref = refs[pos:pos + 6]
    pos += 6
    o_ref = refs[pos]
    pos += 1
    if want_ckv:
        ckvn_ref = refs[pos]
        pos += 1
    k_sc, vlo_sc, vhi_sc = refs[pos:pos + 3]

    H = MLA_HEADS
    n_k = H * LANES
    lo = lax.broadcasted_iota(jnp.int32, (1, LANES), 1) < MLA_V

    def head_ms(xs):
        return [jnp.sum(x * x, axis=-1, keepdims=True) * (1.0 / MLA_QK) for x in xs]

    def fill_kv(ckvn, kpe, row0, n_rows, with_rope):
        rows = slice(row0, row0 + n_rows)
        kv = jnp.dot(ckvn.astype(BF16), wkvb_ref[...], preferred_element_type=F32)
        khs = [kv[:, h * LANES:(h + 1) * LANES] + kpe for h in range(H)]
        for h, (kh, ms) in enumerate(zip(khs, head_ms(khs))):
            kh = kh * lax.rsqrt(ms + EPS) * kg_ref[...]
            if with_rope:
                kh = _rope(kh, rk_refs[0][...], rk_refs[1][...], rk_refs[2][...], MLA_ROPE // 4)
            k_sc[rows, h * LANES:(h + 1) * LANES] = kh.astype(BF16)
        ones = jnp.ones((n_rows, LANES), BF16)
        for p in range(H // 2):
            v = kv[:, n_k + p * LANES:n_k + (p + 1) * LANES]
            vlo_sc[rows, 2 * p * LANES:(2 * p + 1) * LANES] = jnp.where(lo, v, 0.0).astype(BF16)
            vhi_sc[rows, 2 * p * LANES:(2 * p + 1) * LANES] = jnp.where(lo, 0.0, v).astype(BF16)
            vlo_sc[rows, (2 * p + 1) * LANES:(2 * p + 2) * LANES] = ones
            vhi_sc[rows, (2 * p + 1) * LANES:(2 * p + 2) * LANES] = ones

    S = S_ctx + T
    tq = cq_ref.shape[0] // nb

    @pl.when(pl.program_id(1) == 0)
    def _():
        for s in range(nb):
            if S_ctx:
                fill_kv(ckvc_ref[s], kpec_ref[s], s * S, S_ctx, False)
            ckvn = _rms(ckv_ref[s * T:(s + 1) * T, :], kvg_ref[...])
            if want_ckv:
                ckvn_ref[s, 0] = ckvn
                if n_zero_slots:
                    ckvn_ref[s, 1:] = jnp.zeros((n_zero_slots, T, MLA_KV_RANK), F32)
            fill_kv(ckvn, kpe_ref[s * T:(s + 1) * T, :].astype(F32), s * S + S_ctx, T, rope)

    cqn = _rms(cq_ref[...].astype(F32), qng_ref[...])
    q = jnp.dot(cqn.astype(BF16), wqb_ref[...], preferred_element_type=F32)
    units = [(s, h) for s in range(nb) for h in range(H)]
    qhs = [q[s * tq:(s + 1) * tq, h * LANES:(h + 1) * LANES] for s, h in units]
    qs = []
    for qh, ms in zip(qhs, head_ms(qhs)):
        qh = qh * lax.rsqrt(ms + EPS) * qg_ref[...]
        if rope:
            qh = _rope(qh, rq_refs[0][...], rq_refs[1][...], rq_refs[2][...], MLA_ROPE // 4)
        qs.append(qh.astype(BF16))
    ks = [functools.partial(lambda s, h: k_sc[s * S:(s + 1) * S, h * LANES:(h + 1) * LANES], s, h)
          for s, h in units]
    vs = [functools.partial(lambda s, h: (vlo_sc, vhi_sc)[h % 2][s * S:(s + 1) * S,
                                                                (h // 2) * 2 * LANES:(h // 2 + 1) * 2 * LANES], s, h)
          for s, h in units]
    outs = _attend(qs, ks, vs, group)
    for s in range(nb):
        for p in range(H // 2):
            o_pair = outs[s * H + 2 * p] + outs[s * H + 2 * p + 1]
            o_ref[s * tq:(s + 1) * tq, p * LANES:(p + 1) * LANES] = o_pair.astype(o_ref.dtype)


def _mla(proj, ckv, ctx, rope_tabs, params, *, B, T, tq, row0=0, ckv_out=None, nb=1):
    want_ckv = ckv_out is not None
    nq = T // tq
    assert nb == 1 or nq == 1
    S_ctx = ctx[0].shape[2] if ctx is not None else 0
    S = S_ctx + T
    rope = rope_tabs is not None
    cq_blk = (2 * GLA_QK + 2 * GLA_V) // MLA_Q_RANK
    kpe_blk = (2 * GLA_QK + 2 * GLA_V + MLA_Q_RANK) // LANES
    off_q, off_t = row0 // (nb * tq), row0 // (nb * T)
    assert row0 % (nb * T) == 0
    in_specs = [pl.BlockSpec((nb * tq, MLA_Q_RANK), lambda b, j: (off_q + b * nq + j, cq_blk)),
                pl.BlockSpec((nb * T, LANES), lambda b, j: (off_t + b, kpe_blk)),
                pl.BlockSpec((nb * T, MLA_KV_RANK), lambda b, j: (off_t + b, 0))]
    args = [proj, proj, ckv]
    if S_ctx:
        layer = ctx[2]
        in_specs += [pl.BlockSpec((nb, None, S_ctx, MLA_KV_RANK), lambda b, j: (b, layer, 0, 0)),
                     pl.BlockSpec((nb, None, S_ctx, LANES), lambda b, j: (b, layer, 0, 0))]
        args += list(ctx[:2])
    if rope:
        in_specs += [pl.BlockSpec((tq, LANES), lambda b, j: (j, 0))] * 3
        in_specs += [pl.BlockSpec((T, LANES), lambda b, j: (0, 0))] * 3
        args += list(rope_tabs) * 2
    in_specs += [pl.BlockSpec(p.shape, lambda b, j: (0, 0)) for p in params]
    args += list(params)
    out_specs = [pl.BlockSpec((nb * tq, MLA_HEADS * MLA_V), lambda b, j: (b * nq + j, 0))]
    out_shape = [jax.ShapeDtypeStruct((B * T, MLA_HEADS * MLA_V), BF16)]
    aliased, n_zero = [], 0
    if want_ckv:
        spec, shape, aliased, n_zero = _slot_output(ckv_out, B, (T, MLA_KV_RANK), F32, nb)
        out_specs.append(spec)
        out_shape.append(shape)
    res = pl.pallas_call(
        functools.partial(_mla_body, T=T, S_ctx=S_ctx, rope=rope, want_ckv=want_ckv,
                          group=_head_group(S, MLA_HEADS), n_alias=len(aliased), n_zero_slots=n_zero, nb=nb),
        grid=(B // nb, nq),
        in_specs=[pl.BlockSpec(memory_space=pl.ANY)] * len(aliased) + in_specs,
        out_specs=out_specs,
        out_shape=out_shape,
        input_output_aliases={i: 1 + i for i in range(len(aliased))},
        scratch_shapes=[pltpu.VMEM((nb * S, MLA_HEADS * LANES), BF16)] * 3,
        compiler_params=_cparams(("arbitrary", "arbitrary"), VMEM_LIMIT),
        name="mla",
    )(*aliased, *args)
    return res if want_ckv else (res[0], None)


def _gqa_body(*refs, T, S_ctx, rope, want_kv, group, n_alias, n_zero_slots, nb):
    refs = refs[n_alias:]
    q_ref, k_ref, v_ref = refs[:3]
    pos = 3
    if S_ctx:
        kc_ref, vc_ref = refs[pos:pos + 2]
        pos += 2
    if rope:
        rq_refs = refs[pos:pos + 3]
        rk_refs = refs[pos + 3:pos + 6]
        pos += 6
    qg_ref, kg_ref = refs[pos:pos + 2]
    pos += 2
    o_ref = refs[pos]
    pos += 1
    if want_kv:
        kn_ref, vo_ref = refs[pos:pos + 2]
        pos += 2
    klo_sc, khi_sc, vlo_sc, vhi_sc = refs[pos:pos + 4]

    lo = lax.broadcasted_iota(jnp.int32, (1, LANES), 1) < GQA_DH
    half_sum = _group_ones(GQA_DH)

    def head_ms(x):
        return _split_dot(x * x, half_sum) * (1.0 / GQA_DH)

    def scatter_halves(x, lo_sc, hi_sc, c, rows):
        rolled = pltpu.roll(x, GQA_DH, 1)
        lo_sc[2 * c, rows, :LANES] = jnp.where(lo, x, 0.0).astype(BF16)
        hi_sc[2 * c, rows, :LANES] = jnp.where(lo, 0.0, rolled).astype(BF16)
        lo_sc[2 * c + 1, rows, :LANES] = jnp.where(lo, rolled, 0.0).astype(BF16)
        hi_sc[2 * c + 1, rows, :LANES] = jnp.where(lo, 0.0, x).astype(BF16)

    def fill_ones(rows):
        ones = jnp.ones((rows.stop - rows.start, LANES), BF16)
        for g in range(GQA_KV_HEADS):
            vlo_sc[g, rows, LANES:] = ones
            vhi_sc[g, rows, LANES:] = ones

    S = S_ctx + T
    tq = q_ref.shape[0] // nb

    @pl.when(pl.program_id(1) == 0)
    def _():
        fill_ones(slice(0, nb * S))
        for s in range(nb):
            if want_kv and n_zero_slots:
                zeros = jnp.zeros((n_zero_slots, T, GQA_KV_HEADS * GQA_DH), F32)
                kn_ref[s, 1:] = zeros
                vo_ref[s, 1:] = zeros
            for c in range(GQA_KV_HEADS // 2):
                cols = slice(c * LANES, (c + 1) * LANES)
                if S_ctx:
                    scatter_halves(kc_ref[s, :, cols], klo_sc, khi_sc, c, slice(s * S, s * S + S_ctx))
                    scatter_halves(vc_ref[s, :, cols], vlo_sc, vhi_sc, c, slice(s * S, s * S + S_ctx))
                kx = k_ref[s * T:(s + 1) * T, cols]
                kn = kx * lax.rsqrt(head_ms(kx) + EPS) * kg_ref[...]
                vx = v_ref[s * T:(s + 1) * T, cols]
                if want_kv:
                    kn_ref[s, 0, :, cols] = kn
                    vo_ref[s, 0, :, cols] = vx
                if rope:
                    kn = _rope(kn, rk_refs[0][...], rk_refs[1][...], rk_refs[2][...], GQA_DH // 4)
                scatter_halves(kn, klo_sc, khi_sc, c, slice(s * S + S_ctx, (s + 1) * S))
                scatter_halves(vx, vlo_sc, vhi_sc, c, slice(s * S + S_ctx, (s + 1) * S))

    n_pairs = GQA_HEADS // 2
    units = [(s, p) for s in range(nb) for p in range(n_pairs)]
    qxs = [q_ref[s * tq:(s + 1) * tq, p * LANES:(p + 1) * LANES].astype(F32) for s, p in units]
    qs, ks, vs = [], [], []
    for (s, p), qx, ms in zip(units, qxs, _stacked_ms(qxs, half_sum, 1.0 / GQA_DH)):
        qn = qx * lax.rsqrt(ms + EPS) * qg_ref[...]
        if rope:
            qn = _rope(qn, rq_refs[0][...], rq_refs[1][...], rq_refs[2][...], GQA_DH // 4)
        g = p // 2
        qs += [qn.astype(BF16)] * 2
        ks += [functools.partial(lambda r, g, s: r[g, s * S:(s + 1) * S, :], r, g, s) for r in (klo_sc, khi_sc)]
        vs += [functools.partial(lambda r, g, s: r[g, s * S:(s + 1) * S, :], r, g, s) for r in (vlo_sc, vhi_sc)]
    outs = _attend(qs, ks, vs, group)
    for i, (s, p) in enumerate(units):
        o_pair = outs[2 * i] + outs[2 * i + 1]
        o_ref[s * tq:(s + 1) * tq, p * LANES:(p + 1) * LANES] = o_pair.astype(o_ref.dtype)


def _gqa(q, k, v, ctx, rope_tabs, params, *, B, T, tq, row0=0, kv_out=None, nb=1):
    want_kv = kv_out is not None
    nq = T // tq
    assert nb == 1 or nq == 1
    S_ctx = ctx[0].shape[2] if ctx is not None else 0
    S = S_ctx + T
    rope = rope_tabs is not None
    n_q = GQA_HEADS * GQA_DH
    n_kv = GQA_KV_HEADS * GQA_DH
    off_q, off_t = row0 // (nb * tq), row0 // (nb * T)
    assert row0 % (nb * T) == 0
    in_specs = [pl.BlockSpec((nb * tq, n_q), lambda b, j: (off_q + b * nq + j, 0)),
                pl.BlockSpec((nb * T, n_kv), lambda b, j: (off_t + b, 0)),
                pl.BlockSpec((nb * T, n_kv), lambda b, j: (off_t + b, 0))]
    args = [q, k, v]
    if S_ctx:
        layer = ctx[2]
        in_specs += [pl.BlockSpec((nb, None, S_ctx, n_kv), lambda b, j: (b, layer, 0, 0))] * 2
        args += list(ctx[:2])
    if rope:
        in_specs += [pl.BlockSpec((tq, LANES), lambda b, j: (j, 0))] * 3
        in_specs += [pl.BlockSpec((T, LANES), lambda b, j: (0, 0))] * 3
        args += list(rope_tabs) * 2
    in_specs += [pl.BlockSpec(p.shape, lambda b, j: (0, 0)) for p in params]
    args += list(params)
    out_specs = [pl.BlockSpec((nb * tq, n_q), lambda b, j: (b * nq + j, 0))]
    out_shape = [jax.ShapeDtypeStruct((B * T, n_q), BF16)]
    aliased, n_zero = [], 0
    if want_kv:
        slot, n_slots, prev_k, prev_v = kv_out
        for prev in (prev_k, prev_v):
            spec, shape, al, n_zero = _slot_output((slot, n_slots, prev), B, (T, n_kv), F32, nb)
            out_specs.append(spec)
            out_shape.append(shape)
            aliased += al
    res = pl.pallas_call(
        functools.partial(_gqa_body, T=T, S_ctx=S_ctx, rope=rope, want_kv=want_kv,
                          group=_head_group(S, GQA_HEADS), n_alias=len(aliased), n_zero_slots=n_zero, nb=nb),
        grid=(B // nb, nq),
        in_specs=[pl.BlockSpec(memory_space=pl.ANY)] * len(aliased) + in_specs,
        out_specs=out_specs,
        out_shape=out_shape,
        input_output_aliases={i: 1 + i for i in range(len(aliased))},
        scratch_shapes=([pltpu.VMEM((GQA_KV_HEADS, nb * S, LANES), BF16)] * 2
                        + [pltpu.VMEM((GQA_KV_HEADS, nb * S, 2 * LANES), BF16)] * 2),
        compiler_params=_cparams(("arbitrary", "arbitrary"), VMEM_LIMIT),
        name="gqa",
    )(*aliased, *args)
    return res if want_kv else (res[0], None, None)


def _rope_tables(n_tok, d_rot, lead, reps):
    t = jnp.arange(n_tok, dtype=jnp.int32)
    posn = jnp.stack([t // GRID_W, t % GRID_W], axis=-1).astype(F32)
    quarter = d_rot // 4
    inv = jnp.power(ROPE_THETA, -jnp.arange(quarter, dtype=F32) / quarter)
    ang = posn[:, :, None] * inv
    cos, sin = jnp.cos(ang), jnp.sin(ang)
    zero = jnp.zeros_like(sin)
    c_rot = jnp.stack([cos, cos], axis=2).reshape(n_tok, d_rot)
    sp_rot = jnp.stack([-sin, zero], axis=2).reshape(n_tok, d_rot)
    sm_rot = jnp.stack([zero, sin], axis=2).reshape(n_tok, d_rot)
    tail = LANES - lead - reps * d_rot

    def embed(rot, fill):
        parts = [jnp.full((n_tok, lead), fill, F32)] + [rot] * reps + [jnp.full((n_tok, tail), fill, F32)]
        return jnp.concatenate(parts, axis=1)

    return embed(c_rot, 1.0), embed(sp_rot, 0.0), embed(sm_rot, 0.0)


def _pad_lanes(x, lead, width=LANES):
    pad = [(0, 0)] * (x.ndim - 1) + [(lead, width - lead - x.shape[-1])]
    return jnp.pad(x, pad)


def _ab_params(w_in, w_out, a_w2, a_b, out_g, q_norm_g, w_qb, kv_norm_g, w_kvb, qn_g, kn_g):
    d = w_in.shape[0]
    o_alo = 2 * GLA_QK + 2 * GLA_V
    o_cq = o_alo + 2 * GLA_RANK
    o_ckv = o_cq + MLA_Q_RANK
    o_kpe = o_ckv + MLA_KV_RANK
    w_kpe = w_in[:, o_kpe:]
    w_perm = jnp.concatenate([
        w_in[:, :o_alo], w_in[:, o_cq:o_ckv], _pad_lanes(w_kpe, MLA_NOPE),
        w_in[:, o_ckv:o_kpe],
        _pad_lanes(jnp.concatenate([w_kpe, w_in[:, o_alo:o_cq]], axis=1), 0),
    ], axis=1).astype(BF16)
    w2bd = jnp.zeros((LANES, 2 * GLA_QK), F32)
    w2bd = (w2bd.at[MLA_ROPE:MLA_ROPE + GLA_RANK, :GLA_QK].set(a_w2[0])
            .at[MLA_ROPE + GLA_RANK:MLA_ROPE + 2 * GLA_RANK, GLA_QK:].set(a_w2[1]))
    w_qb_p = _pad_lanes(w_qb.reshape(MLA_Q_RANK, MLA_HEADS, MLA_QK), 0).reshape(MLA_Q_RANK, MLA_HEADS * LANES)
    kvb = w_kvb.reshape(MLA_KV_RANK, MLA_HEADS, MLA_NOPE + MLA_V)
    w_kvb_p = jnp.concatenate([
        _pad_lanes(kvb[:, :, :MLA_NOPE], 0).reshape(MLA_KV_RANK, MLA_HEADS * LANES),
        kvb[:, :, MLA_NOPE:].reshape(MLA_KV_RANK, MLA_HEADS * MLA_V)], axis=1)
    return dict(
        w_perm=w_perm,
        w2bd=w2bd.astype(BF16),
        a_b=a_b.reshape(1, 2 * GLA_QK),
        out_g=jnp.tile(out_g, GLA_HEADS).reshape(1, GLA_V),
        mla=(q_norm_g.reshape(1, -1), w_qb_p.astype(BF16),
             _pad_lanes(qn_g * (MLA_QK ** -0.5 * LOG2E), 0).reshape(1, LANES),
             kv_norm_g.reshape(1, -1), w_kvb_p.astype(BF16),
             _pad_lanes(kn_g, 0).reshape(1, LANES)),
        w_out_gla=w_out[:GLA_V].astype(BF16),
        w_out_mla=w_out[GLA_V:].astype(BF16),
    )


AB_OUTS = ((2 * GLA_QK + 2 * GLA_V + MLA_Q_RANK + LANES, BF16), (MLA_KV_RANK, F32), (LANES, F32))
C_OUTS = ((GQA_HEADS * GQA_DH, BF16), (GQA_KV_HEADS * GQA_DH, F32), (GQA_KV_HEADS * GQA_DH, F32))


def kernel(x_prompt, x_sample, c, cache_mla_ckv, cache_mla_kpe, state_gla, cache_gqa_k, cache_gqa_v,
           c_ctx, ada_w, ada_b, norm_mix_g, norm_ffn_g, ffn_w_in, ffn_w_out, ab_w_in, ab_w_out,
           gla_a_w2, gla_a_b, gla_out_g, mla_q_norm_g, mla_w_qb, mla_kv_norm_g, mla_w_kvb, mla_qn_g,
           mla_kn_g, gqa_w_in, gqa_w_out, gqa_qn_g, gqa_kn_g):
    Bp, Tp, D = x_prompt.shape
    Bs, Ts, _ = x_sample.shape
    depth = ada_w.shape[0]
    xp = x_prompt.reshape(Bp * Tp, D)
    xs = x_sample.reshape(Bs * Ts, D)

    cond = jnp.zeros((8, D), F32).at[0].set(c_ctx).at[2:2 + Bs].set(c)
    mods = _adaln(cond, ada_w, ada_b).reshape(depth, 8, 6, D)

    rope_mla = _rope_tables(Ts, MLA_ROPE, MLA_NOPE, 1)
    rope_gqa = _rope_tables(Ts, GQA_DH, 0, 2)

    streams = (dict(mod_base=0, rows_per_cond=None), dict(mod_base=2, rows_per_cond=Ts))
    proj_tm = 512
    ffn_tm = 2048
    prompt_nb = 4
    n_p = Bp * Tp

    n_ab, n_c = (depth + 1) // 2, depth // 2
    n_kv = GQA_KV_HEADS * GQA_DH
    kpe_cache = _pad_lanes(cache_mla_kpe, MLA_NOPE)
    gqa_k_cache = cache_gqa_k.reshape(Bs, n_c, -1, n_kv)
    gqa_v_cache = cache_gqa_v.reshape(Bs, n_c, -1, n_kv)
    new_ckv = new_kpe = new_gla = new_k = new_v = None
    for l in range(depth):
        i = l // 2
        mod = mods[l]
        proj = functools.partial(_inproj, xp, xs, norm_mix_g[l], mod, sample_mod_base=2, rows_per_cond=Ts,
                                 tm=proj_tm)
        if l % 2 == 0:
            P = _ab_params(ab_w_in[i], ab_w_out[i], gla_a_w2[i], gla_a_b[i], gla_out_g[i], mla_q_norm_g[i],
                           mla_w_qb[i], mla_kv_norm_g[i], mla_w_kvb[i], mla_qn_g[i], mla_kn_g[i])
            pj, ckv, alk = proj(P['w_perm'], AB_OUTS)
            gla = functools.partial(_gla, pj, alk, P['w2bd'], P['a_b'], P['out_g'])
            og_p, new_gla, new_kpe = gla(None, B=Bp, T=Tp, state_out=(i, n_ab, new_gla, new_kpe), nb=prompt_nb)
            og_s, _ = gla((state_gla, i), B=Bs, T=Ts, row0=n_p)
            om_p, new_ckv = _mla(pj, ckv, None, None, P['mla'], B=Bp, T=Tp, tq=Tp,
                                 ckv_out=(i, n_ab, new_ckv), nb=prompt_nb // 2)
            om_s, _ = _mla(pj, ckv, (cache_mla_ckv, kpe_cache, i), rope_mla, P['mla'], B=Bs, T=Ts, tq=256,
                           row0=n_p)
            wps = (P['w_out_gla'], P['w_out_mla'])
            acts_p, acts_s = (og_p, om_p), (og_s, om_s)
        else:
            w_in = gqa_w_in[i].astype(BF16)
            w_out = gqa_w_out[i].astype(BF16)
            gp = (jnp.tile(gqa_qn_g[i] * (GQA_DH ** -0.5 * LOG2E), 2).reshape(1, LANES),
                  jnp.tile(gqa_kn_g[i], 2).reshape(1, LANES))
            q, k, v = proj(w_in, C_OUTS)
            o_p, new_k, new_v = _gqa(q, k, v, None, None, gp, B=Bp, T=Tp, tq=Tp,
                                     kv_out=(i, n_c, new_k, new_v), nb=prompt_nb)
            o_s, _, _ = _gqa(q, k, v, (gqa_k_cache, gqa_v_cache, i), rope_gqa, gp, B=Bs, T=Ts, tq=256, row0=n_p)
            wps = (w_out,)
            acts_p, acts_s = (o_p,), (o_s,)
        xp = _mix_ffn(acts_p, wps, xp, norm_ffn_g[l], mod, ffn_w_in, ffn_w_out, l, **streams[0], tm=ffn_tm)
        xs = _mix_ffn(acts_s, wps, xs, norm_ffn_g[l], mod, ffn_w_in, ffn_w_out, l, **streams[1], tm=ffn_tm)
    kv_shape = (Bp, n_c, Tp, GQA_KV_HEADS, GQA_DH)
    return (xp.reshape(Bp, Tp, D), xs.reshape(Bs, Ts, D), new_ckv, new_kpe, new_gla,
            new_k.reshape(kv_shape), new_v.reshape(kv_shape))
```

```python
import functools

import jax
import jax.numpy as jnp
from jax import lax
from jax.experimental import pallas as pl
from jax.experimental.pallas import tpu as pltpu

F32 = jnp.float32
BF16 = jnp.bfloat16

EPS = 1e-6
ROPE_THETA = 10000.0
GRID_W = 64
LANES = 128
GLA_HEADS, GLA_DK, GLA_DV = 4, 64, 128
GLA_QK = GLA_HEADS * GLA_DK
GLA_V = GLA_HEADS * GLA_DV
GLA_RANK = 16
GLA_TAU = 16.0
GLA_CHUNK = 64
MLA_HEADS = 8
MLA_Q_RANK, MLA_KV_RANK = 384, 256
MLA_NOPE, MLA_ROPE, MLA_V = 64, 32, 64
MLA_QK = MLA_NOPE + MLA_ROPE
GQA_HEADS, GQA_KV_HEADS, GQA_DH = 16, 4, 64
VMEM_LIMIT = 56 << 20


def _cparams(sem, vmem=None):
    return pltpu.CompilerParams(dimension_semantics=sem, vmem_limit_bytes=vmem)


def _nt_dot(a, b):
    return lax.dot_general(a, b, (((1,), (1,)), ((), ())), preferred_element_type=F32)


def _tn_dot(a, b):
    return lax.dot_general(a, b, (((0,), (0,)), ((), ())), preferred_element_type=F32)


def _ds(start, size):
    if isinstance(start, int):
        return pl.ds(start, size)
    return pl.ds(pl.multiple_of(start, size), size)


def _rms(x, g):
    ms = jnp.mean(x * x, axis=-1, keepdims=True)
    return x * lax.rsqrt(ms + EPS) * g


def _silu(x):
    return x * jax.nn.sigmoid(x)


def _rope(x, c, sp, sm, shift):
    return x * c + pltpu.roll(x, LANES - shift, 1) * sp + pltpu.roll(x, shift, 1) * sm


def _adaln_body(c_ref, w_ref, b_ref, o_ref):
    c = c_ref[...]
    s = _silu(c).astype(BF16)
    o_ref[0] = jnp.dot(s, w_ref[0].astype(BF16), preferred_element_type=F32) + b_ref[0]


def _adaln(cond, ada_w, ada_b):
    L, D, E = ada_w.shape
    tn = 1536
    return pl.pallas_call(
        _adaln_body,
        grid=(L, E // tn),
        in_specs=[pl.BlockSpec((8, D), lambda l, n: (0, 0)),
                  pl.BlockSpec((1, D, tn), lambda l, n: (l, 0, n)),
                  pl.BlockSpec((1, 1, tn), lambda l, n: (l, 0, n))],
        out_specs=pl.BlockSpec((1, 8, tn), lambda l, n: (l, 0, n)),
        out_shape=jax.ShapeDtypeStruct((L, 8, E), F32),
        compiler_params=_cparams(("arbitrary", "arbitrary")),
        name="adaln",
    )(cond, ada_w, ada_b.reshape(L, 1, E))


def _slot_output(slot_out, batch, tail, dtype, nb=1):
    slot, n_slots, prev = slot_out
    zeros = (0,) * len(tail)
    shape = jax.ShapeDtypeStruct((batch, n_slots) + tail, dtype)
    if slot == 0:
        return pl.BlockSpec((nb, n_slots) + tail, lambda b, *_: (b, 0) + zeros), shape, [], n_slots - 1
    return pl.BlockSpec((nb, 1) + tail, lambda b, *_: (b, slot) + zeros), shape, [prev], 0


def _mod_index(mod_base, rows_per_cond, tm):
    if rows_per_cond is None:
        return lambda i: (mod_base, 0, 0)
    n_mod = max(tm // rows_per_cond, 1)
    return lambda i: ((mod_base + (i * tm) // rows_per_cond) // n_mod, 0, 0)


def _inproj_body(xp_ref, xs_ref, g_ref, mod_ref, w_ref, *o_refs, widths, n_prompt_tiles):
    def project(x_ref):
        h = _rms(x_ref[...], g_ref[...])
        h = h * (1.0 + mod_ref[0, 1:2, :]) + mod_ref[0, 0:1, :]
        acc = jnp.dot(h.astype(BF16), w_ref[...], preferred_element_type=F32)
        off = 0
        for o_ref, w in zip(o_refs, widths):
            o_ref[...] = acc[:, off:off + w].astype(o_ref.dtype)
            off += w

    is_prompt = pl.program_id(0) < n_prompt_tiles
    pl.when(is_prompt)(lambda: project(xp_ref))
    pl.when(jnp.logical_not(is_prompt))(lambda: project(xs_ref))


def _inproj(xp, xs, g, mod, w, outs, *, sample_mod_base, rows_per_cond, tm):
    n_p, d = xp.shape
    n = n_p + xs.shape[0]
    np_t = n_p // tm
    widths = tuple(o[0] for o in outs)

    def mod_idx(i):
        return (jnp.where(i < np_t, 0, sample_mod_base + ((i - np_t) * tm) // rows_per_cond), 0, 0)

    return pl.pallas_call(
        functools.partial(_inproj_body, widths=widths, n_prompt_tiles=np_t),
        grid=(n // tm,),
        in_specs=[pl.BlockSpec((tm, d), lambda i: (jnp.minimum(i, np_t - 1), 0)),
                  pl.BlockSpec((tm, d), lambda i: (jnp.maximum(i - np_t, 0), 0)),
                  pl.BlockSpec((1, d), lambda i: (0, 0)),
                  pl.BlockSpec((1, 6, d), mod_idx),
                  pl.BlockSpec(w.shape, lambda i: (0, 0))],
        out_specs=[pl.BlockSpec((tm, wd), lambda i: (i, 0)) for wd in widths],
        out_shape=[jax.ShapeDtypeStruct((n, wd), dt) for wd, dt in outs],
        compiler_params=_cparams(("arbitrary",), VMEM_LIMIT),
        name="inproj",
    )(xp, xs, g.reshape(1, d), mod, w)


def _ffn_body(*refs, n_in, nk, row_chunk, rows_per_cond):
    a_refs, wp_refs = refs[:n_in], refs[n_in:2 * n_in]
    x_ref, g_ref, mod_ref, wg_ref, wu_ref, wo_ref, o_ref, h_ref, wgu_ref, wob_ref = refs[2 * n_in:]
    k = pl.program_id(1)
    tm = x_ref.shape[0]
    tk = wg_ref.shape[2]
    n_chunks = tm // row_chunk

    def mod_row(c):
        return 0 if rows_per_cond is None else (c * row_chunk) // rows_per_cond

    def mix_rows(c):
        rows, m = _ds(c * row_chunk, row_chunk), mod_row(c)
        acc = None
        for a_ref, wp_ref in zip(a_refs, wp_refs):
            part = jnp.dot(a_ref[rows, :], wp_ref[...], preferred_element_type=F32)
            acc = part if acc is None else acc + part
        x_new = x_ref[rows, :] + mod_ref[m, 2:3, :] * acc
        o_ref[rows, :] = x_new
        h = _rms(x_new, g_ref[...])
        h = h * (1.0 + mod_ref[m, 4:5, :]) + mod_ref[m, 3:4, :]
        h_ref[rows, :] = h.astype(BF16)

    def ffn_rows(c):
        rows, m = _ds(c * row_chunk, row_chunk), mod_row(c)
        gu = jnp.dot(h_ref[rows, :], wgu_ref[...], preferred_element_type=F32)
        a = (_silu(gu[:, :tk]) * gu[:, tk:]).astype(BF16)
        o_ref[rows, :] += mod_ref[m, 5:6, :] * jnp.dot(a, wob_ref[...], preferred_element_type=F32)

    def cast_weights():
        wgu_ref[:, :tk] = wg_ref[0].astype(BF16)
        wgu_ref[:, tk:] = wu_ref[0].astype(BF16)
        wob_ref[...] = wo_ref[0].astype(BF16)

    @pl.when(k == 0)
    def _():
        cast_weights()
        for c in range(n_chunks):
            mix_rows(c)
            ffn_rows(c)

    @pl.when(k > 0)
    def _():
        cast_weights()
        for c in range(n_chunks):
            ffn_rows(c)


def _mix_ffn(acts, wps, x, g, mod, w_in, w_out, layer, *, mod_base, rows_per_cond, tm):
    n, d = x.shape
    hidden = w_out.shape[1]
    tk = 256
    nk = hidden // tk
    n_in = len(acts)
    n_mod = 1 if rows_per_cond is None else max(tm // rows_per_cond, 1)
    once = dict(pipeline_mode=pl.Buffered(1))
    return pl.pallas_call(
        functools.partial(_ffn_body, n_in=n_in, nk=nk, row_chunk=512, rows_per_cond=rows_per_cond),
        grid=(n // tm, nk),
        in_specs=([pl.BlockSpec((tm, a.shape[1]), lambda i, k: (i, 0), **once) for a in acts]
                  + [pl.BlockSpec(w.shape, lambda i, k: (0, 0), **once) for w in wps]
                  + [pl.BlockSpec((tm, d), lambda i, k: (i, 0)),
                     pl.BlockSpec((1, d), lambda i, k: (0, 0)),
                     pl.BlockSpec((n_mod, 6, d),
                                  (lambda f: (lambda i, k: f(i)))(_mod_index(mod_base, rows_per_cond, tm))),
                     pl.BlockSpec((1, d, tk), lambda i, k: (layer, 0, k)),
                     pl.BlockSpec((1, d, tk), lambda i, k: (layer, 0, nk + k)),
                     pl.BlockSpec((1, tk, d), lambda i, k: (layer, k, 0))]),
        out_specs=pl.BlockSpec((tm, d), lambda i, k: (i, 0)),
        out_shape=jax.ShapeDtypeStruct((n, d), F32),
        scratch_shapes=[pltpu.VMEM((tm, d), BF16),
                        pltpu.VMEM((d, 2 * tk), BF16),
                        pltpu.VMEM((tk, d), BF16)],
        compiler_params=_cparams(("arbitrary", "arbitrary"), VMEM_LIMIT),
        name="mix_ffn",
    )(*acts, *wps, x, g.reshape(1, d), mod, w_in, w_in, w_out)


def _log_sigmoid(x):
    return jnp.minimum(x, 0.0) - jnp.log1p(jnp.exp(-jnp.abs(x)))


def _gla_body(*refs, T, has_s0, want_state, n_alias, n_zero_slots, nb):
    refs = refs[n_alias:]
    qkvr_ref, alk_ref, w2_ref, ab_ref, og_ref = refs[:5]
    pos = 5
    s0_ref = None
    if has_s0:
        s0_ref = refs[pos]
        pos += 1
    o_ref = refs[pos]
    pos += 1
    st_out_ref = kpe_out_ref = None
    if want_state:
        st_out_ref, kpe_out_ref = refs[pos:pos + 2]
        pos += 2
    osc_ref, st_ref, la_ref, qin_ref, kst_ref, dec_ref, upd_ref, stq_ref = refs[pos:pos + 8]

    C = GLA_CHUNK
    nc = T // C

    logit = jnp.dot(alk_ref[...].astype(BF16), w2_ref[...], preferred_element_type=F32) + ab_ref[...]
    la_ref[...] = _log_sigmoid(logit) * (1.0 / GLA_TAU)

    for s in range(nb):
        for d in range(2):
            if has_s0:
                st_ref[2 * s + d] = jnp.concatenate([s0_ref[s, d, h] for h in range(GLA_HEADS)], axis=0).T
            else:
                st_ref[2 * s + d] = jnp.zeros((GLA_DV, GLA_QK), F32)

    r64 = lax.broadcasted_iota(jnp.int32, (C, C), 0)
    c64 = lax.broadcasted_iota(jnp.int32, (C, C), 1)
    tri_f = (r64 >= c64).astype(BF16)
    tri_b = (c64 >= r64).astype(BF16)
    t_idx = lax.broadcasted_iota(jnp.int32, (C, GLA_QK), 0)
    s_idx = lax.broadcasted_iota(jnp.int32, (C, GLA_QK), 1) % C
    causal_f = t_idx >= s_idx
    causal_b = t_idx <= s_idx
    bm_k = (lax.broadcasted_iota(jnp.int32, (GLA_QK, GLA_QK), 0) // GLA_DK
            == lax.broadcasted_iota(jnp.int32, (GLA_QK, GLA_QK), 1) // GLA_DK)
    bm_v = (lax.broadcasted_iota(jnp.int32, (GLA_QK, GLA_V), 0) // C
            == lax.broadcasted_iota(jnp.int32, (GLA_QK, GLA_V), 1) // GLA_DV)
    head_lanes = [lax.broadcasted_iota(jnp.int32, (1, GLA_QK), 1) // GLA_DK == h for h in range(GLA_HEADS)]
    directions = ((tri_f, causal_f, C // 2 - 1, C - 1), (tri_b, causal_b, C // 2, 0))

    def v_rows(rows):
        return qkvr_ref[rows, 2 * GLA_QK:2 * GLA_QK + GLA_V]

    G = 4
    units = [(i, d) for i in range(G) for d in range(2)]

    n_groups = nb * nc // G

    def for_groups(fn):
        if n_groups <= 4:
            for g in range(n_groups):
                fn(g * G)
        else:
            def body(g, carry):
                fn(g * G)
                return carry
            lax.fori_loop(0, n_groups, body, 0)

    def intra_group(n0):
        rows = [_ds((n0 + i) * C, C) for i in range(G)]
        b = {}
        for i, d in units:
            b[i, d] = _split_dot(la_ref[rows[i], d * GLA_QK:(d + 1) * GLA_QK], directions[d][0], w_left=True)
        q_loc, k_bd = {}, {}
        for i in range(G):
            qc = qkvr_ref[rows[i], 0:GLA_QK].astype(F32) * (GLA_DK ** -0.5)
            kc = qkvr_ref[rows[i], GLA_QK:2 * GLA_QK].astype(F32)
            for d in range(2):
                _, _, ref_row, last_row = directions[d]
                bb = b[i, d]
                b_ref = bb[ref_row:ref_row + 1, :]
                b_last = bb[last_row:last_row + 1, :]
                q_loc[i, d] = (qc * jnp.exp(bb - b_ref)).astype(BF16)
                k_loc = kc * jnp.exp(b_ref - bb)
                k_bd[i, d] = jnp.where(bm_k, jnp.concatenate([k_loc] * GLA_HEADS, axis=0), 0.0).astype(BF16)
                qin_ref[d, rows[i], :] = (qc * jnp.exp(bb)).astype(BF16)
                kst_ref[d, rows[i], :] = (kc * jnp.exp(b_last - bb)).astype(BF16)
                dec_ref[d, _ds((n0 + i) * 8, 8), :] = jnp.broadcast_to(jnp.exp(b_last), (8, GLA_QK))
        a = {u: jnp.where(directions[u[1]][1], _nt_dot(q_loc[u], k_bd[u]), 0.0).astype(BF16) for u in units}
        v_bd = [jnp.where(bm_v, jnp.concatenate([v_rows(rows[i])] * GLA_HEADS, axis=0), jnp.zeros((), BF16))
                for i in range(G)]
        o = {u: jnp.dot(a[u], v_bd[u[0]], preferred_element_type=F32) for u in units}
        for i in range(G):
            osc_ref[rows[i], :] = o[i, 0] + o[i, 1]

    def update_group(n0):
        rows = [_ds((n0 + i) * C, C) for i in range(G)]
        upd = {u: _tn_dot(v_rows(rows[u[0]]), kst_ref[u[1], rows[u[0]], :]) for u in units}
        for i, d in units:
            acc = None
            for h, m in enumerate(head_lanes):
                part = jnp.where(m, upd[i, d][h * GLA_DV:(h + 1) * GLA_DV, :], 0.0)
                acc = part if acc is None else acc + part
            upd_ref[d, n0 + i] = acc

    def scan_step(i, carry):
        for s in range(nb):
            for d, n in ((0, s * nc + i), (1, s * nc + nc - 1 - i)):
                st = st_ref[2 * s + d]
                stq_ref[d, n] = st.astype(BF16)
                st_ref[2 * s + d] = st * dec_ref[d, pl.ds(n * 8, 1), :] + upd_ref[d, n]
        return carry

    def readout_group(n0):
        rows = [_ds((n0 + i) * C, C) for i in range(G)]
        o = {}
        for i, d in units:
            q_in = qin_ref[d, rows[i], :]
            q_heads = jnp.concatenate([jnp.where(m, q_in, jnp.zeros((), BF16)) for m in head_lanes], axis=0)
            o[i, d] = _nt_dot(q_heads, stq_ref[d, n0 + i])
        for i in range(G):
            both = o[i, 0] + o[i, 1]
            for h in range(GLA_HEADS):
                osc_ref[rows[i], h * GLA_DV:(h + 1) * GLA_DV] += both[h * C:(h + 1) * C, :]

    for_groups(intra_group)
    for_groups(update_group)
    if nc == G:
        for i in range(nc):
            scan_step(i, 0)
    else:
        lax.fori_loop(0, nc, scan_step, 0)
    for_groups(readout_group)

    rb = 256
    for i in range(nb * T // rb):
        rows = slice(i * rb, (i + 1) * rb)
        for h in range(GLA_HEADS):
            cols = slice(h * GLA_DV, (h + 1) * GLA_DV)
            o = _rms(osc_ref[rows, cols], og_ref[:, cols])
            r = qkvr_ref[rows, 2 * GLA_QK + GLA_V + h * GLA_DV:2 * GLA_QK + GLA_V + (h + 1) * GLA_DV].astype(F32)
            o_ref[rows, cols] = (o * _silu(r)).astype(o_ref.dtype)

    if want_state:
        for s in range(nb):
            for d in range(2):
                s_all = st_ref[2 * s + d].T
                for h in range(GLA_HEADS):
                    st_out_ref[s, 0, d, h] = s_all[h * GLA_DK:(h + 1) * GLA_DK, :]
            kpe_out_ref[s, 0] = alk_ref[s * T:(s + 1) * T, 0:MLA_ROPE]
            if n_zero_slots:
                st_out_ref[s, 1:] = jnp.zeros((n_zero_slots, 2, GLA_HEADS, GLA_DK, GLA_DV), F32)
                kpe_out_ref[s, 1:] = jnp.zeros((n_zero_slots, T, MLA_ROPE), F32)


def _gla(proj, alk, w2bd, a_b, out_g, s0, *, B, T, row0=0, state_out=None, nb=1):
    has_s0 = s0 is not None
    want_state = state_out is not None
    n_qkvr = 2 * GLA_QK + 2 * GLA_V
    R = nb * T
    n_chunks = R // GLA_CHUNK
    off = row0 // R
    assert row0 % R == 0
    in_specs = [pl.BlockSpec((R, n_qkvr), lambda b: (off + b, 0)),
                pl.BlockSpec((R, LANES), lambda b: (off + b, 0)),
                pl.BlockSpec(w2bd.shape, lambda b: (0, 0)),
                pl.BlockSpec((1, 2 * GLA_QK), lambda b: (0, 0)),
                pl.BlockSpec((1, GLA_V), lambda b: (0, 0))]
    args = [proj, alk, w2bd, a_b, out_g]
    if has_s0:
        states, layer = s0
        in_specs.append(pl.BlockSpec((nb, None, 2, GLA_HEADS, GLA_DK, GLA_DV),
                                     lambda b: (b, layer, 0, 0, 0, 0)))
        args.append(states)
    out_specs = [pl.BlockSpec((R, GLA_V), lambda b: (b, 0))]
    out_shape = [jax.ShapeDtypeStruct((B * T, GLA_V), BF16)]
    aliased, n_zero = [], 0
    if want_state:
        slot, n_slots, prev_state, prev_kpe = state_out
        for prev, tail in ((prev_state, (2, GLA_HEADS, GLA_DK, GLA_DV)), (prev_kpe, (T, MLA_ROPE))):
            spec, shape, al, n_zero = _slot_output((slot, n_slots, prev), B, tail, F32, nb)
            out_specs.append(spec)
            out_shape.append(shape)
            aliased += al
    res = pl.pallas_call(
        functools.partial(_gla_body, T=T, has_s0=has_s0, want_state=want_state, n_alias=len(aliased),
                          n_zero_slots=n_zero, nb=nb),
        grid=(B // nb,),
        in_specs=[pl.BlockSpec(memory_space=pl.ANY)] * len(aliased) + in_specs,
        out_specs=out_specs,
        out_shape=out_shape,
        input_output_aliases={i: 1 + i for i in range(len(aliased))},
        scratch_shapes=[pltpu.VMEM((R, GLA_V), F32),
                        pltpu.VMEM((2 * nb, GLA_DV, GLA_QK), F32),
                        pltpu.VMEM((R, 2 * GLA_QK), F32),
                        pltpu.VMEM((2, R, GLA_QK), BF16),
                        pltpu.VMEM((2, R, GLA_QK), BF16),
                        pltpu.VMEM((2, 8 * n_chunks, GLA_QK), F32),
                        pltpu.VMEM((2, n_chunks, GLA_DV, GLA_QK), F32),
                        pltpu.VMEM((2, n_chunks, GLA_DV, GLA_QK), BF16)],
        compiler_params=_cparams(("arbitrary",), VMEM_LIMIT),
        name="gla",
    )(*aliased, *args)
    return res if want_state else (res[0], None)


LOG2E = 1.4426950408889634


def _attend(qs, ks, vs, group):
    def qk(g0):
        return [_nt_dot(q, k()) for q, k in zip(qs[g0:g0 + group], ks[g0:g0 + group])]

    outs = []
    scores = qk(0)
    for g0 in range(0, len(qs), group):
        probs = [jnp.exp2(s - jnp.max(s, axis=-1, keepdims=True)).astype(BF16) for s in scores]
        if g0 + group < len(qs):
            scores = qk(g0 + group)
        res = [jnp.dot(p, v(), preferred_element_type=F32) for p, v in zip(probs, vs[g0:g0 + group])]
        outs += [r[:, :LANES] / r[:, LANES:] for r in res]
    return outs


def _head_group(n_keys, n_heads):
    return 4 if n_keys <= 512 else 2


def _stacked_ms(xs, ones_mat, inv_n):
    n = xs[0].shape[0]
    ms = _split_dot(jnp.concatenate([x * x for x in xs], axis=0), ones_mat) * inv_n
    return [ms[i * n:(i + 1) * n] for i in range(len(xs))]


def _split_dot(x, w, w_left=False):
    hi = x.astype(BF16)
    lo = (x - hi.astype(F32)).astype(BF16)
    if w_left:
        return jnp.dot(w, hi, preferred_element_type=F32) + jnp.dot(w, lo, preferred_element_type=F32)
    return jnp.dot(hi, w, preferred_element_type=F32) + jnp.dot(lo, w, preferred_element_type=F32)


def _group_ones(group):
    r = lax.broadcasted_iota(jnp.int32, (LANES, LANES), 0) // group
    c = lax.broadcasted_iota(jnp.int32, (LANES, LANES), 1) // group
    return (r == c).astype(BF16)


def _mla_body(*refs, T, S_ctx, rope, want_ckv, group, n_alias, n_zero_slots, nb):
    refs = refs[n_alias:]
    cq_ref, kpe_ref, ckv_ref = refs[:3]
    pos = 3
    if S_ctx:
        ckvc_ref, kpec_ref = refs[pos:pos + 2]
        pos += 2
    if rope:
        rq_refs = refs[pos:pos + 3]
        rk_refs = refs[pos + 3:pos + 6]
        pos += 6
    qng_ref, wqb_ref, qg_ref, kvg_ref, wkvb_ref, kg_ref = refs[pos:pos + 6]
    pos += 6
    o_ref = refs[pos]
    pos += 1
    if want_ckv:
        ckvn_ref = refs[pos]
        pos += 1
    k_sc, vlo_sc, vhi_sc = refs[pos:pos + 3]

    H = MLA_HEADS
    n_k = H * LANES
    lo = lax.broadcasted_iota(jnp.int32, (1, LANES), 1) < MLA_V

    def head_ms(xs):
        return [jnp.sum(x * x, axis=-1, keepdims=True) * (1.0 / MLA_QK) for x in xs]

    def fill_kv(ckvn, kpe, row0, n_rows, with_rope):
        rows = slice(row0, row0 + n_rows)
        kv = jnp.dot(ckvn.astype(BF16), wkvb_ref[...], preferred_element_type=F32)
        khs = [kv[:, h * LANES:(h + 1) * LANES] + kpe for h in range(H)]
        for h, (kh, ms) in enumerate(zip(khs, head_ms(khs))):
            kh = kh * lax.rsqrt(ms + EPS) * kg_ref[...]
            if with_rope:
                kh = _rope(kh, rk_refs[0][...], rk_refs[1][...], rk_refs[2][...], MLA_ROPE // 4)
            k_sc[rows, h * LANES:(h + 1) * LANES] = kh.astype(BF16)
        ones = jnp.ones((n_rows, LANES), BF16)
        for p in range(H // 2):
            v = kv[:, n_k + p * LANES:n_k + (p + 1) * LANES]
            vlo_sc[rows, 2 * p * LANES:(2 * p + 1) * LANES] = jnp.where(lo, v, 0.0).astype(BF16)
            vhi_sc[rows, 2 * p * LANES:(2 * p + 1) * LANES] = jnp.where(lo, 0.0, v).astype(BF16)
            vlo_sc[rows, (2 * p + 1) * LANES:(2 * p + 2) * LANES] = ones
            vhi_sc[rows, (2 * p + 1) * LANES:(2 * p + 2) * LANES] = ones

    S = S_ctx + T
    tq = cq_ref.shape[0] // nb

    @pl.when(pl.program_id(1) == 0)
    def _():
        for s in range(nb):
            if S_ctx:
                fill_kv(ckvc_ref[s], kpec_ref[s], s * S, S_ctx, False)
            ckvn = _rms(ckv_ref[s * T:(s + 1) * T, :], kvg_ref[...])
            if want_ckv:
                ckvn_ref[s, 0] = ckvn
                if n_zero_slots:
                    ckvn_ref[s, 1:] = jnp.zeros((n_zero_slots, T, MLA_KV_RANK), F32)
            fill_kv(ckvn, kpe_ref[s * T:(s + 1) * T, :].astype(F32), s * S + S_ctx, T, rope)

    cqn = _rms(cq_ref[...].astype(F32), qng_ref[...])
    q = jnp.dot(cqn.astype(BF16), wqb_ref[...], preferred_element_type=F32)
    units = [(s, h) for s in range(nb) for h in range(H)]
    qhs = [q[s * tq:(s + 1) * tq, h * LANES:(h + 1) * LANES] for s, h in units]
    qs = []
    for qh, ms in zip(qhs, head_ms(qhs)):
        qh = qh * lax.rsqrt(ms + EPS) * qg_ref[...]
        if rope:
            qh = _rope(qh, rq_refs[0][...], rq_refs[1][...], rq_refs[2][...], MLA_ROPE // 4)
        qs.append(qh.astype(BF16))
    ks = [functools.partial(lambda s, h: k_sc[s * S:(s + 1) * S, h * LANES:(h + 1) * LANES], s, h)
          for s, h in units]
    vs = [functools.partial(lambda s, h: (vlo_sc, vhi_sc)[h % 2][s * S:(s + 1) * S,
                                                                (h // 2) * 2 * LANES:(h // 2 + 1) * 2 * LANES], s, h)
          for s, h in units]
    outs = _attend(qs, ks, vs, group)
    for s in range(nb):
        for p in range(H // 2):
            o_pair = outs[s * H + 2 * p] + outs[s * H + 2 * p + 1]
            o_ref[s * tq:(s + 1) * tq, p * LANES:(p + 1) * LANES] = o_pair.astype(o_ref.dtype)


def _mla(proj, ckv, ctx, rope_tabs, params, *, B, T, tq, row0=0, ckv_out=None, nb=1):
    want_ckv = ckv_out is not None
    nq = T // tq
    assert nb == 1 or nq == 1
    S_ctx = ctx[0].shape[2] if ctx is not None else 0
    S = S_ctx + T
    rope = rope_tabs is not None
    cq_blk = (2 * GLA_QK + 2 * GLA_V) // MLA_Q_RANK
    kpe_blk = (2 * GLA_QK + 2 * GLA_V + MLA_Q_RANK) // LANES
    off_q, off_t = row0 // (nb * tq), row0 // (nb * T)
    assert row0 % (nb * T) == 0
    in_specs = [pl.BlockSpec((nb * tq, MLA_Q_RANK), lambda b, j: (off_q + b * nq + j, cq_blk)),
                pl.BlockSpec((nb * T, LANES), lambda b, j: (off_t + b, kpe_blk)),
                pl.BlockSpec((nb * T, MLA_KV_RANK), lambda b, j: (off_t + b, 0))]
    args = [proj, proj, ckv]
    if S_ctx:
        layer = ctx[2]
        in_specs += [pl.BlockSpec((nb, None, S_ctx, MLA_KV_RANK), lambda b, j: (b, layer, 0, 0)),
                     pl.BlockSpec((nb, None, S_ctx, LANES), lambda b, j: (b, layer, 0, 0))]
        args += list(ctx[:2])
    if rope:
        in_specs += [pl.BlockSpec((tq, LANES), lambda b, j: (j, 0))] * 3
        in_specs += [pl.BlockSpec((T, LANES), lambda b, j: (0, 0))] * 3
        args += list(rope_tabs) * 2
    in_specs += [pl.BlockSpec(p.shape, lambda b, j: (0, 0)) for p in params]
    args += list(params)
    out_specs = [pl.BlockSpec((nb * tq, MLA_HEADS * MLA_V), lambda b, j: (b * nq + j, 0))]
    out_shape = [jax.ShapeDtypeStruct((B * T, MLA_HEADS * MLA_V), BF16)]
    aliased, n_zero = [], 0
    if want_ckv:
        spec, shape, aliased, n_zero = _slot_output(ckv_out, B, (T, MLA_KV_RANK), F32, nb)
        out_specs.append(spec)
        out_shape.append(shape)
    res = pl.pallas_call(
        functools.partial(_mla_body, T=T, S_ctx=S_ctx, rope=rope, want_ckv=want_ckv,
                          group=_head_group(S, MLA_HEADS), n_alias=len(aliased), n_zero_slots=n_zero, nb=nb),
        grid=(B // nb, nq),
        in_specs=[pl.BlockSpec(memory_space=pl.ANY)] * len(aliased) + in_specs,
        out_specs=out_specs,
        out_shape=out_shape,
        input_output_aliases={i: 1 + i for i in range(len(aliased))},
        scratch_shapes=[pltpu.VMEM((nb * S, MLA_HEADS * LANES), BF16)] * 3,
        compiler_params=_cparams(("arbitrary", "arbitrary"), VMEM_LIMIT),
        name="mla",
    )(*aliased, *args)
    return res if want_ckv else (res[0], None)


def _gqa_body(*refs, T, S_ctx, rope, want_kv, group, n_alias, n_zero_slots, nb):
    refs = refs[n_alias:]
    q_ref, k_ref, v_ref = refs[:3]
    pos = 3
    if S_ctx:
        kc_ref, vc_ref = refs[pos:pos + 2]
        pos += 2
    if rope:
        rq_refs = refs[pos:pos + 3]
        rk_refs = refs[pos + 3:pos + 6]
        pos += 6
    qg_ref, kg_ref = refs[pos:pos + 2]
    pos += 2
    o_ref = refs[pos]
    pos += 1
    if want_kv:
        kn_ref, vo_ref = refs[pos:pos + 2]
        pos += 2
    klo_sc, khi_sc, vlo_sc, vhi_sc = refs[pos:pos + 4]

    lo = lax.broadcasted_iota(jnp.int32, (1, LANES), 1) < GQA_DH
    half_sum = _group_ones(GQA_DH)

    def head_ms(x):
        return _split_dot(x * x, half_sum) * (1.0 / GQA_DH)

    def scatter_halves(x, lo_sc, hi_sc, c, rows):
        rolled = pltpu.roll(x, GQA_DH, 1)
        lo_sc[2 * c, rows, :LANES] = jnp.where(lo, x, 0.0).astype(BF16)
        hi_sc[2 * c, rows, :LANES] = jnp.where(lo, 0.0, rolled).astype(BF16)
        lo_sc[2 * c + 1, rows, :LANES] = jnp.where(lo, rolled, 0.0).astype(BF16)
        hi_sc[2 * c + 1, rows, :LANES] = jnp.where(lo, 0.0, x).astype(BF16)

    def fill_ones(rows):
        ones = jnp.ones((rows.stop - rows.start, LANES), BF16)
        for g in range(GQA_KV_HEADS):
            vlo_sc[g, rows, LANES:] = ones
            vhi_sc[g, rows, LANES:] = ones

    S = S_ctx + T
    tq = q_ref.shape[0] // nb

    @pl.when(pl.program_id(1) == 0)
    def _():
        fill_ones(slice(0, nb * S))
        for s in range(nb):
            if want_kv and n_zero_slots:
                zeros = jnp.zeros((n_zero_slots, T, GQA_KV_HEADS * GQA_DH), F32)
                kn_ref[s, 1:] = zeros
                vo_ref[s, 1:] = zeros
            for c in range(GQA_KV_HEADS // 2):
                cols = slice(c * LANES, (c + 1) * LANES)
                if S_ctx:
                    scatter_halves(kc_ref[s, :, cols], klo_sc, khi_sc, c, slice(s * S, s * S + S_ctx))
                    scatter_halves(vc_ref[s, :, cols], vlo_sc, vhi_sc, c, slice(s * S, s * S + S_ctx))
                kx = k_ref[s * T:(s + 1) * T, cols]
                kn = kx * lax.rsqrt(head_ms(kx) + EPS) * kg_ref[...]
                vx = v_ref[s * T:(s + 1) * T, cols]
                if want_kv:
                    kn_ref[s, 0, :, cols] = kn
                    vo_ref[s, 0, :, cols] = vx
                if rope:
                    kn = _rope(kn, rk_refs[0][...], rk_refs[1][...], rk_refs[2][...], GQA_DH // 4)
                scatter_halves(kn, klo_sc, khi_sc, c, slice(s * S + S_ctx, (s + 1) * S))
                scatter_halves(vx, vlo_sc, vhi_sc, c, slice(s * S + S_ctx, (s + 1) * S))

    n_pairs = GQA_HEADS // 2
    units = [(s, p) for s in range(nb) for p in range(n_pairs)]
    qxs = [q_ref[s * tq:(s + 1) * tq, p * LANES:(p + 1) * LANES].astype(F32) for s, p in units]
    qs, ks, vs = [], [], []
    for (s, p), qx, ms in zip(units, qxs, _stacked_ms(qxs, half_sum, 1.0 / GQA_DH)):
        qn = qx * lax.rsqrt(ms + EPS) * qg_ref[...]
        if rope:
            qn = _rope(qn, rq_refs[0][...], rq_refs[1][...], rq_refs[2][...], GQA_DH // 4)
        g = p // 2
        qs += [qn.astype(BF16)] * 2
        ks += [functools.partial(lambda r, g, s: r[g, s * S:(s + 1) * S, :], r, g, s) for r in (klo_sc, khi_sc)]
        vs += [functools.partial(lambda r, g, s: r[g, s * S:(s + 1) * S, :], r, g, s) for r in (vlo_sc, vhi_sc)]
    outs = _attend(qs, ks, vs, group)
    for i, (s, p) in enumerate(units):
        o_pair = outs[2 * i] + outs[2 * i + 1]
        o_ref[s * tq:(s + 1) * tq, p * LANES:(p + 1) * LANES] = o_pair.astype(o_ref.dtype)


def _gqa(q, k, v, ctx, rope_tabs, params, *, B, T, tq, row0=0, kv_out=None, nb=1):
    want_kv = kv_out is not None
    nq = T // tq
    assert nb == 1 or nq == 1
    S_ctx = ctx[0].shape[2] if ctx is not None else 0
    S = S_ctx + T
    rope = rope_tabs is not None
    n_q = GQA_HEADS * GQA_DH
    n_kv = GQA_KV_HEADS * GQA_DH
    off_q, off_t = row0 // (nb * tq), row0 // (nb * T)
    assert row0 % (nb * T) == 0
    in_specs = [pl.BlockSpec((nb * tq, n_q), lambda b, j: (off_q + b * nq + j, 0)),
                pl.BlockSpec((nb * T, n_kv), lambda b, j: (off_t + b, 0)),
                pl.BlockSpec((nb * T, n_kv), lambda b, j: (off_t + b, 0))]
    args = [q, k, v]
    if S_ctx:
        layer = ctx[2]
        in_specs += [pl.BlockSpec((nb, None, S_ctx, n_kv), lambda b, j: (b, layer, 0, 0))] * 2
        args += list(ctx[:2])
    if rope:
        in_specs += [pl.BlockSpec((tq, LANES), lambda b, j: (j, 0))] * 3
        in_specs += [pl.BlockSpec((T, LANES), lambda b, j: (0, 0))] * 3
        args += list(rope_tabs) * 2
    in_specs += [pl.BlockSpec(p.shape, lambda b, j: (0, 0)) for p in params]
    args += list(params)
    out_specs = [pl.BlockSpec((nb * tq, n_q), lambda b, j: (b * nq + j, 0))]
    out_shape = [jax.ShapeDtypeStruct((B * T, n_q), BF16)]
    aliased, n_zero = [], 0
    if want_kv:
        slot, n_slots, prev_k, prev_v = kv_out
        for prev in (prev_k, prev_v):
            spec, shape, al, n_zero = _slot_output((slot, n_slots, prev), B, (T, n_kv), F32, nb)
            out_specs.append(spec)
            out_shape.append(shape)
            aliased += al
    res = pl.pallas_call(
        functools.partial(_gqa_body, T=T, S_ctx=S_ctx, rope=rope, want_kv=want_kv,
                          group=_head_group(S, GQA_HEADS), n_alias=len(aliased), n_zero_slots=n_zero, nb=nb),
        grid=(B // nb, nq),
        in_specs=[pl.BlockSpec(memory_space=pl.ANY)] * len(aliased) + in_specs,
        out_specs=out_specs,
        out_shape=out_shape,
        input_output_aliases={i: 1 + i for i in range(len(aliased))},
        scratch_shapes=([pltpu.VMEM((GQA_KV_HEADS, nb * S, LANES), BF16)] * 2
                        + [pltpu.VMEM((GQA_KV_HEADS, nb * S, 2 * LANES), BF16)] * 2),
        compiler_params=_cparams(("arbitrary", "arbitrary"), VMEM_LIMIT),
        name="gqa",
    )(*aliased, *args)
    return res if want_kv else (res[0], None, None)


def _rope_tables(n_tok, d_rot, lead, reps):
    t = jnp.arange(n_tok, dtype=jnp.int32)
    posn = jnp.stack([t // GRID_W, t % GRID_W], axis=-1).astype(F32)
    quarter = d_rot // 4
    inv = jnp.power(ROPE_THETA, -jnp.arange(quarter, dtype=F32) / quarter)
    ang = posn[:, :, None] * inv
    cos, sin = jnp.cos(ang), jnp.sin(ang)
    zero = jnp.zeros_like(sin)
    c_rot = jnp.stack([cos, cos], axis=2).reshape(n_tok, d_rot)
    sp_rot = jnp.stack([-sin, zero], axis=2).reshape(n_tok, d_rot)
    sm_rot = jnp.stack([zero, sin], axis=2).reshape(n_tok, d_rot)
    tail = LANES - lead - reps * d_rot

    def embed(rot, fill):
        parts = [jnp.full((n_tok, lead), fill, F32)] + [rot] * reps + [jnp.full((n_tok, tail), fill, F32)]
        return jnp.concatenate(parts, axis=1)

    return embed(c_rot, 1.0), embed(sp_rot, 0.0), embed(sm_rot, 0.0)


def _pad_lanes(x, lead, width=LANES):
    pad = [(0, 0)] * (x.ndim - 1) + [(lead, width - lead - x.shape[-1])]
    return jnp.pad(x, pad)


def _ab_params(w_in, w_out, a_w2, a_b, out_g, q_norm_g, w_qb, kv_norm_g, w_kvb, qn_g, kn_g):
    d = w_in.shape[0]
    o_alo = 2 * GLA_QK + 2 * GLA_V
    o_cq = o_alo + 2 * GLA_RANK
    o_ckv = o_cq + MLA_Q_RANK
    o_kpe = o_ckv + MLA_KV_RANK
    w_kpe = w_in[:, o_kpe:]
    w_perm = jnp.concatenate([
        w_in[:, :o_alo], w_in[:, o_cq:o_ckv], _pad_lanes(w_kpe, MLA_NOPE),
        w_in[:, o_ckv:o_kpe],
        _pad_lanes(jnp.concatenate([w_kpe, w_in[:, o_alo:o_cq]], axis=1), 0),
    ], axis=1).astype(BF16)
    w2bd = jnp.zeros((LANES, 2 * GLA_QK), F32)
    w2bd = (w2bd.at[MLA_ROPE:MLA_ROPE + GLA_RANK, :GLA_QK].set(a_w2[0])
            .at[MLA_ROPE + GLA_RANK:MLA_ROPE + 2 * GLA_RANK, GLA_QK:].set(a_w2[1]))
    w_qb_p = _pad_lanes(w_qb.reshape(MLA_Q_RANK, MLA_HEADS, MLA_QK), 0).reshape(MLA_Q_RANK, MLA_HEADS * LANES)
    kvb = w_kvb.reshape(MLA_KV_RANK, MLA_HEADS, MLA_NOPE + MLA_V)
    w_kvb_p = jnp.concatenate([
        _pad_lanes(kvb[:, :, :MLA_NOPE], 0).reshape(MLA_KV_RANK, MLA_HEADS * LANES),
        kvb[:, :, MLA_NOPE:].reshape(MLA_KV_RANK, MLA_HEADS * MLA_V)], axis=1)
    return dict(
        w_perm=w_perm,
        w2bd=w2bd.astype(BF16),
        a_b=a_b.reshape(1, 2 * GLA_QK),
        out_g=jnp.tile(out_g, GLA_HEADS).reshape(1, GLA_V),
        mla=(q_norm_g.reshape(1, -1), w_qb_p.astype(BF16),
             _pad_lanes(qn_g * (MLA_QK ** -0.5 * LOG2E), 0).reshape(1, LANES),
             kv_norm_g.reshape(1, -1), w_kvb_p.astype(BF16),
             _pad_lanes(kn_g, 0).reshape(1, LANES)),
        w_out_gla=w_out[:GLA_V].astype(BF16),
        w_out_mla=w_out[GLA_V:].astype(BF16),
    )


AB_OUTS = ((2 * GLA_QK + 2 * GLA_V + MLA_Q_RANK + LANES, BF16), (MLA_KV_RANK, F32), (LANES, F32))
C_OUTS = ((GQA_HEADS * GQA_DH, BF16), (GQA_KV_HEADS * GQA_DH, F32), (GQA_KV_HEADS * GQA_DH, F32))


def kernel(x_prompt, x_sample, c, cache_mla_ckv, cache_mla_kpe, state_gla, cache_gqa_k, cache_gqa_v,
           c_ctx, ada_w, ada_b, norm_mix_g, norm_ffn_g, ffn_w_in, ffn_w_out, ab_w_in, ab_w_out,
           gla_a_w2, gla_a_b, gla_out_g, mla_q_norm_g, mla_w_qb, mla_kv_norm_g, mla_w_kvb, mla_qn_g,
           mla_kn_g, gqa_w_in, gqa_w_out, gqa_qn_g, gqa_kn_g):
    Bp, Tp, D = x_prompt.shape
    Bs, Ts, _ = x_sample.shape
    depth = ada_w.shape[0]
    xp = x_prompt.reshape(Bp * Tp, D)
    xs = x_sample.reshape(Bs * Ts, D)

    cond = jnp.zeros((8, D), F32).at[0].set(c_ctx).at[2:2 + Bs].set(c)
    mods = _adaln(cond, ada_w, ada_b).reshape(depth, 8, 6, D)

    rope_mla = _rope_tables(Ts, MLA_ROPE, MLA_NOPE, 1)
    rope_gqa = _rope_tables(Ts, GQA_DH, 0, 2)

    streams = (dict(mod_base=0, rows_per_cond=None), dict(mod_base=2, rows_per_cond=Ts))
    proj_tm = 512
    ffn_tm = 2048
    prompt_nb = 4
    n_p = Bp * Tp

    n_ab, n_c = (depth + 1) // 2, depth // 2
    n_kv = GQA_KV_HEADS * GQA_DH
    kpe_cache = _pad_lanes(cache_mla_kpe, MLA_NOPE)
    gqa_k_cache = cache_gqa_k.reshape(Bs, n_c, -1, n_kv)
    gqa_v_cache = cache_gqa_v.reshape(Bs, n_c, -1, n_kv)
    new_ckv = new_kpe = new_gla = new_k = new_v = None
    for l in range(depth):
        i = l // 2
        mod = mods[l]
        proj = functools.partial(_inproj, xp, xs, norm_mix_g[l], mod, sample_mod_base=2, rows_per_cond=Ts,
                                 tm=proj_tm)
        if l % 2 == 0:
            P = _ab_params(ab_w_in[i], ab_w_out[i], gla_a_w2[i], gla_a_b[i], gla_out_g[i], mla_q_norm_g[i],
                           mla_w_qb[i], mla_kv_norm_g[i], mla_w_kvb[i], mla_qn_g[i], mla_kn_g[i])
            pj, ckv, alk = proj(P['w_perm'], AB_OUTS)
            gla = functools.partial(_gla, pj, alk, P['w2bd'], P['a_b'], P['out_g'])
            og_p, new_gla, new_kpe = gla(None, B=Bp, T=Tp, state_out=(i, n_ab, new_gla, new_kpe), nb=prompt_nb)
            og_s, _ = gla((state_gla, i), B=Bs, T=Ts, row0=n_p)
            om_p, new_ckv = _mla(pj, ckv, None, None, P['mla'], B=Bp, T=Tp, tq=Tp,
                                 ckv_out=(i, n_ab, new_ckv), nb=prompt_nb // 2)
            om_s, _ = _mla(pj, ckv, (cache_mla_ckv, kpe_cache, i), rope_mla, P['mla'], B=Bs, T=Ts, tq=256,
                           row0=n_p)
            wps = (P['w_out_gla'], P['w_out_mla'])
            acts_p, acts_s = (og_p, om_p), (og_s, om_s)
        else:
            w_in = gqa_w_in[i].astype(BF16)
            w_out = gqa_w_out[i].astype(BF16)
            gp = (jnp.tile(gqa_qn_g[i] * (GQA_DH ** -0.5 * LOG2E), 2).reshape(1, LANES),
                  jnp.tile(gqa_kn_g[i], 2).reshape(1, LANES))
            q, k, v = proj(w_in, C_OUTS)
            o_p, new_k, new_v = _gqa(q, k, v, None, None, gp, B=Bp, T=Tp, tq=Tp,
                                     kv_out=(i, n_c, new_k, new_v), nb=prompt_nb)
            o_s, _, _ = _gqa(q, k, v, (gqa_k_cache, gqa_v_cache, i), rope_gqa, gp, B=Bs, T=Ts, tq=256, row0=n_p)
            wps = (w_out,)
            acts_p, acts_s = (o_p,), (o_s,)
        xp = _mix_ffn(acts_p, wps, xp, norm_ffn_g[l], mod, ffn_w_in, ffn_w_out, l, **streams[0], tm=ffn_tm)
        xs = _mix_ffn(acts_s, wps, xs, norm_ffn_g[l], mod, ffn_w_in, ffn_w_out, l, **streams[1], tm=ffn_tm)
    kv_shape = (Bp, n_c, Tp, GQA_KV_HEADS, GQA_DH)
    return (xp.reshape(Bp, Tp, D), xs.reshape(Bs, Ts, D), new_ckv, new_kpe, new_gla,
            new_k.reshape(kv_shape), new_v.reshape(kv_shape))
```

```python
import functools

import jax
import jax.numpy as jnp
from jax import lax
from jax.experimental import pallas as pl
from jax.experimental.pallas import tpu as pltpu

F32 = jnp.float32
BF16 = jnp.bfloat16

EPS = 1e-6
ROPE_THETA = 10000.0
GRID_W = 64
LANES = 128
GLA_HEADS, GLA_DK, GLA_DV = 4, 64, 128
GLA_QK = GLA_HEADS * GLA_DK
GLA_V = GLA_HEADS * GLA_DV
GLA_RANK = 16
GLA_TAU = 16.0
GLA_CHUNK = 64
MLA_HEADS = 8
MLA_Q_RANK, MLA_KV_RANK = 384, 256
MLA_NOPE, MLA_ROPE, MLA_V = 64, 32, 64
MLA_QK = MLA_NOPE + MLA_ROPE
GQA_HEADS, GQA_KV_HEADS, GQA_DH = 16, 4, 64
VMEM_LIMIT = 56 << 20


def _cparams(sem, vmem=None):
    return pltpu.CompilerParams(dimension_semantics=sem, vmem_limit_bytes=vmem)


def _nt_dot(a, b):
    return lax.dot_general(a, b, (((1,), (1,)), ((), ())), preferred_element_type=F32)


def _tn_dot(a, b):
    return lax.dot_general(a, b, (((0,), (0,)), ((), ())), preferred_element_type=F32)


def _ds(start, size):
    if isinstance(start, int):
        return pl.ds(start, size)
    return pl.ds(pl.multiple_of(start, size), size)


def _rms(x, g):
    ms = jnp.mean(x * x, axis=-1, keepdims=True)
    return x * lax.rsqrt(ms + EPS) * g


def _silu(x):
    return x * jax.nn.sigmoid(x)


def _rope(x, c, sp, sm, shift):
    return x * c + pltpu.roll(x, LANES - shift, 1) * sp + pltpu.roll(x, shift, 1) * sm


def _adaln_body(c_ref, w_ref, b_ref, o_ref):
    c = c_ref[...]
    s = _silu(c).astype(BF16)
    o_ref[0] = jnp.dot(s, w_ref[0].astype(BF16), preferred_element_type=F32) + b_ref[0]


def _adaln(cond, ada_w, ada_b):
    L, D, E = ada_w.shape
    tn = 1536
    return pl.pallas_call(
        _adaln_body,
        grid=(L, E // tn),
        in_specs=[pl.BlockSpec((8, D), lambda l, n: (0, 0)),
                  pl.BlockSpec((1, D, tn), lambda l, n: (l, 0, n)),
                  pl.BlockSpec((1, 1, tn), lambda l, n: (l, 0, n))],
        out_specs=pl.BlockSpec((1, 8, tn), lambda l, n: (l, 0, n)),
        out_shape=jax.ShapeDtypeStruct((L, 8, E), F32),
        compiler_params=_cparams(("arbitrary", "arbitrary")),
        name="adaln",
    )(cond, ada_w, ada_b.reshape(L, 1, E))


def _slot_output(slot_out, batch, tail, dtype, nb=1):
    slot, n_slots, prev = slot_out
    zeros = (0,) * len(tail)
    shape = jax.ShapeDtypeStruct((batch, n_slots) + tail, dtype)
    if slot == 0:
        return pl.BlockSpec((nb, n_slots) + tail, lambda b, *_: (b, 0) + zeros), shape, [], n_slots - 1
    return pl.BlockSpec((nb, 1) + tail, lambda b, *_: (b, slot) + zeros), shape, [prev], 0


def _mod_index(mod_base, rows_per_cond, tm):
    if rows_per_cond is None:
        return lambda i: (mod_base, 0, 0)
    n_mod = max(tm // rows_per_cond, 1)
    return lambda i: ((mod_base + (i * tm) // rows_per_cond) // n_mod, 0, 0)


def _inproj_body(xp_ref, xs_ref, g_ref, mod_ref, w_ref, *o_refs, widths, n_prompt_tiles):
    def project(x_ref):
        h = _rms(x_ref[...], g_ref[...])
        h = h * (1.0 + mod_ref[0, 1:2, :]) + mod_ref[0, 0:1, :]
        acc = jnp.dot(h.astype(BF16), w_ref[...], preferred_element_type=F32)
        off = 0
        for o_ref, w in zip(o_refs, widths):
            o_ref[...] = acc[:, off:off + w].astype(o_ref.dtype)
            off += w

    is_prompt = pl.program_id(0) < n_prompt_tiles
    pl.when(is_prompt)(lambda: project(xp_ref))
    pl.when(jnp.logical_not(is_prompt))(lambda: project(xs_ref))


def _inproj(xp, xs, g, mod, w, outs, *, sample_mod_base, rows_per_cond, tm):
    n_p, d = xp.shape
    n = n_p + xs.shape[0]
    np_t = n_p // tm
    widths = tuple(o[0] for o in outs)

    def mod_idx(i):
        return (jnp.where(i < np_t, 0, sample_mod_base + ((i - np_t) * tm) // rows_per_cond), 0, 0)

    return pl.pallas_call(
        functools.partial(_inproj_body, widths=widths, n_prompt_tiles=np_t),
        grid=(n // tm,),
        in_specs=[pl.BlockSpec((tm, d), lambda i: (jnp.minimum(i, np_t - 1), 0)),
                  pl.BlockSpec((tm, d), lambda i: (jnp.maximum(i - np_t, 0), 0)),
                  pl.BlockSpec((1, d), lambda i: (0, 0)),
                  pl.BlockSpec((1, 6, d), mod_idx),
                  pl.BlockSpec(w.shape, lambda i: (0, 0))],
        out_specs=[pl.BlockSpec((tm, wd), lambda i: (i, 0)) for wd in widths],
        out_shape=[jax.ShapeDtypeStruct((n, wd), dt) for wd, dt in outs],
        compiler_params=_cparams(("arbitrary",), VMEM_LIMIT),
        name="inproj",
    )(xp, xs, g.reshape(1, d), mod, w)


def _ffn_body(*refs, n_in, nk, row_chunk, rows_per_cond):
    a_refs, wp_refs = refs[:n_in], refs[n_in:2 * n_in]
    x_ref, g_ref, mod_ref, wg_ref, wu_ref, wo_ref, o_ref, h_ref, wgu_ref, wob_ref = refs[2 * n_in:]
    k = pl.program_id(1)
    tm = x_ref.shape[0]
    tk = wg_ref.shape[2]
    n_chunks = tm // row_chunk

    def mod_row(c):
        return 0 if rows_per_cond is None else (c * row_chunk) // rows_per_cond

    def mix_rows(c):
        rows, m = _ds(c * row_chunk, row_chunk), mod_row(c)
        acc = None
        for a_ref, wp_ref in zip(a_refs, wp_refs):
            part = jnp.dot(a_ref[rows, :], wp_ref[...], preferred_element_type=F32)
            acc = part if acc is None else acc + part
        x_new = x_ref[rows, :] + mod_ref[m, 2:3, :] * acc
        o_ref[rows, :] = x_new
        h = _rms(x_new, g_ref[...])
        h = h * (1.0 + mod_ref[m, 4:5, :]) + mod_ref[m, 3:4, :]
        h_ref[rows, :] = h.astype(BF16)

    def ffn_rows(c):
        rows, m = _ds(c * row_chunk, row_chunk), mod_row(c)
        gu = jnp.dot(h_ref[rows, :], wgu_ref[...], preferred_element_type=F32)
        a = (_silu(gu[:, :tk]) * gu[:, tk:]).astype(BF16)
        o_ref[rows, :] += mod_ref[m, 5:6, :] * jnp.dot(a, wob_ref[...], preferred_element_type=F32)

    def cast_weights():
        wgu_ref[:, :tk] = wg_ref[0].astype(BF16)
        wgu_ref[:, tk:] = wu_ref[0].astype(BF16)
        wob_ref[...] = wo_ref[0].astype(BF16)

    @pl.when(k == 0)
    def _():
        cast_weights()
        for c in range(n_chunks):
            mix_rows(c)
            ffn_rows(c)

    @pl.when(k > 0)
    def _():
        cast_weights()
        for c in range(n_chunks):
            ffn_rows(c)


def _mix_ffn(acts, wps, x, g, mod, w_in, w_out, layer, *, mod_base, rows_per_cond, tm):
    n, d = x.shape
    hidden = w_out.shape[1]
    tk = 256
    nk = hidden // tk
    n_in = len(acts)
    n_mod = 1 if rows_per_cond is None else max(tm // rows_per_cond, 1)
    once = dict(pipeline_mode=pl.Buffered(1))
    return pl.pallas_call(
        functools.partial(_ffn_body, n_in=n_in, nk=nk, row_chunk=512, rows_per_cond=rows_per_cond),
        grid=(n // tm, nk),
        in_specs=([pl.BlockSpec((tm, a.shape[1]), lambda i, k: (i, 0), **once) for a in acts]
                  + [pl.BlockSpec(w.shape, lambda i, k: (0, 0), **once) for w in wps]
                  + [pl.BlockSpec((tm, d), lambda i, k: (i, 0)),
                     pl.BlockSpec((1, d), lambda i, k: (0, 0)),
                     pl.BlockSpec((n_mod, 6, d),
                                  (lambda f: (lambda i, k: f(i)))(_mod_index(mod_base, rows_per_cond, tm))),
                     pl.BlockSpec((1, d, tk), lambda i, k: (layer, 0, k)),
                     pl.BlockSpec((1, d, tk), lambda i, k: (layer, 0, nk + k)),
                     pl.BlockSpec((1, tk, d), lambda i, k: (layer, k, 0))]),
        out_specs=pl.BlockSpec((tm, d), lambda i, k: (i, 0)),
        out_shape=jax.ShapeDtypeStruct((n, d), F32),
        scratch_shapes=[pltpu.VMEM((tm, d), BF16),
                        pltpu.VMEM((d, 2 * tk), BF16),
                        pltpu.VMEM((tk, d), BF16)],
        compiler_params=_cparams(("arbitrary", "arbitrary"), VMEM_LIMIT),
        name="mix_ffn",
    )(*acts, *wps, x, g.reshape(1, d), mod, w_in, w_in, w_out)


def _log_sigmoid(x):
    return jnp.minimum(x, 0.0) - jnp.log1p(jnp.exp(-jnp.abs(x)))


def _gla_body(*refs, T, has_s0, want_state, n_alias, n_zero_slots, nb):
    refs = refs[n_alias:]
    qkvr_ref, alk_ref, w2_ref, ab_ref, og_ref = refs[:5]
    pos = 5
    s0_ref = None
    if has_s0:
        s0_ref = refs[pos]
        pos += 1
    o_ref = refs[pos]
    pos += 1
    st_out_ref = kpe_out_ref = None
    if want_state:
        st_out_ref, kpe_out_ref = refs[pos:pos + 2]
        pos += 2
    osc_ref, st_ref, la_ref, qin_ref, kst_ref, dec_ref, upd_ref, stq_ref = refs[pos:pos + 8]

    C = GLA_CHUNK
    nc = T // C

    logit = jnp.dot(alk_ref[...].astype(BF16), w2_ref[...], preferred_element_type=F32) + ab_ref[...]
    la_ref[...] = _log_sigmoid(logit) * (1.0 / GLA_TAU)

    for s in range(nb):
        for d in range(2):
            if has_s0:
                st_ref[2 * s + d] = jnp.concatenate([s0_ref[s, d, h] for h in range(GLA_HEADS)], axis=0).T
            else:
                st_ref[2 * s + d] = jnp.zeros((GLA_DV, GLA_QK), F32)

    r64 = lax.broadcasted_iota(jnp.int32, (C, C), 0)
    c64 = lax.broadcasted_iota(jnp.int32, (C, C), 1)
    tri_f = (r64 >= c64).astype(BF16)
    tri_b = (c64 >= r64).astype(BF16)
    t_idx = lax.broadcasted_iota(jnp.int32, (C, GLA_QK), 0)
    s_idx = lax.broadcasted_iota(jnp.int32, (C, GLA_QK), 1) % C
    causal_f = t_idx >= s_idx
    causal_b = t_idx <= s_idx
    bm_k = (lax.broadcasted_iota(jnp.int32, (GLA_QK, GLA_QK), 0) // GLA_DK
            == lax.broadcasted_iota(jnp.int32, (GLA_QK, GLA_QK), 1) // GLA_DK)
    bm_v = (lax.broadcasted_iota(jnp.int32, (GLA_QK, GLA_V), 0) // C
            == lax.broadcasted_iota(jnp.int32, (GLA_QK, GLA_V), 1) // GLA_DV)
    head_lanes = [lax.broadcasted_iota(jnp.int32, (1, GLA_QK), 1) // GLA_DK == h for h in range(GLA_HEADS)]
    directions = ((tri_f, causal_f, C // 2 - 1, C - 1), (tri_b, causal_b, C // 2, 0))

    def v_rows(rows):
        return qkvr_ref[rows, 2 * GLA_QK:2 * GLA_QK + GLA_V]

    G = 4
    units = [(i, d) for i in range(G) for d in range(2)]

    n_groups = nb * nc // G

    def for_groups(fn):
        if n_groups <= 4:
            for g in range(n_groups):
                fn(g * G)
        else:
            def body(g, carry):
                fn(g * G)
                return carry
            lax.fori_loop(0, n_groups, body, 0)

    def intra_group(n0):
        rows = [_ds((n0 + i) * C, C) for i in range(G)]
        b = {}
        for i, d in units:
            b[i, d] = _split_dot(la_ref[rows[i], d * GLA_QK:(d + 1) * GLA_QK], directions[d][0], w_left=True)
        q_loc, k_bd = {}, {}
        for i in range(G):
            qc = qkvr_ref[rows[i], 0:GLA_QK].astype(F32) * (GLA_DK ** -0.5)
            kc = qkvr_ref[rows[i], GLA_QK:2 * GLA_QK].astype(F32)
            for d in range(2):
                _, _, ref_row, last_row = directions[d]
                bb = b[i, d]
                b_ref = bb[ref_row:ref_row + 1, :]
                b_last = bb[last_row:last_row + 1, :]
                q_loc[i, d] = (qc * jnp.exp(bb - b_ref)).astype(BF16)
                k_loc = kc * jnp.exp(b_ref - bb)
                k_bd[i, d] = jnp.where(bm_k, jnp.concatenate([k_loc] * GLA_HEADS, axis=0), 0.0).astype(BF16)
                qin_ref[d, rows[i], :] = (qc * jnp.exp(bb)).astype(BF16)
                kst_ref[d, rows[i], :] = (kc * jnp.exp(b_last - bb)).astype(BF16)
                dec_ref[d, _ds((n0 + i) * 8, 8), :] = jnp.broadcast_to(jnp.exp(b_last), (8, GLA_QK))
        a = {u: jnp.where(directions[u[1]][1], _nt_dot(q_loc[u], k_bd[u]), 0.0).astype(BF16) for u in units}
        v_bd = [jnp.where(bm_v, jnp.concatenate([v_rows(rows[i])] * GLA_HEADS, axis=0), jnp.zeros((), BF16))
                for i in range(G)]
        o = {u: jnp.dot(a[u], v_bd[u[0]], preferred_element_type=F32) for u in units}
        for i in range(G):
            osc_ref[rows[i], :] = o[i, 0] + o[i, 1]

    def update_group(n0):
        rows = [_ds((n0 + i) * C, C) for i in range(G)]
        upd = {u: _tn_dot(v_rows(rows[u[0]]), kst_ref[u[1], rows[u[0]], :]) for u in units}
        for i, d in units:
            acc = None
            for h, m in enumerate(head_lanes):
                part = jnp.where(m, upd[i, d][h * GLA_DV:(h + 1) * GLA_DV, :], 0.0)
                acc = part if acc is None else acc + part
            upd_ref[d, n0 + i] = acc

    def scan_step(i, carry):
        for s in range(nb):
            for d, n in ((0, s * nc + i), (1, s * nc + nc - 1 - i)):
                st = st_ref[2 * s + d]
                stq_ref[d, n] = st.astype(BF16)
                st_ref[2 * s + d] = st * dec_ref[d, pl.ds(n * 8, 1), :] + upd_ref[d, n]
        return carry

    def readout_group(n0):
        rows = [_ds((n0 + i) * C, C) for i in range(G)]
        o = {}
        for i, d in units:
            q_in = qin_ref[d, rows[i], :]
            q_heads = jnp.concatenate([jnp.where(m, q_in, jnp.zeros((), BF16)) for m in head_lanes], axis=0)
            o[i, d] = _nt_dot(q_heads, stq_ref[d, n0 + i])
        for i in range(G):
            both = o[i, 0] + o[i, 1]
            for h in range(GLA_HEADS):
                osc_ref[rows[i], h * GLA_DV:(h + 1) * GLA_DV] += both[h * C:(h + 1) * C, :]

    for_groups(intra_group)
    for_groups(update_group)
    if nc == G:
        for i in range(nc):
            scan_step(i, 0)
    else:
        lax.fori_loop(0, nc, scan_step, 0)
    for_groups(readout_group)

    rb = 256
    for i in range(nb * T // rb):
        rows = slice(i * rb, (i + 1) * rb)
        for h in range(GLA_HEADS):
            cols = slice(h * GLA_DV, (h + 1) * GLA_DV)
            o = _rms(osc_ref[rows, cols], og_ref[:, cols])
            r = qkvr_ref[rows, 2 * GLA_QK + GLA_V + h * GLA_DV:2 * GLA_QK + GLA_V + (h + 1) * GLA_DV].astype(F32)
            o_ref[rows, cols] = (o * _silu(r)).astype(o_ref.dtype)

    if want_state:
        for s in range(nb):
            for d in range(2):
                s_all = st_ref[2 * s + d].T
                for h in range(GLA_HEADS):
                    st_out_ref[s, 0, d, h] = s_all[h * GLA_DK:(h + 1) * GLA_DK, :]
            kpe_out_ref[s, 0] = alk_ref[s * T:(s + 1) * T, 0:MLA_ROPE]
            if n_zero_slots:
                st_out_ref[s, 1:] = jnp.zeros((n_zero_slots, 2, GLA_HEADS, GLA_DK, GLA_DV), F32)
                kpe_out_ref[s, 1:] = jnp.zeros((n_zero_slots, T, MLA_ROPE), F32)


def _gla(proj, alk, w2bd, a_b, out_g, s0, *, B, T, row0=0, state_out=None, nb=1):
    has_s0 = s0 is not None
    want_state = state_out is not None
    n_qkvr = 2 * GLA_QK + 2 * GLA_V
    R = nb * T
    n_chunks = R // GLA_CHUNK
    off = row0 // R
    assert row0 % R == 0
    in_specs = [pl.BlockSpec((R, n_qkvr), lambda b: (off + b, 0)),
                pl.BlockSpec((R, LANES), lambda b: (off + b, 0)),
                pl.BlockSpec(w2bd.shape, lambda b: (0, 0)),
                pl.BlockSpec((1, 2 * GLA_QK), lambda b: (0, 0)),
                pl.BlockSpec((1, GLA_V), lambda b: (0, 0))]
    args = [proj, alk, w2bd, a_b, out_g]
    if has_s0:
        states, layer = s0
        in_specs.append(pl.BlockSpec((nb, None, 2, GLA_HEADS, GLA_DK, GLA_DV),
                                     lambda b: (b, layer, 0, 0, 0, 0)))
        args.append(states)
    out_specs = [pl.BlockSpec((R, GLA_V), lambda b: (b, 0))]
    out_shape = [jax.ShapeDtypeStruct((B * T, GLA_V), BF16)]
    aliased, n_zero = [], 0
    if want_state:
        slot, n_slots, prev_state, prev_kpe = state_out
        for prev, tail in ((prev_state, (2, GLA_HEADS, GLA_DK, GLA_DV)), (prev_kpe, (T, MLA_ROPE))):
            spec, shape, al, n_zero = _slot_output((slot, n_slots, prev), B, tail, F32, nb)
            out_specs.append(spec)
            out_shape.append(shape)
            aliased += al
    res = pl.pallas_call(
        functools.partial(_gla_body, T=T, has_s0=has_s0, want_state=want_state, n_alias=len(aliased),
                          n_zero_slots=n_zero, nb=nb),
        grid=(B // nb,),
        in_specs=[pl.BlockSpec(memory_space=pl.ANY)] * len(aliased) + in_specs,
        out_specs=out_specs,
        out_shape=out_shape,
        input_output_aliases={i: 1 + i for i in range(len(aliased))},
        scratch_shapes=[pltpu.VMEM((R, GLA_V), F32),
                        pltpu.VMEM((2 * nb, GLA_DV, GLA_QK), F32),
                        pltpu.VMEM((R, 2 * GLA_QK), F32),
                        pltpu.VMEM((2, R, GLA_QK), BF16),
                        pltpu.VMEM((2, R, GLA_QK), BF16),
                        pltpu.VMEM((2, 8 * n_chunks, GLA_QK), F32),
                        pltpu.VMEM((2, n_chunks, GLA_DV, GLA_QK), F32),
                        pltpu.VMEM((2, n_chunks, GLA_DV, GLA_QK), BF16)],
        compiler_params=_cparams(("arbitrary",), VMEM_LIMIT),
        name="gla",
    )(*aliased, *args)
    return res if want_state else (res[0], None)


LOG2E = 1.4426950408889634


V_ROWS = 80


def _attend(qs, ks, vs, group, key_major_dv=None):
    key_major = key_major_dv is not None
    axis = 0 if key_major else -1

    def qk(g0):
        return [_nt_dot(k(), q) if key_major else _nt_dot(q, k())
                for q, k in zip(qs[g0:g0 + group], ks[g0:g0 + group])]

    outs = []
    scores = qk(0)
    for g0 in range(0, len(qs), group):
        probs = [jnp.exp2(s - jnp.max(s, axis=axis, keepdims=True)).astype(BF16) for s in scores]
        if g0 + group < len(qs):
            scores = qk(g0 + group)
        if key_major:
            dv = key_major_dv
            res = [jnp.dot(v(), p, preferred_element_type=F32) for p, v in zip(probs, vs[g0:g0 + group])]
            outs += [r[:dv, :] / r[dv:dv + 1, :] for r in res]
        else:
            res = [jnp.dot(p, v(), preferred_element_type=F32) for p, v in zip(probs, vs[g0:g0 + group])]
            outs += [r[:, :LANES] / r[:, LANES:] for r in res]
    return outs


def _head_group(n_keys, n_heads):
    return 4 if n_keys <= 512 else 2


def _stacked_ms(xs, ones_mat, inv_n):
    n = xs[0].shape[0]
    ms = _split_dot(jnp.concatenate([x * x for x in xs], axis=0), ones_mat) * inv_n
    return [ms[i * n:(i + 1) * n] for i in range(len(xs))]


def _split_dot(x, w, w_left=False):
    hi = x.astype(BF16)
    lo = (x - hi.astype(F32)).astype(BF16)
    if w_left:
        return jnp.dot(w, hi, preferred_element_type=F32) + jnp.dot(w, lo, preferred_element_type=F32)
    return jnp.dot(hi, w, preferred_element_type=F32) + jnp.dot(lo, w, preferred_element_type=F32)


def _group_ones(group):
    r = lax.broadcasted_iota(jnp.int32, (LANES, LANES), 0) // group
    c = lax.broadcasted_iota(jnp.int32, (LANES, LANES), 1) // group
    return (r == c).astype(BF16)


def _mla_body(*refs, T, S_ctx, rope, want_ckv, group, n_alias, n_zero_slots, nb):
    refs = refs[n_alias:]
    cq_ref, kpe_ref, ckv_ref = refs[:3]
    pos = 3
    if S_ctx:
        ckvc_ref, kpec_ref = refs[pos:pos + 2]
        pos += 2
    if rope:
        rq_refs = refs[pos:pos + 3]
        rk_refs = refs[pos + 3:pos + 6]
        pos += 6
    qng_ref, wqb_ref, qg_ref, kvg_ref, wkvb_ref, kg_ref = refs[pos:pos + 6]
    pos += 6
    o_ref = refs[pos]
    pos += 1
    if want_ckv:
        ckvn_ref = refs[pos]
        pos += 1
    k_sc, vlo_sc, vhi_sc = refs[pos:pos + 3]
    lo = lax.broadcasted_iota(jnp.int32, (1, LANES), 1) < MLA_V

    H = MLA_HEADS
    n_k = H * LANES
    def head_ms(xs):
        return [jnp.sum(x * x, axis=-1, keepdims=True) * (1.0 / MLA_QK) for x in xs]

    def fill_kv(ckvn, kpe, row0, n_rows, with_rope):
        rows = slice(row0, row0 + n_rows)
        kv = jnp.dot(ckvn.astype(BF16), wkvb_ref[...], preferred_element_type=F32)
        khs = [kv[:, h * LANES:(h + 1) * LANES] + kpe for h in range(H)]
        for h, (kh, ms) in enumerate(zip(khs, head_ms(khs))):
            kh = kh * lax.rsqrt(ms + EPS) * kg_ref[...]
            if with_rope:
                kh = _rope(kh, rk_refs[0][...], rk_refs[1][...], rk_refs[2][...], MLA_ROPE // 4)
            k_sc[rows, h * LANES:(h + 1) * LANES] = kh.astype(BF16)
        ones = jnp.ones((n_rows, LANES), BF16)
        for p in range(H // 2):
            v = kv[:, n_k + p * LANES:n_k + (p + 1) * LANES]
            vlo_sc[rows, 2 * p * LANES:(2 * p + 1) * LANES] = jnp.where(lo, v, 0.0).astype(BF16)
            vhi_sc[rows, 2 * p * LANES:(2 * p + 1) * LANES] = jnp.where(lo, 0.0, v).astype(BF16)
            vlo_sc[rows, (2 * p + 1) * LANES:(2 * p + 2) * LANES] = ones
            vhi_sc[rows, (2 * p + 1) * LANES:(2 * p + 2) * LANES] = ones

    S = S_ctx + T
    tq = cq_ref.shape[0] // nb

    @pl.when(pl.program_id(1) == 0)
    def _():
        for s in range(nb):
            if S_ctx:
                fill_kv(ckvc_ref[s], kpec_ref[s], s * S, S_ctx, False)
            ckvn = _rms(ckv_ref[s * T:(s + 1) * T, :], kvg_ref[...])
            if want_ckv:
                ckvn_ref[s, 0] = ckvn
                if n_zero_slots:
                    ckvn_ref[s, 1:] = jnp.zeros((n_zero_slots, T, MLA_KV_RANK), F32)
            fill_kv(ckvn, kpe_ref[s * T:(s + 1) * T, :].astype(F32), s * S + S_ctx, T, rope)

    cqn = _rms(cq_ref[...].astype(F32), qng_ref[...])
    q = jnp.dot(cqn.astype(BF16), wqb_ref[...], preferred_element_type=F32)
    units = [(s, h) for s in range(nb) for h in range(H)]
    qhs = [q[s * tq:(s + 1) * tq, h * LANES:(h + 1) * LANES] for s, h in units]
    qs = []
    for qh, ms in zip(qhs, head_ms(qhs)):
        qh = qh * lax.rsqrt(ms + EPS) * qg_ref[...]
        if rope:
            qh = _rope(qh, rq_refs[0][...], rq_refs[1][...], rq_refs[2][...], MLA_ROPE // 4)
        qs.append(qh.astype(BF16))
    ks = [functools.partial(lambda s, h: k_sc[s * S:(s + 1) * S, h * LANES:(h + 1) * LANES], s, h)
          for s, h in units]
    vs = [functools.partial(lambda s, h: (vlo_sc, vhi_sc)[h % 2][s * S:(s + 1) * S,
                                                                (h // 2) * 2 * LANES:(h // 2 + 1) * 2 * LANES], s, h)
          for s, h in units]
    outs = _attend(qs, ks, vs, group)
    for s in range(nb):
        for p in range(H // 2):
            o_pair = outs[s * H + 2 * p] + outs[s * H + 2 * p + 1]
            o_ref[s * tq:(s + 1) * tq, p * LANES:(p + 1) * LANES] = o_pair.astype(o_ref.dtype)


def _mla(proj, ckv, ctx, rope_tabs, params, *, B, T, tq, row0=0, ckv_out=None, nb=1):
    want_ckv = ckv_out is not None
    nq = T // tq
    assert nb == 1 or nq == 1
    S_ctx = ctx[0].shape[2] if ctx is not None else 0
    S = S_ctx + T
    rope = rope_tabs is not None
    cq_blk = (2 * GLA_QK + 2 * GLA_V) // MLA_Q_RANK
    kpe_blk = (2 * GLA_QK + 2 * GLA_V + MLA_Q_RANK) // LANES
    off_q, off_t = row0 // (nb * tq), row0 // (nb * T)
    assert row0 % (nb * T) == 0
    in_specs = [pl.BlockSpec((nb * tq, MLA_Q_RANK), lambda b, j: (off_q + b * nq + j, cq_blk)),
                pl.BlockSpec((nb * T, LANES), lambda b, j: (off_t + b, kpe_blk)),
                pl.BlockSpec((nb * T, MLA_KV_RANK), lambda b, j: (off_t + b, 0))]
    args = [proj, proj, ckv]
    if S_ctx:
        layer = ctx[2]
        in_specs += [pl.BlockSpec((nb, None, S_ctx, MLA_KV_RANK), lambda b, j: (b, layer, 0, 0)),
                     pl.BlockSpec((nb, None, S_ctx, LANES), lambda b, j: (b, layer, 0, 0))]
        args += list(ctx[:2])
    if rope:
        in_specs += [pl.BlockSpec((tq, LANES), lambda b, j: (j, 0))] * 3
        in_specs += [pl.BlockSpec((T, LANES), lambda b, j: (0, 0))] * 3
        args += list(rope_tabs) * 2
    in_specs += [pl.BlockSpec(p.shape, lambda b, j: (0, 0)) for p in params]
    args += list(params)
    out_specs = [pl.BlockSpec((nb * tq, MLA_HEADS * MLA_V), lambda b, j: (b * nq + j, 0))]
    out_shape = [jax.ShapeDtypeStruct((B * T, MLA_HEADS * MLA_V), BF16)]
    aliased, n_zero = [], 0
    if want_ckv:
        spec, shape, aliased, n_zero = _slot_output(ckv_out, B, (T, MLA_KV_RANK), F32, nb)
        out_specs.append(spec)
        out_shape.append(shape)
    res = pl.pallas_call(
        functools.partial(_mla_body, T=T, S_ctx=S_ctx, rope=rope, want_ckv=want_ckv,
                          group=_head_group(S, MLA_HEADS), n_alias=len(aliased), n_zero_slots=n_zero, nb=nb),
        grid=(B // nb, nq),
        in_specs=[pl.BlockSpec(memory_space=pl.ANY)] * len(aliased) + in_specs,
        out_specs=out_specs,
        out_shape=out_shape,
        input_output_aliases={i: 1 + i for i in range(len(aliased))},
        scratch_shapes=[pltpu.VMEM((nb * S, MLA_HEADS * LANES), BF16)] * 3,
        compiler_params=_cparams(("arbitrary", "arbitrary"), VMEM_LIMIT),
        name="mla",
    )(*aliased, *args)
    return res if want_ckv else (res[0], None)


def _gqa_body(*refs, T, S_ctx, rope, want_kv, group, n_alias, n_zero_slots, nb):
    refs = refs[n_alias:]
    q_ref, k_ref, v_ref = refs[:3]
    pos = 3
    if S_ctx:
        kc_ref, vc_ref = refs[pos:pos + 2]
        pos += 2
    if rope:
        rq_refs = refs[pos:pos + 3]
        rk_refs = refs[pos + 3:pos + 6]
        pos += 6
    qg_ref, kg_ref = refs[pos:pos + 2]
    pos += 2
    o_ref = refs[pos]
    pos += 1
    if want_kv:
        kn_ref, vo_ref = refs[pos:pos + 2]
        pos += 2
    klo_sc, khi_sc, vt_sc = refs[pos:pos + 3]

    lo = lax.broadcasted_iota(jnp.int32, (1, LANES), 1) < GQA_DH
    half_sum = _group_ones(GQA_DH)

    def head_ms(x):
        return _split_dot(x * x, half_sum) * (1.0 / GQA_DH)

    def scatter_halves(x, lo_sc, hi_sc, c, rows):
        rolled = pltpu.roll(x, GQA_DH, 1)
        lo_sc[2 * c, rows, :LANES] = jnp.where(lo, x, 0.0).astype(BF16)
        hi_sc[2 * c, rows, :LANES] = jnp.where(lo, 0.0, rolled).astype(BF16)
        lo_sc[2 * c + 1, rows, :LANES] = jnp.where(lo, rolled, 0.0).astype(BF16)
        hi_sc[2 * c + 1, rows, :LANES] = jnp.where(lo, 0.0, x).astype(BF16)

    def fill_vt(x, c, rows):
        x_t = x.T
        for half in range(2):
            vt_sc[2 * c + half, 0:GQA_DH, rows] = x_t[half * GQA_DH:(half + 1) * GQA_DH, :].astype(BF16)
            vt_sc[2 * c + half, GQA_DH:, rows] = jnp.ones((V_ROWS - GQA_DH, rows.stop - rows.start), BF16)

    S = S_ctx + T
    tq = q_ref.shape[0] // nb

    @pl.when(pl.program_id(1) == 0)
    def _():
        for s in range(nb):
            if want_kv and n_zero_slots:
                zeros = jnp.zeros((n_zero_slots, T, GQA_KV_HEADS * GQA_DH), F32)
                kn_ref[s, 1:] = zeros
                vo_ref[s, 1:] = zeros
            for c in range(GQA_KV_HEADS // 2):
                cols = slice(c * LANES, (c + 1) * LANES)
                if S_ctx:
                    scatter_halves(kc_ref[s, :, cols], klo_sc, khi_sc, c, slice(s * S, s * S + S_ctx))
                    fill_vt(vc_ref[s, :, cols], c, slice(s * S, s * S + S_ctx))
                kx = k_ref[s * T:(s + 1) * T, cols]
                kn = kx * lax.rsqrt(head_ms(kx) + EPS) * kg_ref[...]
                vx = v_ref[s * T:(s + 1) * T, cols]
                if want_kv:
                    kn_ref[s, 0, :, cols] = kn
                    vo_ref[s, 0, :, cols] = vx
                if rope:
                    kn = _rope(kn, rk_refs[0][...], rk_refs[1][...], rk_refs[2][...], GQA_DH // 4)
                scatter_halves(kn, klo_sc, khi_sc, c, slice(s * S + S_ctx, (s + 1) * S))
                fill_vt(vx, c, slice(s * S + S_ctx, (s + 1) * S))

    n_pairs = GQA_HEADS // 2
    units = [(s, p) for s in range(nb) for p in range(n_pairs)]
    qxs = [q_ref[s * tq:(s + 1) * tq, p * LANES:(p + 1) * LANES].astype(F32) for s, p in units]
    qs, ks, vs = [], [], []
    for (s, p), qx, ms in zip(units, qxs, _stacked_ms(qxs, half_sum, 1.0 / GQA_DH)):
        qn = qx * lax.rsqrt(ms + EPS) * qg_ref[...]
        if rope:
            qn = _rope(qn, rq_refs[0][...], rq_refs[1][...], rq_refs[2][...], GQA_DH // 4)
        g = p // 2
        qs += [qn.astype(BF16)] * 2
        ks += [functools.partial(lambda r, g, s: r[g, s * S:(s + 1) * S, :], r, g, s) for r in (klo_sc, khi_sc)]
        vs += [functools.partial(lambda g, s: vt_sc[g, :, s * S:(s + 1) * S], g, s)] * 2
    outs = _attend(qs, ks, vs, group, key_major_dv=GQA_DH)
    for s in range(nb):
        o_t = jnp.concatenate(outs[s * GQA_HEADS:(s + 1) * GQA_HEADS], axis=0)
        o_ref[s * tq:(s + 1) * tq, :] = o_t.T.astype(o_ref.dtype)


def _gqa(q, k, v, ctx, rope_tabs, params, *, B, T, tq, row0=0, kv_out=None, nb=1):
    want_kv = kv_out is not None
    nq = T // tq
    assert nb == 1 or nq == 1
    S_ctx = ctx[0].shape[2] if ctx is not None else 0
    S = S_ctx + T
    rope = rope_tabs is not None
    n_q = GQA_HEADS * GQA_DH
    n_kv = GQA_KV_HEADS * GQA_DH
    off_q, off_t = row0 // (nb * tq), row0 // (nb * T)
    assert row0 % (nb * T) == 0
    in_specs = [pl.BlockSpec((nb * tq, n_q), lambda b, j: (off_q + b * nq + j, 0)),
                pl.BlockSpec((nb * T, n_kv), lambda b, j: (off_t + b, 0)),
                pl.BlockSpec((nb * T, n_kv), lambda b, j: (off_t + b, 0))]
    args = [q, k, v]
    if S_ctx:
        layer = ctx[2]
        in_specs += [pl.BlockSpec((nb, None, S_ctx, n_kv), lambda b, j: (b, layer, 0, 0))] * 2
        args += list(ctx[:2])
    if rope:
        in_specs += [pl.BlockSpec((tq, LANES), lambda b, j: (j, 0))] * 3
        in_specs += [pl.BlockSpec((T, LANES), lambda b, j: (0, 0))] * 3
        args += list(rope_tabs) * 2
    in_specs += [pl.BlockSpec(p.shape, lambda b, j: (0, 0)) for p in params]
    args += list(params)
    out_specs = [pl.BlockSpec((nb * tq, n_q), lambda b, j: (b * nq + j, 0))]
    out_shape = [jax.ShapeDtypeStruct((B * T, n_q), BF16)]
    aliased, n_zero = [], 0
    if want_kv:
        slot, n_slots, prev_k, prev_v = kv_out
        for prev in (prev_k, prev_v):
            spec, shape, al, n_zero = _slot_output((slot, n_slots, prev), B, (T, n_kv), F32, nb)
            out_specs.append(spec)
            out_shape.append(shape)
            aliased += al
    res = pl.pallas_call(
        functools.partial(_gqa_body, T=T, S_ctx=S_ctx, rope=rope, want_kv=want_kv,
                          group=_head_group(S, GQA_HEADS), n_alias=len(aliased), n_zero_slots=n_zero, nb=nb),
        grid=(B // nb, nq),
        in_specs=[pl.BlockSpec(memory_space=pl.ANY)] * len(aliased) + in_specs,
        out_specs=out_specs,
        out_shape=out_shape,
        input_output_aliases={i: 1 + i for i in range(len(aliased))},
        scratch_shapes=[pltpu.VMEM((GQA_KV_HEADS, nb * S, LANES), BF16),
                        pltpu.VMEM((GQA_KV_HEADS, nb * S, LANES), BF16),
                        pltpu.VMEM((GQA_KV_HEADS, V_ROWS, nb * S), BF16)],
        compiler_params=_cparams(("arbitrary", "arbitrary"), VMEM_LIMIT),
        name="gqa",
    )(*aliased, *args)
    return res if want_kv else (res[0], None, None)


def _rope_tables(n_tok, d_rot, lead, reps):
    t = jnp.arange(n_tok, dtype=jnp.int32)
    posn = jnp.stack([t // GRID_W, t % GRID_W], axis=-1).astype(F32)
    quarter = d_rot // 4
    inv = jnp.power(ROPE_THETA, -jnp.arange(quarter, dtype=F32) / quarter)
    ang = posn[:, :, None] * inv
    cos, sin = jnp.cos(ang), jnp.sin(ang)
    zero = jnp.zeros_like(sin)
    c_rot = jnp.stack([cos, cos], axis=2).reshape(n_tok, d_rot)
    sp_rot = jnp.stack([-sin, zero], axis=2).reshape(n_tok, d_rot)
    sm_rot = jnp.stack([zero, sin], axis=2).reshape(n_tok, d_rot)
    tail = LANES - lead - reps * d_rot

    def embed(rot, fill):
        parts = [jnp.full((n_tok, lead), fill, F32)] + [rot] * reps + [jnp.full((n_tok, tail), fill, F32)]
        return jnp.concatenate(parts, axis=1)

    return embed(c_rot, 1.0), embed(sp_rot, 0.0), embed(sm_rot, 0.0)


def _pad_lanes(x, lead, width=LANES):
    pad = [(0, 0)] * (x.ndim - 1) + [(lead, width - lead - x.shape[-1])]
    return jnp.pad(x, pad)


def _ab_params(w_in, w_out, a_w2, a_b, out_g, q_norm_g, w_qb, kv_norm_g, w_kvb, qn_g, kn_g):
    d = w_in.shape[0]
    o_alo = 2 * GLA_QK + 2 * GLA_V
    o_cq = o_alo + 2 * GLA_RANK
    o_ckv = o_cq + MLA_Q_RANK
    o_kpe = o_ckv + MLA_KV_RANK
    w_kpe = w_in[:, o_kpe:]
    w_perm = jnp.concatenate([
        w_in[:, :o_alo], w_in[:, o_cq:o_ckv], _pad_lanes(w_kpe, MLA_NOPE),
        w_in[:, o_ckv:o_kpe],
        _pad_lanes(jnp.concatenate([w_kpe, w_in[:, o_alo:o_cq]], axis=1), 0),
    ], axis=1).astype(BF16)
    w2bd = jnp.zeros((LANES, 2 * GLA_QK), F32)
    w2bd = (w2bd.at[MLA_ROPE:MLA_ROPE + GLA_RANK, :GLA_QK].set(a_w2[0])
            .at[MLA_ROPE + GLA_RANK:MLA_ROPE + 2 * GLA_RANK, GLA_QK:].set(a_w2[1]))
    w_qb_p = _pad_lanes(w_qb.reshape(MLA_Q_RANK, MLA_HEADS, MLA_QK), 0).reshape(MLA_Q_RANK, MLA_HEADS * LANES)
    kvb = w_kvb.reshape(MLA_KV_RANK, MLA_HEADS, MLA_NOPE + MLA_V)
    w_kvb_p = jnp.concatenate([
        _pad_lanes(kvb[:, :, :MLA_NOPE], 0).reshape(MLA_KV_RANK, MLA_HEADS * LANES),
        kvb[:, :, MLA_NOPE:].reshape(MLA_KV_RANK, MLA_HEADS * MLA_V)], axis=1)
    return dict(
        w_perm=w_perm,
        w2bd=w2bd.astype(BF16),
        a_b=a_b.reshape(1, 2 * GLA_QK),
        out_g=jnp.tile(out_g, GLA_HEADS).reshape(1, GLA_V),
        mla=(q_norm_g.reshape(1, -1), w_qb_p.astype(BF16),
             _pad_lanes(qn_g * (MLA_QK ** -0.5 * LOG2E), 0).reshape(1, LANES),
             kv_norm_g.reshape(1, -1), w_kvb_p.astype(BF16),
             _pad_lanes(kn_g, 0).reshape(1, LANES)),
        w_out_gla=w_out[:GLA_V].astype(BF16),
        w_out_mla=w_out[GLA_V:].astype(BF16),
    )


AB_OUTS = ((2 * GLA_QK + 2 * GLA_V + MLA_Q_RANK + LANES, BF16), (MLA_KV_RANK, F32), (LANES, F32))
C_OUTS = ((GQA_HEADS * GQA_DH, BF16), (GQA_KV_HEADS * GQA_DH, F32), (GQA_KV_HEADS * GQA_DH, F32))


def kernel(x_prompt, x_sample, c, cache_mla_ckv, cache_mla_kpe, state_gla, cache_gqa_k, cache_gqa_v,
           c_ctx, ada_w, ada_b, norm_mix_g, norm_ffn_g, ffn_w_in, ffn_w_out, ab_w_in, ab_w_out,
           gla_a_w2, gla_a_b, gla_out_g, mla_q_norm_g, mla_w_qb, mla_kv_norm_g, mla_w_kvb, mla_qn_g,
           mla_kn_g, gqa_w_in, gqa_w_out, gqa_qn_g, gqa_kn_g):
    Bp, Tp, D = x_prompt.shape
    Bs, Ts, _ = x_sample.shape
    depth = ada_w.shape[0]
    xp = x_prompt.reshape(Bp * Tp, D)
    xs = x_sample.reshape(Bs * Ts, D)

    cond = jnp.zeros((8, D), F32).at[0].set(c_ctx).at[2:2 + Bs].set(c)
    mods = _adaln(cond, ada_w, ada_b).reshape(depth, 8, 6, D)

    rope_mla = _rope_tables(Ts, MLA_ROPE, MLA_NOPE, 1)
    rope_gqa = _rope_tables(Ts, GQA_DH, 0, 2)

    streams = (dict(mod_base=0, rows_per_cond=None), dict(mod_base=2, rows_per_cond=Ts))
    proj_tm = 512
    ffn_tm = 2048
    prompt_nb = 4
    n_p = Bp * Tp

    n_ab, n_c = (depth + 1) // 2, depth // 2
    n_kv = GQA_KV_HEADS * GQA_DH
    kpe_cache = _pad_lanes(cache_mla_kpe, MLA_NOPE)
    gqa_k_cache = cache_gqa_k.reshape(Bs, n_c, -1, n_kv)
    gqa_v_cache = cache_gqa_v.reshape(Bs, n_c, -1, n_kv)
    new_ckv = new_kpe = new_gla = new_k = new_v = None
    for l in range(depth):
        i = l // 2
        mod = mods[l]
        proj = functools.partial(_inproj, xp, xs, norm_mix_g[l], mod, sample_mod_base=2, rows_per_cond=Ts,
                                 tm=proj_tm)
        if l % 2 == 0:
            P = _ab_params(ab_w_in[i], ab_w_out[i], gla_a_w2[i], gla_a_b[i], gla_out_g[i], mla_q_norm_g[i],
                           mla_w_qb[i], mla_kv_norm_g[i], mla_w_kvb[i], mla_qn_g[i], mla_kn_g[i])
            pj, ckv, alk = proj(P['w_perm'], AB_OUTS)
            gla = functools.partial(_gla, pj, alk, P['w2bd'], P['a_b'], P['out_g'])
            og_p, new_gla, new_kpe = gla(None, B=Bp, T=Tp, state_out=(i, n_ab, new_gla, new_kpe), nb=prompt_nb)
            og_s, _ = gla((state_gla, i), B=Bs, T=Ts, row0=n_p)
            om_p, new_ckv = _mla(pj, ckv, None, None, P['mla'], B=Bp, T=Tp, tq=Tp,
                                 ckv_out=(i, n_ab, new_ckv), nb=prompt_nb // 2)
            om_s, _ = _mla(pj, ckv, (cache_mla_ckv, kpe_cache, i), rope_mla, P['mla'], B=Bs, T=Ts, tq=256,
                           row0=n_p)
            wps = (P['w_out_gla'], P['w_out_mla'])
            acts_p, acts_s = (og_p, om_p), (og_s, om_s)
        else:
            w_in = gqa_w_in[i].astype(BF16)
            w_out = gqa_w_out[i].astype(BF16)
            gp = (jnp.tile(gqa_qn_g[i] * (GQA_DH ** -0.5 * LOG2E), 2).reshape(1, LANES),
                  jnp.tile(gqa_kn_g[i], 2).reshape(1, LANES))
            q, k, v = proj(w_in, C_OUTS)
            o_p, new_k, new_v = _gqa(q, k, v, None, None, gp, B=Bp, T=Tp, tq=Tp,
                                     kv_out=(i, n_c, new_k, new_v), nb=prompt_nb)
            o_s, _, _ = _gqa(q, k, v, (gqa_k_cache, gqa_v_cache, i), rope_gqa, gp, B=Bs, T=Ts, tq=256, row0=n_p)
            wps = (w_out,)
            acts_p, acts_s = (o_p,), (o_s,)
        xp = _mix_ffn(acts_p, wps, xp, norm_ffn_g[l], mod, ffn_w_in, ffn_w_out, l, **streams[0], tm=ffn_tm)
        xs = _mix_ffn(acts_s, wps, xs, norm_ffn_g[l], mod, ffn_w_in, ffn_w_out, l, **streams[1], tm=ffn_tm)
    kv_shape = (Bp, n_c, Tp, GQA_KV_HEADS, GQA_DH)
    return (xp.reshape(Bp, Tp, D), xs.reshape(Bs, Ts, D), new_ckv, new_kpe, new_gla,
            new_k.reshape(kv_shape), new_v.reshape(kv_shape))
```

```python
import functools

import jax
import jax.numpy as jnp
from jax import lax
from jax.experimental import pallas as pl
from jax.experimental.pallas import tpu as pltpu

F32 = jnp.float32
BF16 = jnp.bfloat16

EPS = 1e-6
ROPE_THETA = 10000.0
GRID_W = 64
LANES = 128
GLA_HEADS, GLA_DK, GLA_DV = 4, 64, 128
GLA_QK = GLA_HEADS * GLA_DK
GLA_V = GLA_HEADS * GLA_DV
GLA_RANK = 16
GLA_TAU = 16.0
GLA_CHUNK = 64
MLA_HEADS = 8
MLA_Q_RANK, MLA_KV_RANK = 384, 256
MLA_NOPE, MLA_ROPE, MLA_V = 64, 32, 64
MLA_QK = MLA_NOPE + MLA_ROPE
GQA_HEADS, GQA_KV_HEADS, GQA_DH = 16, 4, 64
VMEM_LIMIT = 56 << 20


def _cparams(sem, vmem=None):
    return pltpu.CompilerParams(dimension_semantics=sem, vmem_limit_bytes=vmem)


def _nt_dot(a, b):
    return lax.dot_general(a, b, (((1,), (1,)), ((), ())), preferred_element_type=F32)


def _tn_dot(a, b):
    return lax.dot_general(a, b, (((0,), (0,)), ((), ())), preferred_element_type=F32)


def _ds(start, size):
    if isinstance(start, int):
        return pl.ds(start, size)
    return pl.ds(pl.multiple_of(start, size), size)


def _rms(x, g):
    ms = jnp.mean(x * x, axis=-1, keepdims=True)
    return x * lax.rsqrt(ms + EPS) * g


def _silu(x):
    return x * jax.nn.sigmoid(x)


def _rope(x, c, sp, sm, shift):
    return x * c + pltpu.roll(x, LANES - shift, 1) * sp + pltpu.roll(x, shift, 1) * sm


def _adaln_body(c_ref, w_ref, b_ref, o_ref):
    c = c_ref[...]
    s = _silu(c).astype(BF16)
    o_ref[0] = jnp.dot(s, w_ref[0].astype(BF16), preferred_element_type=F32) + b_ref[0]


def _adaln(cond, ada_w, ada_b):
    L, D, E = ada_w.shape
    tn = 1536
    return pl.pallas_call(
        _adaln_body,
        grid=(L, E // tn),
        in_specs=[pl.BlockSpec((8, D), lambda l, n: (0, 0)),
                  pl.BlockSpec((1, D, tn), lambda l, n: (l, 0, n)),
                  pl.BlockSpec((1, 1, tn), lambda l, n: (l, 0, n))],
        out_specs=pl.BlockSpec((1, 8, tn), lambda l, n: (l, 0, n)),
        out_shape=jax.ShapeDtypeStruct((L, 8, E), F32),
        compiler_params=_cparams(("arbitrary", "arbitrary")),
        name="adaln",
    )(cond, ada_w, ada_b.reshape(L, 1, E))


def _slot_output(slot_out, batch, tail, dtype, nb=1):
    slot, n_slots, prev = slot_out
    zeros = (0,) * len(tail)
    shape = jax.ShapeDtypeStruct((batch, n_slots) + tail, dtype)
    if slot == 0:
        return pl.BlockSpec((nb, n_slots) + tail, lambda b, *_: (b, 0) + zeros), shape, [], n_slots - 1
    return pl.BlockSpec((nb, 1) + tail, lambda b, *_: (b, slot) + zeros), shape, [prev], 0


def _mod_index(mod_base, rows_per_cond, tm):
    if rows_per_cond is None:
        return lambda i: (mod_base, 0, 0)
    n_mod = max(tm // rows_per_cond, 1)
    return lambda i: ((mod_base + (i * tm) // rows_per_cond) // n_mod, 0, 0)


def _inproj_body(xp_ref, xs_ref, g_ref, mod_ref, w_ref, *o_refs, widths, n_prompt_tiles):
    def project(x_ref):
        h = _rms(x_ref[...], g_ref[...])
        h = h * (1.0 + mod_ref[0, 1:2, :]) + mod_ref[0, 0:1, :]
        acc = jnp.dot(h.astype(BF16), w_ref[...], preferred_element_type=F32)
        off = 0
        for o_ref, w in zip(o_refs, widths):
            o_ref[...] = acc[:, off:off + w].astype(o_ref.dtype)
            off += w

    is_prompt = pl.program_id(0) < n_prompt_tiles
    pl.when(is_prompt)(lambda: project(xp_ref))
    pl.when(jnp.logical_not(is_prompt))(lambda: project(xs_ref))


def _inproj(xp, xs, g, mod, w, outs, *, sample_mod_base, rows_per_cond, tm):
    n_p, d = xp.shape
    n = n_p + xs.shape[0]
    np_t = n_p // tm
    widths = tuple(o[0] for o in outs)

    def mod_idx(i):
        return (jnp.where(i < np_t, 0, sample_mod_base + ((i - np_t) * tm) // rows_per_cond), 0, 0)

    return pl.pallas_call(
        functools.partial(_inproj_body, widths=widths, n_prompt_tiles=np_t),
        grid=(n // tm,),
        in_specs=[pl.BlockSpec((tm, d), lambda i: (jnp.minimum(i, np_t - 1), 0)),
                  pl.BlockSpec((tm, d), lambda i: (jnp.maximum(i - np_t, 0), 0)),
                  pl.BlockSpec((1, d), lambda i: (0, 0)),
                  pl.BlockSpec((1, 6, d), mod_idx),
                  pl.BlockSpec(w.shape, lambda i: (0, 0))],
        out_specs=[pl.BlockSpec((tm, wd), lambda i: (i, 0)) for wd in widths],
        out_shape=[jax.ShapeDtypeStruct((n, wd), dt) for wd, dt in outs],
        compiler_params=_cparams(("arbitrary",), VMEM_LIMIT),
        name="inproj",
    )(xp, xs, g.reshape(1, d), mod, w)


def _ffn_body(*refs, n_in, nk, row_chunk, rows_per_cond):
    a_refs, wp_refs = refs[:n_in], refs[n_in:2 * n_in]
    x_ref, g_ref, mod_ref, wg_ref, wu_ref, wo_ref, o_ref, h_ref, wgu_ref, wob_ref = refs[2 * n_in:]
    k = pl.program_id(1)
    tm = x_ref.shape[0]
    tk = wg_ref.shape[2]
    n_chunks = tm // row_chunk

    def mod_row(c):
        return 0 if rows_per_cond is None else (c * row_chunk) // rows_per_cond

    def mix_proj(c):
        rows = _ds(c * row_chunk, row_chunk)
        acc = None
        for a_ref, wp_ref in zip(a_refs, wp_refs):
            part = jnp.dot(a_ref[rows, :], wp_ref[...], preferred_element_type=F32)
            acc = part if acc is None else acc + part
        return acc

    def mix_rows(c, acc):
        rows, m = _ds(c * row_chunk, row_chunk), mod_row(c)
        x_new = x_ref[rows, :] + mod_ref[m, 2:3, :] * acc
        o_ref[rows, :] = x_new
        h = _rms(x_new, g_ref[...])
        h = h * (1.0 + mod_ref[m, 4:5, :]) + mod_ref[m, 3:4, :]
        h_ref[rows, :] = h.astype(BF16)

    def ffn_rows(c):
        rows, m = _ds(c * row_chunk, row_chunk), mod_row(c)
        gu = jnp.dot(h_ref[rows, :], wgu_ref[...], preferred_element_type=F32)
        a = (_silu(gu[:, :tk]) * gu[:, tk:]).astype(BF16)
        o_ref[rows, :] += mod_ref[m, 5:6, :] * jnp.dot(a, wob_ref[...], preferred_element_type=F32)

    def cast_weights():
        wgu_ref[:, :tk] = wg_ref[0].astype(BF16)
        wgu_ref[:, tk:] = wu_ref[0].astype(BF16)
        wob_ref[...] = wo_ref[0].astype(BF16)

    @pl.when(k == 0)
    def _():
        cast_weights()
        accs = [mix_proj(c) for c in range(n_chunks)]
        for c in range(n_chunks):
            mix_rows(c, accs[c])
            ffn_rows(c)

    @pl.when(k > 0)
    def _():
        cast_weights()
        for c in range(n_chunks):
            ffn_rows(c)


def _mix_ffn(acts, wps, x, g, mod, w_in, w_out, layer, *, mod_base, rows_per_cond, tm):
    n, d = x.shape
    hidden = w_out.shape[1]
    tk = 256
    nk = hidden // tk
    n_in = len(acts)
    n_mod = 1 if rows_per_cond is None else max(tm // rows_per_cond, 1)
    once = dict(pipeline_mode=pl.Buffered(1))
    return pl.pallas_call(
        functools.partial(_ffn_body, n_in=n_in, nk=nk, row_chunk=512, rows_per_cond=rows_per_cond),
        grid=(n // tm, nk),
        in_specs=([pl.BlockSpec((tm, a.shape[1]), lambda i, k: (i, 0), **once) for a in acts]
                  + [pl.BlockSpec(w.shape, lambda i, k: (0, 0), **once) for w in wps]
                  + [pl.BlockSpec((tm, d), lambda i, k: (i, 0)),
                     pl.BlockSpec((1, d), lambda i, k: (0, 0)),
                     pl.BlockSpec((n_mod, 6, d),
                                  (lambda f: (lambda i, k: f(i)))(_mod_index(mod_base, rows_per_cond, tm))),
                     pl.BlockSpec((1, d, tk), lambda i, k: (layer, 0, k)),
                     pl.BlockSpec((1, d, tk), lambda i, k: (layer, 0, nk + k)),
                     pl.BlockSpec((1, tk, d), lambda i, k: (layer, k, 0))]),
        out_specs=pl.BlockSpec((tm, d), lambda i, k: (i, 0)),
        out_shape=jax.ShapeDtypeStruct((n, d), F32),
        scratch_shapes=[pltpu.VMEM((tm, d), BF16),
                        pltpu.VMEM((d, 2 * tk), BF16),
                        pltpu.VMEM((tk, d), BF16)],
        compiler_params=_cparams(("arbitrary", "arbitrary"), VMEM_LIMIT),
        name="mix_ffn",
    )(*acts, *wps, x, g.reshape(1, d), mod, w_in, w_in, w_out)


def _log_sigmoid(x):
    return jnp.minimum(x, 0.0) - jnp.log(1.0 + jnp.exp(-jnp.abs(x)))


def _gla_body(*refs, T, has_s0, want_state, n_alias, n_zero_slots, nb):
    refs = refs[n_alias:]
    qkvr_ref, alk_ref, w2_ref, ab_ref, og_ref = refs[:5]
    pos = 5
    s0_ref = None
    if has_s0:
        s0_ref = refs[pos]
        pos += 1
    o_ref = refs[pos]
    pos += 1
    st_out_ref = kpe_out_ref = None
    if want_state:
        st_out_ref, kpe_out_ref = refs[pos:pos + 2]
        pos += 2
    osc_ref, st_ref, la_ref, qin_ref, kst_ref, dec_ref, upd_ref, stq_ref = refs[pos:pos + 8]

    C = GLA_CHUNK
    nc = T // C

    logit = jnp.dot(alk_ref[...].astype(BF16), w2_ref[...], preferred_element_type=F32) + ab_ref[...]
    la_ref[...] = _log_sigmoid(logit) * (1.0 / GLA_TAU)

    for s in range(nb):
        for d in range(2):
            if has_s0:
                st_ref[2 * s + d] = jnp.concatenate([s0_ref[s, d, h] for h in range(GLA_HEADS)], axis=0).T
            else:
                st_ref[2 * s + d] = jnp.zeros((GLA_DV, GLA_QK), F32)

    r64 = lax.broadcasted_iota(jnp.int32, (C, C), 0)
    c64 = lax.broadcasted_iota(jnp.int32, (C, C), 1)
    tri_f = (r64 >= c64).astype(BF16)
    tri_b = (c64 >= r64).astype(BF16)
    t_idx = lax.broadcasted_iota(jnp.int32, (C, GLA_QK), 0)
    s_idx = lax.broadcasted_iota(jnp.int32, (C, GLA_QK), 1) % C
    causal_f = t_idx >= s_idx
    causal_b = t_idx <= s_idx
    bm_k = (lax.broadcasted_iota(jnp.int32, (GLA_QK, GLA_QK), 0) // GLA_DK
            == lax.broadcasted_iota(jnp.int32, (GLA_QK, GLA_QK), 1) // GLA_DK)
    bm_v = (lax.broadcasted_iota(jnp.int32, (GLA_QK, GLA_V), 0) // C
            == lax.broadcasted_iota(jnp.int32, (GLA_QK, GLA_V), 1) // GLA_DV)
    head_lanes = [lax.broadcasted_iota(jnp.int32, (1, GLA_QK), 1) // GLA_DK == h for h in range(GLA_HEADS)]
    directions = ((tri_f, causal_f, C // 2 - 1, C - 1), (tri_b, causal_b, C // 2, 0))

    def v_rows(rows):
        return qkvr_ref[rows, 2 * GLA_QK:2 * GLA_QK + GLA_V]

    G = 4
    units = [(i, d) for i in range(G) for d in range(2)]

    n_groups = nb * nc // G

    def for_groups(fn):
        if n_groups <= 4:
            for g in range(n_groups):
                fn(g * G)
        else:
            def body(g, carry):
                fn(g * G)
                return carry
            lax.fori_loop(0, n_groups, body, 0)

    def intra_group(n0):
        rows = [_ds((n0 + i) * C, C) for i in range(G)]
        b = {}
        for i, d in units:
            b[i, d] = _split_dot(la_ref[rows[i], d * GLA_QK:(d + 1) * GLA_QK], directions[d][0], w_left=True)
        q_loc, k_bd = {}, {}
        for i in range(G):
            qc = qkvr_ref[rows[i], 0:GLA_QK].astype(F32) * (GLA_DK ** -0.5)
            kc = qkvr_ref[rows[i], GLA_QK:2 * GLA_QK].astype(F32)
            for d in range(2):
                _, _, ref_row, last_row = directions[d]
                bb = b[i, d]
                b_ref = bb[ref_row:ref_row + 1, :]
                b_last = bb[last_row:last_row + 1, :]
                q_loc[i, d] = (qc * jnp.exp(bb - b_ref)).astype(BF16)
                k_loc = kc * jnp.exp(b_ref - bb)
                k_bd[i, d] = jnp.where(bm_k, jnp.concatenate([k_loc] * GLA_HEADS, axis=0), 0.0).astype(BF16)
                qin_ref[d, rows[i], :] = (qc * jnp.exp(bb)).astype(BF16)
                kst_ref[d, rows[i], :] = (kc * jnp.exp(b_last - bb)).astype(BF16)
                dec_ref[d, _ds((n0 + i) * 8, 8), :] = jnp.broadcast_to(jnp.exp(b_last), (8, GLA_QK))
        a = {u: jnp.where(directions[u[1]][1], _nt_dot(q_loc[u], k_bd[u]), 0.0).astype(BF16) for u in units}
        v_bd = [jnp.where(bm_v, jnp.concatenate([v_rows(rows[i])] * GLA_HEADS, axis=0), jnp.zeros((), BF16))
                for i in range(G)]
        o = {u: jnp.dot(a[u], v_bd[u[0]], preferred_element_type=F32) for u in units}
        for i in range(G):
            osc_ref[rows[i], :] = o[i, 0] + o[i, 1]

    def update_group(n0):
        rows = [_ds((n0 + i) * C, C) for i in range(G)]
        upd = {u: _tn_dot(v_rows(rows[u[0]]), kst_ref[u[1], rows[u[0]], :]) for u in units}
        for i, d in units:
            acc = None
            for h, m in enumerate(head_lanes):
                part = jnp.where(m, upd[i, d][h * GLA_DV:(h + 1) * GLA_DV, :], 0.0)
                acc = part if acc is None else acc + part
            upd_ref[d, n0 + i] = acc

    def scan_step(i, carry):
        for s in range(nb):
            for d, n in ((0, s * nc + i), (1, s * nc + nc - 1 - i)):
                st = st_ref[2 * s + d]
                stq_ref[d, n] = st.astype(BF16)
                st_ref[2 * s + d] = st * dec_ref[d, pl.ds(n * 8, 1), :] + upd_ref[d, n]
        return carry

    def readout_group(n0):
        rows = [_ds((n0 + i) * C, C) for i in range(G)]
        o = {}
        for i, d in units:
            q_in = qin_ref[d, rows[i], :]
            q_heads = jnp.concatenate([jnp.where(m, q_in, jnp.zeros((), BF16)) for m in head_lanes], axis=0)
            o[i, d] = _nt_dot(q_heads, stq_ref[d, n0 + i])
        for i in range(G):
            both = o[i, 0] + o[i, 1]
            for h in range(GLA_HEADS):
                osc_ref[rows[i], h * GLA_DV:(h + 1) * GLA_DV] += both[h * C:(h + 1) * C, :]

    for_groups(intra_group)
    for_groups(update_group)
    if nc == G:
        for i in range(nc):
            scan_step(i, 0)
    else:
        lax.fori_loop(0, nc, scan_step, 0)
    for_groups(readout_group)

    rb = 256
    for i in range(nb * T // rb):
        rows = slice(i * rb, (i + 1) * rb)
        for h in range(GLA_HEADS):
            cols = slice(h * GLA_DV, (h + 1) * GLA_DV)
            o = _rms(osc_ref[rows, cols], og_ref[:, cols])
            r = qkvr_ref[rows, 2 * GLA_QK + GLA_V + h * GLA_DV:2 * GLA_QK + GLA_V + (h + 1) * GLA_DV].astype(F32)
            o_ref[rows, cols] = (o * _silu(r)).astype(o_ref.dtype)

    if want_state:
        for s in range(nb):
            for d in range(2):
                s_all = st_ref[2 * s + d].T
                for h in range(GLA_HEADS):
                    st_out_ref[s, 0, d, h] = s_all[h * GLA_DK:(h + 1) * GLA_DK, :]
            kpe_out_ref[s, 0] = alk_ref[s * T:(s + 1) * T, 0:MLA_ROPE]
            if n_zero_slots:
                st_out_ref[s, 1:] = jnp.zeros((n_zero_slots, 2, GLA_HEADS, GLA_DK, GLA_DV), F32)
                kpe_out_ref[s, 1:] = jnp.zeros((n_zero_slots, T, MLA_ROPE), F32)


def _gla(proj, alk, w2bd, a_b, out_g, s0, *, B, T, row0=0, state_out=None, nb=1):
    has_s0 = s0 is not None
    want_state = state_out is not None
    n_qkvr = 2 * GLA_QK + 2 * GLA_V
    R = nb * T
    n_chunks = R // GLA_CHUNK
    off = row0 // R
    assert row0 % R == 0
    in_specs = [pl.BlockSpec((R, n_qkvr), lambda b: (off + b, 0)),
                pl.BlockSpec((R, LANES), lambda b: (off + b, 0)),
                pl.BlockSpec(w2bd.shape, lambda b: (0, 0)),
                pl.BlockSpec((1, 2 * GLA_QK), lambda b: (0, 0)),
                pl.BlockSpec((1, GLA_V), lambda b: (0, 0))]
    args = [proj, alk, w2bd, a_b, out_g]
    if has_s0:
        states, layer = s0
        in_specs.append(pl.BlockSpec((nb, None, 2, GLA_HEADS, GLA_DK, GLA_DV),
                                     lambda b: (b, layer, 0, 0, 0, 0)))
        args.append(states)
    out_specs = [pl.BlockSpec((R, GLA_V), lambda b: (b, 0))]
    out_shape = [jax.ShapeDtypeStruct((B * T, GLA_V), BF16)]
    aliased, n_zero = [], 0
    if want_state:
        slot, n_slots, prev_state, prev_kpe = state_out
        for prev, tail in ((prev_state, (2, GLA_HEADS, GLA_DK, GLA_DV)), (prev_kpe, (T, MLA_ROPE))):
            spec, shape, al, n_zero = _slot_output((slot, n_slots, prev), B, tail, F32, nb)
            out_specs.append(spec)
            out_shape.append(shape)
            aliased += al
    res = pl.pallas_call(
        functools.partial(_gla_body, T=T, has_s0=has_s0, want_state=want_state, n_alias=len(aliased),
                          n_zero_slots=n_zero, nb=nb),
        grid=(B // nb,),
        in_specs=[pl.BlockSpec(memory_space=pl.ANY)] * len(aliased) + in_specs,
        out_specs=out_specs,
        out_shape=out_shape,
        input_output_aliases={i: 1 + i for i in range(len(aliased))},
        scratch_shapes=[pltpu.VMEM((R, GLA_V), F32),
                        pltpu.VMEM((2 * nb, GLA_DV, GLA_QK), F32),
                        pltpu.VMEM((R, 2 * GLA_QK), F32),
                        pltpu.VMEM((2, R, GLA_QK), BF16),
                        pltpu.VMEM((2, R, GLA_QK), BF16),
                        pltpu.VMEM((2, 8 * n_chunks, GLA_QK), F32),
                        pltpu.VMEM((2, n_chunks, GLA_DV, GLA_QK), F32),
                        pltpu.VMEM((2, n_chunks, GLA_DV, GLA_QK), BF16)],
        compiler_params=_cparams(("arbitrary",), VMEM_LIMIT),
        name="gla",
    )(*aliased, *args)
    return res if want_state else (res[0], None)


LOG2E = 1.4426950408889634


V_ROWS = 80


def _attend(qs, ks, vs, group, key_major_dv=None, depth=1):
    key_major = key_major_dv is not None
    axis = 0 if key_major else -1

    def qk(g0):
        return [_nt_dot(k(), q) if key_major else _nt_dot(q, k())
                for q, k in zip(qs[g0:g0 + group], ks[g0:g0 + group])]

    outs = []
    starts = list(range(0, len(qs), group))
    pending = [qk(g0) for g0 in starts[:depth]]
    for i, g0 in enumerate(starts):
        scores = pending.pop(0)
        probs = [jnp.exp2(s - jnp.max(s, axis=axis, keepdims=True)).astype(BF16) for s in scores]
        if i + depth < len(starts):
            pending.append(qk(starts[i + depth]))
        if key_major:
            dv = key_major_dv
            res = [jnp.dot(v(), p, preferred_element_type=F32) for p, v in zip(probs, vs[g0:g0 + group])]
            outs += [r[:dv, :] / r[dv:dv + 1, :] for r in res]
        else:
            res = [jnp.dot(p, v(), preferred_element_type=F32) for p, v in zip(probs, vs[g0:g0 + group])]
            outs += [r[:, :LANES] / r[:, LANES:] for r in res]
    return outs


def _head_group(n_keys, n_heads):
    return 4 if n_keys <= 512 else 2


def _stacked_ms(xs, ones_mat, inv_n):
    n = xs[0].shape[0]
    sq = jnp.concatenate([x * x for x in xs], axis=0).astype(BF16)
    ms = jnp.dot(sq, ones_mat, preferred_element_type=F32) * inv_n
    return [ms[i * n:(i + 1) * n] for i in range(len(xs))]


def _split_dot(x, w, w_left=False):
    hi = x.astype(BF16)
    lo = (x - hi.astype(F32)).astype(BF16)
    if w_left:
        return jnp.dot(w, hi, preferred_element_type=F32) + jnp.dot(w, lo, preferred_element_type=F32)
    return jnp.dot(hi, w, preferred_element_type=F32) + jnp.dot(lo, w, preferred_element_type=F32)


def _group_ones(group):
    r = lax.broadcasted_iota(jnp.int32, (LANES, LANES), 0) // group
    c = lax.broadcasted_iota(jnp.int32, (LANES, LANES), 1) // group
    return (r == c).astype(BF16)


def _mla_body(*refs, T, S_ctx, rope, want_ckv, group, n_alias, n_zero_slots, nb):
    refs = refs[n_alias:]
    cq_ref, kpe_ref, ckv_ref = refs[:3]
    pos = 3
    if S_ctx:
        ckvc_ref, kpec_ref = refs[pos:pos + 2]
        pos += 2
    if rope:
        rq_refs = refs[pos:pos + 3]
        rk_refs = refs[pos + 3:pos + 6]
        pos += 6
    qng_ref, wqb_ref, qg_ref, kvg_ref, wkb_ref, wv_ref, kg_ref = refs[pos:pos + 7]
    pos += 7
    o_ref = refs[pos]
    pos += 1
    if want_ckv:
        ckvn_ref = refs[pos]
        pos += 1
    k_sc, vlo_sc, vhi_sc = refs[pos:pos + 3]

    H = MLA_HEADS
    lo = lax.broadcasted_iota(jnp.int32, (1, LANES), 1) < MLA_V
    lane_sum = _group_ones(LANES)

    def head_ms(xs):
        return [jnp.sum(x * x, axis=-1, keepdims=True) * (1.0 / MLA_QK) for x in xs]

    def fill_kv(ckvn, kpe, row0, n_rows, with_rope):
        rows = slice(row0, row0 + n_rows)
        ckvn = ckvn.astype(BF16)
        k_nope = jnp.dot(ckvn, wkb_ref[...], preferred_element_type=F32)
        v = jnp.dot(ckvn, wv_ref[...], preferred_element_type=F32)
        pe = kpe * kg_ref[...]
        if with_rope:
            pe = _rope(pe, rk_refs[0][...], rk_refs[1][...], rk_refs[2][...], MLA_ROPE // 4)
        khs = [k_nope[:, h * LANES:(h + 1) * LANES] for h in range(H)]
        full = [kh + kpe for kh in khs]
        mss = _stacked_ms(full, lane_sum, 1.0 / MLA_QK) if n_rows >= 512 else head_ms(full)
        for h, (kh, ms) in enumerate(zip(khs, mss)):
            k_sc[rows, h * LANES:(h + 1) * LANES] = ((kh * kg_ref[...] + pe) * lax.rsqrt(ms + EPS)).astype(BF16)
        ones = jnp.ones((n_rows, LANES), BF16)
        for p in range(H // 2):
            vp = v[:, p * LANES:(p + 1) * LANES]
            vlo_sc[rows, 2 * p * LANES:(2 * p + 1) * LANES] = jnp.where(lo, vp, 0.0).astype(BF16)
            vhi_sc[rows, 2 * p * LANES:(2 * p + 1) * LANES] = jnp.where(lo, 0.0, vp).astype(BF16)
            vlo_sc[rows, (2 * p + 1) * LANES:(2 * p + 2) * LANES] = ones
            vhi_sc[rows, (2 * p + 1) * LANES:(2 * p + 2) * LANES] = ones

    S = S_ctx + T
    tq = cq_ref.shape[0] // nb

    @pl.when(pl.program_id(1) == 0)
    def _():
        for s in range(nb):
            if S_ctx:
                fill_kv(ckvc_ref[s], kpec_ref[s], s * S, S_ctx, False)
            ckvn = _rms(ckv_ref[s * T:(s + 1) * T, :], kvg_ref[...])
            if want_ckv:
                ckvn_ref[s, 0] = ckvn
                if n_zero_slots:
                    ckvn_ref[s, 1:] = jnp.zeros((n_zero_slots, T, MLA_KV_RANK), F32)
            fill_kv(ckvn, kpe_ref[s * T:(s + 1) * T, :].astype(F32), s * S + S_ctx, T, rope)

    cqn = _rms(cq_ref[...].astype(F32), qng_ref[...])
    q = jnp.dot(cqn.astype(BF16), wqb_ref[...], preferred_element_type=F32)
    units = [(s, h) for s in range(nb) for h in range(H)]
    qhs = [q[s * tq:(s + 1) * tq, h * LANES:(h + 1) * LANES] for s, h in units]
    qs = []
    for qh, ms in zip(qhs, head_ms(qhs)):
        qh = qh * lax.rsqrt(ms + EPS) * qg_ref[...]
        if rope:
            qh = _rope(qh, rq_refs[0][...], rq_refs[1][...], rq_refs[2][...], MLA_ROPE // 4)
        qs.append(qh.astype(BF16))
    ks = [functools.partial(lambda s, h: k_sc[s * S:(s + 1) * S, h * LANES:(h + 1) * LANES], s, h)
          for s, h in units]
    vs = [functools.partial(lambda s, h: (vlo_sc, vhi_sc)[h % 2][s * S:(s + 1) * S,
                                                                (h // 2) * 2 * LANES:(h // 2 + 1) * 2 * LANES], s, h)
          for s, h in units]
    outs = _attend(qs, ks, vs, group)
    for s in range(nb):
        for p in range(H // 2):
            o_pair = outs[s * H + 2 * p] + outs[s * H + 2 * p + 1]
            o_ref[s * tq:(s + 1) * tq, p * LANES:(p + 1) * LANES] = o_pair.astype(o_ref.dtype)


def _mla(proj, ckv, ctx, rope_tabs, params, *, B, T, tq, row0=0, ckv_out=None, nb=1):
    want_ckv = ckv_out is not None
    nq = T // tq
    assert nb == 1 or nq == 1
    S_ctx = ctx[0].shape[2] if ctx is not None else 0
    S = S_ctx + T
    rope = rope_tabs is not None
    cq_blk = (2 * GLA_QK + 2 * GLA_V) // MLA_Q_RANK
    kpe_blk = (2 * GLA_QK + 2 * GLA_V + MLA_Q_RANK) // LANES
    off_q, off_t = row0 // (nb * tq), row0 // (nb * T)
    assert row0 % (nb * T) == 0
    in_specs = [pl.BlockSpec((nb * tq, MLA_Q_RANK), lambda b, j: (off_q + b * nq + j, cq_blk)),
                pl.BlockSpec((nb * T, LANES), lambda b, j: (off_t + b, kpe_blk)),
                pl.BlockSpec((nb * T, MLA_KV_RANK), lambda b, j: (off_t + b, 0))]
    args = [proj, proj, ckv]
    if S_ctx:
        layer = ctx[2]
        in_specs += [pl.BlockSpec((nb, None, S_ctx, MLA_KV_RANK), lambda b, j: (b, layer, 0, 0)),
                     pl.BlockSpec((nb, None, S_ctx, LANES), lambda b, j: (b, layer, 0, 0))]
        args += list(ctx[:2])
    if rope:
        in_specs += [pl.BlockSpec((tq, LANES), lambda b, j: (j, 0))] * 3
        in_specs += [pl.BlockSpec((T, LANES), lambda b, j: (0, 0))] * 3
        args += list(rope_tabs) * 2
    in_specs += [pl.BlockSpec(p.shape, lambda b, j: (0, 0)) for p in params]
    args += list(params)
    out_specs = [pl.BlockSpec((nb * tq, MLA_HEADS * MLA_V), lambda b, j: (b * nq + j, 0))]
    out_shape = [jax.ShapeDtypeStruct((B * T, MLA_HEADS * MLA_V), BF16)]
    aliased, n_zero = [], 0
    if want_ckv:
        spec, shape, aliased, n_zero = _slot_output(ckv_out, B, (T, MLA_KV_RANK), F32, nb)
        out_specs.append(spec)
        out_shape.append(shape)
    res = pl.pallas_call(
        functools.partial(_mla_body, T=T, S_ctx=S_ctx, rope=rope, want_ckv=want_ckv,
                          group=_head_group(S, MLA_HEADS), n_alias=len(aliased), n_zero_slots=n_zero, nb=nb),
        grid=(B // nb, nq),
        in_specs=[pl.BlockSpec(memory_space=pl.ANY)] * len(aliased) + in_specs,
        out_specs=out_specs,
        out_shape=out_shape,
        input_output_aliases={i: 1 + i for i in range(len(aliased))},
        scratch_shapes=[pltpu.VMEM((nb * S, MLA_HEADS * LANES), BF16)] * 3,
        compiler_params=_cparams(("arbitrary", "arbitrary"), VMEM_LIMIT),
        name="mla",
    )(*aliased, *args)
    return res if want_ckv else (res[0], None)


def _gqa_body(*refs, T, S_ctx, rope, want_kv, group, n_alias, n_zero_slots, nb):
    refs = refs[n_alias:]
    q_ref, k_ref, v_ref = refs[:3]
    pos = 3
    if S_ctx:
        kc_ref, vc_ref = refs[pos:pos + 2]
        pos += 2
    if rope:
        rq_refs = refs[pos:pos + 3]
        rk_refs = refs[pos + 3:pos + 6]
        pos += 6
    qg_ref, kg_ref = refs[pos:pos + 2]
    pos += 2
    o_ref = refs[pos]
    pos += 1
    if want_kv:
        kn_ref, vo_ref = refs[pos:pos + 2]
        pos += 2
    klo_sc, khi_sc, vt_sc = refs[pos:pos + 3]

    lo = lax.broadcasted_iota(jnp.int32, (1, LANES), 1) < GQA_DH
    half_sum = _group_ones(GQA_DH)

    def head_ms(x):
        return _split_dot(x * x, half_sum) * (1.0 / GQA_DH)

    def scatter_halves(x, lo_sc, hi_sc, c, rows):
        rolled = pltpu.roll(x, GQA_DH, 1)
        lo_sc[2 * c, rows, :LANES] = jnp.where(lo, x, 0.0).astype(BF16)
        hi_sc[2 * c, rows, :LANES] = jnp.where(lo, 0.0, rolled).astype(BF16)
        lo_sc[2 * c + 1, rows, :LANES] = jnp.where(lo, rolled, 0.0).astype(BF16)
        hi_sc[2 * c + 1, rows, :LANES] = jnp.where(lo, 0.0, x).astype(BF16)

    def fill_vt(x, c, rows):
        x_t = x.T
        for half in range(2):
            vt_sc[2 * c + half, 0:GQA_DH, rows] = x_t[half * GQA_DH:(half + 1) * GQA_DH, :].astype(BF16)
            vt_sc[2 * c + half, GQA_DH:, rows] = jnp.ones((V_ROWS - GQA_DH, rows.stop - rows.start), BF16)

    S = S_ctx + T
    tq = q_ref.shape[0] // nb

    @pl.when(pl.program_id(1) == 0)
    def _():
        for s in range(nb):
            if want_kv and n_zero_slots:
                zeros = jnp.zeros((n_zero_slots, T, GQA_KV_HEADS * GQA_DH), F32)
                kn_ref[s, 1:] = zeros
                vo_ref[s, 1:] = zeros
            for c in range(GQA_KV_HEADS // 2):
                cols = slice(c * LANES, (c + 1) * LANES)
                if S_ctx:
                    scatter_halves(kc_ref[s, :, cols], klo_sc, khi_sc, c, slice(s * S, s * S + S_ctx))
                    fill_vt(vc_ref[s, :, cols], c, slice(s * S, s * S + S_ctx))
                kx = k_ref[s * T:(s + 1) * T, cols]
                kn = kx * lax.rsqrt(head_ms(kx) + EPS) * kg_ref[...]
                vx = v_ref[s * T:(s + 1) * T, cols]
                if want_kv:
                    kn_ref[s, 0, :, cols] = kn
                    vo_ref[s, 0, :, cols] = vx
                if rope:
                    kn = _rope(kn, rk_refs[0][...], rk_refs[1][...], rk_refs[2][...], GQA_DH // 4)
                scatter_halves(kn, klo_sc, khi_sc, c, slice(s * S + S_ctx, (s + 1) * S))
                fill_vt(vx, c, slice(s * S + S_ctx, (s + 1) * S))

    n_pairs = GQA_HEADS // 2
    units = [(s, p) for s in range(nb) for p in range(n_pairs)]
    qxs = [q_ref[s * tq:(s + 1) * tq, p * LANES:(p + 1) * LANES].astype(F32) for s, p in units]
    qs, ks, vs = [], [], []
    for (s, p), qx, ms in zip(units, qxs, _stacked_ms(qxs, half_sum, 1.0 / GQA_DH)):
        qn = qx * lax.rsqrt(ms + EPS) * qg_ref[...]
        if rope:
            qn = _rope(qn, rq_refs[0][...], rq_refs[1][...], rq_refs[2][...], GQA_DH // 4)
        g = p // 2
        qs += [qn.astype(BF16)] * 2
        ks += [functools.partial(lambda r, g, s: r[g, s * S:(s + 1) * S, :], r, g, s) for r in (klo_sc, khi_sc)]
        vs += [functools.partial(lambda g, s: vt_sc[g, :, s * S:(s + 1) * S], g, s)] * 2
    outs = _attend(qs, ks, vs, group, key_major_dv=GQA_DH, depth=2)
    for s in range(nb):
        o_t = jnp.concatenate(outs[s * GQA_HEADS:(s + 1) * GQA_HEADS], axis=0)
        o_ref[s * tq:(s + 1) * tq, :] = o_t.T.astype(o_ref.dtype)


def _gqa(q, k, v, ctx, rope_tabs, params, *, B, T, tq, row0=0, kv_out=None, nb=1):
    want_kv = kv_out is not None
    nq = T // tq
    assert nb == 1 or nq == 1
    S_ctx = ctx[0].shape[2] if ctx is not None else 0
    S = S_ctx + T
    rope = rope_tabs is not None
    n_q = GQA_HEADS * GQA_DH
    n_kv = GQA_KV_HEADS * GQA_DH
    off_q, off_t = row0 // (nb * tq), row0 // (nb * T)
    assert row0 % (nb * T) == 0
    in_specs = [pl.BlockSpec((nb * tq, n_q), lambda b, j: (off_q + b * nq + j, 0)),
                pl.BlockSpec((nb * T, n_kv), lambda b, j: (off_t + b, 0)),
                pl.BlockSpec((nb * T, n_kv), lambda b, j: (off_t + b, 0))]
    args = [q, k, v]
    if S_ctx:
        layer = ctx[2]
        in_specs += [pl.BlockSpec((nb, None, S_ctx, n_kv), lambda b, j: (b, layer, 0, 0))] * 2
        args += list(ctx[:2])
    if rope:
        in_specs += [pl.BlockSpec((tq, LANES), lambda b, j: (j, 0))] * 3
        in_specs += [pl.BlockSpec((T, LANES), lambda b, j: (0, 0))] * 3
        args += list(rope_tabs) * 2
    in_specs += [pl.BlockSpec(p.shape, lambda b, j: (0, 0)) for p in params]
    args += list(params)
    out_specs = [pl.BlockSpec((nb * tq, n_q), lambda b, j: (b * nq + j, 0))]
    out_shape = [jax.ShapeDtypeStruct((B * T, n_q), BF16)]
    aliased, n_zero = [], 0
    if want_kv:
        slot, n_slots, prev_k, prev_v = kv_out
        for prev in (prev_k, prev_v):
            spec, shape, al, n_zero = _slot_output((slot, n_slots, prev), B, (T, n_kv), F32, nb)
            out_specs.append(spec)
            out_shape.append(shape)
            aliased += al
    res = pl.pallas_call(
        functools.partial(_gqa_body, T=T, S_ctx=S_ctx, rope=rope, want_kv=want_kv,
                          group=_head_group(S, GQA_HEADS), n_alias=len(aliased), n_zero_slots=n_zero, nb=nb),
        grid=(B // nb, nq),
        in_specs=[pl.BlockSpec(memory_space=pl.ANY)] * len(aliased) + in_specs,
        out_specs=out_specs,
        out_shape=out_shape,
        input_output_aliases={i: 1 + i for i in range(len(aliased))},
        scratch_shapes=[pltpu.VMEM((GQA_KV_HEADS, nb * S, LANES), BF16),
                        pltpu.VMEM((GQA_KV_HEADS, nb * S, LANES), BF16),
                        pltpu.VMEM((GQA_KV_HEADS, V_ROWS, nb * S), BF16)],
        compiler_params=_cparams(("arbitrary", "arbitrary"), VMEM_LIMIT),
        name="gqa",
    )(*aliased, *args)
    return res if want_kv else (res[0], None, None)


def _rope_tables(n_tok, d_rot, lead, reps):
    t = jnp.arange(n_tok, dtype=jnp.int32)
    posn = jnp.stack([t // GRID_W, t % GRID_W], axis=-1).astype(F32)
    quarter = d_rot // 4
    inv = jnp.power(ROPE_THETA, -jnp.arange(quarter, dtype=F32) / quarter)
    ang = posn[:, :, None] * inv
    cos, sin = jnp.cos(ang), jnp.sin(ang)
    zero = jnp.zeros_like(sin)
    c_rot = jnp.stack([cos, cos], axis=2).reshape(n_tok, d_rot)
    sp_rot = jnp.stack([-sin, zero], axis=2).reshape(n_tok, d_rot)
    sm_rot = jnp.stack([zero, sin], axis=2).reshape(n_tok, d_rot)
    tail = LANES - lead - reps * d_rot

    def embed(rot, fill):
        parts = [jnp.full((n_tok, lead), fill, F32)] + [rot] * reps + [jnp.full((n_tok, tail), fill, F32)]
        return jnp.concatenate(parts, axis=1)

    return embed(c_rot, 1.0), embed(sp_rot, 0.0), embed(sm_rot, 0.0)


def _pad_lanes(x, lead, width=LANES):
    pad = [(0, 0)] * (x.ndim - 1) + [(lead, width - lead - x.shape[-1])]
    return jnp.pad(x, pad)


def _ab_params(w_in, w_out, a_w2, a_b, out_g, q_norm_g, w_qb, kv_norm_g, w_kvb, qn_g, kn_g):
    d = w_in.shape[0]
    o_alo = 2 * GLA_QK + 2 * GLA_V
    o_cq = o_alo + 2 * GLA_RANK
    o_ckv = o_cq + MLA_Q_RANK
    o_kpe = o_ckv + MLA_KV_RANK
    w_kpe = w_in[:, o_kpe:]
    w_perm = jnp.concatenate([
        w_in[:, :o_alo], w_in[:, o_cq:o_ckv], _pad_lanes(w_kpe, MLA_NOPE),
        w_in[:, o_ckv:o_kpe],
        _pad_lanes(jnp.concatenate([w_kpe, w_in[:, o_alo:o_cq]], axis=1), 0),
    ], axis=1).astype(BF16)
    w2bd = jnp.zeros((LANES, 2 * GLA_QK), F32)
    w2bd = (w2bd.at[MLA_ROPE:MLA_ROPE + GLA_RANK, :GLA_QK].set(a_w2[0])
            .at[MLA_ROPE + GLA_RANK:MLA_ROPE + 2 * GLA_RANK, GLA_QK:].set(a_w2[1]))
    w_qb_p = _pad_lanes(w_qb.reshape(MLA_Q_RANK, MLA_HEADS, MLA_QK), 0).reshape(MLA_Q_RANK, MLA_HEADS * LANES)
    kvb = w_kvb.reshape(MLA_KV_RANK, MLA_HEADS, MLA_NOPE + MLA_V)
    w_kb_p = _pad_lanes(kvb[:, :, :MLA_NOPE], 0).reshape(MLA_KV_RANK, MLA_HEADS * LANES)
    w_v = kvb[:, :, MLA_NOPE:].reshape(MLA_KV_RANK, MLA_HEADS * MLA_V)
    return dict(
        w_perm=w_perm,
        w2bd=w2bd.astype(BF16),
        a_b=a_b.reshape(1, 2 * GLA_QK),
        out_g=jnp.tile(out_g, GLA_HEADS).reshape(1, GLA_V),
        mla=(q_norm_g.reshape(1, -1), w_qb_p.astype(BF16),
             _pad_lanes(qn_g * (MLA_QK ** -0.5 * LOG2E), 0).reshape(1, LANES),
             kv_norm_g.reshape(1, -1), w_kb_p.astype(BF16), w_v.astype(BF16),
             _pad_lanes(kn_g, 0).reshape(1, LANES)),
        w_out_gla=w_out[:GLA_V].astype(BF16),
        w_out_mla=w_out[GLA_V:].astype(BF16),
    )


AB_OUTS = ((2 * GLA_QK + 2 * GLA_V + MLA_Q_RANK + LANES, BF16), (MLA_KV_RANK, F32), (LANES, F32))
C_OUTS = ((GQA_HEADS * GQA_DH, BF16), (GQA_KV_HEADS * GQA_DH, F32), (GQA_KV_HEADS * GQA_DH, F32))


def kernel(x_prompt, x_sample, c, cache_mla_ckv, cache_mla_kpe, state_gla, cache_gqa_k, cache_gqa_v,
           c_ctx, ada_w, ada_b, norm_mix_g, norm_ffn_g, ffn_w_in, ffn_w_out, ab_w_in, ab_w_out,
           gla_a_w2, gla_a_b, gla_out_g, mla_q_norm_g, mla_w_qb, mla_kv_norm_g, mla_w_kvb, mla_qn_g,
           mla_kn_g, gqa_w_in, gqa_w_out, gqa_qn_g, gqa_kn_g):
    Bp, Tp, D = x_prompt.shape
    Bs, Ts, _ = x_sample.shape
    depth = ada_w.shape[0]
    xp = x_prompt.reshape(Bp * Tp, D)
    xs = x_sample.reshape(Bs * Ts, D)

    cond = jnp.zeros((8, D), F32).at[0].set(c_ctx).at[2:2 + Bs].set(c)
    mods = _adaln(cond, ada_w, ada_b).reshape(depth, 8, 6, D)

    rope_mla = _rope_tables(Ts, MLA_ROPE, MLA_NOPE, 1)
    rope_gqa = _rope_tables(Ts, GQA_DH, 0, 2)

    streams = (dict(mod_base=0, rows_per_cond=None), dict(mod_base=2, rows_per_cond=Ts))
    proj_tm = 512
    ffn_tm = 2048
    prompt_nb = 4
    n_p = Bp * Tp

    n_ab, n_c = (depth + 1) // 2, depth // 2
    n_kv = GQA_KV_HEADS * GQA_DH
    kpe_cache = _pad_lanes(cache_mla_kpe, MLA_NOPE)
    gqa_k_cache = cache_gqa_k.reshape(Bs, n_c, -1, n_kv)
    gqa_v_cache = cache_gqa_v.reshape(Bs, n_c, -1, n_kv)
    new_ckv = new_kpe = new_gla = new_k = new_v = None
    for l in range(depth):
        i = l // 2
        mod = mods[l]
        proj = functools.partial(_inproj, xp, xs, norm_mix_g[l], mod, sample_mod_base=2, rows_per_cond=Ts,
                                 tm=proj_tm)
        if l % 2 == 0:
            P = _ab_params(ab_w_in[i], ab_w_out[i], gla_a_w2[i], gla_a_b[i], gla_out_g[i], mla_q_norm_g[i],
                           mla_w_qb[i], mla_kv_norm_g[i], mla_w_kvb[i], mla_qn_g[i], mla_kn_g[i])
            pj, ckv, alk = proj(P['w_perm'], AB_OUTS)
            gla = functools.partial(_gla, pj, alk, P['w2bd'], P['a_b'], P['out_g'])
            og_p, new_gla, new_kpe = gla(None, B=Bp, T=Tp, state_out=(i, n_ab, new_gla, new_kpe), nb=prompt_nb)
            og_s, _ = gla((state_gla, i), B=Bs, T=Ts, row0=n_p)
            om_p, new_ckv = _mla(pj, ckv, None, None, P['mla'], B=Bp, T=Tp, tq=Tp,
                                 ckv_out=(i, n_ab, new_ckv), nb=prompt_nb)
            om_s, _ = _mla(pj, ckv, (cache_mla_ckv, kpe_cache, i), rope_mla, P['mla'], B=Bs, T=Ts, tq=256,
                           row0=n_p)
            wps = (P['w_out_gla'], P['w_out_mla'])
            acts_p, acts_s = (og_p, om_p), (og_s, om_s)
        else:
            w_in = gqa_w_in[i].astype(BF16)
            w_out = gqa_w_out[i].astype(BF16)
            gp = (jnp.tile(gqa_qn_g[i] * (GQA_DH ** -0.5 * LOG2E), 2).reshape(1, LANES),
                  jnp.tile(gqa_kn_g[i], 2).reshape(1, LANES))
            q, k, v = proj(w_in, C_OUTS)
            o_p, new_k, new_v = _gqa(q, k, v, None, None, gp, B=Bp, T=Tp, tq=Tp,
                                     kv_out=(i, n_c, new_k, new_v), nb=prompt_nb)
            o_s, _, _ = _gqa(q, k, v, (gqa_k_cache, gqa_v_cache, i), rope_gqa, gp, B=Bs, T=Ts, tq=256, row0=n_p)
            wps = (w_out,)
            acts_p, acts_s = (o_p,), (o_s,)
        xp = _mix_ffn(acts_p, wps, xp, norm_ffn_g[l], mod, ffn_w_in, ffn_w_out, l, **streams[0], tm=ffn_tm)
        xs = _mix_ffn(acts_s, wps, xs, norm_ffn_g[l], mod, ffn_w_in, ffn_w_out, l, **streams[1], tm=ffn_tm)
    kv_shape = (Bp, n_c, Tp, GQA_KV_HEADS, GQA_DH)
    return (xp.reshape(Bp, Tp, D), xs.reshape(Bs, Ts, D), new_ckv, new_kpe, new_gla,
            new_k.reshape(kv_shape), new_v.reshape(kv_shape))
```

```python
import functools

import jax
import jax.numpy as jnp
from jax import lax
from jax.experimental import pallas as pl
from jax.experimental.pallas import tpu as pltpu

F32 = jnp.float32
BF16 = jnp.bfloat16

EPS = 1e-6
ROPE_THETA = 10000.0
GRID_W = 64
LANES = 128
GLA_HEADS, GLA_DK, GLA_DV = 4, 64, 128
GLA_QK = GLA_HEADS * GLA_DK
GLA_V = GLA_HEADS * GLA_DV
GLA_RANK = 16
GLA_TAU = 16.0
GLA_CHUNK = 64
MLA_HEADS = 8
MLA_Q_RANK, MLA_KV_RANK = 384, 256
MLA_NOPE, MLA_ROPE, MLA_V = 64, 32, 64
MLA_QK = MLA_NOPE + MLA_ROPE
GQA_HEADS, GQA_KV_HEADS, GQA_DH = 16, 4, 64
SUBLANES = 8
VMEM_LIMIT = 56 << 20

ADALN_TN = 1536
PROJ_TM = 512
FFN_TM = 2048
FFN_TK = 256
FFN_ROW_CHUNK = 512
LATENT_TQ = 256
PROMPT_NB = 4
GLA_GROUP = 4
GLA_EPILOGUE_ROWS = 256


def _cparams(sem, vmem=None):
    return pltpu.CompilerParams(dimension_semantics=sem, vmem_limit_bytes=vmem)


def _nt_dot(a, b):
    return lax.dot_general(a, b, (((1,), (1,)), ((), ())), preferred_element_type=F32)


def _tn_dot(a, b):
    return lax.dot_general(a, b, (((0,), (0,)), ((), ())), preferred_element_type=F32)


def _ds(start, size):
    if isinstance(start, int):
        return pl.ds(start, size)
    return pl.ds(pl.multiple_of(start, size), size)


def _rms(x, g):
    ms = jnp.mean(x * x, axis=-1, keepdims=True)
    return x * lax.rsqrt(ms + EPS) * g


def _silu(x):
    return x * jax.nn.sigmoid(x)


def _rope(x, c, sp, sm, shift):
    return x * c + pltpu.roll(x, LANES - shift, 1) * sp + pltpu.roll(x, shift, 1) * sm


def _adaln_body(c_ref, w_ref, b_ref, o_ref):
    c = c_ref[...]
    s = _silu(c).astype(BF16)
    o_ref[0] = jnp.dot(s, w_ref[0].astype(BF16), preferred_element_type=F32) + b_ref[0]


def _adaln(cond, ada_w, ada_b):
    L, D, E = ada_w.shape
    tn = ADALN_TN
    return pl.pallas_call(
        _adaln_body,
        grid=(L, E // tn),
        in_specs=[pl.BlockSpec((SUBLANES, D), lambda l, n: (0, 0)),
                  pl.BlockSpec((1, D, tn), lambda l, n: (l, 0, n)),
                  pl.BlockSpec((1, 1, tn), lambda l, n: (l, 0, n))],
        out_specs=pl.BlockSpec((1, SUBLANES, tn), lambda l, n: (l, 0, n)),
        out_shape=jax.ShapeDtypeStruct((L, SUBLANES, E), F32),
        compiler_params=_cparams(("arbitrary", "arbitrary")),
        name="adaln",
    )(cond, ada_w, ada_b.reshape(L, 1, E))


def _slot_output(slot_out, batch, tail, dtype, nb=1):
    slot, n_slots, prev = slot_out
    zeros = (0,) * len(tail)
    shape = jax.ShapeDtypeStruct((batch, n_slots) + tail, dtype)
    if slot == 0:
        return pl.BlockSpec((nb, n_slots) + tail, lambda b, *_: (b, 0) + zeros), shape, [], n_slots - 1
    return pl.BlockSpec((nb, 1) + tail, lambda b, *_: (b, slot) + zeros), shape, [prev], 0


def _mod_index(mod_base, rows_per_cond, tm):
    if rows_per_cond is None:
        return lambda i: (mod_base, 0, 0)
    n_mod = max(tm // rows_per_cond, 1)
    return lambda i: ((mod_base + (i * tm) // rows_per_cond) // n_mod, 0, 0)


def _inproj_body(xp_ref, xs_ref, g_ref, mod_ref, w_ref, *o_refs, widths, n_prompt_tiles):
    def project(x_ref):
        h = _rms(x_ref[...], g_ref[...])
        h = h * (1.0 + mod_ref[0, 1:2, :]) + mod_ref[0, 0:1, :]
        acc = jnp.dot(h.astype(BF16), w_ref[...], preferred_element_type=F32)
        off = 0
        for o_ref, w in zip(o_refs, widths):
            o_ref[...] = acc[:, off:off + w].astype(o_ref.dtype)
            off += w

    is_prompt = pl.program_id(0) < n_prompt_tiles
    pl.when(is_prompt)(lambda: project(xp_ref))
    pl.when(jnp.logical_not(is_prompt))(lambda: project(xs_ref))


def _inproj(xp, xs, g, mod, w, outs, *, sample_mod_base, rows_per_cond, tm):
    n_p, d = xp.shape
    n = n_p + xs.shape[0]
    np_t = n_p // tm
    widths = tuple(o[0] for o in outs)

    def mod_idx(i):
        return (jnp.where(i < np_t, 0, sample_mod_base + ((i - np_t) * tm) // rows_per_cond), 0, 0)

    return pl.pallas_call(
        functools.partial(_inproj_body, widths=widths, n_prompt_tiles=np_t),
        grid=(n // tm,),
        in_specs=[pl.BlockSpec((tm, d), lambda i: (jnp.minimum(i, np_t - 1), 0)),
                  pl.BlockSpec((tm, d), lambda i: (jnp.maximum(i - np_t, 0), 0)),
                  pl.BlockSpec((1, d), lambda i: (0, 0)),
                  pl.BlockSpec((1, 6, d), mod_idx),
                  pl.BlockSpec(w.shape, lambda i: (0, 0))],
        out_specs=[pl.BlockSpec((tm, wd), lambda i: (i, 0)) for wd in widths],
        out_shape=[jax.ShapeDtypeStruct((n, wd), dt) for wd, dt in outs],
        compiler_params=_cparams(("arbitrary",), VMEM_LIMIT),
        name="inproj",
    )(xp, xs, g.reshape(1, d), mod, w)


def _ffn_body(*refs, n_in, nk, row_chunk, rows_per_cond):
    a_refs, wp_refs = refs[:n_in], refs[n_in:2 * n_in]
    x_ref, g_ref, mod_ref, wg_ref, wu_ref, wo_ref, o_ref, h_ref, wgu_ref, wob_ref = refs[2 * n_in:]
    k = pl.program_id(1)
    tm = x_ref.shape[0]
    tk = wg_ref.shape[2]
    n_chunks = tm // row_chunk

    def mod_row(c):
        return 0 if rows_per_cond is None else (c * row_chunk) // rows_per_cond

    def mix_proj(c):
        rows = _ds(c * row_chunk, row_chunk)
        acc = None
        for a_ref, wp_ref in zip(a_refs, wp_refs):
            part = jnp.dot(a_ref[rows, :], wp_ref[...], preferred_element_type=F32)
            acc = part if acc is None else acc + part
        return acc

    def mix_rows(c, acc):
        rows, m = _ds(c * row_chunk, row_chunk), mod_row(c)
        x_new = x_ref[rows, :] + mod_ref[m, 2:3, :] * acc
        o_ref[rows, :] = x_new
        h = _rms(x_new, g_ref[...])
        h = h * (1.0 + mod_ref[m, 4:5, :]) + mod_ref[m, 3:4, :]
        h_ref[rows, :] = h.astype(BF16)

    def ffn_rows(c):
        rows, m = _ds(c * row_chunk, row_chunk), mod_row(c)
        gu = jnp.dot(h_ref[rows, :], wgu_ref[...], preferred_element_type=F32)
        a = (_silu(gu[:, :tk]) * gu[:, tk:]).astype(BF16)
        o_ref[rows, :] += mod_ref[m, 5:6, :] * jnp.dot(a, wob_ref[...], preferred_element_type=F32)

    def cast_weights():
        wgu_ref[:, :tk] = wg_ref[0].astype(BF16)
        wgu_ref[:, tk:] = wu_ref[0].astype(BF16)
        wob_ref[...] = wo_ref[0].astype(BF16)

    @pl.when(k == 0)
    def _():
        cast_weights()
        accs = [mix_proj(c) for c in range(n_chunks)]
        for c in range(n_chunks):
            mix_rows(c, accs[c])
            ffn_rows(c)

    @pl.when(k > 0)
    def _():
        cast_weights()
        for c in range(n_chunks):
            ffn_rows(c)


def _mix_ffn(acts, wps, x, g, mod, w_in, w_out, layer, *, mod_base, rows_per_cond, tm):
    n, d = x.shape
    hidden = w_out.shape[1]
    tk = FFN_TK
    nk = hidden // tk
    n_in = len(acts)
    n_mod = 1 if rows_per_cond is None else max(tm // rows_per_cond, 1)
    once = dict(pipeline_mode=pl.Buffered(1))
    return pl.pallas_call(
        functools.partial(_ffn_body, n_in=n_in, nk=nk, row_chunk=FFN_ROW_CHUNK,
                          rows_per_cond=rows_per_cond),
        grid=(n // tm, nk),
        in_specs=([pl.BlockSpec((tm, a.shape[1]), lambda i, k: (i, 0), **once) for a in acts]
                  + [pl.BlockSpec(w.shape, lambda i, k: (0, 0), **once) for w in wps]
                  + [pl.BlockSpec((tm, d), lambda i, k: (i, 0)),
                     pl.BlockSpec((1, d), lambda i, k: (0, 0)),
                     pl.BlockSpec((n_mod, 6, d),
                                  (lambda f: (lambda i, k: f(i)))(_mod_index(mod_base, rows_per_cond, tm))),
                     pl.BlockSpec((1, d, tk), lambda i, k: (layer, 0, k)),
                     pl.BlockSpec((1, d, tk), lambda i, k: (layer, 0, nk + k)),
                     pl.BlockSpec((1, tk, d), lambda i, k: (layer, k, 0))]),
        out_specs=pl.BlockSpec((tm, d), lambda i, k: (i, 0)),
        out_shape=jax.ShapeDtypeStruct((n, d), F32),
        scratch_shapes=[pltpu.VMEM((tm, d), BF16),
                        pltpu.VMEM((d, 2 * tk), BF16),
                        pltpu.VMEM((tk, d), BF16)],
        compiler_params=_cparams(("arbitrary", "arbitrary"), VMEM_LIMIT),
        name="mix_ffn",
    )(*acts, *wps, x, g.reshape(1, d), mod, w_in, w_in, w_out)


def _log_sigmoid(x):
    return jnp.minimum(x, 0.0) - jnp.log(1.0 + jnp.exp(-jnp.abs(x)))


def _gla_body(*refs, T, has_s0, want_state, n_alias, n_zero_slots, nb):
    refs = refs[n_alias:]
    qkvr_ref, alk_ref, w2_ref, ab_ref, og_ref = refs[:5]
    pos = 5
    s0_ref = None
    if has_s0:
        s0_ref = refs[pos]
        pos += 1
    o_ref = refs[pos]
    pos += 1
    st_out_ref = kpe_out_ref = None
    if want_state:
        st_out_ref, kpe_out_ref = refs[pos:pos + 2]
        pos += 2
    osc_ref, st_ref, la_ref, qin_ref, kst_ref, dec_ref, upd_ref, stq_ref = refs[pos:pos + 8]

    C = GLA_CHUNK
    nc = T // C

    logit = jnp.dot(alk_ref[...].astype(BF16), w2_ref[...], preferred_element_type=F32) + ab_ref[...]
    la_ref[...] = _log_sigmoid(logit) * (1.0 / GLA_TAU)

    for s in range(nb):
        for d in range(2):
            if has_s0:
                st_ref[2 * s + d] = jnp.concatenate([s0_ref[s, d, h] for h in range(GLA_HEADS)], axis=0).T
            else:
                st_ref[2 * s + d] = jnp.zeros((GLA_DV, GLA_QK), F32)

    r64 = lax.broadcasted_iota(jnp.int32, (C, C), 0)
    c64 = lax.broadcasted_iota(jnp.int32, (C, C), 1)
    tri_f = (r64 >= c64).astype(BF16)
    tri_b = (c64 >= r64).astype(BF16)
    t_idx = lax.broadcasted_iota(jnp.int32, (C, GLA_QK), 0)
    s_idx = lax.broadcasted_iota(jnp.int32, (C, GLA_QK), 1) % C
    causal_f = t_idx >= s_idx
    causal_b = t_idx <= s_idx
    bm_k = (lax.broadcasted_iota(jnp.int32, (GLA_QK, GLA_QK), 0) // GLA_DK
            == lax.broadcasted_iota(jnp.int32, (GLA_QK, GLA_QK), 1) // GLA_DK)
    bm_v = (lax.broadcasted_iota(jnp.int32, (GLA_QK, GLA_V), 0) // C
            == lax.broadcasted_iota(jnp.int32, (GLA_QK, GLA_V), 1) // GLA_DV)
    head_lanes = [lax.broadcasted_iota(jnp.int32, (1, GLA_QK), 1) // GLA_DK == h for h in range(GLA_HEADS)]
    directions = ((tri_f, causal_f, C // 2 - 1, C - 1), (tri_b, causal_b, C // 2, 0))

    def v_rows(rows):
        return qkvr_ref[rows, 2 * GLA_QK:2 * GLA_QK + GLA_V]

    G = GLA_GROUP
    units = [(i, d) for i in range(G) for d in range(2)]

    n_groups = nb * nc // G

    def for_groups(fn):
        if n_groups <= 4:
            for g in range(n_groups):
                fn(g * G)
        else:
            def body(g, carry):
                fn(g * G)
                return carry
            lax.fori_loop(0, n_groups, body, 0)

    def intra_group(n0):
        rows = [_ds((n0 + i) * C, C) for i in range(G)]
        b = {}
        for i, d in units:
            b[i, d] = _split_dot(la_ref[rows[i], d * GLA_QK:(d + 1) * GLA_QK], directions[d][0], w_left=True)
        q_loc, k_bd = {}, {}
        for i in range(G):
            qc = qkvr_ref[rows[i], 0:GLA_QK].astype(F32) * (GLA_DK ** -0.5)
            kc = qkvr_ref[rows[i], GLA_QK:2 * GLA_QK].astype(F32)
            for d in range(2):
                _, _, ref_row, last_row = directions[d]
                bb = b[i, d]
                b_ref = bb[ref_row:ref_row + 1, :]
                b_last = bb[last_row:last_row + 1, :]
                q_loc[i, d] = (qc * jnp.exp(bb - b_ref)).astype(BF16)
                k_loc = kc * jnp.exp(b_ref - bb)
                k_bd[i, d] = jnp.where(bm_k, jnp.concatenate([k_loc] * GLA_HEADS, axis=0), 0.0).astype(BF16)
                qin_ref[d, rows[i], :] = (qc * jnp.exp(bb)).astype(BF16)
                kst_ref[d, rows[i], :] = (kc * jnp.exp(b_last - bb)).astype(BF16)
                dec_ref[d, _ds((n0 + i) * SUBLANES, SUBLANES), :] = jnp.broadcast_to(jnp.exp(b_last),
                                                                                      (SUBLANES, GLA_QK))
        a = {u: jnp.where(directions[u[1]][1], _nt_dot(q_loc[u], k_bd[u]), 0.0).astype(BF16) for u in units}
        v_bd = [jnp.where(bm_v, jnp.concatenate([v_rows(rows[i])] * GLA_HEADS, axis=0), jnp.zeros((), BF16))
                for i in range(G)]
        o = {u: jnp.dot(a[u], v_bd[u[0]], preferred_element_type=F32) for u in units}
        for i in range(G):
            osc_ref[rows[i], :] = o[i, 0] + o[i, 1]

    def update_group(n0):
        rows = [_ds((n0 + i) * C, C) for i in range(G)]
        upd = {u: _tn_dot(v_rows(rows[u[0]]), kst_ref[u[1], rows[u[0]], :]) for u in units}
        for i, d in units:
            acc = None
            for h, m in enumerate(head_lanes):
                part = jnp.where(m, upd[i, d][h * GLA_DV:(h + 1) * GLA_DV, :], 0.0)
                acc = part if acc is None else acc + part
            upd_ref[d, n0 + i] = acc

    def scan_step(i, carry):
        for s in range(nb):
            for d, n in ((0, s * nc + i), (1, s * nc + nc - 1 - i)):
                st = st_ref[2 * s + d]
                stq_ref[d, n] = st.astype(BF16)
                st_ref[2 * s + d] = st * dec_ref[d, pl.ds(n * SUBLANES, 1), :] + upd_ref[d, n]
        return carry

    def readout_group(n0):
        rows = [_ds((n0 + i) * C, C) for i in range(G)]
        o = {}
        for i, d in units:
            q_in = qin_ref[d, rows[i], :]
            q_heads = jnp.concatenate([jnp.where(m, q_in, jnp.zeros((), BF16)) for m in head_lanes], axis=0)
            o[i, d] = _nt_dot(q_heads, stq_ref[d, n0 + i])
        for i in range(G):
            both = o[i, 0] + o[i, 1]
            for h in range(GLA_HEADS):
                osc_ref[rows[i], h * GLA_DV:(h + 1) * GLA_DV] += both[h * C:(h + 1) * C, :]

    for_groups(intra_group)
    for_groups(update_group)
    if nc == G:
        for i in range(nc):
            scan_step(i, 0)
    else:
        lax.fori_loop(0, nc, scan_step, 0)
    for_groups(readout_group)

    rb = GLA_EPILOGUE_ROWS
    for i in range(nb * T // rb):
        rows = slice(i * rb, (i + 1) * rb)
        for h in range(GLA_HEADS):
            cols = slice(h * GLA_DV, (h + 1) * GLA_DV)
            o = _rms(osc_ref[rows, cols], og_ref[:, cols])
            r = qkvr_ref[rows, 2 * GLA_QK + GLA_V + h * GLA_DV:2 * GLA_QK + GLA_V + (h + 1) * GLA_DV].astype(F32)
            o_ref[rows, cols] = (o * _silu(r)).astype(o_ref.dtype)

    if want_state:
        for s in range(nb):
            for d in range(2):
                s_all = st_ref[2 * s + d].T
                for h in range(GLA_HEADS):
                    st_out_ref[s, 0, d, h] = s_all[h * GLA_DK:(h + 1) * GLA_DK, :]
            kpe_out_ref[s, 0] = alk_ref[s * T:(s + 1) * T, 0:MLA_ROPE]
            if n_zero_slots:
                st_out_ref[s, 1:] = jnp.zeros((n_zero_slots, 2, GLA_HEADS, GLA_DK, GLA_DV), F32)
                kpe_out_ref[s, 1:] = jnp.zeros((n_zero_slots, T, MLA_ROPE), F32)


def _gla(proj, alk, w2bd, a_b, out_g, s0, *, B, T, row0=0, state_out=None, nb=1):
    has_s0 = s0 is not None
    want_state = state_out is not None
    n_qkvr = 2 * GLA_QK + 2 * GLA_V
    R = nb * T
    n_chunks = R // GLA_CHUNK
    off = row0 // R
    assert row0 % R == 0
    in_specs = [pl.BlockSpec((R, n_qkvr), lambda b: (off + b, 0)),
                pl.BlockSpec((R, LANES), lambda b: (off + b, 0)),
                pl.BlockSpec(w2bd.shape, lambda b: (0, 0)),
                pl.BlockSpec((1, 2 * GLA_QK), lambda b: (0, 0)),
                pl.BlockSpec((1, GLA_V), lambda b: (0, 0))]
    args = [proj, alk, w2bd, a_b, out_g]
    if has_s0:
        states, layer = s0
        in_specs.append(pl.BlockSpec((nb, None, 2, GLA_HEADS, GLA_DK, GLA_DV),
                                     lambda b: (b, layer, 0, 0, 0, 0)))
        args.append(states)
    out_specs = [pl.BlockSpec((R, GLA_V), lambda b: (b, 0))]
    out_shape = [jax.ShapeDtypeStruct((B * T, GLA_V), BF16)]
    aliased, n_zero = [], 0
    if want_state:
        slot, n_slots, prev_state, prev_kpe = state_out
        for prev, tail in ((prev_state, (2, GLA_HEADS, GLA_DK, GLA_DV)), (prev_kpe, (T, MLA_ROPE))):
            spec, shape, al, n_zero = _slot_output((slot, n_slots, prev), B, tail, F32, nb)
            out_specs.append(spec)
            out_shape.append(shape)
            aliased += al
    res = pl.pallas_call(
        functools.partial(_gla_body, T=T, has_s0=has_s0, want_state=want_state, n_alias=len(aliased),
                          n_zero_slots=n_zero, nb=nb),
        grid=(B // nb,),
        in_specs=[pl.BlockSpec(memory_space=pl.ANY)] * len(aliased) + in_specs,
        out_specs=out_specs,
        out_shape=out_shape,
        input_output_aliases={i: 1 + i for i in range(len(aliased))},
        scratch_shapes=[pltpu.VMEM((R, GLA_V), F32),
                        pltpu.VMEM((2 * nb, GLA_DV, GLA_QK), F32),
                        pltpu.VMEM((R, 2 * GLA_QK), F32),
                        pltpu.VMEM((2, R, GLA_QK), BF16),
                        pltpu.VMEM((2, R, GLA_QK), BF16),
                        pltpu.VMEM((2, SUBLANES * n_chunks, GLA_QK), F32),
                        pltpu.VMEM((2, n_chunks, GLA_DV, GLA_QK), F32),
                        pltpu.VMEM((2, n_chunks, GLA_DV, GLA_QK), BF16)],
        compiler_params=_cparams(("arbitrary",), VMEM_LIMIT),
        name="gla",
    )(*aliased, *args)
    return res if want_state else (res[0], None)


LOG2E = 1.4426950408889634


V_ROWS = 80


def _attend(qs, ks, vs, group, key_major_dv=None):
    key_major = key_major_dv is not None
    axis = 0 if key_major else -1

    def qk(g0):
        return [_nt_dot(k(), q) if key_major else _nt_dot(q, k())
                for q, k in zip(qs[g0:g0 + group], ks[g0:g0 + group])]

    outs = []
    scores = qk(0)
    for g0 in range(0, len(qs), group):
        probs = [jnp.exp2(s - jnp.max(s, axis=axis, keepdims=True)).astype(BF16) for s in scores]
        if g0 + group < len(qs):
            scores = qk(g0 + group)
        if key_major:
            dv = key_major_dv
            res = [jnp.dot(v(), p, preferred_element_type=F32) for p, v in zip(probs, vs[g0:g0 + group])]
            outs += [r[:dv, :] / r[dv:dv + 1, :] for r in res]
        else:
            res = [jnp.dot(p, v(), preferred_element_type=F32) for p, v in zip(probs, vs[g0:g0 + group])]
            outs += [r[:, :LANES] / r[:, LANES:] for r in res]
    return outs


def _head_group(n_keys, n_heads):
    return 4 if n_keys <= 512 else 2


def _stacked_ms(xs, ones_mat, inv_n):
    n = xs[0].shape[0]
    sq = jnp.concatenate([x * x for x in xs], axis=0).astype(BF16)
    ms = jnp.dot(sq, ones_mat, preferred_element_type=F32) * inv_n
    return [ms[i * n:(i + 1) * n] for i in range(len(xs))]


def _split_dot(x, w, w_left=False):
    hi = x.astype(BF16)
    lo = (x - hi.astype(F32)).astype(BF16)
    if w_left:
        return jnp.dot(w, hi, preferred_element_type=F32) + jnp.dot(w, lo, preferred_element_type=F32)
    return jnp.dot(hi, w, preferred_element_type=F32) + jnp.dot(lo, w, preferred_element_type=F32)


def _group_ones(group):
    r = lax.broadcasted_iota(jnp.int32, (LANES, LANES), 0) // group
    c = lax.broadcasted_iota(jnp.int32, (LANES, LANES), 1) // group
    return (r == c).astype(BF16)


def _mla_body(*refs, T, S_ctx, rope, want_ckv, group, n_alias, n_zero_slots, nb):
    refs = refs[n_alias:]
    cq_ref, kpe_ref, ckv_ref = refs[:3]
    pos = 3
    if S_ctx:
        ckvc_ref, kpec_ref = refs[pos:pos + 2]
        pos += 2
    if rope:
        rq_refs = refs[pos:pos + 3]
        rk_refs = refs[pos + 3:pos + 6]
        pos += 6
    qng_ref, wqb_ref, qg_ref, kvg_ref, wkvb_ref, kg_ref = refs[pos:pos + 6]
    pos += 6
    o_ref = refs[pos]
    pos += 1
    if want_ckv:
        ckvn_ref = refs[pos]
        pos += 1
    k_sc, vlo_sc, vhi_sc = refs[pos:pos + 3]
    lo = lax.broadcasted_iota(jnp.int32, (1, LANES), 1) < MLA_V

    H = MLA_HEADS
    n_k = H * LANES
    def head_ms(xs):
        return [jnp.sum(x * x, axis=-1, keepdims=True) * (1.0 / MLA_QK) for x in xs]

    def fill_kv(ckvn, kpe, row0, n_rows, with_rope):
        rows = slice(row0, row0 + n_rows)
        kv = jnp.dot(ckvn.astype(BF16), wkvb_ref[...], preferred_element_type=F32)
        khs = [kv[:, h * LANES:(h + 1) * LANES] + kpe for h in range(H)]
        for h, (kh, ms) in enumerate(zip(khs, head_ms(khs))):
            kh = kh * lax.rsqrt(ms + EPS) * kg_ref[...]
            if with_rope:
                kh = _rope(kh, rk_refs[0][...], rk_refs[1][...], rk_refs[2][...], MLA_ROPE // 4)
            k_sc[rows, h * LANES:(h + 1) * LANES] = kh.astype(BF16)
        ones = jnp.ones((n_rows, LANES), BF16)
        for p in range(H // 2):
            v = kv[:, n_k + p * LANES:n_k + (p + 1) * LANES]
            vlo_sc[rows, 2 * p * LANES:(2 * p + 1) * LANES] = jnp.where(lo, v, 0.0).astype(BF16)
            vhi_sc[rows, 2 * p * LANES:(2 * p + 1) * LANES] = jnp.where(lo, 0.0, v).astype(BF16)
            vlo_sc[rows, (2 * p + 1) * LANES:(2 * p + 2) * LANES] = ones
            vhi_sc[rows, (2 * p + 1) * LANES:(2 * p + 2) * LANES] = ones

    S = S_ctx + T
    tq = cq_ref.shape[0] // nb

    @pl.when(pl.program_id(1) == 0)
    def _():
        for s in range(nb):
            if S_ctx:
                fill_kv(ckvc_ref[s], kpec_ref[s], s * S, S_ctx, False)
            ckvn = _rms(ckv_ref[s * T:(s + 1) * T, :], kvg_ref[...])
            if want_ckv:
                ckvn_ref[s, 0] = ckvn
                if n_zero_slots:
                    ckvn_ref[s, 1:] = jnp.zeros((n_zero_slots, T, MLA_KV_RANK), F32)
            fill_kv(ckvn, kpe_ref[s * T:(s + 1) * T, :].astype(F32), s * S + S_ctx, T, rope)

    cqn = _rms(cq_ref[...].astype(F32), qng_ref[...])
    q = jnp.dot(cqn.astype(BF16), wqb_ref[...], preferred_element_type=F32)
    units = [(s, h) for s in range(nb) for h in range(H)]
    qhs = [q[s * tq:(s + 1) * tq, h * LANES:(h + 1) * LANES] for s, h in units]
    qs = []
    for qh, ms in zip(qhs, head_ms(qhs)):
        qh = qh * lax.rsqrt(ms + EPS) * qg_ref[...]
        if rope:
            qh = _rope(qh, rq_refs[0][...], rq_refs[1][...], rq_refs[2][...], MLA_ROPE // 4)
        qs.append(qh.astype(BF16))
    ks = [functools.partial(lambda s, h: k_sc[s * S:(s + 1) * S, h * LANES:(h + 1) * LANES], s, h)
          for s, h in units]
    vs = [functools.partial(lambda s, h: (vlo_sc, vhi_sc)[h % 2][s * S:(s + 1) * S,
                                                                (h // 2) * 2 * LANES:(h // 2 + 1) * 2 * LANES], s, h)
          for s, h in units]
    outs = _attend(qs, ks, vs, group)
    for s in range(nb):
        for p in range(H // 2):
            o_pair = outs[s * H + 2 * p] + outs[s * H + 2 * p + 1]
            o_ref[s * tq:(s + 1) * tq, p * LANES:(p + 1) * LANES] = o_pair.astype(o_ref.dtype)


def _mla(proj, ckv, ctx, rope_tabs, params, *, B, T, tq, row0=0, ckv_out=None, nb=1):
    want_ckv = ckv_out is not None
    nq = T // tq
    assert nb == 1 or nq == 1
    S_ctx = ctx[0].shape[2] if ctx is not None else 0
    S = S_ctx + T
    rope = rope_tabs is not None
    cq_blk = (2 * GLA_QK + 2 * GLA_V) // MLA_Q_RANK
    kpe_blk = (2 * GLA_QK + 2 * GLA_V + MLA_Q_RANK) // LANES
    off_q, off_t = row0 // (nb * tq), row0 // (nb * T)
    assert row0 % (nb * T) == 0
    in_specs = [pl.BlockSpec((nb * tq, MLA_Q_RANK), lambda b, j: (off_q + b * nq + j, cq_blk)),
                pl.BlockSpec((nb * T, LANES), lambda b, j: (off_t + b, kpe_blk)),
                pl.BlockSpec((nb * T, MLA_KV_RANK), lambda b, j: (off_t + b, 0))]
    args = [proj, proj, ckv]
    if S_ctx:
        layer = ctx[2]
        in_specs += [pl.BlockSpec((nb, None, S_ctx, MLA_KV_RANK), lambda b, j: (b, layer, 0, 0)),
                     pl.BlockSpec((nb, None, S_ctx, LANES), lambda b, j: (b, layer, 0, 0))]
        args += list(ctx[:2])
    if rope:
        in_specs += [pl.BlockSpec((tq, LANES), lambda b, j: (j, 0))] * 3
        in_specs += [pl.BlockSpec((T, LANES), lambda b, j: (0, 0))] * 3
        args += list(rope_tabs) * 2
    in_specs += [pl.BlockSpec(p.shape, lambda b, j: (0, 0)) for p in params]
    args += list(params)
    out_specs = [pl.BlockSpec((nb * tq, MLA_HEADS * MLA_V), lambda b, j: (b * nq + j, 0))]
    out_shape = [jax.ShapeDtypeStruct((B * T, MLA_HEADS * MLA_V), BF16)]
    aliased, n_zero = [], 0
    if want_ckv:
        spec, shape, aliased, n_zero = _slot_output(ckv_out, B, (T, MLA_KV_RANK), F32, nb)
        out_specs.append(spec)
        out_shape.append(shape)
    res = pl.pallas_call(
        functools.partial(_mla_body, T=T, S_ctx=S_ctx, rope=rope, want_ckv=want_ckv,
                          group=_head_group(S, MLA_HEADS), n_alias=len(aliased), n_zero_slots=n_zero, nb=nb),
        grid=(B // nb, nq),
        in_specs=[pl.BlockSpec(memory_space=pl.ANY)] * len(aliased) + in_specs,
        out_specs=out_specs,
        out_shape=out_shape,
        input_output_aliases={i: 1 + i for i in range(len(aliased))},
        scratch_shapes=[pltpu.VMEM((nb * S, MLA_HEADS * LANES), BF16)] * 3,
        compiler_params=_cparams(("arbitrary", "arbitrary"), VMEM_LIMIT),
        name="mla",
    )(*aliased, *args)
    return res if want_ckv else (res[0], None)


def _gqa_body(*refs, T, S_ctx, rope, want_kv, group, n_alias, n_zero_slots, nb):
    refs = refs[n_alias:]
    q_ref, k_ref, v_ref = refs[:3]
    pos = 3
    if S_ctx:
        kc_ref, vc_ref = refs[pos:pos + 2]
        pos += 2
    if rope:
        rq_refs = refs[pos:pos + 3]
        rk_refs = refs[pos + 3:pos + 6]
        pos += 6
    qg_ref, kg_ref = refs[pos:pos + 2]
    pos += 2
    o_ref = refs[pos]
    pos += 1
    if want_kv:
        kn_ref, vo_ref = refs[pos:pos + 2]
        pos += 2
    klo_sc, khi_sc, vt_sc = refs[pos:pos + 3]

    lo = lax.broadcasted_iota(jnp.int32, (1, LANES), 1) < GQA_DH
    half_sum = _group_ones(GQA_DH)

    def head_ms(x):
        return _split_dot(x * x, half_sum) * (1.0 / GQA_DH)

    def scatter_halves(x, lo_sc, hi_sc, c, rows):
        rolled = pltpu.roll(x, GQA_DH, 1)
        lo_sc[2 * c, rows, :LANES] = jnp.where(lo, x, 0.0).astype(BF16)
        hi_sc[2 * c, rows, :LANES] = jnp.where(lo, 0.0, rolled).astype(BF16)
        lo_sc[2 * c + 1, rows, :LANES] = jnp.where(lo, rolled, 0.0).astype(BF16)
        hi_sc[2 * c + 1, rows, :LANES] = jnp.where(lo, 0.0, x).astype(BF16)

    def fill_vt(x, c, rows):
        x_t = x.T
        for half in range(2):
            vt_sc[2 * c + half, 0:GQA_DH, rows] = x_t[half * GQA_DH:(half + 1) * GQA_DH, :].astype(BF16)
            vt_sc[2 * c + half, GQA_DH:, rows] = jnp.ones((V_ROWS - GQA_DH, rows.stop - rows.start), BF16)

    S = S_ctx + T
    tq = q_ref.shape[0] // nb

    @pl.when(pl.program_id(1) == 0)
    def _():
        for s in range(nb):
            if want_kv and n_zero_slots:
                zeros = jnp.zeros((n_zero_slots, T, GQA_KV_HEADS * GQA_DH), F32)
                kn_ref[s, 1:] = zeros
                vo_ref[s, 1:] = zeros
            for c in range(GQA_KV_HEADS // 2):
                cols = slice(c * LANES, (c + 1) * LANES)
                if S_ctx:
                    scatter_halves(kc_ref[s, :, cols], klo_sc, khi_sc, c, slice(s * S, s * S + S_ctx))
                    fill_vt(vc_ref[s, :, cols], c, slice(s * S, s * S + S_ctx))
                kx = k_ref[s * T:(s + 1) * T, cols]
                kn = kx * lax.rsqrt(head_ms(kx) + EPS) * kg_ref[...]
                vx = v_ref[s * T:(s + 1) * T, cols]
                if want_kv:
                    kn_ref[s, 0, :, cols] = kn
                    vo_ref[s, 0, :, cols] = vx
                if rope:
                    kn = _rope(kn, rk_refs[0][...], rk_refs[1][...], rk_refs[2][...], GQA_DH // 4)
                scatter_halves(kn, klo_sc, khi_sc, c, slice(s * S + S_ctx, (s + 1) * S))
                fill_vt(vx, c, slice(s * S + S_ctx, (s + 1) * S))

    n_pairs = GQA_HEADS // 2
    units = [(s, p) for s in range(nb) for p in range(n_pairs)]
    qxs = [q_ref[s * tq:(s + 1) * tq, p * LANES:(p + 1) * LANES].astype(F32) for s, p in units]
    qs, ks, vs = [], [], []
    for (s, p), qx, ms in zip(units, qxs, _stacked_ms(qxs, half_sum, 1.0 / GQA_DH)):
        qn = qx * lax.rsqrt(ms + EPS) * qg_ref[...]
        if rope:
            qn = _rope(qn, rq_refs[0][...], rq_refs[1][...], rq_refs[2][...], GQA_DH // 4)
        g = p // 2
        qs += [qn.astype(BF16)] * 2
        ks += [functools.partial(lambda r, g, s: r[g, s * S:(s + 1) * S, :], r, g, s) for r in (klo_sc, khi_sc)]
        vs += [functools.partial(lambda g, s: vt_sc[g, :, s * S:(s + 1) * S], g, s)] * 2
    outs = _attend(qs, ks, vs, group, key_major_dv=GQA_DH)
    for s in range(nb):
        o_t = jnp.concatenate(outs[s * GQA_HEADS:(s + 1) * GQA_HEADS], axis=0)
        o_ref[s * tq:(s + 1) * tq, :] = o_t.T.astype(o_ref.dtype)


def _gqa(q, k, v, ctx, rope_tabs, params, *, B, T, tq, row0=0, kv_out=None, nb=1):
    want_kv = kv_out is not None
    nq = T // tq
    assert nb == 1 or nq == 1
    S_ctx = ctx[0].shape[2] if ctx is not None else 0
    S = S_ctx + T
    rope = rope_tabs is not None
    n_q = GQA_HEADS * GQA_DH
    n_kv = GQA_KV_HEADS * GQA_DH
    off_q, off_t = row0 // (nb * tq), row0 // (nb * T)
    assert row0 % (nb * T) == 0
    in_specs = [pl.BlockSpec((nb * tq, n_q), lambda b, j: (off_q + b * nq + j, 0)),
                pl.BlockSpec((nb * T, n_kv), lambda b, j: (off_t + b, 0)),
                pl.BlockSpec((nb * T, n_kv), lambda b, j: (off_t + b, 0))]
    args = [q, k, v]
    if S_ctx:
        layer = ctx[2]
        in_specs += [pl.BlockSpec((nb, None, S_ctx, n_kv), lambda b, j: (b, layer, 0, 0))] * 2
        args += list(ctx[:2])
    if rope:
        in_specs += [pl.BlockSpec((tq, LANES), lambda b, j: (j, 0))] * 3
        in_specs += [pl.BlockSpec((T, LANES), lambda b, j: (0, 0))] * 3
        args += list(rope_tabs) * 2
    in_specs += [pl.BlockSpec(p.shape, lambda b, j: (0, 0)) for p in params]
    args += list(params)
    out_specs = [pl.BlockSpec((nb * tq, n_q), lambda b, j: (b * nq + j, 0))]
    out_shape = [jax.ShapeDtypeStruct((B * T, n_q), BF16)]
    aliased, n_zero = [], 0
    if want_kv:
        slot, n_slots, prev_k, prev_v = kv_out
        for prev in (prev_k, prev_v):
            spec, shape, al, n_zero = _slot_output((slot, n_slots, prev), B, (T, n_kv), F32, nb)
            out_specs.append(spec)
            out_shape.append(shape)
            aliased += al
    res = pl.pallas_call(
        functools.partial(_gqa_body, T=T, S_ctx=S_ctx, rope=rope, want_kv=want_kv,
                          group=_head_group(S, GQA_HEADS), n_alias=len(aliased), n_zero_slots=n_zero, nb=nb),
        grid=(B // nb, nq),
        in_specs=[pl.BlockSpec(memory_space=pl.ANY)] * len(aliased) + in_specs,
        out_specs=out_specs,
        out_shape=out_shape,
        input_output_aliases={i: 1 + i for i in range(len(aliased))},
        scratch_shapes=[pltpu.VMEM((GQA_KV_HEADS, nb * S, LANES), BF16),
                        pltpu.VMEM((GQA_KV_HEADS, nb * S, LANES), BF16),
                        pltpu.VMEM((GQA_KV_HEADS, V_ROWS, nb * S), BF16)],
        compiler_params=_cparams(("arbitrary", "arbitrary"), VMEM_LIMIT),
        name="gqa",
    )(*aliased, *args)
    return res if want_kv else (res[0], None, None)


def _rope_tables(n_tok, d_rot, lead, reps):
    t = jnp.arange(n_tok, dtype=jnp.int32)
    posn = jnp.stack([t // GRID_W, t % GRID_W], axis=-1).astype(F32)
    quarter = d_rot // 4
    inv = jnp.power(ROPE_THETA, -jnp.arange(quarter, dtype=F32) / quarter)
    ang = posn[:, :, None] * inv
    cos, sin = jnp.cos(ang), jnp.sin(ang)
    zero = jnp.zeros_like(sin)
    c_rot = jnp.stack([cos, cos], axis=2).reshape(n_tok, d_rot)
    sp_rot = jnp.stack([-sin, zero], axis=2).reshape(n_tok, d_rot)
    sm_rot = jnp.stack([zero, sin], axis=2).reshape(n_tok, d_rot)
    tail = LANES - lead - reps * d_rot

    def embed(rot, fill):
        parts = [jnp.full((n_tok, lead), fill, F32)] + [rot] * reps + [jnp.full((n_tok, tail), fill, F32)]
        return jnp.concatenate(parts, axis=1)

    return embed(c_rot, 1.0), embed(sp_rot, 0.0), embed(sm_rot, 0.0)


def _pad_lanes(x, lead, width=LANES):
    pad = [(0, 0)] * (x.ndim - 1) + [(lead, width - lead - x.shape[-1])]
    return jnp.pad(x, pad)


def _ab_params(w_in, w_out, a_w2, a_b, out_g, q_norm_g, w_qb, kv_norm_g, w_kvb, qn_g, kn_g):
    d = w_in.shape[0]
    o_alo = 2 * GLA_QK + 2 * GLA_V
    o_cq = o_alo + 2 * GLA_RANK
    o_ckv = o_cq + MLA_Q_RANK
    o_kpe = o_ckv + MLA_KV_RANK
    w_kpe = w_in[:, o_kpe:]
    w_perm = jnp.concatenate([
        w_in[:, :o_alo], w_in[:, o_cq:o_ckv], _pad_lanes(w_kpe, MLA_NOPE),
        w_in[:, o_ckv:o_kpe],
        _pad_lanes(jnp.concatenate([w_kpe, w_in[:, o_alo:o_cq]], axis=1), 0),
    ], axis=1).astype(BF16)
    w2bd = jnp.zeros((LANES, 2 * GLA_QK), F32)
    w2bd = (w2bd.at[MLA_ROPE:MLA_ROPE + GLA_RANK, :GLA_QK].set(a_w2[0])
            .at[MLA_ROPE + GLA_RANK:MLA_ROPE + 2 * GLA_RANK, GLA_QK:].set(a_w2[1]))
    w_qb_p = _pad_lanes(w_qb.reshape(MLA_Q_RANK, MLA_HEADS, MLA_QK), 0).reshape(MLA_Q_RANK, MLA_HEADS * LANES)
    kvb = w_kvb.reshape(MLA_KV_RANK, MLA_HEADS, MLA_NOPE + MLA_V)
    w_kvb_p = jnp.concatenate([
        _pad_lanes(kvb[:, :, :MLA_NOPE], 0).reshape(MLA_KV_RANK, MLA_HEADS * LANES),
        kvb[:, :, MLA_NOPE:].reshape(MLA_KV_RANK, MLA_HEADS * MLA_V)], axis=1)
    return dict(
        w_perm=w_perm,
        w2bd=w2bd.astype(BF16),
        a_b=a_b.reshape(1, 2 * GLA_QK),
        out_g=jnp.tile(out_g, GLA_HEADS).reshape(1, GLA_V),
        mla=(q_norm_g.reshape(1, -1), w_qb_p.astype(BF16),
             _pad_lanes(qn_g * (MLA_QK ** -0.5 * LOG2E), 0).reshape(1, LANES),
             kv_norm_g.reshape(1, -1), w_kvb_p.astype(BF16),
             _pad_lanes(kn_g, 0).reshape(1, LANES)),
        w_out_gla=w_out[:GLA_V].astype(BF16),
        w_out_mla=w_out[GLA_V:].astype(BF16),
    )


AB_OUTS = ((2 * GLA_QK + 2 * GLA_V + MLA_Q_RANK + LANES, BF16), (MLA_KV_RANK, F32), (LANES, F32))
C_OUTS = ((GQA_HEADS * GQA_DH, BF16), (GQA_KV_HEADS * GQA_DH, F32), (GQA_KV_HEADS * GQA_DH, F32))


def kernel(x_prompt, x_sample, c, cache_mla_ckv, cache_mla_kpe, state_gla, cache_gqa_k, cache_gqa_v,
           c_ctx, ada_w, ada_b, norm_mix_g, norm_ffn_g, ffn_w_in, ffn_w_out, ab_w_in, ab_w_out,
           gla_a_w2, gla_a_b, gla_out_g, mla_q_norm_g, mla_w_qb, mla_kv_norm_g, mla_w_kvb, mla_qn_g,
           mla_kn_g, gqa_w_in, gqa_w_out, gqa_qn_g, gqa_kn_g):
    Bp, Tp, D = x_prompt.shape
    Bs, Ts, _ = x_sample.shape
    depth = ada_w.shape[0]
    xp = x_prompt.reshape(Bp * Tp, D)
    xs = x_sample.reshape(Bs * Ts, D)

    assert 2 + Bs <= SUBLANES
    cond = jnp.zeros((SUBLANES, D), F32).at[0].set(c_ctx).at[2:2 + Bs].set(c)
    mods = _adaln(cond, ada_w, ada_b).reshape(depth, SUBLANES, 6, D)

    rope_mla = _rope_tables(Ts, MLA_ROPE, MLA_NOPE, 1)
    rope_gqa = _rope_tables(Ts, GQA_DH, 0, 2)

    streams = (dict(mod_base=0, rows_per_cond=None), dict(mod_base=2, rows_per_cond=Ts))
    proj_tm, ffn_tm, prompt_nb = PROJ_TM, FFN_TM, PROMPT_NB
    n_p = Bp * Tp

    n_ab, n_c = (depth + 1) // 2, depth // 2
    n_kv = GQA_KV_HEADS * GQA_DH
    kpe_cache = _pad_lanes(cache_mla_kpe, MLA_NOPE)
    gqa_k_cache = cache_gqa_k.reshape(Bs, n_c, -1, n_kv)
    gqa_v_cache = cache_gqa_v.reshape(Bs, n_c, -1, n_kv)
    new_ckv = new_kpe = new_gla = new_k = new_v = None
    for l in range(depth):
        i = l // 2
        mod = mods[l]
        proj = functools.partial(_inproj, xp, xs, norm_mix_g[l], mod, sample_mod_base=2, rows_per_cond=Ts,
                                 tm=proj_tm)
        if l % 2 == 0:
            P = _ab_params(ab_w_in[i], ab_w_out[i], gla_a_w2[i], gla_a_b[i], gla_out_g[i], mla_q_norm_g[i],
                           mla_w_qb[i], mla_kv_norm_g[i], mla_w_kvb[i], mla_qn_g[i], mla_kn_g[i])
            pj, ckv, alk = proj(P['w_perm'], AB_OUTS)
            gla = functools.partial(_gla, pj, alk, P['w2bd'], P['a_b'], P['out_g'])
            og_p, new_gla, new_kpe = gla(None, B=Bp, T=Tp, state_out=(i, n_ab, new_gla, new_kpe), nb=prompt_nb)
            og_s, _ = gla((state_gla, i), B=Bs, T=Ts, row0=n_p)
            om_p, new_ckv = _mla(pj, ckv, None, None, P['mla'], B=Bp, T=Tp, tq=Tp,
                                 ckv_out=(i, n_ab, new_ckv), nb=prompt_nb // 2)
            om_s, _ = _mla(pj, ckv, (cache_mla_ckv, kpe_cache, i), rope_mla, P['mla'], B=Bs, T=Ts,
                           tq=LATENT_TQ, row0=n_p)
            wps = (P['w_out_gla'], P['w_out_mla'])
            acts_p, acts_s = (og_p, om_p), (og_s, om_s)
        else:
            w_in = gqa_w_in[i].astype(BF16)
            w_out = gqa_w_out[i].astype(BF16)
            gp = (jnp.tile(gqa_qn_g[i] * (GQA_DH ** -0.5 * LOG2E), 2).reshape(1, LANES),
                  jnp.tile(gqa_kn_g[i], 2).reshape(1, LANES))
            q, k, v = proj(w_in, C_OUTS)
            o_p, new_k, new_v = _gqa(q, k, v, None, None, gp, B=Bp, T=Tp, tq=Tp,
                                     kv_out=(i, n_c, new_k, new_v), nb=prompt_nb)
            o_s, _, _ = _gqa(q, k, v, (gqa_k_cache, gqa_v_cache, i), rope_gqa, gp, B=Bs, T=Ts, tq=LATENT_TQ,
                             row0=n_p)
            wps = (w_out,)
            acts_p, acts_s = (o_p,), (o_s,)
        xp = _mix_ffn(acts_p, wps, xp, norm_ffn_g[l], mod, ffn_w_in, ffn_w_out, l, **streams[0], tm=ffn_tm)
        xs = _mix_ffn(acts_s, wps, xs, norm_ffn_g[l], mod, ffn_w_in, ffn_w_out, l, **streams[1], tm=ffn_tm)
    kv_shape = (Bp, n_c, Tp, GQA_KV_HEADS, GQA_DH)
    return (xp.reshape(Bp, Tp, D), xs.reshape(Bs, Ts, D), new_ckv, new_kpe, new_gla,
            new_k.reshape(kv_shape), new_v.reshape(kv_shape))
```

```python
import functools

import jax
import jax.numpy as jnp
from jax import lax
from jax.experimental import pallas as pl
from jax.experimental.pallas import tpu as pltpu

F32 = jnp.float32
BF16 = jnp.bfloat16

EPS = 1e-6
ROPE_THETA = 10000.0
GRID_W = 64
LANES = 128
GLA_HEADS, GLA_DK, GLA_DV = 4, 64, 128
GLA_QK = GLA_HEADS * GLA_DK
GLA_V = GLA_HEADS * GLA_DV
GLA_RANK = 16
GLA_TAU = 16.0
GLA_CHUNK = 64
MLA_HEADS = 8
MLA_Q_RANK, MLA_KV_RANK = 384, 256
MLA_NOPE, MLA_ROPE, MLA_V = 64, 32, 64
MLA_QK = MLA_NOPE + MLA_ROPE
GQA_HEADS, GQA_KV_HEADS, GQA_DH = 16, 4, 64
SUBLANES = 8
VMEM_LIMIT = 56 << 20
FFN_VMEM_LIMIT = 60 << 20

ADALN_TN = 1536
PROJ_TM = 512
FFN_TM = 2048
FFN_TK = 256
FFN_ROW_CHUNK = 512
LATENT_TQ = 256
PROMPT_NB = 4
GLA_GROUP = 4
GLA_EPILOGUE_ROWS = 256


def _cparams(sem, vmem=None):
    return pltpu.CompilerParams(dimension_semantics=sem, vmem_limit_bytes=vmem)


def _nt_dot(a, b):
    return lax.dot_general(a, b, (((1,), (1,)), ((), ())), preferred_element_type=F32)


def _tn_dot(a, b):
    return lax.dot_general(a, b, (((0,), (0,)), ((), ())), preferred_element_type=F32)


def _ds(start, size):
    if isinstance(start, int):
        return pl.ds(start, size)
    return pl.ds(pl.multiple_of(start, size), size)


def _rms(x, g):
    ms = jnp.mean(x * x, axis=-1, keepdims=True)
    return x * lax.rsqrt(ms + EPS) * g


def _silu(x):
    return x * jax.nn.sigmoid(x)


def _rope(x, c, sp, sm, shift):
    return x * c + pltpu.roll(x, LANES - shift, 1) * sp + pltpu.roll(x, shift, 1) * sm


def _adaln_body(c_ref, w_ref, b_ref, o_ref):
    c = c_ref[...]
    s = _silu(c).astype(BF16)
    o_ref[0] = jnp.dot(s, w_ref[0].astype(BF16), preferred_element_type=F32) + b_ref[0]


def _adaln(cond, ada_w, ada_b):
    L, D, E = ada_w.shape
    tn = ADALN_TN
    return pl.pallas_call(
        _adaln_body,
        grid=(L, E // tn),
        in_specs=[pl.BlockSpec((SUBLANES, D), lambda l, n: (0, 0)),
                  pl.BlockSpec((1, D, tn), lambda l, n: (l, 0, n)),
                  pl.BlockSpec((1, 1, tn), lambda l, n: (l, 0, n))],
        out_specs=pl.BlockSpec((1, SUBLANES, tn), lambda l, n: (l, 0, n)),
        out_shape=jax.ShapeDtypeStruct((L, SUBLANES, E), F32),
        compiler_params=_cparams(("arbitrary", "arbitrary")),
        name="adaln",
    )(cond, ada_w, ada_b.reshape(L, 1, E))


def _slot_output(slot_out, batch, tail, dtype, nb=1):
    slot, n_slots, prev = slot_out
    zeros = (0,) * len(tail)
    shape = jax.ShapeDtypeStruct((batch, n_slots) + tail, dtype)
    if slot == 0:
        return pl.BlockSpec((nb, n_slots) + tail, lambda b, *_: (b, 0) + zeros), shape, [], n_slots - 1
    return pl.BlockSpec((nb, 1) + tail, lambda b, *_: (b, slot) + zeros), shape, [prev], 0


def _mod_index(mod_base, rows_per_cond, tm):
    if rows_per_cond is None:
        return lambda i: (mod_base, 0, 0)
    n_mod = max(tm // rows_per_cond, 1)
    return lambda i: ((mod_base + (i * tm) // rows_per_cond) // n_mod, 0, 0)


def _inproj_body(xp_ref, xs_ref, g_ref, mod_ref, w_ref, *o_refs, widths, n_prompt_tiles):
    def project(x_ref):
        h = _rms(x_ref[...], g_ref[...])
        h = h * (1.0 + mod_ref[0, 1:2, :]) + mod_ref[0, 0:1, :]
        acc = jnp.dot(h.astype(BF16), w_ref[...], preferred_element_type=F32)
        off = 0
        for o_ref, w in zip(o_refs, widths):
            o_ref[...] = acc[:, off:off + w].astype(o_ref.dtype)
            off += w

    is_prompt = pl.program_id(0) < n_prompt_tiles
    pl.when(is_prompt)(lambda: project(xp_ref))
    pl.when(jnp.logical_not(is_prompt))(lambda: project(xs_ref))


def _inproj(xp, xs, g, mod, w, outs, *, sample_mod_base, rows_per_cond, tm):
    n_p, d = xp.shape
    n = n_p + xs.shape[0]
    np_t = n_p // tm
    widths = tuple(o[0] for o in outs)

    def mod_idx(i):
        return (jnp.where(i < np_t, 0, sample_mod_base + ((i - np_t) * tm) // rows_per_cond), 0, 0)

    return pl.pallas_call(
        functools.partial(_inproj_body, widths=widths, n_prompt_tiles=np_t),
        grid=(n // tm,),
        in_specs=[pl.BlockSpec((tm, d), lambda i: (jnp.minimum(i, np_t - 1), 0)),
                  pl.BlockSpec((tm, d), lambda i: (jnp.maximum(i - np_t, 0), 0)),
                  pl.BlockSpec((1, d), lambda i: (0, 0)),
                  pl.BlockSpec((1, 6, d), mod_idx),
                  pl.BlockSpec(w.shape, lambda i: (0, 0))],
        out_specs=[pl.BlockSpec((tm, wd), lambda i: (i, 0)) for wd in widths],
        out_shape=[jax.ShapeDtypeStruct((n, wd), dt) for wd, dt in outs],
        compiler_params=_cparams(("arbitrary",), VMEM_LIMIT),
        name="inproj",
    )(xp, xs, g.reshape(1, d), mod, w)


def _ffn_body(*refs, n_in, nk, row_chunk, rows_per_cond):
    a_refs, wp_refs = refs[:n_in], refs[n_in:2 * n_in]
    x_ref, g_ref, mod_ref, wg_ref, wu_ref, wo_ref, o_ref, h_ref, wgu_ref, wob_ref = refs[2 * n_in:]
    k = pl.program_id(1)
    tm = x_ref.shape[0]
    tk = wg_ref.shape[2]
    n_chunks = tm // row_chunk

    def mod_row(c):
        return 0 if rows_per_cond is None else (c * row_chunk) // rows_per_cond

    def mix_proj(c):
        rows = _ds(c * row_chunk, row_chunk)
        acc = None
        for a_ref, wp_ref in zip(a_refs, wp_refs):
            part = jnp.dot(a_ref[rows, :], wp_ref[...], preferred_element_type=F32)
            acc = part if acc is None else acc + part
        return acc

    def mix_rows(c, acc):
        rows, m = _ds(c * row_chunk, row_chunk), mod_row(c)
        x_new = x_ref[rows, :] + mod_ref[m, 2:3, :] * acc
        o_ref[rows, :] = x_new
        h = _rms(x_new, g_ref[...])
        h = h * (1.0 + mod_ref[m, 4:5, :]) + mod_ref[m, 3:4, :]
        h_ref[rows, :] = h.astype(BF16)

    def ffn_rows(c):
        rows, m = _ds(c * row_chunk, row_chunk), mod_row(c)
        gu = jnp.dot(h_ref[rows, :], wgu_ref[...], preferred_element_type=F32)
        a = (_silu(gu[:, :tk]) * gu[:, tk:]).astype(BF16)
        o_ref[rows, :] += mod_ref[m, 5:6, :] * jnp.dot(a, wob_ref[...], preferred_element_type=F32)

    def cast_weights():
        wgu_ref[:, :tk] = wg_ref[0].astype(BF16)
        wgu_ref[:, tk:] = wu_ref[0].astype(BF16)
        wob_ref[...] = wo_ref[0].astype(BF16)

    @pl.when(k == 0)
    def _():
        cast_weights()
        accs = [mix_proj(c) for c in range(n_chunks)]
        for c in range(n_chunks):
            mix_rows(c, accs[c])
            ffn_rows(c)

    @pl.when(k > 0)
    def _():
        cast_weights()
        for c in range(n_chunks):
            ffn_rows(c)


def _mix_ffn(acts, wps, x, g, mod, w_in, w_out, layer, *, mod_base, rows_per_cond, tm):
    n, d = x.shape
    hidden = w_out.shape[1]
    tk = FFN_TK
    nk = hidden // tk
    n_in = len(acts)
    n_mod = 1 if rows_per_cond is None else max(tm // rows_per_cond, 1)
    once = dict(pipeline_mode=pl.Buffered(1))
    return pl.pallas_call(
        functools.partial(_ffn_body, n_in=n_in, nk=nk, row_chunk=FFN_ROW_CHUNK,
                          rows_per_cond=rows_per_cond),
        grid=(n // tm, nk),
        in_specs=([pl.BlockSpec((tm, a.shape[1]), lambda i, k: (i, 0)) for a in acts]
                  + [pl.BlockSpec(w.shape, lambda i, k: (0, 0), **once) for w in wps]
                  + [pl.BlockSpec((tm, d), lambda i, k: (i, 0)),
                     pl.BlockSpec((1, d), lambda i, k: (0, 0)),
                     pl.BlockSpec((n_mod, 6, d),
                                  (lambda f: (lambda i, k: f(i)))(_mod_index(mod_base, rows_per_cond, tm))),
                     pl.BlockSpec((1, d, tk), lambda i, k: (layer, 0, k)),
                     pl.BlockSpec((1, d, tk), lambda i, k: (layer, 0, nk + k)),
                     pl.BlockSpec((1, tk, d), lambda i, k: (layer, k, 0))]),
        out_specs=pl.BlockSpec((tm, d), lambda i, k: (i, 0)),
        out_shape=jax.ShapeDtypeStruct((n, d), F32),
        scratch_shapes=[pltpu.VMEM((tm, d), BF16),
                        pltpu.VMEM((d, 2 * tk), BF16),
                        pltpu.VMEM((tk, d), BF16)],
        compiler_params=_cparams(("arbitrary", "arbitrary"), FFN_VMEM_LIMIT),
        name="mix_ffn",
    )(*acts, *wps, x, g.reshape(1, d), mod, w_in, w_in, w_out)


def _log_sigmoid(x):
    return jnp.minimum(x, 0.0) - jnp.log(1.0 + jnp.exp(-jnp.abs(x)))


def _gla_body(*refs, T, has_s0, want_state, n_alias, n_zero_slots, nb):
    refs = refs[n_alias:]
    qkvr_ref, alk_ref, w2_ref, ab_ref, og_ref = refs[:5]
    pos = 5
    s0_ref = None
    if has_s0:
        s0_ref = refs[pos]
        pos += 1
    o_ref = refs[pos]
    pos += 1
    st_out_ref = kpe_out_ref = None
    if want_state:
        st_out_ref, kpe_out_ref = refs[pos:pos + 2]
        pos += 2
    osc_ref, st_ref, la_ref, qin_ref, kst_ref, dec_ref, upd_ref, stq_ref = refs[pos:pos + 8]

    C = GLA_CHUNK
    nc = T // C

    logit = jnp.dot(alk_ref[...].astype(BF16), w2_ref[...], preferred_element_type=F32) + ab_ref[...]
    la_ref[...] = _log_sigmoid(logit) * (1.0 / GLA_TAU)

    for s in range(nb):
        for d in range(2):
            if has_s0:
                st_ref[2 * s + d] = jnp.concatenate([s0_ref[s, d, h] for h in range(GLA_HEADS)], axis=0).T
            else:
                st_ref[2 * s + d] = jnp.zeros((GLA_DV, GLA_QK), F32)

    r64 = lax.broadcasted_iota(jnp.int32, (C, C), 0)
    c64 = lax.broadcasted_iota(jnp.int32, (C, C), 1)
    tri_f = (r64 >= c64).astype(BF16)
    tri_b = (c64 >= r64).astype(BF16)
    t_idx = lax.broadcasted_iota(jnp.int32, (C, GLA_QK), 0)
    s_idx = lax.broadcasted_iota(jnp.int32, (C, GLA_QK), 1) % C
    causal_f = t_idx >= s_idx
    causal_b = t_idx <= s_idx
    bm_k = (lax.broadcasted_iota(jnp.int32, (GLA_QK, GLA_QK), 0) // GLA_DK
            == lax.broadcasted_iota(jnp.int32, (GLA_QK, GLA_QK), 1) // GLA_DK)
    bm_v = (lax.broadcasted_iota(jnp.int32, (GLA_QK, GLA_V), 0) // C
            == lax.broadcasted_iota(jnp.int32, (GLA_QK, GLA_V), 1) // GLA_DV)
    head_lanes = [lax.broadcasted_iota(jnp.int32, (1, GLA_QK), 1) // GLA_DK == h for h in range(GLA_HEADS)]
    directions = ((tri_f, causal_f, C // 2 - 1, C - 1), (tri_b, causal_b, C // 2, 0))

    def v_rows(rows):
        return qkvr_ref[rows, 2 * GLA_QK:2 * GLA_QK + GLA_V]

    G = GLA_GROUP
    units = [(i, d) for i in range(G) for d in range(2)]

    n_groups = nb * nc // G

    def for_groups(fn):
        if n_groups <= 4:
            for g in range(n_groups):
                fn(g * G)
        else:
            def body(g, carry):
                fn(g * G)
                return carry
            lax.fori_loop(0, n_groups, body, 0)

    def intra_group(n0):
        rows = [_ds((n0 + i) * C, C) for i in range(G)]
        b = {}
        for i, d in units:
            b[i, d] = _split_dot(la_ref[rows[i], d * GLA_QK:(d + 1) * GLA_QK], directions[d][0], w_left=True)
        q_loc, k_bd = {}, {}
        for i in range(G):
            qc = qkvr_ref[rows[i], 0:GLA_QK].astype(F32) * (GLA_DK ** -0.5)
            kc = qkvr_ref[rows[i], GLA_QK:2 * GLA_QK].astype(F32)
            for d in range(2):
                _, _, ref_row, last_row = directions[d]
                bb = b[i, d]
                b_ref = bb[ref_row:ref_row + 1, :]
                b_last = bb[last_row:last_row + 1, :]
                q_loc[i, d] = (qc * jnp.exp(bb - b_ref)).astype(BF16)
                k_loc = kc * jnp.exp(b_ref - bb)
                k_bd[i, d] = jnp.where(bm_k, jnp.concatenate([k_loc] * GLA_HEADS, axis=0), 0.0).astype(BF16)
                qin_ref[d, rows[i], :] = (qc * jnp.exp(bb)).astype(BF16)
                kst_ref[d, rows[i], :] = (kc * jnp.exp(b_last - bb)).astype(BF16)
                dec_ref[d, _ds((n0 + i) * SUBLANES, SUBLANES), :] = jnp.broadcast_to(jnp.exp(b_last),
                                                                                      (SUBLANES, GLA_QK))
        a = {u: jnp.where(directions[u[1]][1], _nt_dot(q_loc[u], k_bd[u]), 0.0).astype(BF16) for u in units}
        v_bd = [jnp.where(bm_v, jnp.concatenate([v_rows(rows[i])] * GLA_HEADS, axis=0), jnp.zeros((), BF16))
                for i in range(G)]
        o = {u: jnp.dot(a[u], v_bd[u[0]], preferred_element_type=F32) for u in units}
        for i in range(G):
            osc_ref[rows[i], :] = o[i, 0] + o[i, 1]

    def update_group(n0):
        rows = [_ds((n0 + i) * C, C) for i in range(G)]
        upd = {u: _tn_dot(v_rows(rows[u[0]]), kst_ref[u[1], rows[u[0]], :]) for u in units}
        for i, d in units:
            acc = None
            for h, m in enumerate(head_lanes):
                part = jnp.where(m, upd[i, d][h * GLA_DV:(h + 1) * GLA_DV, :], 0.0)
                acc = part if acc is None else acc + part
            upd_ref[d, n0 + i] = acc

    def scan_step(i, carry):
        for s in range(nb):
            for d, n in ((0, s * nc + i), (1, s * nc + nc - 1 - i)):
                st = st_ref[2 * s + d]
                stq_ref[d, n] = st.astype(BF16)
                st_ref[2 * s + d] = st * dec_ref[d, pl.ds(n * SUBLANES, 1), :] + upd_ref[d, n]
        return carry

    def readout_group(n0):
        rows = [_ds((n0 + i) * C, C) for i in range(G)]
        o = {}
        for i, d in units:
            q_in = qin_ref[d, rows[i], :]
            q_heads = jnp.concatenate([jnp.where(m, q_in, jnp.zeros((), BF16)) for m in head_lanes], axis=0)
            o[i, d] = _nt_dot(q_heads, stq_ref[d, n0 + i])
        for i in range(G):
            both = o[i, 0] + o[i, 1]
            for h in range(GLA_HEADS):
                osc_ref[rows[i], h * GLA_DV:(h + 1) * GLA_DV] += both[h * C:(h + 1) * C, :]

    for_groups(intra_group)
    for_groups(update_group)
    if nc == G:
        for i in range(nc):
            scan_step(i, 0)
    else:
        lax.fori_loop(0, nc, scan_step, 0)
    for_groups(readout_group)

    rb = GLA_EPILOGUE_ROWS
    for i in range(nb * T // rb):
        rows = slice(i * rb, (i + 1) * rb)
        for h in range(GLA_HEADS):
            cols = slice(h * GLA_DV, (h + 1) * GLA_DV)
            o = _rms(osc_ref[rows, cols], og_ref[:, cols])
            r = qkvr_ref[rows, 2 * GLA_QK + GLA_V + h * GLA_DV:2 * GLA_QK + GLA_V + (h + 1) * GLA_DV].astype(F32)
            o_ref[rows, cols] = (o * _silu(r)).astype(o_ref.dtype)

    if want_state:
        for s in range(nb):
            for d in range(2):
                s_all = st_ref[2 * s + d].T
                for h in range(GLA_HEADS):
                    st_out_ref[s, 0, d, h] = s_all[h * GLA_DK:(h + 1) * GLA_DK, :]
            kpe_out_ref[s, 0] = alk_ref[s * T:(s + 1) * T, 0:MLA_ROPE]
            if n_zero_slots:
                st_out_ref[s, 1:] = jnp.zeros((n_zero_slots, 2, GLA_HEADS, GLA_DK, GLA_DV), F32)
                kpe_out_ref[s, 1:] = jnp.zeros((n_zero_slots, T, MLA_ROPE), F32)


def _gla(proj, alk, w2bd, a_b, out_g, s0, *, B, T, row0=0, state_out=None, nb=1):
    has_s0 = s0 is not None
    want_state = state_out is not None
    n_qkvr = 2 * GLA_QK + 2 * GLA_V
    R = nb * T
    n_chunks = R // GLA_CHUNK
    off = row0 // R
    assert row0 % R == 0
    in_specs = [pl.BlockSpec((R, n_qkvr), lambda b: (off + b, 0)),
                pl.BlockSpec((R, LANES), lambda b: (off + b, 0)),
                pl.BlockSpec(w2bd.shape, lambda b: (0, 0)),
                pl.BlockSpec((1, 2 * GLA_QK), lambda b: (0, 0)),
                pl.BlockSpec((1, GLA_V), lambda b: (0, 0))]
    args = [proj, alk, w2bd, a_b, out_g]
    if has_s0:
        states, layer = s0
        in_specs.append(pl.BlockSpec((nb, None, 2, GLA_HEADS, GLA_DK, GLA_DV),
                                     lambda b: (b, layer, 0, 0, 0, 0)))
        args.append(states)
    out_specs = [pl.BlockSpec((R, GLA_V), lambda b: (b, 0))]
    out_shape = [jax.ShapeDtypeStruct((B * T, GLA_V), BF16)]
    aliased, n_zero = [], 0
    if want_state:
        slot, n_slots, prev_state, prev_kpe = state_out
        for prev, tail in ((prev_state, (2, GLA_HEADS, GLA_DK, GLA_DV)), (prev_kpe, (T, MLA_ROPE))):
            spec, shape, al, n_zero = _slot_output((slot, n_slots, prev), B, tail, F32, nb)
            out_specs.append(spec)
            out_shape.append(shape)
            aliased += al
    res = pl.pallas_call(
        functools.partial(_gla_body, T=T, has_s0=has_s0, want_state=want_state, n_alias=len(aliased),
                          n_zero_slots=n_zero, nb=nb),
        grid=(B // nb,),
        in_specs=[pl.BlockSpec(memory_space=pl.ANY)] * len(aliased) + in_specs,
        out_specs=out_specs,
        out_shape=out_shape,
        input_output_aliases={i: 1 + i for i in range(len(aliased))},
        scratch_shapes=[pltpu.VMEM((R, GLA_V), F32),
                        pltpu.VMEM((2 * nb, GLA_DV, GLA_QK), F32),
                        pltpu.VMEM((R, 2 * GLA_QK), F32),
                        pltpu.VMEM((2, R, GLA_QK), BF16),
                        pltpu.VMEM((2, R, GLA_QK), BF16),
                        pltpu.VMEM((2, SUBLANES * n_chunks, GLA_QK), F32),
                        pltpu.VMEM((2, n_chunks, GLA_DV, GLA_QK), F32),
                        pltpu.VMEM((2, n_chunks, GLA_DV, GLA_QK), BF16)],
        compiler_params=_cparams(("arbitrary",), VMEM_LIMIT),
        name="gla",
    )(*aliased, *args)
    return res if want_state else (res[0], None)


LOG2E = 1.4426950408889634


V_ROWS = 80


def _attend(qs, ks, vs, group, key_major_dv=None):
    key_major = key_major_dv is not None
    axis = 0 if key_major else -1

    def qk(g0):
        return [_nt_dot(k(), q) if key_major else _nt_dot(q, k())
                for q, k in zip(qs[g0:g0 + group], ks[g0:g0 + group])]

    outs = []
    scores = qk(0)
    for g0 in range(0, len(qs), group):
        probs = [jnp.exp2(s - jnp.max(s, axis=axis, keepdims=True)).astype(BF16) for s in scores]
        if g0 + group < len(qs):
            scores = qk(g0 + group)
        if key_major:
            dv = key_major_dv
            res = [jnp.dot(v(), p, preferred_element_type=F32) for p, v in zip(probs, vs[g0:g0 + group])]
            outs += [r[:dv, :] / r[dv:dv + 1, :] for r in res]
        else:
            res = [jnp.dot(p, v(), preferred_element_type=F32) for p, v in zip(probs, vs[g0:g0 + group])]
            outs += [r[:, :LANES] / r[:, LANES:] for r in res]
    return outs


def _head_group(n_keys, n_heads):
    return 4 if n_keys <= 512 else 2


def _stacked_ms(xs, ones_mat, inv_n):
    n = xs[0].shape[0]
    sq = jnp.concatenate([x * x for x in xs], axis=0).astype(BF16)
    ms = jnp.dot(sq, ones_mat, preferred_element_type=F32) * inv_n
    return [ms[i * n:(i + 1) * n] for i in range(len(xs))]


def _split_dot(x, w, w_left=False):
    hi = x.astype(BF16)
    lo = (x - hi.astype(F32)).astype(BF16)
    if w_left:
        return jnp.dot(w, hi, preferred_element_type=F32) + jnp.dot(w, lo, preferred_element_type=F32)
    return jnp.dot(hi, w, preferred_element_type=F32) + jnp.dot(lo, w, preferred_element_type=F32)


def _group_ones(group):
    r = lax.broadcasted_iota(jnp.int32, (LANES, LANES), 0) // group
    c = lax.broadcasted_iota(jnp.int32, (LANES, LANES), 1) // group
    return (r == c).astype(BF16)


def _mla_body(*refs, T, S_ctx, rope, want_ckv, group, n_alias, n_zero_slots, nb):
    refs = refs[n_alias:]
    cq_ref, kpe_ref, ckv_ref = refs[:3]
    pos = 3
    if S_ctx:
        ckvc_ref, kpec_ref = refs[pos:pos + 2]
        pos += 2
    if rope:
        rq_refs = refs[pos:pos + 3]
        rk_refs = refs[pos + 3:pos + 6]
        pos += 6
    qng_ref, wqb_ref, qg_ref, kvg_ref, wkvb_ref, kg_ref = refs[pos:pos + 6]
    pos += 6
    o_ref = refs[pos]
    pos += 1
    if want_ckv:
        ckvn_ref = refs[pos]
        pos += 1
    k_sc, vlo_sc, vhi_sc = refs[pos:pos + 3]
    lo = lax.broadcasted_iota(jnp.int32, (1, LANES), 1) < MLA_V

    H = MLA_HEADS
    n_k = H * LANES
    def head_ms(xs):
        return [jnp.sum(x * x, axis=-1, keepdims=True) * (1.0 / MLA_QK) for x in xs]

    def fill_kv(ckvn, kpe, row0, n_rows, with_rope):
        rows = slice(row0, row0 + n_rows)
        kv = jnp.dot(ckvn.astype(BF16), wkvb_ref[...], preferred_element_type=F32)
        khs = [kv[:, h * LANES:(h + 1) * LANES] + kpe for h in range(H)]
        for h, (kh, ms) in enumerate(zip(khs, head_ms(khs))):
            kh = kh * lax.rsqrt(ms + EPS) * kg_ref[...]
            if with_rope:
                kh = _rope(kh, rk_refs[0][...], rk_refs[1][...], rk_refs[2][...], MLA_ROPE // 4)
            k_sc[rows, h * LANES:(h + 1) * LANES] = kh.astype(BF16)
        ones = jnp.ones((n_rows, LANES), BF16)
        for p in range(H // 2):
            v = kv[:, n_k + p * LANES:n_k + (p + 1) * LANES]
            vlo_sc[rows, 2 * p * LANES:(2 * p + 1) * LANES] = jnp.where(lo, v, 0.0).astype(BF16)
            vhi_sc[rows, 2 * p * LANES:(2 * p + 1) * LANES] = jnp.where(lo, 0.0, v).astype(BF16)
            vlo_sc[rows, (2 * p + 1) * LANES:(2 * p + 2) * LANES] = ones
            vhi_sc[rows, (2 * p + 1) * LANES:(2 * p + 2) * LANES] = ones

    S = S_ctx + T
    tq = cq_ref.shape[0] // nb

    @pl.when(pl.program_id(1) == 0)
    def _():
        for s in range(nb):
            if S_ctx:
                fill_kv(ckvc_ref[s], kpec_ref[s], s * S, S_ctx, False)
            ckvn = _rms(ckv_ref[s * T:(s + 1) * T, :], kvg_ref[...])
            if want_ckv:
                ckvn_ref[s, 0] = ckvn
                if n_zero_slots:
                    ckvn_ref[s, 1:] = jnp.zeros((n_zero_slots, T, MLA_KV_RANK), F32)
            fill_kv(ckvn, kpe_ref[s * T:(s + 1) * T, :].astype(F32), s * S + S_ctx, T, rope)

    cqn = _rms(cq_ref[...].astype(F32), qng_ref[...])
    q = jnp.dot(cqn.astype(BF16), wqb_ref[...], preferred_element_type=F32)
    units = [(s, h) for s in range(nb) for h in range(H)]
    qhs = [q[s * tq:(s + 1) * tq, h * LANES:(h + 1) * LANES] for s, h in units]
    qs = []
    for qh, ms in zip(qhs, head_ms(qhs)):
        qh = qh * lax.rsqrt(ms + EPS) * qg_ref[...]
        if rope:
            qh = _rope(qh, rq_refs[0][...], rq_refs[1][...], rq_refs[2][...], MLA_ROPE // 4)
        qs.append(qh.astype(BF16))
    ks = [functools.partial(lambda s, h: k_sc[s * S:(s + 1) * S, h * LANES:(h + 1) * LANES], s, h)
          for s, h in units]
    vs = [functools.partial(lambda s, h: (vlo_sc, vhi_sc)[h % 2][s * S:(s + 1) * S,
                                                                (h // 2) * 2 * LANES:(h // 2 + 1) * 2 * LANES], s, h)
          for s, h in units]
    outs = _attend(qs, ks, vs, group)
    for s in range(nb):
        for p in range(H // 2):
            o_pair = outs[s * H + 2 * p] + outs[s * H + 2 * p + 1]
            o_ref[s * tq:(s + 1) * tq, p * LANES:(p + 1) * LANES] = o_pair.astype(o_ref.dtype)


def _mla(proj, ckv, ctx, rope_tabs, params, *, B, T, tq, row0=0, ckv_out=None, nb=1):
    want_ckv = ckv_out is not None
    nq = T // tq
    assert nb == 1 or nq == 1
    S_ctx = ctx[0].shape[2] if ctx is not None else 0
    S = S_ctx + T
    rope = rope_tabs is not None
    cq_blk = (2 * GLA_QK + 2 * GLA_V) // MLA_Q_RANK
    kpe_blk = (2 * GLA_QK + 2 * GLA_V + MLA_Q_RANK) // LANES
    off_q, off_t = row0 // (nb * tq), row0 // (nb * T)
    assert row0 % (nb * T) == 0
    in_specs = [pl.BlockSpec((nb * tq, MLA_Q_RANK), lambda b, j: (off_q + b * nq + j, cq_blk)),
                pl.BlockSpec((nb * T, LANES), lambda b, j: (off_t + b, kpe_blk)),
                pl.BlockSpec((nb * T, MLA_KV_RANK), lambda b, j: (off_t + b, 0))]
    args = [proj, proj, ckv]
    if S_ctx:
        layer = ctx[2]
        in_specs += [pl.BlockSpec((nb, None, S_ctx, MLA_KV_RANK), lambda b, j: (b, layer, 0, 0)),
                     pl.BlockSpec((nb, None, S_ctx, LANES), lambda b, j: (b, layer, 0, 0))]
        args += list(ctx[:2])
    if rope:
        in_specs += [pl.BlockSpec((tq, LANES), lambda b, j: (j, 0))] * 3
        in_specs += [pl.BlockSpec((T, LANES), lambda b, j: (0, 0))] * 3
        args += list(rope_tabs) * 2
    in_specs += [pl.BlockSpec(p.shape, lambda b, j: (0, 0)) for p in params]
    args += list(params)
    out_specs = [pl.BlockSpec((nb * tq, MLA_HEADS * MLA_V), lambda b, j: (b * nq + j, 0))]
    out_shape = [jax.ShapeDtypeStruct((B * T, MLA_HEADS * MLA_V), BF16)]
    aliased, n_zero = [], 0
    if want_ckv:
        spec, shape, aliased, n_zero = _slot_output(ckv_out, B, (T, MLA_KV_RANK), F32, nb)
        out_specs.append(spec)
        out_shape.append(shape)
    res = pl.pallas_call(
        functools.partial(_mla_body, T=T, S_ctx=S_ctx, rope=rope, want_ckv=want_ckv,
                          group=_head_group(S, MLA_HEADS), n_alias=len(aliased), n_zero_slots=n_zero, nb=nb),
        grid=(B // nb, nq),
        in_specs=[pl.BlockSpec(memory_space=pl.ANY)] * len(aliased) + in_specs,
        out_specs=out_specs,
        out_shape=out_shape,
        input_output_aliases={i: 1 + i for i in range(len(aliased))},
        scratch_shapes=[pltpu.VMEM((nb * S, MLA_HEADS * LANES), BF16)] * 3,
        compiler_params=_cparams(("arbitrary", "arbitrary"), VMEM_LIMIT),
        name="mla",
    )(*aliased, *args)
    return res if want_ckv else (res[0], None)


def _gqa_body(*refs, T, S_ctx, rope, want_kv, group, n_alias, n_zero_slots, nb):
    refs = refs[n_alias:]
    q_ref, k_ref, v_ref = refs[:3]
    pos = 3
    if S_ctx:
        kc_ref, vc_ref = refs[pos:pos + 2]
        pos += 2
    if rope:
        rq_refs = refs[pos:pos + 3]
        rk_refs = refs[pos + 3:pos + 6]
        pos += 6
    qg_ref, kg_ref = refs[pos:pos + 2]
    pos += 2
    o_ref = refs[pos]
    pos += 1
    if want_kv:
        kn_ref, vo_ref = refs[pos:pos + 2]
        pos += 2
    klo_sc, khi_sc, vt_sc = refs[pos:pos + 3]

    lo = lax.broadcasted_iota(jnp.int32, (1, LANES), 1) < GQA_DH
    half_sum = _group_ones(GQA_DH)

    def head_ms(x):
        return _split_dot(x * x, half_sum) * (1.0 / GQA_DH)

    def scatter_halves(x, lo_sc, hi_sc, c, rows):
        rolled = pltpu.roll(x, GQA_DH, 1)
        lo_sc[2 * c, rows, :LANES] = jnp.where(lo, x, 0.0).astype(BF16)
        hi_sc[2 * c, rows, :LANES] = jnp.where(lo, 0.0, rolled).astype(BF16)
        lo_sc[2 * c + 1, rows, :LANES] = jnp.where(lo, rolled, 0.0).astype(BF16)
        hi_sc[2 * c + 1, rows, :LANES] = jnp.where(lo, 0.0, x).astype(BF16)

    def fill_vt(x, c, rows):
        x_t = x.T
        for half in range(2):
            vt_sc[2 * c + half, 0:GQA_DH, rows] = x_t[half * GQA_DH:(half + 1) * GQA_DH, :].astype(BF16)
            vt_sc[2 * c + half, GQA_DH:, rows] = jnp.ones((V_ROWS - GQA_DH, rows.stop - rows.start), BF16)

    S = S_ctx + T
    tq = q_ref.shape[0] // nb

    @pl.when(pl.program_id(1) == 0)
    def _():
        for s in range(nb):
            if want_kv and n_zero_slots:
                zeros = jnp.zeros((n_zero_slots, T, GQA_KV_HEADS * GQA_DH), F32)
                kn_ref[s, 1:] = zeros
                vo_ref[s, 1:] = zeros
            for c in range(GQA_KV_HEADS // 2):
                cols = slice(c * LANES, (c + 1) * LANES)
                if S_ctx:
                    scatter_halves(kc_ref[s, :, cols], klo_sc, khi_sc, c, slice(s * S, s * S + S_ctx))
                    fill_vt(vc_ref[s, :, cols], c, slice(s * S, s * S + S_ctx))
                kx = k_ref[s * T:(s + 1) * T, cols]
                kn = kx * lax.rsqrt(head_ms(kx) + EPS) * kg_ref[...]
                vx = v_ref[s * T:(s + 1) * T, cols]
                if want_kv:
                    kn_ref[s, 0, :, cols] = kn
                    vo_ref[s, 0, :, cols] = vx
                if rope:
                    kn = _rope(kn, rk_refs[0][...], rk_refs[1][...], rk_refs[2][...], GQA_DH // 4)
                scatter_halves(kn, klo_sc, khi_sc, c, slice(s * S + S_ctx, (s + 1) * S))
                fill_vt(vx, c, slice(s * S + S_ctx, (s + 1) * S))

    n_pairs = GQA_HEADS // 2
    units = [(s, p) for s in range(nb) for p in range(n_pairs)]
    qxs = [q_ref[s * tq:(s + 1) * tq, p * LANES:(p + 1) * LANES].astype(F32) for s, p in units]
    qs, ks, vs = [], [], []
    for (s, p), qx, ms in zip(units, qxs, _stacked_ms(qxs, half_sum, 1.0 / GQA_DH)):
        qn = qx * lax.rsqrt(ms + EPS) * qg_ref[...]
        if rope:
            qn = _rope(qn, rq_refs[0][...], rq_refs[1][...], rq_refs[2][...], GQA_DH // 4)
        g = p // 2
        qs += [qn.astype(BF16)] * 2
        ks += [functools.partial(lambda r, g, s: r[g, s * S:(s + 1) * S, :], r, g, s) for r in (klo_sc, khi_sc)]
        vs += [functools.partial(lambda g, s: vt_sc[g, :, s * S:(s + 1) * S], g, s)] * 2
    outs = _attend(qs, ks, vs, group, key_major_dv=GQA_DH)
    for s in range(nb):
        o_t = jnp.concatenate(outs[s * GQA_HEADS:(s + 1) * GQA_HEADS], axis=0)
        o_ref[s * tq:(s + 1) * tq, :] = o_t.T.astype(o_ref.dtype)


def _gqa(q, k, v, ctx, rope_tabs, params, *, B, T, tq, row0=0, kv_out=None, nb=1):
    want_kv = kv_out is not None
    nq = T // tq
    assert nb == 1 or nq == 1
    S_ctx = ctx[0].shape[2] if ctx is not None else 0
    S = S_ctx + T
    rope = rope_tabs is not None
    n_q = GQA_HEADS * GQA_DH
    n_kv = GQA_KV_HEADS * GQA_DH
    off_q, off_t = row0 // (nb * tq), row0 // (nb * T)
    assert row0 % (nb * T) == 0
    in_specs = [pl.BlockSpec((nb * tq, n_q), lambda b, j: (off_q + b * nq + j, 0)),
                pl.BlockSpec((nb * T, n_kv), lambda b, j: (off_t + b, 0)),
                pl.BlockSpec((nb * T, n_kv), lambda b, j: (off_t + b, 0))]
    args = [q, k, v]
    if S_ctx:
        layer = ctx[2]
        in_specs += [pl.BlockSpec((nb, None, S_ctx, n_kv), lambda b, j: (b, layer, 0, 0))] * 2
        args += list(ctx[:2])
    if rope:
        in_specs += [pl.BlockSpec((tq, LANES), lambda b, j: (j, 0))] * 3
        in_specs += [pl.BlockSpec((T, LANES), lambda b, j: (0, 0))] * 3
        args += list(rope_tabs) * 2
    in_specs += [pl.BlockSpec(p.shape, lambda b, j: (0, 0)) for p in params]
    args += list(params)
    out_specs = [pl.BlockSpec((nb * tq, n_q), lambda b, j: (b * nq + j, 0))]
    out_shape = [jax.ShapeDtypeStruct((B * T, n_q), BF16)]
    aliased, n_zero = [], 0
    if want_kv:
        slot, n_slots, prev_k, prev_v = kv_out
        for prev in (prev_k, prev_v):
            spec, shape, al, n_zero = _slot_output((slot, n_slots, prev), B, (T, n_kv), F32, nb)
            out_specs.append(spec)
            out_shape.append(shape)
            aliased += al
    res = pl.pallas_call(
        functools.partial(_gqa_body, T=T, S_ctx=S_ctx, rope=rope, want_kv=want_kv,
                          group=_head_group(S, GQA_HEADS), n_alias=len(aliased), n_zero_slots=n_zero, nb=nb),
        grid=(B // nb, nq),
        in_specs=[pl.BlockSpec(memory_space=pl.ANY)] * len(aliased) + in_specs,
        out_specs=out_specs,
        out_shape=out_shape,
        input_output_aliases={i: 1 + i for i in range(len(aliased))},
        scratch_shapes=[pltpu.VMEM((GQA_KV_HEADS, nb * S, LANES), BF16),
                        pltpu.VMEM((GQA_KV_HEADS, nb * S, LANES), BF16),
                        pltpu.VMEM((GQA_KV_HEADS, V_ROWS, nb * S), BF16)],
        compiler_params=_cparams(("arbitrary", "arbitrary"), VMEM_LIMIT),
        name="gqa",
    )(*aliased, *args)
    return res if want_kv else (res[0], None, None)


def _rope_tables(n_tok, d_rot, lead, reps):
    t = jnp.arange(n_tok, dtype=jnp.int32)
    posn = jnp.stack([t // GRID_W, t % GRID_W], axis=-1).astype(F32)
    quarter = d_rot // 4
    inv = jnp.power(ROPE_THETA, -jnp.arange(quarter, dtype=F32) / quarter)
    ang = posn[:, :, None] * inv
    cos, sin = jnp.cos(ang), jnp.sin(ang)
    zero = jnp.zeros_like(sin)
    c_rot = jnp.stack([cos, cos], axis=2).reshape(n_tok, d_rot)
    sp_rot = jnp.stack([-sin, zero], axis=2).reshape(n_tok, d_rot)
    sm_rot = jnp.stack([zero, sin], axis=2).reshape(n_tok, d_rot)
    tail = LANES - lead - reps * d_rot

    def embed(rot, fill):
        parts = [jnp.full((n_tok, lead), fill, F32)] + [rot] * reps + [jnp.full((n_tok, tail), fill, F32)]
        return jnp.concatenate(parts, axis=1)

    return embed(c_rot, 1.0), embed(sp_rot, 0.0), embed(sm_rot, 0.0)


def _pad_lanes(x, lead, width=LANES):
    pad = [(0, 0)] * (x.ndim - 1) + [(lead, width - lead - x.shape[-1])]
    return jnp.pad(x, pad)


def _ab_params(w_in, w_out, a_w2, a_b, out_g, q_norm_g, w_qb, kv_norm_g, w_kvb, qn_g, kn_g):
    d = w_in.shape[0]
    o_alo = 2 * GLA_QK + 2 * GLA_V
    o_cq = o_alo + 2 * GLA_RANK
    o_ckv = o_cq + MLA_Q_RANK
    o_kpe = o_ckv + MLA_KV_RANK
    w_kpe = w_in[:, o_kpe:]
    w_perm = jnp.concatenate([
        w_in[:, :o_alo], w_in[:, o_cq:o_ckv], _pad_lanes(w_kpe, MLA_NOPE),
        w_in[:, o_ckv:o_kpe],
        _pad_lanes(jnp.concatenate([w_kpe, w_in[:, o_alo:o_cq]], axis=1), 0),
    ], axis=1).astype(BF16)
    w2bd = jnp.zeros((LANES, 2 * GLA_QK), F32)
    w2bd = (w2bd.at[MLA_ROPE:MLA_ROPE + GLA_RANK, :GLA_QK].set(a_w2[0])
            .at[MLA_ROPE + GLA_RANK:MLA_ROPE + 2 * GLA_RANK, GLA_QK:].set(a_w2[1]))
    w_qb_p = _pad_lanes(w_qb.reshape(MLA_Q_RANK, MLA_HEADS, MLA_QK), 0).reshape(MLA_Q_RANK, MLA_HEADS * LANES)
    kvb = w_kvb.reshape(MLA_KV_RANK, MLA_HEADS, MLA_NOPE + MLA_V)
    w_kvb_p = jnp.concatenate([
        _pad_lanes(kvb[:, :, :MLA_NOPE], 0).reshape(MLA_KV_RANK, MLA_HEADS * LANES),
        kvb[:, :, MLA_NOPE:].reshape(MLA_KV_RANK, MLA_HEADS * MLA_V)], axis=1)
    return dict(
        w_perm=w_perm,
        w2bd=w2bd.astype(BF16),
        a_b=a_b.reshape(1, 2 * GLA_QK),
        out_g=jnp.tile(out_g, GLA_HEADS).reshape(1, GLA_V),
        mla=(q_norm_g.reshape(1, -1), w_qb_p.astype(BF16),
             _pad_lanes(qn_g * (MLA_QK ** -0.5 * LOG2E), 0).reshape(1, LANES),
             kv_norm_g.reshape(1, -1), w_kvb_p.astype(BF16),
             _pad_lanes(kn_g, 0).reshape(1, LANES)),
        w_out_gla=w_out[:GLA_V].astype(BF16),
        w_out_mla=w_out[GLA_V:].astype(BF16),
    )


AB_OUTS = ((2 * GLA_QK + 2 * GLA_V + MLA_Q_RANK + LANES, BF16), (MLA_KV_RANK, F32), (LANES, F32))
C_OUTS = ((GQA_HEADS * GQA_DH, BF16), (GQA_KV_HEADS * GQA_DH, F32), (GQA_KV_HEADS * GQA_DH, F32))


def kernel(x_prompt, x_sample, c, cache_mla_ckv, cache_mla_kpe, state_gla, cache_gqa_k, cache_gqa_v,
           c_ctx, ada_w, ada_b, norm_mix_g, norm_ffn_g, ffn_w_in, ffn_w_out, ab_w_in, ab_w_out,
           gla_a_w2, gla_a_b, gla_out_g, mla_q_norm_g, mla_w_qb, mla_kv_norm_g, mla_w_kvb, mla_qn_g,
           mla_kn_g, gqa_w_in, gqa_w_out, gqa_qn_g, gqa_kn_g):
    Bp, Tp, D = x_prompt.shape
    Bs, Ts, _ = x_sample.shape
    depth = ada_w.shape[0]
    xp = x_prompt.reshape(Bp * Tp, D)
    xs = x_sample.reshape(Bs * Ts, D)

    assert 2 + Bs <= SUBLANES
    cond = jnp.zeros((SUBLANES, D), F32).at[0].set(c_ctx).at[2:2 + Bs].set(c)
    mods = _adaln(cond, ada_w, ada_b).reshape(depth, SUBLANES, 6, D)

    rope_mla = _rope_tables(Ts, MLA_ROPE, MLA_NOPE, 1)
    rope_gqa = _rope_tables(Ts, GQA_DH, 0, 2)

    streams = (dict(mod_base=0, rows_per_cond=None), dict(mod_base=2, rows_per_cond=Ts))
    proj_tm, ffn_tm, prompt_nb = PROJ_TM, FFN_TM, PROMPT_NB
    n_p = Bp * Tp

    n_ab, n_c = (depth + 1) // 2, depth // 2
    n_kv = GQA_KV_HEADS * GQA_DH
    kpe_cache = _pad_lanes(cache_mla_kpe, MLA_NOPE)
    gqa_k_cache = cache_gqa_k.reshape(Bs, n_c, -1, n_kv)
    gqa_v_cache = cache_gqa_v.reshape(Bs, n_c, -1, n_kv)
    new_ckv = new_kpe = new_gla = new_k = new_v = None
    for l in range(depth):
        i = l // 2
        mod = mods[l]
        proj = functools.partial(_inproj, xp, xs, norm_mix_g[l], mod, sample_mod_base=2, rows_per_cond=Ts,
                                 tm=proj_tm)
        if l % 2 == 0:
            P = _ab_params(ab_w_in[i], ab_w_out[i], gla_a_w2[i], gla_a_b[i], gla_out_g[i], mla_q_norm_g[i],
                           mla_w_qb[i], mla_kv_norm_g[i], mla_w_kvb[i], mla_qn_g[i], mla_kn_g[i])
            pj, ckv, alk = proj(P['w_perm'], AB_OUTS)
            gla = functools.partial(_gla, pj, alk, P['w2bd'], P['a_b'], P['out_g'])
            og_p, new_gla, new_kpe = gla(None, B=Bp, T=Tp, state_out=(i, n_ab, new_gla, new_kpe), nb=prompt_nb)
            og_s, _ = gla((state_gla, i), B=Bs, T=Ts, row0=n_p)
            om_p, new_ckv = _mla(pj, ckv, None, None, P['mla'], B=Bp, T=Tp, tq=Tp,
                                 ckv_out=(i, n_ab, new_ckv), nb=prompt_nb // 2)
            om_s, _ = _mla(pj, ckv, (cache_mla_ckv, kpe_cache, i), rope_mla, P['mla'], B=Bs, T=Ts,
                           tq=LATENT_TQ, row0=n_p)
            wps = (P['w_out_gla'], P['w_out_mla'])
            acts_p, acts_s = (og_p, om_p), (og_s, om_s)
        else:
            w_in = gqa_w_in[i].astype(BF16)
            w_out = gqa_w_out[i].astype(BF16)
            gp = (jnp.tile(gqa_qn_g[i] * (GQA_DH ** -0.5 * LOG2E), 2).reshape(1, LANES),
                  jnp.tile(gqa_kn_g[i], 2).reshape(1, LANES))
            q, k, v = proj(w_in, C_OUTS)
            o_p, new_k, new_v = _gqa(q, k, v, None, None, gp, B=Bp, T=Tp, tq=Tp,
                                     kv_out=(i, n_c, new_k, new_v), nb=prompt_nb)
            o_s, _, _ = _gqa(q, k, v, (gqa_k_cache, gqa_v_cache, i), rope_gqa, gp, B=Bs, T=Ts, tq=LATENT_TQ,
                             row0=n_p)
            wps = (w_out,)
            acts_p, acts_s = (o_p,), (o_s,)
        xp = _mix_ffn(acts_p, wps, xp, norm_ffn_g[l], mod, ffn_w_in, ffn_w_out, l, **streams[0], tm=ffn_tm)
        xs = _mix_ffn(acts_s, wps, xs, norm_ffn_g[l], mod, ffn_w_in, ffn_w_out, l, **streams[1], tm=ffn_tm)
    kv_shape = (Bp, n_c, Tp, GQA_KV_HEADS, GQA_DH)
    return (xp.reshape(Bp, Tp, D), xs.reshape(Bs, Ts, D), new_ckv, new_kpe, new_gla,
            new_k.reshape(kv_shape), new_v.reshape(kv_shape))
```

```python
import functools

import jax
import jax.numpy as jnp
from jax import lax
from jax.experimental import pallas as pl
from jax.experimental.pallas import tpu as pltpu

F32 = jnp.float32
BF16 = jnp.bfloat16

EPS = 1e-6
ROPE_THETA = 10000.0
GRID_W = 64
LANES = 128
GLA_HEADS, GLA_DK, GLA_DV = 4, 64, 128
GLA_QK = GLA_HEADS * GLA_DK
GLA_V = GLA_HEADS * GLA_DV
GLA_RANK = 16
GLA_TAU = 16.0
GLA_CHUNK = 64
MLA_HEADS = 8
MLA_Q_RANK, MLA_KV_RANK = 384, 256
MLA_NOPE, MLA_ROPE, MLA_V = 64, 32, 64
MLA_QK = MLA_NOPE + MLA_ROPE
GQA_HEADS, GQA_KV_HEADS, GQA_DH = 16, 4, 64
SUBLANES = 8
VMEM_LIMIT = 56 << 20
FFN_VMEM_LIMIT = 60 << 20

ADALN_TN = 1536
PROJ_TM = 512
FFN_TM = 2048
FFN_TK = 256
FFN_ROW_CHUNK = 512
LATENT_TQ = 256
PROMPT_NB = 4
GLA_GROUP = 4
GLA_EPILOGUE_ROWS = 256


def _cparams(sem, vmem=None):
    return pltpu.CompilerParams(dimension_semantics=sem, vmem_limit_bytes=vmem)


def _nt_dot(a, b):
    return lax.dot_general(a, b, (((1,), (1,)), ((), ())), preferred_element_type=F32)


def _tn_dot(a, b):
    return lax.dot_general(a, b, (((0,), (0,)), ((), ())), preferred_element_type=F32)


def _ds(start, size):
    if isinstance(start, int):
        return pl.ds(start, size)
    return pl.ds(pl.multiple_of(start, size), size)


def _rms(x, g):
    ms = jnp.mean(x * x, axis=-1, keepdims=True)
    return x * lax.rsqrt(ms + EPS) * g


def _silu(x):
    return x * jax.nn.sigmoid(x)


def _rope(x, c, sp, sm, shift):
    return x * c + pltpu.roll(x, LANES - shift, 1) * sp + pltpu.roll(x, shift, 1) * sm


def _adaln_body(c_ref, w_ref, b_ref, o_ref):
    c = c_ref[...]
    s = _silu(c).astype(BF16)
    o_ref[0] = jnp.dot(s, w_ref[0].astype(BF16), preferred_element_type=F32) + b_ref[0]


def _adaln(cond, ada_w, ada_b):
    L, D, E = ada_w.shape
    tn = ADALN_TN
    return pl.pallas_call(
        _adaln_body,
        grid=(L, E // tn),
        in_specs=[pl.BlockSpec((SUBLANES, D), lambda l, n: (0, 0)),
                  pl.BlockSpec((1, D, tn), lambda l, n: (l, 0, n)),
                  pl.BlockSpec((1, 1, tn), lambda l, n: (l, 0, n))],
        out_specs=pl.BlockSpec((1, SUBLANES, tn), lambda l, n: (l, 0, n)),
        out_shape=jax.ShapeDtypeStruct((L, SUBLANES, E), F32),
        compiler_params=_cparams(("arbitrary", "arbitrary")),
        name="adaln",
    )(cond, ada_w, ada_b.reshape(L, 1, E))


def _slot_output(slot_out, batch, tail, dtype, nb=1):
    slot, n_slots, prev = slot_out
    zeros = (0,) * len(tail)
    shape = jax.ShapeDtypeStruct((batch, n_slots) + tail, dtype)
    if slot == 0:
        return pl.BlockSpec((nb, n_slots) + tail, lambda b, *_: (b, 0) + zeros), shape, [], n_slots - 1
    return pl.BlockSpec((nb, 1) + tail, lambda b, *_: (b, slot) + zeros), shape, [prev], 0


def _weight_array(w):
    return w[0] if isinstance(w, tuple) else w


def _weight_spec(w, **kw):
    if isinstance(w, tuple):
        arr, layer = w
        return pl.BlockSpec((None,) + arr.shape[1:], lambda *_: (layer, 0, 0), **kw)
    return pl.BlockSpec(w.shape, lambda *_: (0, 0), **kw)


def _mod_index(mod_base, rows_per_cond, tm):
    if rows_per_cond is None:
        return lambda i: (mod_base, 0, 0)
    n_mod = max(tm // rows_per_cond, 1)
    return lambda i: ((mod_base + (i * tm) // rows_per_cond) // n_mod, 0, 0)


def _inproj_body(xp_ref, xs_ref, g_ref, mod_ref, w_ref, *o_refs, widths, n_prompt_tiles):
    def project(x_ref):
        h = _rms(x_ref[...], g_ref[...])
        h = h * (1.0 + mod_ref[0, 1:2, :]) + mod_ref[0, 0:1, :]
        acc = jnp.dot(h.astype(BF16), w_ref[...].astype(BF16), preferred_element_type=F32)
        off = 0
        for o_ref, w in zip(o_refs, widths):
            o_ref[...] = acc[:, off:off + w].astype(o_ref.dtype)
            off += w

    is_prompt = pl.program_id(0) < n_prompt_tiles
    pl.when(is_prompt)(lambda: project(xp_ref))
    pl.when(jnp.logical_not(is_prompt))(lambda: project(xs_ref))


def _inproj(xp, xs, g, mod, w, outs, *, sample_mod_base, rows_per_cond, tm):
    n_p, d = xp.shape
    n = n_p + xs.shape[0]
    np_t = n_p // tm
    widths = tuple(o[0] for o in outs)

    def mod_idx(i):
        return (jnp.where(i < np_t, 0, sample_mod_base + ((i - np_t) * tm) // rows_per_cond), 0, 0)

    return pl.pallas_call(
        functools.partial(_inproj_body, widths=widths, n_prompt_tiles=np_t),
        grid=(n // tm,),
        in_specs=[pl.BlockSpec((tm, d), lambda i: (jnp.minimum(i, np_t - 1), 0)),
                  pl.BlockSpec((tm, d), lambda i: (jnp.maximum(i - np_t, 0), 0)),
                  pl.BlockSpec((1, d), lambda i: (0, 0)),
                  pl.BlockSpec((1, 6, d), mod_idx),
                  _weight_spec(w)],
        out_specs=[pl.BlockSpec((tm, wd), lambda i: (i, 0)) for wd in widths],
        out_shape=[jax.ShapeDtypeStruct((n, wd), dt) for wd, dt in outs],
        compiler_params=_cparams(("arbitrary",), VMEM_LIMIT),
        name="inproj",
    )(xp, xs, g.reshape(1, d), mod, _weight_array(w))


def _ffn_body(*refs, n_in, nk, row_chunk, rows_per_cond):
    a_refs, wp_refs = refs[:n_in], refs[n_in:2 * n_in]
    x_ref, g_ref, mod_ref, wg_ref, wu_ref, wo_ref, o_ref, h_ref, wgu_ref, wob_ref = refs[2 * n_in:]
    k = pl.program_id(1)
    tm = x_ref.shape[0]
    tk = wg_ref.shape[2]
    n_chunks = tm // row_chunk

    def mod_row(c):
        return 0 if rows_per_cond is None else (c * row_chunk) // rows_per_cond

    def mix_proj(c):
        rows = _ds(c * row_chunk, row_chunk)
        acc = None
        for a_ref, wp_ref in zip(a_refs, wp_refs):
            part = jnp.dot(a_ref[rows, :], wp_ref[...].astype(BF16), preferred_element_type=F32)
            acc = part if acc is None else acc + part
        return acc

    def mix_rows(c, acc):
        rows, m = _ds(c * row_chunk, row_chunk), mod_row(c)
        x_new = x_ref[rows, :] + mod_ref[m, 2:3, :] * acc
        o_ref[rows, :] = x_new
        h = _rms(x_new, g_ref[...])
        h = h * (1.0 + mod_ref[m, 4:5, :]) + mod_ref[m, 3:4, :]
        h_ref[rows, :] = h.astype(BF16)

    def ffn_rows(c):
        rows, m = _ds(c * row_chunk, row_chunk), mod_row(c)
        gu = jnp.dot(h_ref[rows, :], wgu_ref[...], preferred_element_type=F32)
        a = (_silu(gu[:, :tk]) * gu[:, tk:]).astype(BF16)
        o_ref[rows, :] += mod_ref[m, 5:6, :] * jnp.dot(a, wob_ref[...], preferred_element_type=F32)

    def cast_weights():
        wgu_ref[:, :tk] = wg_ref[0].astype(BF16)
        wgu_ref[:, tk:] = wu_ref[0].astype(BF16)
        wob_ref[...] = wo_ref[0].astype(BF16)

    @pl.when(k == 0)
    def _():
        cast_weights()
        accs = [mix_proj(c) for c in range(n_chunks)]
        for c in range(n_chunks):
            mix_rows(c, accs[c])
            ffn_rows(c)

    @pl.when(k > 0)
    def _():
        cast_weights()
        for c in range(n_chunks):
            ffn_rows(c)


def _mix_ffn(acts, wps, x, g, mod, w_in, w_out, layer, *, mod_base, rows_per_cond, tm):
    n, d = x.shape
    hidden = w_out.shape[1]
    tk = FFN_TK
    nk = hidden // tk
    n_in = len(acts)
    n_mod = 1 if rows_per_cond is None else max(tm // rows_per_cond, 1)
    once = dict(pipeline_mode=pl.Buffered(1))
    return pl.pallas_call(
        functools.partial(_ffn_body, n_in=n_in, nk=nk, row_chunk=FFN_ROW_CHUNK,
                          rows_per_cond=rows_per_cond),
        grid=(n // tm, nk),
        in_specs=([pl.BlockSpec((tm, a.shape[1]), lambda i, k: (i, 0)) for a in acts]
                  + [_weight_spec(w, **once) for w in wps]
                  + [pl.BlockSpec((tm, d), lambda i, k: (i, 0)),
                     pl.BlockSpec((1, d), lambda i, k: (0, 0)),
                     pl.BlockSpec((n_mod, 6, d),
                                  (lambda f: (lambda i, k: f(i)))(_mod_index(mod_base, rows_per_cond, tm))),
                     pl.BlockSpec((1, d, tk), lambda i, k: (layer, 0, k)),
                     pl.BlockSpec((1, d, tk), lambda i, k: (layer, 0, nk + k)),
                     pl.BlockSpec((1, tk, d), lambda i, k: (layer, k, 0))]),
        out_specs=pl.BlockSpec((tm, d), lambda i, k: (i, 0)),
        out_shape=jax.ShapeDtypeStruct((n, d), F32),
        scratch_shapes=[pltpu.VMEM((tm, d), BF16),
                        pltpu.VMEM((d, 2 * tk), BF16),
                        pltpu.VMEM((tk, d), BF16)],
        compiler_params=_cparams(("arbitrary", "arbitrary"), FFN_VMEM_LIMIT),
        name="mix_ffn",
    )(*acts, *[_weight_array(w) for w in wps], x, g.reshape(1, d), mod, w_in, w_in, w_out)


def _log_sigmoid(x):
    return jnp.minimum(x, 0.0) - jnp.log(1.0 + jnp.exp(-jnp.abs(x)))


def _gla_body(*refs, T, has_s0, want_state, n_alias, n_zero_slots, nb):
    refs = refs[n_alias:]
    qkvr_ref, alk_ref, w2_ref, ab_ref, og_ref = refs[:5]
    pos = 5
    s0_ref = None
    if has_s0:
        s0_ref = refs[pos]
        pos += 1
    o_ref = refs[pos]
    pos += 1
    st_out_ref = kpe_out_ref = None
    if want_state:
        st_out_ref, kpe_out_ref = refs[pos:pos + 2]
        pos += 2
    osc_ref, st_ref, la_ref, qin_ref, kst_ref, dec_ref, upd_ref, stq_ref = refs[pos:pos + 8]

    C = GLA_CHUNK
    nc = T // C

    logit = jnp.dot(alk_ref[...].astype(BF16), w2_ref[...], preferred_element_type=F32) + ab_ref[...]
    la_ref[...] = _log_sigmoid(logit) * (1.0 / GLA_TAU)

    for s in range(nb):
        for d in range(2):
            if has_s0:
                st_ref[2 * s + d] = jnp.concatenate([s0_ref[s, d, h] for h in range(GLA_HEADS)], axis=0).T
            else:
                st_ref[2 * s + d] = jnp.zeros((GLA_DV, GLA_QK), F32)

    r64 = lax.broadcasted_iota(jnp.int32, (C, C), 0)
    c64 = lax.broadcasted_iota(jnp.int32, (C, C), 1)
    tri_f = (r64 >= c64).astype(BF16)
    tri_b = (c64 >= r64).astype(BF16)
    t_idx = lax.broadcasted_iota(jnp.int32, (C, GLA_QK), 0)
    s_idx = lax.broadcasted_iota(jnp.int32, (C, GLA_QK), 1) % C
    causal_f = t_idx >= s_idx
    causal_b = t_idx <= s_idx
    bm_k = (lax.broadcasted_iota(jnp.int32, (GLA_QK, GLA_QK), 0) // GLA_DK
            == lax.broadcasted_iota(jnp.int32, (GLA_QK, GLA_QK), 1) // GLA_DK)
    bm_v = (lax.broadcasted_iota(jnp.int32, (GLA_QK, GLA_V), 0) // C
            == lax.broadcasted_iota(jnp.int32, (GLA_QK, GLA_V), 1) // GLA_DV)
    head_lanes = [lax.broadcasted_iota(jnp.int32, (1, GLA_QK), 1) // GLA_DK == h for h in range(GLA_HEADS)]
    directions = ((tri_f, causal_f, C // 2 - 1, C - 1), (tri_b, causal_b, C // 2, 0))

    def v_rows(rows):
        return qkvr_ref[rows, 2 * GLA_QK:2 * GLA_QK + GLA_V]

    G = GLA_GROUP
    units = [(i, d) for i in range(G) for d in range(2)]

    n_groups = nb * nc // G

    def for_groups(fn):
        if n_groups <= 4:
            for g in range(n_groups):
                fn(g * G)
        else:
            def body(g, carry):
                fn(g * G)
                return carry
            lax.fori_loop(0, n_groups, body, 0)

    def intra_group(n0):
        rows = [_ds((n0 + i) * C, C) for i in range(G)]
        b = {}
        for i, d in units:
            b[i, d] = _split_dot(la_ref[rows[i], d * GLA_QK:(d + 1) * GLA_QK], directions[d][0], w_left=True)
        q_loc, k_bd = {}, {}
        for i in range(G):
            qc = qkvr_ref[rows[i], 0:GLA_QK].astype(F32) * (GLA_DK ** -0.5)
            kc = qkvr_ref[rows[i], GLA_QK:2 * GLA_QK].astype(F32)
            for d in range(2):
                _, _, ref_row, last_row = directions[d]
                bb = b[i, d]
                b_ref = bb[ref_row:ref_row + 1, :]
                b_last = bb[last_row:last_row + 1, :]
                q_loc[i, d] = (qc * jnp.exp(bb - b_ref)).astype(BF16)
                k_loc = kc * jnp.exp(b_ref - bb)
                k_bd[i, d] = jnp.where(bm_k, jnp.concatenate([k_loc] * GLA_HEADS, axis=0), 0.0).astype(BF16)
                qin_ref[d, rows[i], :] = (qc * jnp.exp(bb)).astype(BF16)
                kst_ref[d, rows[i], :] = (kc * jnp.exp(b_last - bb)).astype(BF16)
                dec_ref[d, _ds((n0 + i) * SUBLANES, SUBLANES), :] = jnp.broadcast_to(jnp.exp(b_last),
                                                                                      (SUBLANES, GLA_QK))
        a = {u: jnp.where(directions[u[1]][1], _nt_dot(q_loc[u], k_bd[u]), 0.0).astype(BF16) for u in units}
        v_bd = [jnp.where(bm_v, jnp.concatenate([v_rows(rows[i])] * GLA_HEADS, axis=0), jnp.zeros((), BF16))
                for i in range(G)]
        o = {u: jnp.dot(a[u], v_bd[u[0]], preferred_element_type=F32) for u in units}
        for i in range(G):
            osc_ref[rows[i], :] = o[i, 0] + o[i, 1]

    def update_group(n0):
        rows = [_ds((n0 + i) * C, C) for i in range(G)]
        upd = {u: _tn_dot(v_rows(rows[u[0]]), kst_ref[u[1], rows[u[0]], :]) for u in units}
        for i, d in units:
            acc = None
            for h, m in enumerate(head_lanes):
                part = jnp.where(m, upd[i, d][h * GLA_DV:(h + 1) * GLA_DV, :], 0.0)
                acc = part if acc is None else acc + part
            upd_ref[d, n0 + i] = acc

    def scan_step(i, carry):
        for s in range(nb):
            for d, n in ((0, s * nc + i), (1, s * nc + nc - 1 - i)):
                st = st_ref[2 * s + d]
                stq_ref[d, n] = st.astype(BF16)
                st_ref[2 * s + d] = st * dec_ref[d, pl.ds(n * SUBLANES, 1), :] + upd_ref[d, n]
        return carry

    def readout_group(n0):
        rows = [_ds((n0 + i) * C, C) for i in range(G)]
        o = {}
        for i, d in units:
            q_in = qin_ref[d, rows[i], :]
            q_heads = jnp.concatenate([jnp.where(m, q_in, jnp.zeros((), BF16)) for m in head_lanes], axis=0)
            o[i, d] = _nt_dot(q_heads, stq_ref[d, n0 + i])
        for i in range(G):
            both = o[i, 0] + o[i, 1]
            for h in range(GLA_HEADS):
                osc_ref[rows[i], h * GLA_DV:(h + 1) * GLA_DV] += both[h * C:(h + 1) * C, :]

    for_groups(intra_group)
    for_groups(update_group)
    if nc == G:
        for i in range(nc):
            scan_step(i, 0)
    else:
        lax.fori_loop(0, nc, scan_step, 0)
    for_groups(readout_group)

    rb = GLA_EPILOGUE_ROWS
    for i in range(nb * T // rb):
        rows = slice(i * rb, (i + 1) * rb)
        for h in range(GLA_HEADS):
            cols = slice(h * GLA_DV, (h + 1) * GLA_DV)
            o = _rms(osc_ref[rows, cols], og_ref[:, cols])
            r = qkvr_ref[rows, 2 * GLA_QK + GLA_V + h * GLA_DV:2 * GLA_QK + GLA_V + (h + 1) * GLA_DV].astype(F32)
            o_ref[rows, cols] = (o * _silu(r)).astype(o_ref.dtype)

    if want_state:
        for s in range(nb):
            for d in range(2):
                s_all = st_ref[2 * s + d].T
                for h in range(GLA_HEADS):
                    st_out_ref[s, 0, d, h] = s_all[h * GLA_DK:(h + 1) * GLA_DK, :]
            kpe_out_ref[s, 0] = alk_ref[s * T:(s + 1) * T, 0:MLA_ROPE]
            if n_zero_slots:
                st_out_ref[s, 1:] = jnp.zeros((n_zero_slots, 2, GLA_HEADS, GLA_DK, GLA_DV), F32)
                kpe_out_ref[s, 1:] = jnp.zeros((n_zero_slots, T, MLA_ROPE), F32)


def _gla(proj, alk, w2bd, a_b, out_g, s0, *, B, T, row0=0, state_out=None, nb=1):
    has_s0 = s0 is not None
    want_state = state_out is not None
    n_qkvr = 2 * GLA_QK + 2 * GLA_V
    R = nb * T
    n_chunks = R // GLA_CHUNK
    off = row0 // R
    assert row0 % R == 0
    in_specs = [pl.BlockSpec((R, n_qkvr), lambda b: (off + b, 0)),
                pl.BlockSpec((R, LANES), lambda b: (off + b, 0)),
                pl.BlockSpec(w2bd.shape, lambda b: (0, 0)),
                pl.BlockSpec((1, 2 * GLA_QK), lambda b: (0, 0)),
                pl.BlockSpec((1, GLA_V), lambda b: (0, 0))]
    args = [proj, alk, w2bd, a_b, out_g]
    if has_s0:
        states, layer = s0
        in_specs.append(pl.BlockSpec((nb, None, 2, GLA_HEADS, GLA_DK, GLA_DV),
                                     lambda b: (b, layer, 0, 0, 0, 0)))
        args.append(states)
    out_specs = [pl.BlockSpec((R, GLA_V), lambda b: (b, 0))]
    out_shape = [jax.ShapeDtypeStruct((B * T, GLA_V), BF16)]
    aliased, n_zero = [], 0
    if want_state:
        slot, n_slots, prev_state, prev_kpe = state_out
        for prev, tail in ((prev_state, (2, GLA_HEADS, GLA_DK, GLA_DV)), (prev_kpe, (T, MLA_ROPE))):
            spec, shape, al, n_zero = _slot_output((slot, n_slots, prev), B, tail, F32, nb)
            out_specs.append(spec)
            out_shape.append(shape)
            aliased += al
    res = pl.pallas_call(
        functools.partial(_gla_body, T=T, has_s0=has_s0, want_state=want_state, n_alias=len(aliased),
                          n_zero_slots=n_zero, nb=nb),
        grid=(B // nb,),
        in_specs=[pl.BlockSpec(memory_space=pl.ANY)] * len(aliased) + in_specs,
        out_specs=out_specs,
        out_shape=out_shape,
        input_output_aliases={i: 1 + i for i in range(len(aliased))},
        scratch_shapes=[pltpu.VMEM((R, GLA_V), F32),
                        pltpu.VMEM((2 * nb, GLA_DV, GLA_QK), F32),
                        pltpu.VMEM((R, 2 * GLA_QK), F32),
                        pltpu.VMEM((2, R, GLA_QK), BF16),
                        pltpu.VMEM((2, R, GLA_QK), BF16),
                        pltpu.VMEM((2, SUBLANES * n_chunks, GLA_QK), F32),
                        pltpu.VMEM((2, n_chunks, GLA_DV, GLA_QK), F32),
                        pltpu.VMEM((2, n_chunks, GLA_DV, GLA_QK), BF16)],
        compiler_params=_cparams(("arbitrary",), VMEM_LIMIT),
        name="gla",
    )(*aliased, *args)
    return res if want_state else (res[0], None)


LOG2E = 1.4426950408889634


V_ROWS = 80


def _attend(qs, ks, vs, group, key_major_dv=None):
    key_major = key_major_dv is not None
    axis = 0 if key_major else -1

    def qk(g0):
        return [_nt_dot(k(), q) if key_major else _nt_dot(q, k())
                for q, k in zip(qs[g0:g0 + group], ks[g0:g0 + group])]

    outs = []
    scores = qk(0)
    for g0 in range(0, len(qs), group):
        probs = [jnp.exp2(s - jnp.max(s, axis=axis, keepdims=True)).astype(BF16) for s in scores]
        if g0 + group < len(qs):
            scores = qk(g0 + group)
        if key_major:
            dv = key_major_dv
            res = [jnp.dot(v(), p, preferred_element_type=F32) for p, v in zip(probs, vs[g0:g0 + group])]
            outs += [r[:dv, :] / r[dv:dv + 1, :] for r in res]
        else:
            res = [jnp.dot(p, v(), preferred_element_type=F32) for p, v in zip(probs, vs[g0:g0 + group])]
            outs += [r[:, :LANES] / r[:, LANES:] for r in res]
    return outs


def _head_group(n_keys, n_heads):
    return 4 if n_keys <= 512 else 2


def _stacked_ms(xs, ones_mat, inv_n):
    n = xs[0].shape[0]
    sq = jnp.concatenate([x * x for x in xs], axis=0).astype(BF16)
    ms = jnp.dot(sq, ones_mat, preferred_element_type=F32) * inv_n
    return [ms[i * n:(i + 1) * n] for i in range(len(xs))]


def _split_dot(x, w, w_left=False):
    hi = x.astype(BF16)
    lo = (x - hi.astype(F32)).astype(BF16)
    if w_left:
        return jnp.dot(w, hi, preferred_element_type=F32) + jnp.dot(w, lo, preferred_element_type=F32)
    return jnp.dot(hi, w, preferred_element_type=F32) + jnp.dot(lo, w, preferred_element_type=F32)


def _group_ones(group):
    r = lax.broadcasted_iota(jnp.int32, (LANES, LANES), 0) // group
    c = lax.broadcasted_iota(jnp.int32, (LANES, LANES), 1) // group
    return (r == c).astype(BF16)


def _mla_body(*refs, T, S_ctx, rope, want_ckv, group, n_alias, n_zero_slots, nb):
    refs = refs[n_alias:]
    cq_ref, kpe_ref, ckv_ref = refs[:3]
    pos = 3
    if S_ctx:
        ckvc_ref, kpec_ref = refs[pos:pos + 2]
        pos += 2
    if rope:
        rq_refs = refs[pos:pos + 3]
        rk_refs = refs[pos + 3:pos + 6]
        pos += 6
    qng_ref, wqb_ref, qg_ref, kvg_ref, wkvb_ref, kg_ref = refs[pos:pos + 6]
    pos += 6
    o_ref = refs[pos]
    pos += 1
    if want_ckv:
        ckvn_ref = refs[pos]
        pos += 1
    k_sc, vlo_sc, vhi_sc = refs[pos:pos + 3]
    lo = lax.broadcasted_iota(jnp.int32, (1, LANES), 1) < MLA_V

    H = MLA_HEADS
    n_k = H * LANES
    def head_ms(xs):
        return [jnp.sum(x * x, axis=-1, keepdims=True) * (1.0 / MLA_QK) for x in xs]

    def fill_kv(ckvn, kpe, row0, n_rows, with_rope):
        rows = slice(row0, row0 + n_rows)
        kv = jnp.dot(ckvn.astype(BF16), wkvb_ref[...], preferred_element_type=F32)
        khs = [kv[:, h * LANES:(h + 1) * LANES] + kpe for h in range(H)]
        for h, (kh, ms) in enumerate(zip(khs, head_ms(khs))):
            kh = kh * lax.rsqrt(ms + EPS) * kg_ref[...]
            if with_rope:
                kh = _rope(kh, rk_refs[0][...], rk_refs[1][...], rk_refs[2][...], MLA_ROPE // 4)
            k_sc[rows, h * LANES:(h + 1) * LANES] = kh.astype(BF16)
        ones = jnp.ones((n_rows, LANES), BF16)
        for p in range(H // 2):
            v = kv[:, n_k + p * LANES:n_k + (p + 1) * LANES]
            vlo_sc[rows, 2 * p * LANES:(2 * p + 1) * LANES] = jnp.where(lo, v, 0.0).astype(BF16)
            vhi_sc[rows, 2 * p * LANES:(2 * p + 1) * LANES] = jnp.where(lo, 0.0, v).astype(BF16)
            vlo_sc[rows, (2 * p + 1) * LANES:(2 * p + 2) * LANES] = ones
            vhi_sc[rows, (2 * p + 1) * LANES:(2 * p + 2) * LANES] = ones

    S = S_ctx + T
    tq = cq_ref.shape[0] // nb

    @pl.when(pl.program_id(1) == 0)
    def _():
        for s in range(nb):
            if S_ctx:
                fill_kv(ckvc_ref[s], kpec_ref[s], s * S, S_ctx, False)
            ckvn = _rms(ckv_ref[s * T:(s + 1) * T, :], kvg_ref[...])
            if want_ckv:
                ckvn_ref[s, 0] = ckvn
                if n_zero_slots:
                    ckvn_ref[s, 1:] = jnp.zeros((n_zero_slots, T, MLA_KV_RANK), F32)
            fill_kv(ckvn, kpe_ref[s * T:(s + 1) * T, :].astype(F32), s * S + S_ctx, T, rope)

    cqn = _rms(cq_ref[...].astype(F32), qng_ref[...])
    q = jnp.dot(cqn.astype(BF16), wqb_ref[...], preferred_element_type=F32)
    units = [(s, h) for s in range(nb) for h in range(H)]
    qhs = [q[s * tq:(s + 1) * tq, h * LANES:(h + 1) * LANES] for s, h in units]
    qs = []
    for qh, ms in zip(qhs, head_ms(qhs)):
        qh = qh * lax.rsqrt(ms + EPS) * qg_ref[...]
        if rope:
            qh = _rope(qh, rq_refs[0][...], rq_refs[1][...], rq_refs[2][...], MLA_ROPE // 4)
        qs.append(qh.astype(BF16))
    ks = [functools.partial(lambda s, h: k_sc[s * S:(s + 1) * S, h * LANES:(h + 1) * LANES], s, h)
          for s, h in units]
    vs = [functools.partial(lambda s, h: (vlo_sc, vhi_sc)[h % 2][s * S:(s + 1) * S,
                                                                (h // 2) * 2 * LANES:(h // 2 + 1) * 2 * LANES], s, h)
          for s, h in units]
    outs = _attend(qs, ks, vs, group)
    for s in range(nb):
        for p in range(H // 2):
            o_pair = outs[s * H + 2 * p] + outs[s * H + 2 * p + 1]
            o_ref[s * tq:(s + 1) * tq, p * LANES:(p + 1) * LANES] = o_pair.astype(o_ref.dtype)


def _mla(proj, ckv, ctx, rope_tabs, params, *, B, T, tq, row0=0, ckv_out=None, nb=1):
    want_ckv = ckv_out is not None
    nq = T // tq
    assert nb == 1 or nq == 1
    S_ctx = ctx[0].shape[2] if ctx is not None else 0
    S = S_ctx + T
    rope = rope_tabs is not None
    cq_blk = (2 * GLA_QK + 2 * GLA_V) // MLA_Q_RANK
    kpe_blk = (2 * GLA_QK + 2 * GLA_V + MLA_Q_RANK) // LANES
    off_q, off_t = row0 // (nb * tq), row0 // (nb * T)
    assert row0 % (nb * T) == 0
    in_specs = [pl.BlockSpec((nb * tq, MLA_Q_RANK), lambda b, j: (off_q + b * nq + j, cq_blk)),
                pl.BlockSpec((nb * T, LANES), lambda b, j: (off_t + b, kpe_blk)),
                pl.BlockSpec((nb * T, MLA_KV_RANK), lambda b, j: (off_t + b, 0))]
    args = [proj, proj, ckv]
    if S_ctx:
        layer = ctx[2]
        in_specs += [pl.BlockSpec((nb, None, S_ctx, MLA_KV_RANK), lambda b, j: (b, layer, 0, 0)),
                     pl.BlockSpec((nb, None, S_ctx, LANES), lambda b, j: (b, layer, 0, 0))]
        args += list(ctx[:2])
    if rope:
        in_specs += [pl.BlockSpec((tq, LANES), lambda b, j: (j, 0))] * 3
        in_specs += [pl.BlockSpec((T, LANES), lambda b, j: (0, 0))] * 3
        args += list(rope_tabs) * 2
    in_specs += [pl.BlockSpec(p.shape, lambda b, j: (0, 0)) for p in params]
    args += list(params)
    out_specs = [pl.BlockSpec((nb * tq, MLA_HEADS * MLA_V), lambda b, j: (b * nq + j, 0))]
    out_shape = [jax.ShapeDtypeStruct((B * T, MLA_HEADS * MLA_V), BF16)]
    aliased, n_zero = [], 0
    if want_ckv:
        spec, shape, aliased, n_zero = _slot_output(ckv_out, B, (T, MLA_KV_RANK), F32, nb)
        out_specs.append(spec)
        out_shape.append(shape)
    res = pl.pallas_call(
        functools.partial(_mla_body, T=T, S_ctx=S_ctx, rope=rope, want_ckv=want_ckv,
                          group=_head_group(S, MLA_HEADS), n_alias=len(aliased), n_zero_slots=n_zero, nb=nb),
        grid=(B // nb, nq),
        in_specs=[pl.BlockSpec(memory_space=pl.ANY)] * len(aliased) + in_specs,
        out_specs=out_specs,
        out_shape=out_shape,
        input_output_aliases={i: 1 + i for i in range(len(aliased))},
        scratch_shapes=[pltpu.VMEM((nb * S, MLA_HEADS * LANES), BF16)] * 3,
        compiler_params=_cparams(("arbitrary", "arbitrary"), VMEM_LIMIT),
        name="mla",
    )(*aliased, *args)
    return res if want_ckv else (res[0], None)


def _gqa_body(*refs, T, S_ctx, rope, want_kv, group, n_alias, n_zero_slots, nb):
    refs = refs[n_alias:]
    q_ref, k_ref, v_ref = refs[:3]
    pos = 3
    if S_ctx:
        kc_ref, vc_ref = refs[pos:pos + 2]
        pos += 2
    if rope:
        rq_refs = refs[pos:pos + 3]
        rk_refs = refs[pos + 3:pos + 6]
        pos += 6
    qg_ref, kg_ref = refs[pos:pos + 2]
    pos += 2
    o_ref = refs[pos]
    pos += 1
    if want_kv:
        kn_ref, vo_ref = refs[pos:pos + 2]
        pos += 2
    klo_sc, khi_sc, vt_sc = refs[pos:pos + 3]

    lo = lax.broadcasted_iota(jnp.int32, (1, LANES), 1) < GQA_DH
    half_sum = _group_ones(GQA_DH)

    def head_ms(x):
        return _split_dot(x * x, half_sum) * (1.0 / GQA_DH)

    def scatter_halves(x, lo_sc, hi_sc, c, rows):
        rolled = pltpu.roll(x, GQA_DH, 1)
        lo_sc[2 * c, rows, :LANES] = jnp.where(lo, x, 0.0).astype(BF16)
        hi_sc[2 * c, rows, :LANES] = jnp.where(lo, 0.0, rolled).astype(BF16)
        lo_sc[2 * c + 1, rows, :LANES] = jnp.where(lo, rolled, 0.0).astype(BF16)
        hi_sc[2 * c + 1, rows, :LANES] = jnp.where(lo, 0.0, x).astype(BF16)

    def fill_vt(x, c, rows):
        x_t = x.T
        for half in range(2):
            vt_sc[2 * c + half, 0:GQA_DH, rows] = x_t[half * GQA_DH:(half + 1) * GQA_DH, :].astype(BF16)
            vt_sc[2 * c + half, GQA_DH:, rows] = jnp.ones((V_ROWS - GQA_DH, rows.stop - rows.start), BF16)

    S = S_ctx + T
    tq = q_ref.shape[0] // nb

    @pl.when(pl.program_id(1) == 0)
    def _():
        for s in range(nb):
            if want_kv and n_zero_slots:
                zeros = jnp.zeros((n_zero_slots, T, GQA_KV_HEADS * GQA_DH), F32)
                kn_ref[s, 1:] = zeros
                vo_ref[s, 1:] = zeros
            for c in range(GQA_KV_HEADS // 2):
                cols = slice(c * LANES, (c + 1) * LANES)
                if S_ctx:
                    scatter_halves(kc_ref[s, :, cols], klo_sc, khi_sc, c, slice(s * S, s * S + S_ctx))
                    fill_vt(vc_ref[s, :, cols], c, slice(s * S, s * S + S_ctx))
                kx = k_ref[s * T:(s + 1) * T, cols]
                kn = kx * lax.rsqrt(head_ms(kx) + EPS) * kg_ref[...]
                vx = v_ref[s * T:(s + 1) * T, cols]
                if want_kv:
                    kn_ref[s, 0, :, cols] = kn
                    vo_ref[s, 0, :, cols] = vx
                if rope:
                    kn = _rope(kn, rk_refs[0][...], rk_refs[1][...], rk_refs[2][...], GQA_DH // 4)
                scatter_halves(kn, klo_sc, khi_sc, c, slice(s * S + S_ctx, (s + 1) * S))
                fill_vt(vx, c, slice(s * S + S_ctx, (s + 1) * S))

    n_pairs = GQA_HEADS // 2
    units = [(s, p) for s in range(nb) for p in range(n_pairs)]
    qxs = [q_ref[s * tq:(s + 1) * tq, p * LANES:(p + 1) * LANES].astype(F32) for s, p in units]
    qs, ks, vs = [], [], []
    for (s, p), qx, ms in zip(units, qxs, _stacked_ms(qxs, half_sum, 1.0 / GQA_DH)):
        qn = qx * lax.rsqrt(ms + EPS) * qg_ref[...]
        if rope:
            qn = _rope(qn, rq_refs[0][...], rq_refs[1][...], rq_refs[2][...], GQA_DH // 4)
        g = p // 2
        qs += [qn.astype(BF16)] * 2
        ks += [functools.partial(lambda r, g, s: r[g, s * S:(s + 1) * S, :], r, g, s) for r in (klo_sc, khi_sc)]
        vs += [functools.partial(lambda g, s: vt_sc[g, :, s * S:(s + 1) * S], g, s)] * 2
    outs = _attend(qs, ks, vs, group, key_major_dv=GQA_DH)
    for s in range(nb):
        o_t = jnp.concatenate(outs[s * GQA_HEADS:(s + 1) * GQA_HEADS], axis=0)
        o_ref[s * tq:(s + 1) * tq, :] = o_t.T.astype(o_ref.dtype)


def _gqa(q, k, v, ctx, rope_tabs, params, *, B, T, tq, row0=0, kv_out=None, nb=1):
    want_kv = kv_out is not None
    nq = T // tq
    assert nb == 1 or nq == 1
    S_ctx = ctx[0].shape[2] if ctx is not None else 0
    S = S_ctx + T
    rope = rope_tabs is not None
    n_q = GQA_HEADS * GQA_DH
    n_kv = GQA_KV_HEADS * GQA_DH
    off_q, off_t = row0 // (nb * tq), row0 // (nb * T)
    assert row0 % (nb * T) == 0
    in_specs = [pl.BlockSpec((nb * tq, n_q), lambda b, j: (off_q + b * nq + j, 0)),
                pl.BlockSpec((nb * T, n_kv), lambda b, j: (off_t + b, 0)),
                pl.BlockSpec((nb * T, n_kv), lambda b, j: (off_t + b, 0))]
    args = [q, k, v]
    if S_ctx:
        layer = ctx[2]
        in_specs += [pl.BlockSpec((nb, None, S_ctx, n_kv), lambda b, j: (b, layer, 0, 0))] * 2
        args += list(ctx[:2])
    if rope:
        in_specs += [pl.BlockSpec((tq, LANES), lambda b, j: (j, 0))] * 3
        in_specs += [pl.BlockSpec((T, LANES), lambda b, j: (0, 0))] * 3
        args += list(rope_tabs) * 2
    in_specs += [pl.BlockSpec(p.shape, lambda b, j: (0, 0)) for p in params]
    args += list(params)
    out_specs = [pl.BlockSpec((nb * tq, n_q), lambda b, j: (b * nq + j, 0))]
    out_shape = [jax.ShapeDtypeStruct((B * T, n_q), BF16)]
    aliased, n_zero = [], 0
    if want_kv:
        slot, n_slots, prev_k, prev_v = kv_out
        for prev in (prev_k, prev_v):
            spec, shape, al, n_zero = _slot_output((slot, n_slots, prev), B, (T, n_kv), F32, nb)
            out_specs.append(spec)
            out_shape.append(shape)
            aliased += al
    res = pl.pallas_call(
        functools.partial(_gqa_body, T=T, S_ctx=S_ctx, rope=rope, want_kv=want_kv,
                          group=_head_group(S, GQA_HEADS), n_alias=len(aliased), n_zero_slots=n_zero, nb=nb),
        grid=(B // nb, nq),
        in_specs=[pl.BlockSpec(memory_space=pl.ANY)] * len(aliased) + in_specs,
        out_specs=out_specs,
        out_shape=out_shape,
        input_output_aliases={i: 1 + i for i in range(len(aliased))},
        scratch_shapes=[pltpu.VMEM((GQA_KV_HEADS, nb * S, LANES), BF16),
                        pltpu.VMEM((GQA_KV_HEADS, nb * S, LANES), BF16),
                        pltpu.VMEM((GQA_KV_HEADS, V_ROWS, nb * S), BF16)],
        compiler_params=_cparams(("arbitrary", "arbitrary"), VMEM_LIMIT),
        name="gqa",
    )(*aliased, *args)
    return res if want_kv else (res[0], None, None)


def _rope_tables(n_tok, d_rot, lead, reps):
    t = jnp.arange(n_tok, dtype=jnp.int32)
    posn = jnp.stack([t // GRID_W, t % GRID_W], axis=-1).astype(F32)
    quarter = d_rot // 4
    inv = jnp.power(ROPE_THETA, -jnp.arange(quarter, dtype=F32) / quarter)
    ang = posn[:, :, None] * inv
    cos, sin = jnp.cos(ang), jnp.sin(ang)
    zero = jnp.zeros_like(sin)
    c_rot = jnp.stack([cos, cos], axis=2).reshape(n_tok, d_rot)
    sp_rot = jnp.stack([-sin, zero], axis=2).reshape(n_tok, d_rot)
    sm_rot = jnp.stack([zero, sin], axis=2).reshape(n_tok, d_rot)
    tail = LANES - lead - reps * d_rot

    def embed(rot, fill):
        parts = [jnp.full((n_tok, lead), fill, F32)] + [rot] * reps + [jnp.full((n_tok, tail), fill, F32)]
        return jnp.concatenate(parts, axis=1)

    return embed(c_rot, 1.0), embed(sp_rot, 0.0), embed(sm_rot, 0.0)


def _pad_lanes(x, lead, width=LANES):
    pad = [(0, 0)] * (x.ndim - 1) + [(lead, width - lead - x.shape[-1])]
    return jnp.pad(x, pad)


def _ab_params(w_in, w_out, a_w2, a_b, out_g, q_norm_g, w_qb, kv_norm_g, w_kvb, qn_g, kn_g):
    d = w_in.shape[0]
    o_alo = 2 * GLA_QK + 2 * GLA_V
    o_cq = o_alo + 2 * GLA_RANK
    o_ckv = o_cq + MLA_Q_RANK
    o_kpe = o_ckv + MLA_KV_RANK
    w_kpe = w_in[:, o_kpe:]
    w_perm = jnp.concatenate([
        w_in[:, :o_alo], w_in[:, o_cq:o_ckv], _pad_lanes(w_kpe, MLA_NOPE),
        w_in[:, o_ckv:o_kpe],
        _pad_lanes(jnp.concatenate([w_kpe, w_in[:, o_alo:o_cq]], axis=1), 0),
    ], axis=1).astype(BF16)
    w2bd = jnp.zeros((LANES, 2 * GLA_QK), F32)
    w2bd = (w2bd.at[MLA_ROPE:MLA_ROPE + GLA_RANK, :GLA_QK].set(a_w2[0])
            .at[MLA_ROPE + GLA_RANK:MLA_ROPE + 2 * GLA_RANK, GLA_QK:].set(a_w2[1]))
    w_qb_p = _pad_lanes(w_qb.reshape(MLA_Q_RANK, MLA_HEADS, MLA_QK), 0).reshape(MLA_Q_RANK, MLA_HEADS * LANES)
    kvb = w_kvb.reshape(MLA_KV_RANK, MLA_HEADS, MLA_NOPE + MLA_V)
    w_kvb_p = jnp.concatenate([
        _pad_lanes(kvb[:, :, :MLA_NOPE], 0).reshape(MLA_KV_RANK, MLA_HEADS * LANES),
        kvb[:, :, MLA_NOPE:].reshape(MLA_KV_RANK, MLA_HEADS * MLA_V)], axis=1)
    return dict(
        w_perm=w_perm,
        w2bd=w2bd.astype(BF16),
        a_b=a_b.reshape(1, 2 * GLA_QK),
        out_g=jnp.tile(out_g, GLA_HEADS).reshape(1, GLA_V),
        mla=(q_norm_g.reshape(1, -1), w_qb_p.astype(BF16),
             _pad_lanes(qn_g * (MLA_QK ** -0.5 * LOG2E), 0).reshape(1, LANES),
             kv_norm_g.reshape(1, -1), w_kvb_p.astype(BF16),
             _pad_lanes(kn_g, 0).reshape(1, LANES)),
        w_out_gla=w_out[:GLA_V].astype(BF16),
        w_out_mla=w_out[GLA_V:].astype(BF16),
    )


AB_OUTS = ((2 * GLA_QK + 2 * GLA_V + MLA_Q_RANK + LANES, BF16), (MLA_KV_RANK, F32), (LANES, F32))
C_OUTS = ((GQA_HEADS * GQA_DH, BF16), (GQA_KV_HEADS * GQA_DH, F32), (GQA_KV_HEADS * GQA_DH, F32))


def kernel(x_prompt, x_sample, c, cache_mla_ckv, cache_mla_kpe, state_gla, cache_gqa_k, cache_gqa_v,
           c_ctx, ada_w, ada_b, norm_mix_g, norm_ffn_g, ffn_w_in, ffn_w_out, ab_w_in, ab_w_out,
           gla_a_w2, gla_a_b, gla_out_g, mla_q_norm_g, mla_w_qb, mla_kv_norm_g, mla_w_kvb, mla_qn_g,
           mla_kn_g, gqa_w_in, gqa_w_out, gqa_qn_g, gqa_kn_g):
    Bp, Tp, D = x_prompt.shape
    Bs, Ts, _ = x_sample.shape
    depth = ada_w.shape[0]
    xp = x_prompt.reshape(Bp * Tp, D)
    xs = x_sample.reshape(Bs * Ts, D)

    assert 2 + Bs <= SUBLANES
    cond = jnp.zeros((SUBLANES, D), F32).at[0].set(c_ctx).at[2:2 + Bs].set(c)
    mods = _adaln(cond, ada_w, ada_b).reshape(depth, SUBLANES, 6, D)

    rope_mla = _rope_tables(Ts, MLA_ROPE, MLA_NOPE, 1)
    rope_gqa = _rope_tables(Ts, GQA_DH, 0, 2)

    streams = (dict(mod_base=0, rows_per_cond=None), dict(mod_base=2, rows_per_cond=Ts))
    proj_tm, ffn_tm, prompt_nb = PROJ_TM, FFN_TM, PROMPT_NB
    n_p = Bp * Tp

    n_ab, n_c = (depth + 1) // 2, depth // 2
    n_kv = GQA_KV_HEADS * GQA_DH
    kpe_cache = _pad_lanes(cache_mla_kpe, MLA_NOPE)
    gqa_k_cache = cache_gqa_k.reshape(Bs, n_c, -1, n_kv)
    gqa_v_cache = cache_gqa_v.reshape(Bs, n_c, -1, n_kv)
    new_ckv = new_kpe = new_gla = new_k = new_v = None
    for l in range(depth):
        i = l // 2
        mod = mods[l]
        proj = functools.partial(_inproj, xp, xs, norm_mix_g[l], mod, sample_mod_base=2, rows_per_cond=Ts,
                                 tm=proj_tm)
        if l % 2 == 0:
            P = _ab_params(ab_w_in[i], ab_w_out[i], gla_a_w2[i], gla_a_b[i], gla_out_g[i], mla_q_norm_g[i],
                           mla_w_qb[i], mla_kv_norm_g[i], mla_w_kvb[i], mla_qn_g[i], mla_kn_g[i])
            pj, ckv, alk = proj(P['w_perm'], AB_OUTS)
            gla = functools.partial(_gla, pj, alk, P['w2bd'], P['a_b'], P['out_g'])
            og_p, new_gla, new_kpe = gla(None, B=Bp, T=Tp, state_out=(i, n_ab, new_gla, new_kpe), nb=prompt_nb)
            og_s, _ = gla((state_gla, i), B=Bs, T=Ts, row0=n_p)
            om_p, new_ckv = _mla(pj, ckv, None, None, P['mla'], B=Bp, T=Tp, tq=Tp,
                                 ckv_out=(i, n_ab, new_ckv), nb=prompt_nb // 2)
            om_s, _ = _mla(pj, ckv, (cache_mla_ckv, kpe_cache, i), rope_mla, P['mla'], B=Bs, T=Ts,
                           tq=LATENT_TQ, row0=n_p)
            wps = (P['w_out_gla'], P['w_out_mla'])
            acts_p, acts_s = (og_p, om_p), (og_s, om_s)
        else:
            w_in, w_out = (gqa_w_in, i), (gqa_w_out, i)
            gp = (jnp.tile(gqa_qn_g[i] * (GQA_DH ** -0.5 * LOG2E), 2).reshape(1, LANES),
                  jnp.tile(gqa_kn_g[i], 2).reshape(1, LANES))
            q, k, v = proj(w_in, C_OUTS)
            o_p, new_k, new_v = _gqa(q, k, v, None, None, gp, B=Bp, T=Tp, tq=Tp,
                                     kv_out=(i, n_c, new_k, new_v), nb=prompt_nb)
            o_s, _, _ = _gqa(q, k, v, (gqa_k_cache, gqa_v_cache, i), rope_gqa, gp, B=Bs, T=Ts, tq=LATENT_TQ,
                             row0=n_p)
            wps = (w_out,)
            acts_p, acts_s = (o_p,), (o_s,)
        xp = _mix_ffn(acts_p, wps, xp, norm_ffn_g[l], mod, ffn_w_in, ffn_w_out, l, **streams[0], tm=ffn_tm)
        xs = _mix_ffn(acts_s, wps, xs, norm_ffn_g[l], mod, ffn_w_in, ffn_w_out, l, **streams[1], tm=ffn_tm)
    kv_shape = (Bp, n_c, Tp, GQA_KV_HEADS, GQA_DH)
    return (xp.reshape(Bp, Tp, D), xs.reshape(Bs, Ts, D), new_ckv, new_kpe, new_gla,
            new_k.reshape(kv_shape), new_v.reshape(kv_shape))
```
